```python
import jax
import jax.numpy as jnp
from jax import lax
import numpy as np

D_MODEL = 1024
BATCH = 8
SEQ = 4096
DEPTH = 4

MEM_LEN = 256

NSA_HEADS = 8
NSA_GROUPS = 2
NSA_HEAD_DIM = 64
CMP_BLOCK = 32
CMP_STRIDE = 16
CMP_HIDDEN = 4 * NSA_HEAD_DIM
SEL_BLOCK = 64
N_SEL = 16
WINDOW = 512
NSA_QBLOCK = 64
FORCE_SCORE = 1e4

GLA_HEADS = 4
GLA_HEAD_DK = 64
GLA_HEAD_DV = 128
GLA_RANK = 16
GLA_TAU = 16.0
GLA_CHUNK = 64

MEM_HEADS = 4
MEM_HEAD_DIM = 128

N_BRANCH = 3
BRANCH_WIDTH = NSA_HEADS * NSA_HEAD_DIM
MLP_HIDDEN = 4 * D_MODEL
ROPE_THETA = 500000.0
ROPE_FRACTION = 4
NORM_EPS = 1e-6

IN_SIZES = (
    NSA_HEADS * NSA_HEAD_DIM,
    NSA_GROUPS * NSA_HEAD_DIM,
    NSA_GROUPS * NSA_HEAD_DIM,
    NSA_GROUPS * NSA_HEAD_DIM,
    NSA_GROUPS * NSA_HEAD_DIM,
    NSA_GROUPS * NSA_HEAD_DIM,
    NSA_GROUPS * NSA_HEAD_DIM,
    NSA_HEADS * 3,
    GLA_HEADS * GLA_HEAD_DK,
    GLA_HEADS * GLA_HEAD_DK,
    GLA_HEADS * GLA_HEAD_DV,
    GLA_HEADS * GLA_HEAD_DV,
    GLA_RANK,
    MEM_HEADS * MEM_HEAD_DIM,
    N_BRANCH * D_MODEL,
)
D_IN = sum(IN_SIZES)

kernel_name = 'nsa_gla_memory_parallel_hybrid'


def rms_norm(x, g):
    xf = x.astype(jnp.float32)
    y = xf * lax.rsqrt(jnp.mean(xf * xf, axis=-1, keepdims=True) + NORM_EPS)
    return (y * g.astype(jnp.float32)).astype(x.dtype)


def split_heads(t, n):
    B, S, _ = t.shape
    return t.reshape(B, S, n, -1).transpose(0, 2, 1, 3)


def partial_rope(x, pos):
    rot = x.shape[-1] // ROPE_FRACTION
    half = rot // 2
    inv_freq = jnp.power(ROPE_THETA, -jnp.arange(half, dtype=jnp.float32) / half)
    ang = pos.astype(jnp.float32)[:, None, :, None] * inv_freq
    cos, sin = jnp.cos(ang), jnp.sin(ang)
    xf = x.astype(jnp.float32)
    x1, x2, rest = xf[..., :half], xf[..., half:rot], xf[..., rot:]
    out = jnp.concatenate([x1 * cos - x2 * sin, x2 * cos + x1 * sin, rest], axis=-1)
    return out.astype(x.dtype)


def masked_softmax(s, mask):
    s = jnp.where(mask, s.astype(jnp.float32), -jnp.inf)
    m = jnp.max(s, axis=-1, keepdims=True)
    m = jnp.where(jnp.isfinite(m), m, 0.0)
    p = jnp.exp(s - m)
    d = jnp.sum(p, axis=-1, keepdims=True)
    return p / jnp.where(d > 0, d, 1.0)


def nsa_attention(q, k_c, v_c, k_s, v_s, k_w, v_w, gate_logits, positions,
                  q_norm, k_norm, cmp_pe, cmp_w1, cmp_w2):
    B, S, _ = q.shape
    H, G, dh, QB = NSA_HEADS, NSA_GROUPS, NSA_HEAD_DIM, NSA_QBLOCK
    R = H // G
    q = partial_rope(rms_norm(split_heads(q, H), q_norm), positions) * (dh ** -0.5)
    k_s = partial_rope(rms_norm(split_heads(k_s, G), k_norm[1]), positions)
    k_w = partial_rope(rms_norm(split_heads(k_w, G), k_norm[2]), positions)
    v_s = split_heads(v_s, G)
    v_w = split_heads(v_w, G)

    n_cmp = (S - CMP_BLOCK) // CMP_STRIDE + 1
    cmp_idx = np.arange(n_cmp)[:, None] * CMP_STRIDE + np.arange(CMP_BLOCK)[None, :]
    cmp_end_np = cmp_idx[:, -1]
    raw = jnp.stack([split_heads(k_c, G), split_heads(v_c, G)])[:, :, :, cmp_idx]
    raw = (raw + cmp_pe[:, None, None, None]).reshape(2, B, G, n_cmp, CMP_BLOCK * dh)
    hid = jax.nn.gelu(jnp.einsum('kbgnf,kfe->kbgne', raw, cmp_w1))
    comp = jnp.einsum('kbgne,ked->kbgnd', hid, cmp_w2)
    k_cmp = partial_rope(rms_norm(comp[0], k_norm[0]), positions[:, cmp_end_np])
    v_cmp = comp[1]
    cmp_end = jnp.asarray(cmp_end_np, jnp.int32)

    n_blk = S // SEL_BLOCK
    c_start = np.arange(n_cmp) * CMP_STRIDE
    b_start = np.arange(n_blk) * SEL_BLOCK
    overlap = jnp.asarray(((c_start[:, None] < b_start[None, :] + SEL_BLOCK)
                           & (b_start[None, :] < c_start[:, None] + CMP_BLOCK)).astype(np.float32))
    n_sel = min(N_SEL, n_blk)
    ks_blk = k_s.reshape(B, G, n_blk, SEL_BLOCK, dh)
    vs_blk = v_s.reshape(B, G, n_blk, SEL_BLOCK, dh)
    kw_pad = jnp.pad(k_w, ((0, 0), (0, 0), (WINDOW, 0), (0, 0)))
    vw_pad = jnp.pad(v_w, ((0, 0), (0, 0), (WINDOW, 0), (0, 0)))

    n_qb = S // QB
    q_blocks = q.reshape(B, H, n_qb, QB, dh).transpose(2, 0, 1, 3, 4).reshape(n_qb, B, G, R, QB, dh)
    gates = jax.nn.sigmoid(gate_logits.astype(jnp.float32)).astype(q.dtype)
    g_blocks = (gates.reshape(B, S, H, 3).transpose(0, 2, 1, 3)
                .reshape(B, H, n_qb, QB, 3).transpose(2, 0, 1, 3, 4).reshape(n_qb, B, G, R, QB, 3))
    gather_blocks = jax.vmap(jax.vmap(lambda blk, ix: blk[ix]))
    blk_ids = jnp.arange(n_blk)

    def query_block(args):
        qi, qb, gb = args
        t = qi * QB + jnp.arange(QB)
        s_c = jnp.einsum('bgrqd,bgnd->bgrqn', qb, k_cmp)
        p_c = masked_softmax(s_c, cmp_end[None, :] <= t[:, None])
        o_c = jnp.einsum('bgrqn,bgnd->bgrqd', p_c.astype(v_cmp.dtype), v_cmp)
        imp = jnp.einsum('bgqn,nj->bgqj', jnp.sum(p_c, axis=2), overlap)
        causal = (blk_ids * SEL_BLOCK)[None, :] <= t[:, None]
        cur = (t // SEL_BLOCK)[:, None]
        forced = causal & ((blk_ids[None, :] == 0) | (blk_ids[None, :] == cur)
                           | (blk_ids[None, :] == cur - 1))
        score = jnp.where(forced, FORCE_SCORE, jnp.where(causal, imp, -FORCE_SCORE))
        _, idx = lax.top_k(score, n_sel)
        k_g = gather_blocks(ks_blk, idx)
        v_g = gather_blocks(vs_blk, idx)
        kpos = idx[..., None] * SEL_BLOCK + jnp.arange(SEL_BLOCK)
        mask_s = (kpos <= t[:, None, None])[:, :, None].reshape(B, G, 1, QB, n_sel * SEL_BLOCK)
        s_s = jnp.einsum('bgrqd,bgqnld->bgrqnl', qb, k_g).reshape(B, G, R, QB, n_sel * SEL_BLOCK)
        p_s = masked_softmax(s_s, mask_s).reshape(B, G, R, QB, n_sel, SEL_BLOCK)
        o_s = jnp.einsum('bgrqnl,bgqnld->bgrqd', p_s.astype(v_g.dtype), v_g)
        kwb = lax.dynamic_slice_in_dim(kw_pad, qi * QB, WINDOW + QB, axis=2)
        vwb = lax.dynamic_slice_in_dim(vw_pad, qi * QB, WINDOW + QB, axis=2)
        kp = qi * QB - WINDOW + jnp.arange(WINDOW + QB)
        mask_w = ((kp[None, :] <= t[:, None]) & (kp[None, :] > t[:, None] - WINDOW)
                  & (kp[None, :] >= 0))
        p_w = masked_softmax(jnp.einsum('bgrqd,bgkd->bgrqk', qb, kwb), mask_w)
        o_w = jnp.einsum('bgrqk,bgkd->bgrqd', p_w.astype(vwb.dtype), vwb)
        return gb[..., 0:1] * o_c + gb[..., 1:2] * o_s + gb[..., 2:3] * o_w

    out = lax.map(query_block, (jnp.arange(n_qb), q_blocks, g_blocks))
    return out.reshape(n_qb, B, H, QB, dh).transpose(1, 0, 3, 2, 4).reshape(B, S, H * dh)


def gated_linear_attention(q, k, v, r, g_low, w_gate, b_gate, norm_g):
    B, S, _ = q.shape
    H, C = GLA_HEADS, GLA_CHUNK
    n_c = S // C
    log_a = jax.nn.log_sigmoid((g_low @ w_gate + b_gate).astype(jnp.float32)) / GLA_TAU

    def chunks(t):
        return t.astype(jnp.float32).reshape(B, n_c, C, H, -1).transpose(1, 0, 3, 2, 4)

    qc = chunks(q) * (GLA_HEAD_DK ** -0.5)
    kc, vc, ac = chunks(k), chunks(v), chunks(log_a)
    tril = jnp.tril(jnp.ones((C, C), dtype=bool))

    def step(state, xs):
        qb, kb, vb, ab = xs
        b = jnp.cumsum(ab, axis=2)
        decay = jnp.exp(jnp.where(tril[:, :, None], b[:, :, :, None, :] - b[:, :, None, :, :], -jnp.inf))
        attn = jnp.einsum('bhid,bhjd,bhijd->bhij', qb, kb, decay)
        o = attn @ vb + jnp.einsum('bhid,bhdv->bhiv', qb * jnp.exp(b), state)
        b_last = b[:, :, -1:, :]
        state = (jnp.exp(b_last)[:, :, 0, :, None] * state
                 + jnp.einsum('bhjd,bhjv->bhdv', kb * jnp.exp(b_last - b), vb))
        return state, o

    state0 = jnp.zeros((B, H, GLA_HEAD_DK, GLA_HEAD_DV), jnp.float32)
    _, o = lax.scan(step, state0, (qc, kc, vc, ac))
    o = o.transpose(1, 0, 3, 2, 4).reshape(B, S, H, GLA_HEAD_DV)
    o = rms_norm(o, norm_g).reshape(B, S, H * GLA_HEAD_DV)
    return (o * jax.nn.silu(r.astype(jnp.float32))).astype(q.dtype)


def memory_attention(q, mem, mem_norm, w_kv, q_norm, k_norm):
    B, S, _ = q.shape
    H, dh = MEM_HEADS, MEM_HEAD_DIM
    k, v = jnp.split(rms_norm(mem, mem_norm) @ w_kv, 2, axis=-1)
    q = rms_norm(split_heads(q, H), q_norm) * (dh ** -0.5)
    k = rms_norm(split_heads(k, H), k_norm)
    v = split_heads(v, H)
    p = jax.nn.softmax(jnp.einsum('bhsd,bhmd->bhsm', q, k).astype(jnp.float32), axis=-1)
    o = jnp.einsum('bhsm,bhmd->bhsd', p.astype(v.dtype), v)
    return o.transpose(0, 2, 1, 3).reshape(B, S, H * dh)


def hybrid_layer(x, mem, positions, ln_mix, w_in, b_merge, nsa_q_norm, nsa_k_norm,
                 cmp_pe, cmp_w1, cmp_w2, gla_w_gate, gla_b_gate, gla_norm,
                 mem_norm, mem_w_kv, mem_q_norm, mem_k_norm, w_branch, w_out,
                 ln_mlp, w_up, w_down):
    B, S, D = x.shape
    h = rms_norm(x, ln_mix)
    z = h @ w_in
    offsets = np.cumsum(IN_SIZES)[:-1].tolist()
    (n_q, n_kc, n_vc, n_ks, n_vs, n_kw, n_vw, n_g,
     g_q, g_k, g_v, g_r, g_low, m_q, merge) = jnp.split(z, offsets, axis=-1)
    o_nsa = nsa_attention(n_q, n_kc, n_vc, n_ks, n_vs, n_kw, n_vw, n_g, positions,
                          nsa_q_norm, nsa_k_norm, cmp_pe, cmp_w1, cmp_w2)
    o_gla = gated_linear_attention(g_q, g_k, g_v, g_r, g_low, gla_w_gate, gla_b_gate, gla_norm)
    o_mem = memory_attention(m_q, mem, mem_norm, mem_w_kv, mem_q_norm, mem_k_norm)
    gates = jax.nn.sigmoid((merge.reshape(B, S, N_BRANCH, D) + b_merge).astype(jnp.float32)).astype(x.dtype)
    merged = (gates[:, :, 0] * (o_nsa @ w_branch[0])
              + gates[:, :, 1] * (o_gla @ w_branch[1])
              + gates[:, :, 2] * (o_mem @ w_branch[2]))
    x = x + merged @ w_out
    h = rms_norm(x, ln_mlp)
    return x + jnp.square(jax.nn.relu(h @ w_up)) @ w_down


def setup_inputs(seed: int = 0) -> dict:
    key = jax.random.key(seed)
    ks = jax.random.split(key, 24)
    f32 = jnp.float32
    dh = NSA_HEAD_DIM
    resid = (2 * DEPTH) ** -0.5

    def nrm(k, shape, scale):
        return jax.random.normal(k, shape, f32) * scale

    def gain(k, shape):
        return 1.0 + 0.02 * jax.random.normal(k, shape, f32)

    positions = (jnp.arange(SEQ, dtype=jnp.int32)[None, :]
                 + jax.random.randint(ks[2], (BATCH, 1), 0, SEQ, dtype=jnp.int32))
    return {
        'x': nrm(ks[0], (BATCH, SEQ, D_MODEL), 1.0),
        'mem': nrm(ks[1], (BATCH, MEM_LEN, D_MODEL), 1.0),
        'positions': positions,
        'ln_mix': gain(ks[3], (DEPTH, D_MODEL)),
        'w_in': nrm(ks[4], (DEPTH, D_MODEL, D_IN), D_MODEL ** -0.5),
        'b_merge': nrm(ks[5], (DEPTH, N_BRANCH, D_MODEL), 0.02),
        'nsa_q_norm': gain(ks[6], (DEPTH, dh)),
        'nsa_k_norm': gain(ks[7], (DEPTH, 3, dh)),
        'cmp_pe': nrm(ks[8], (DEPTH, 2, CMP_BLOCK, dh), 0.02),
        'cmp_w1': nrm(ks[9], (DEPTH, 2, CMP_BLOCK * dh, CMP_HIDDEN), (CMP_BLOCK * dh) ** -0.5),
        'cmp_w2': nrm(ks[10], (DEPTH, 2, CMP_HIDDEN, dh), CMP_HIDDEN ** -0.5),
        'gla_w_gate': nrm(ks[11], (DEPTH, GLA_RANK, GLA_HEADS * GLA_HEAD_DK), GLA_RANK ** -0.5),
        'gla_b_gate': nrm(ks[12], (DEPTH, GLA_HEADS * GLA_HEAD_DK), 0.02),
        'gla_norm': gain(ks[13], (DEPTH, GLA_HEAD_DV)),
        'mem_norm': gain(ks[14], (DEPTH, D_MODEL)),
        'mem_w_kv': nrm(ks[15], (DEPTH, D_MODEL, 2 * MEM_HEADS * MEM_HEAD_DIM), D_MODEL ** -0.5),
        'mem_q_norm': gain(ks[16], (DEPTH, MEM_HEAD_DIM)),
        'mem_k_norm': gain(ks[17], (DEPTH, MEM_HEAD_DIM)),
        'w_branch': nrm(ks[18], (DEPTH, N_BRANCH, BRANCH_WIDTH, D_MODEL), BRANCH_WIDTH ** -0.5),
        'w_out': nrm(ks[19], (DEPTH, D_MODEL, D_MODEL), D_MODEL ** -0.5 * resid),
        'ln_mlp': gain(ks[20], (DEPTH, D_MODEL)),
        'w_up': nrm(ks[21], (DEPTH, D_MODEL, MLP_HIDDEN), D_MODEL ** -0.5),
        'w_down': nrm(ks[22], (DEPTH, MLP_HIDDEN, D_MODEL), MLP_HIDDEN ** -0.5 * resid),
    }


def reference(x, mem, positions, ln_mix, w_in, b_merge, nsa_q_norm, nsa_k_norm,
              cmp_pe, cmp_w1, cmp_w2, gla_w_gate, gla_b_gate, gla_norm,
              mem_norm, mem_w_kv, mem_q_norm, mem_k_norm, w_branch, w_out,
              ln_mlp, w_up, w_down):
    for l in range(DEPTH):
        x = hybrid_layer(x, mem, positions, ln_mix[l], w_in[l], b_merge[l],
                         nsa_q_norm[l], nsa_k_norm[l], cmp_pe[l], cmp_w1[l], cmp_w2[l],
                         gla_w_gate[l], gla_b_gate[l], gla_norm[l],
                         mem_norm[l], mem_w_kv[l], mem_q_norm[l], mem_k_norm[l],
                         w_branch[l], w_out[l], ln_mlp[l], w_up[l], w_down[l])
    return x
```

```python
import functools

import numpy as np
import jax
import jax.numpy as jnp
from jax import lax
from jax.experimental import pallas as pl
from jax.experimental.pallas import tpu as pltpu

NSA_HEADS = 8
NSA_GROUPS = 2
NSA_REP = NSA_HEADS // NSA_GROUPS
NSA_HEAD_DIM = 64
CMP_BLOCK = 32
CMP_STRIDE = 16
CMP_HIDDEN = 4 * NSA_HEAD_DIM
SEL_BLOCK = 64
N_SEL = 16
WINDOW = 512
FORCE_SCORE = 1e4
GLA_HEADS = 4
GLA_HEAD_DK = 64
GLA_HEAD_DV = 128
GLA_RANK = 16
GLA_TAU = 16.0
GLA_CHUNK = 64
GLA_SUB = 16
MEM_HEADS = 4
MEM_HEAD_DIM = 128
N_BRANCH = 3
BRANCH_WIDTH = 512
ROPE_THETA = 500000.0
ROPE_ROT = NSA_HEAD_DIM // 4
ROPE_HALF = ROPE_ROT // 2
NORM_EPS = 1e-6

LANES = 128
VMEM_LIMIT_BYTES = 48 * 1024 * 1024

F32 = jnp.float32
BF16 = jnp.bfloat16
HIGHEST = lax.Precision.HIGHEST
NEG_BIG = -1e30

COL_NQ = 0
COL_GV = 512
COL_GR = 1024
COL_MQ = 1536
COL_MERGE = 2048
COL_GQ = 5120
COL_GK = 5376
COL_KV = 5632
COL_SMALL = 6400
D_IN_PAD = 6656
SMALL_GLOW_LANE = 12


def _in_proj_permutation(d_model):
    sizes = (512, 128, 128, 128, 128, 128, 128, 24, 256, 256, 512, 512, 16, 512, 3 * d_model)
    off = np.concatenate([[0], np.cumsum(sizes)])
    (o_nq, o_kc, o_vc, o_ks, o_vs, o_kw, o_vw, o_ng, o_gq, o_gk, o_gv, o_gr, o_gl, o_mq, o_mg) = off[:-1]
    d_in = int(off[-1])
    perm = np.full((D_IN_PAD,), d_in, np.int32)

    def put(new, old, n):
        perm[new:new + n] = np.arange(old, old + n)

    put(COL_NQ, o_nq, 512)
    put(COL_GV, o_gv, 512)
    put(COL_GR, o_gr, 512)
    put(COL_MQ, o_mq, 512)
    put(COL_MERGE, o_mg, 3 * d_model)
    put(COL_GQ, o_gq, 256)
    put(COL_GK, o_gk, 256)
    put(COL_KV, o_kc, 768)
    for g in range(NSA_GROUPS):
        for br in range(3):
            for r in range(NSA_REP):
                perm[COL_SMALL + g * LANES + br * NSA_REP + r] = o_ng + (g * NSA_REP + r) * 3 + br
    put(COL_SMALL + SMALL_GLOW_LANE, o_gl, GLA_RANK)
    return perm, d_in


def _cparams(*sem):
    return pltpu.CompilerParams(dimension_semantics=sem, vmem_limit_bytes=VMEM_LIMIT_BYTES)


def _norm_matmul_kernel(x_ref, g_ref, w_ref, o_ref, h_ref):
    @pl.when(pl.program_id(1) == 0)
    def _():
        x = x_ref[...].astype(F32)
        ms = jnp.mean(x * x, axis=-1, keepdims=True)
        h_ref[...] = (x * lax.rsqrt(ms + NORM_EPS) * g_ref[...]).astype(h_ref.dtype)

    o_ref[...] = jnp.dot(h_ref[...], w_ref[...], preferred_element_type=F32).astype(o_ref.dtype)


def _norm_matmul(x, g, w, *, tm, tn, out_dtype, name):
    m, k = x.shape
    n = w.shape[1]
    return pl.pallas_call(
        _norm_matmul_kernel,
        grid=(m // tm, n // tn),
        in_specs=[
            pl.BlockSpec((tm, k), lambda i, j: (i, 0)),
            pl.BlockSpec((1, k), lambda i, j: (0, 0)),
            pl.BlockSpec((k, tn), lambda i, j: (0, j)),
        ],
        out_specs=pl.BlockSpec((tm, tn), lambda i, j: (i, j)),
        out_shape=jax.ShapeDtypeStruct((m, n), out_dtype),
        scratch_shapes=[pltpu.VMEM((tm, k), BF16)],
        compiler_params=_cparams("parallel", "arbitrary"),
        name=name,
    )(x, g.reshape(1, k), w)


def _rope_lanes(x, pos, freq):
    n = x.shape[-1]
    ang = pos * freq
    c = jnp.cos(ang)
    s = jnp.sin(ang)
    lane = lax.broadcasted_iota(jnp.int32, (1, n), 1) % NSA_HEAD_DIM
    up = pltpu.roll(x, n - ROPE_HALF, axis=1)
    dn = pltpu.roll(x, ROPE_HALF, axis=1)
    y = jnp.where(lane < ROPE_HALF, -up, jnp.where(lane < ROPE_ROT, dn, 0.0))
    return x * c + y * s


def _head_rms(x, bd, g):
    ms = jnp.dot(x * x, bd, preferred_element_type=F32, precision=HIGHEST)
    return x * lax.rsqrt(ms + NORM_EPS) * g


def _nsa_prep_kernel(q_ref, ks_ref, vs_ref, kw_ref, vw_ref, pos_ref, qg_ref, ksg_ref, kwg_ref,
                     bdq_ref, bdk_ref, fq_ref, fk_ref,
                     qo_ref, kso_ref, vso_ref, kwo_ref, vwo_ref):
    pos = pos_ref[0].astype(F32)
    q = _head_rms(q_ref[0].astype(F32), bdq_ref[...], qg_ref[...])
    q = _rope_lanes(q, pos, fq_ref[...]) * (NSA_HEAD_DIM ** -0.5)
    for h in range(NSA_HEADS):
        qo_ref[0, h] = q[:, h * NSA_HEAD_DIM:(h + 1) * NSA_HEAD_DIM].astype(qo_ref.dtype)
    ks = _rope_lanes(_head_rms(ks_ref[0].astype(F32), bdk_ref[...], ksg_ref[...]), pos, fk_ref[...])
    kw = _rope_lanes(_head_rms(kw_ref[0].astype(F32), bdk_ref[...], kwg_ref[...]), pos, fk_ref[...])
    vs = vs_ref[0]
    vw = vw_ref[0]
    for g in range(NSA_GROUPS):
        sl = slice(g * NSA_HEAD_DIM, (g + 1) * NSA_HEAD_DIM)
        kso_ref[0, g] = ks[:, sl].astype(kso_ref.dtype)
        kwo_ref[0, g] = kw[:, sl].astype(kwo_ref.dtype)
        vso_ref[0, g] = vs[:, sl].astype(vso_ref.dtype)
        vwo_ref[0, g] = vw[:, sl].astype(vwo_ref.dtype)


def _block_diag_mean(n, width):
    i = np.arange(n)
    return jnp.asarray((i[:, None] // width == i[None, :] // width).astype(np.float32) / width)


def _rope_freq_lanes(n):
    p = np.arange(n) % NSA_HEAD_DIM
    inv = ROPE_THETA ** (-(np.arange(ROPE_HALF, dtype=np.float64)) / ROPE_HALF)
    f = np.where(p < ROPE_ROT, inv[p % ROPE_HALF], 0.0)
    return jnp.asarray(f.astype(np.float32)).reshape(1, n)


def _nsa_prep(z3, pos3, q_norm, ks_norm, kw_norm, *, tp):
    b, s, _ = z3.shape
    hd, g = NSA_HEAD_DIM, NSA_GROUPS
    kvb = COL_KV // LANES
    qg = jnp.tile(q_norm, NSA_HEADS).reshape(1, NSA_HEADS * hd)
    ksg = jnp.tile(ks_norm, g).reshape(1, g * hd)
    kwg = jnp.tile(kw_norm, g).reshape(1, g * hd)
    const = lambda shape: pl.BlockSpec(shape, lambda bi, i: (0,) * len(shape))
    kv_out = jax.ShapeDtypeStruct((b, g, s, hd), BF16)
    kv_spec = pl.BlockSpec((1, g, tp, hd), lambda bi, i: (bi, 0, i, 0))
    return pl.pallas_call(
        _nsa_prep_kernel,
        grid=(b, s // tp),
        in_specs=[
            pl.BlockSpec((1, tp, 512), lambda bi, i: (bi, i, COL_NQ // 512)),
            pl.BlockSpec((1, tp, LANES), lambda bi, i: (bi, i, kvb + 2)),
            pl.BlockSpec((1, tp, LANES), lambda bi, i: (bi, i, kvb + 3)),
            pl.BlockSpec((1, tp, LANES), lambda bi, i: (bi, i, kvb + 4)),
            pl.BlockSpec((1, tp, LANES), lambda bi, i: (bi, i, kvb + 5)),
            pl.BlockSpec((1, tp, 1), lambda bi, i: (bi, i, 0)),
            const((1, 512)), const((1, LANES)), const((1, LANES)),
            const((512, 512)), const((LANES, LANES)),
            const((1, 512)), const((1, LANES)),
        ],
        out_specs=[
            pl.BlockSpec((1, NSA_HEADS, tp, hd), lambda bi, i: (bi, 0, i, 0)),
            kv_spec, kv_spec, kv_spec, kv_spec,
        ],
        out_shape=[jax.ShapeDtypeStruct((b, NSA_HEADS, s, hd), BF16), kv_out, kv_out, kv_out, kv_out],
        compiler_params=_cparams("parallel", "parallel"),
        name="nsa_prep",
    )(z3, z3, z3, z3, z3, pos3, qg, ksg, kwg,
      _block_diag_mean(512, hd), _block_diag_mean(LANES, hd), _rope_freq_lanes(512), _rope_freq_lanes(LANES))


def _gelu_tanh(x):
    return 0.5 * x * (1.0 + jnp.tanh(np.sqrt(2.0 / np.pi) * (x + 0.044715 * x * x * x)))


def _nsa_cmp_kernel(x_ref, w1_ref, w2_ref, pe_ref, pos_ref, g_ref, rot_ref, f_ref, o_ref):
    kind = pl.program_id(0)
    nc = x_ref.shape[3]
    x = x_ref[0, 0, 0].astype(BF16)
    w1 = w1_ref[0]
    ab = jnp.dot(x, w1, preferred_element_type=F32)
    r = jnp.dot(pe_ref[0].astype(BF16), w1, preferred_element_type=F32)
    a = ab[:, :CMP_HIDDEN] + r[0:1, :CMP_HIDDEN]
    bm = ab[:, CMP_HIDDEN:] + r[1:2, CMP_HIDDEN:]
    hid = _gelu_tanh(a + pltpu.roll(bm, nc - 1, axis=0))
    comp = jnp.dot(hid.astype(BF16), w2_ref[0], preferred_element_type=F32)

    @pl.when(kind == 0)
    def _():
        ms = jnp.mean(comp * comp, axis=-1, keepdims=True)
        kn = comp * lax.rsqrt(ms + NORM_EPS) * g_ref[...]
        ang = pos_ref[0].astype(F32) * f_ref[...]
        y = jnp.dot(kn, rot_ref[...], preferred_element_type=F32, precision=HIGHEST)
        o_ref[0, 0, 0] = (kn * jnp.cos(ang) + y * jnp.sin(ang)).astype(o_ref.dtype)

    @pl.when(kind != 0)
    def _():
        o_ref[0, 0, 0] = comp.astype(o_ref.dtype)


def _rope_rot_matrix():
    m = np.zeros((NSA_HEAD_DIM, NSA_HEAD_DIM), np.float32)
    for l in range(ROPE_HALF):
        m[l + ROPE_HALF, l] = -1.0
        m[l, l + ROPE_HALF] = 1.0
    return jnp.asarray(m)


def _nsa_compress(xc, w1cat, w2, pe2, pos_cmp, k_norm0):
    _, b, g, nc, _ = xc.shape
    hd = NSA_HEAD_DIM
    const = lambda shape: pl.BlockSpec(shape, lambda k, bi, gi: (0,) * len(shape))
    return pl.pallas_call(
        _nsa_cmp_kernel,
        grid=(2, b, g),
        in_specs=[
            pl.BlockSpec((1, 1, 1, nc, CMP_STRIDE * hd), lambda k, bi, gi: (k, bi, gi, 0, 0)),
            pl.BlockSpec((1, CMP_STRIDE * hd, 2 * CMP_HIDDEN), lambda k, bi, gi: (k, 0, 0)),
            pl.BlockSpec((1, CMP_HIDDEN, hd), lambda k, bi, gi: (k, 0, 0)),
            pl.BlockSpec((1, 8, CMP_STRIDE * hd), lambda k, bi, gi: (k, 0, 0)),
            pl.BlockSpec((1, nc, 1), lambda k, bi, gi: (bi, 0, 0)),
            const((1, hd)), const((hd, hd)), const((1, hd)),
        ],
        out_specs=pl.BlockSpec((1, 1, 1, nc, hd), lambda k, bi, gi: (k, bi, gi, 0, 0)),
        out_shape=jax.ShapeDtypeStruct((2, b, g, nc, hd), BF16),
        compiler_params=_cparams("parallel", "parallel", "parallel"),
        name="nsa_compress",
    )(xc, w1cat, w2, pe2, pos_cmp, k_norm0.reshape(1, hd), _rope_rot_matrix(), _rope_freq_lanes(hd))


def _softmax_rows(s, mask):
    s = jnp.where(mask, s, NEG_BIG)
    m = jnp.max(s, axis=-1, keepdims=True)
    p = jnp.where(mask, jnp.exp(s - m), 0.0)
    d = jnp.sum(p, axis=-1, keepdims=True)
    return p / jnp.where(d > 0, d, 1.0)


def _nsa_attn_kernel(q_ref, kc_ref, vc_ref, ks_ref, vs_ref, kw_ref, vw_ref, gl_ref, ov_ref, o_ref,
                     m_ref, l_ref, acc_ref, *, tq, tk, n_sel):
    rep, hd = NSA_REP, NSA_HEAD_DIM
    rows = rep * tq
    qi = pl.program_id(2)
    t0 = qi * tq
    q4 = q_ref[0].reshape(rows, hd)
    t_row = t0 + (lax.broadcasted_iota(jnp.int32, (rows, 1), 0) & (tq - 1))
    nt = (((1,), (1,)), ((), ()))

    kc = kc_ref[0, 0, 0]
    ncp = kc.shape[0]
    s_c = lax.dot_general(q4, kc, nt, preferred_element_type=F32)
    n_idx = lax.broadcasted_iota(jnp.int32, (1, ncp), 1)
    valid_c = (n_idx * CMP_STRIDE + (CMP_BLOCK - 1) <= t_row) & (n_idx < ncp - 1)
    p_c = _softmax_rows(s_c, valid_c)
    o_c = jnp.dot(p_c.astype(BF16), vc_ref[0, 0, 0], preferred_element_type=F32)
    p_sum = p_c[0:tq]
    for r in range(1, rep):
        p_sum = p_sum + p_c[r * tq:(r + 1) * tq]

    n_blk = ov_ref.shape[0]
    imp_t = lax.dot_general(ov_ref[...], p_sum, nt, preferred_element_type=F32, precision=HIGHEST)
    j_col = lax.broadcasted_iota(jnp.int32, (n_blk, 1), 0)
    t_lane = t0 + lax.broadcasted_iota(jnp.int32, (1, tq), 1)
    causal = j_col * SEL_BLOCK <= t_lane
    cur = t_lane // SEL_BLOCK
    forced = causal & ((j_col == 0) | (j_col == cur) | (j_col == cur - 1))
    score = jnp.where(forced, FORCE_SCORE, jnp.where(causal, imp_t, -FORCE_SCORE))
    rank = jnp.zeros((n_blk, tq), F32)
    j_full = lax.broadcasted_iota(jnp.int32, (n_blk, tq), 0)
    for jp in range(n_blk):
        row = score[jp:jp + 1, :]
        ahead = (row > score) | ((row == score) & (j_full > jp))
        rank = rank + jnp.where(ahead, 1.0, 0.0)
    sel_t = (rank < n_sel).astype(BF16)
    eye = (lax.broadcasted_iota(jnp.int32, (tq, tq), 0) == lax.broadcasted_iota(jnp.int32, (tq, tq), 1)).astype(BF16)
    sel = lax.dot_general(eye, sel_t, nt, preferred_element_type=F32).astype(BF16)

    m_ref[...] = jnp.full(m_ref.shape, NEG_BIG, F32)
    l_ref[...] = jnp.zeros(l_ref.shape, F32)
    acc_ref[...] = jnp.zeros(acc_ref.shape, F32)
    t_q = t0 + lax.broadcasted_iota(jnp.int32, (tq, 1), 0)

    def kv_step(c, carry):
        k0 = pl.multiple_of(c * tk, tk)
        kt = ks_ref[0, 0, pl.ds(k0, tk), :]
        vt = vs_ref[0, 0, pl.ds(k0, tk), :]
        s = lax.dot_general(q4, kt, nt, preferred_element_type=F32).reshape(rep, tq, tk)
        kk = lax.broadcasted_iota(jnp.int32, (1, tk), 1)
        blk = k0 // SEL_BLOCK + kk // SEL_BLOCK
        expand = (lax.broadcasted_iota(jnp.int32, (n_blk, 1), 0) == blk).astype(BF16)
        chosen = jnp.dot(sel, expand, preferred_element_type=F32)
        mask = ((chosen > 0.5) & (k0 + kk <= t_q))[None]
        s = jnp.where(mask, s, NEG_BIG)
        m_old = m_ref[...]
        m_new = jnp.maximum(m_old, jnp.max(s, axis=-1, keepdims=True))
        p = jnp.where(mask, jnp.exp(s - m_new), 0.0)
        alpha = jnp.exp(m_old - m_new)
        l_ref[...] = alpha * l_ref[...] + jnp.sum(p, axis=-1, keepdims=True)
        pv = jnp.dot(p.reshape(rows, tk).astype(BF16), vt, preferred_element_type=F32)
        acc_ref[...] = alpha.reshape(rows, 1) * acc_ref[...] + pv
        m_ref[...] = m_new
        return carry

    lax.fori_loop(0, (t0 + tq + tk - 1) // tk, kv_step, 0)
    o_s = acc_ref[...] / l_ref[...].reshape(rows, 1)

    span = WINDOW + tq
    w0 = pl.multiple_of(jnp.maximum(t0 - WINDOW, 0), tq)
    kwt = kw_ref[0, 0, pl.ds(w0, span), :]
    vwt = vw_ref[0, 0, pl.ds(w0, span), :]
    s_w = lax.dot_general(q4, kwt, nt, preferred_element_type=F32)
    kp = w0 + lax.broadcasted_iota(jnp.int32, (1, span), 1)
    p_w = _softmax_rows(s_w, (kp <= t_row) & (kp > t_row - WINDOW))
    o_w = jnp.dot(p_w.astype(BF16), vwt, preferred_element_type=F32)

    gates = jax.nn.sigmoid(gl_ref[0].astype(F32))
    for r in range(rep):
        rs = slice(r * tq, (r + 1) * tq)
        out = (gates[:, r:r + 1] * o_c[rs] + gates[:, rep + r:rep + r + 1] * o_s[rs]
               + gates[:, 2 * rep + r:2 * rep + r + 1] * o_w[rs])
        o_ref[0, :, r * hd:(r + 1) * hd] = out.astype(o_ref.dtype)


def _overlap_t(n_blk, ncp):
    c_start = np.arange(ncp) * CMP_STRIDE
    b_start = np.arange(n_blk) * SEL_BLOCK
    ov = ((c_start[None, :] < b_start[:, None] + SEL_BLOCK) & (b_start[:, None] < c_start[None, :] + CMP_BLOCK))
    ov[:, ncp - 1] = False
    return jnp.asarray(ov.astype(np.float32))


def _nsa_attention(q, cmp_kv, ks, vs, kw, vw, z3, *, tq, tk):
    b, h, s, hd = q.shape
    g, rep = NSA_GROUPS, NSA_REP
    ncp = cmp_kv.shape[3]
    n_blk = s // SEL_BLOCK
    rows = rep * tq
    full = pl.BlockSpec((1, 1, s, hd), lambda bi, gi, i: (bi, gi, 0, 0))
    kernel = functools.partial(_nsa_attn_kernel, tq=tq, tk=tk, n_sel=min(N_SEL, n_blk))
    return pl.pallas_call(
        kernel,
        grid=(b, g, s // tq),
        in_specs=[
            pl.BlockSpec((1, rep, tq, hd), lambda bi, gi, i: (bi, gi, i, 0)),
            pl.BlockSpec((1, 1, 1, ncp, hd), lambda bi, gi, i: (0, bi, gi, 0, 0)),
            pl.BlockSpec((1, 1, 1, ncp, hd), lambda bi, gi, i: (1, bi, gi, 0, 0)),
            full, full, full, full,
            pl.BlockSpec((1, tq, LANES), lambda bi, gi, i: (bi, i, COL_SMALL // LANES + gi)),
            pl.BlockSpec((n_blk, ncp), lambda bi, gi, i: (0, 0)),
        ],
        out_specs=pl.BlockSpec((1, tq, rep * hd), lambda bi, gi, i: (bi, i, gi)),
        out_shape=jax.ShapeDtypeStruct((b, s, h * hd), BF16),
        scratch_shapes=[pltpu.VMEM((rep, tq, 1), F32), pltpu.VMEM((rep, tq, 1), F32), pltpu.VMEM((rows, hd), F32)],
        compiler_params=_cparams("parallel", "parallel", "arbitrary"),
        name="nsa_attention",
    )(q, cmp_kv, cmp_kv, ks, vs, kw, vw, z3, _overlap_t(n_blk, ncp))


def _gla_kernel(q_ref, k_ref, v_ref, r_ref, sm_ref, wg_ref, bg_ref, ng_ref, o_ref, st_ref):
    c, sub = GLA_CHUNK, GLA_SUB
    dk, dv, nh = GLA_HEAD_DK, GLA_HEAD_DV, GLA_HEADS
    nt = (((1,), (1,)), ((), ()))

    @pl.when(pl.program_id(1) == 0)
    def _():
        st_ref[...] = jnp.zeros(st_ref.shape, F32)

    x = jnp.dot(sm_ref[0].astype(F32), wg_ref[...], preferred_element_type=F32, precision=HIGHEST) + bg_ref[...]
    log_a = (jnp.minimum(x, 0.0) - jnp.log1p(jnp.exp(-jnp.abs(x)))) / GLA_TAU
    ri = lax.broadcasted_iota(jnp.int32, (c, c), 0)
    ci = lax.broadcasted_iota(jnp.int32, (c, c), 1)
    bcum = jnp.dot((ci <= ri).astype(F32), log_a, preferred_element_type=F32, precision=HIGHEST)
    q = q_ref[0].astype(F32) * (dk ** -0.5)
    k = k_ref[0].astype(F32)
    v = v_ref[0].astype(BF16)
    b_last = bcum[c - 1:c, :]
    q_in = (q * jnp.exp(bcum)).astype(BF16)
    k_out = (k * jnp.exp(b_last - bcum)).astype(BF16)
    decay = jnp.exp(b_last)

    o_heads = [[] for _ in range(nh)]
    for i in range(c // sub):
        hi = (i + 1) * sub
        ref = bcum[i * sub - 1:i * sub, :] if i > 0 else jnp.zeros((1, nh * dk), F32)
        q_i = (q[i * sub:hi] * jnp.exp(bcum[i * sub:hi] - ref)).astype(BF16)
        k_i = (k[0:hi] * jnp.exp(ref - bcum[0:hi])).astype(BF16)
        tri = (lax.broadcasted_iota(jnp.int32, (sub, hi), 1)
               <= lax.broadcasted_iota(jnp.int32, (sub, hi), 0) + i * sub)
        for h in range(nh):
            ks_ = slice(h * dk, (h + 1) * dk)
            a = lax.dot_general(q_i[:, ks_], k_i[:, ks_], nt, preferred_element_type=F32)
            a = jnp.where(tri, a, 0.0).astype(BF16)
            o_heads[h].append(jnp.dot(a, v[0:hi, h * dv:(h + 1) * dv], preferred_element_type=F32))

    r_gate = r_ref[0].astype(F32)
    for h in range(nh):
        ks_ = slice(h * dk, (h + 1) * dk)
        vs_ = slice(h * dv, (h + 1) * dv)
        st = st_ref[h]
        o_h = jnp.concatenate(o_heads[h], axis=0)
        o_h = o_h + lax.dot_general(q_in[:, ks_], st.astype(BF16), nt, preferred_element_type=F32)
        v_t = v_ref[0, :, vs_].astype(F32).T.astype(BF16)
        st_ref[h] = st * decay[:, ks_] + jnp.dot(v_t, k_out[:, ks_], preferred_element_type=F32)
        ms = jnp.mean(o_h * o_h, axis=-1, keepdims=True)
        o_n = o_h * lax.rsqrt(ms + NORM_EPS) * ng_ref[...]
        rg = r_gate[:, vs_]
        o_ref[0, :, vs_] = (o_n * (rg * jax.nn.sigmoid(rg))).astype(o_ref.dtype)


def _gla(z3, w_gate, b_gate, norm_g):
    b, s, _ = z3.shape
    c = GLA_CHUNK
    nk = GLA_HEADS * GLA_HEAD_DK
    nv = GLA_HEADS * GLA_HEAD_DV
    wg = jnp.zeros((LANES, nk), F32).at[SMALL_GLOW_LANE:SMALL_GLOW_LANE + GLA_RANK].set(w_gate.astype(F32))
    const = lambda shape: pl.BlockSpec(shape, lambda bi, i: (0,) * len(shape))
    return pl.pallas_call(
        _gla_kernel,
        grid=(b, s // c),
        in_specs=[
            pl.BlockSpec((1, c, nk), lambda bi, i: (bi, i, COL_GQ // nk)),
            pl.BlockSpec((1, c, nk), lambda bi, i: (bi, i, COL_GK // nk)),
            pl.BlockSpec((1, c, nv), lambda bi, i: (bi, i, COL_GV // nv)),
            pl.BlockSpec((1, c, nv), lambda bi, i: (bi, i, COL_GR // nv)),
            pl.BlockSpec((1, c, LANES), lambda bi, i: (bi, i, COL_SMALL // LANES)),
            const((LANES, nk)), const((1, nk)), const((1, GLA_HEAD_DV)),
        ],
        out_specs=pl.BlockSpec((1, c, nv), lambda bi, i: (bi, i, 0)),
        out_shape=jax.ShapeDtypeStruct((b, s, nv), BF16),
        scratch_shapes=[pltpu.VMEM((GLA_HEADS, GLA_HEAD_DV, GLA_HEAD_DK), F32)],
        compiler_params=_cparams("parallel", "arbitrary"),
        name="gla",
    )(z3, z3, z3, z3, z3, wg, b_gate.reshape(1, nk).astype(F32), norm_g.reshape(1, GLA_HEAD_DV).astype(F32))


def _mem_attn_kernel(q_ref, k_ref, v_ref, qg_ref, kg_ref, o_ref):
    dh = MEM_HEAD_DIM
    nt = (((1,), (1,)), ((), ()))
    for h in range(MEM_HEADS):
        sl = slice(h * dh, (h + 1) * dh)
        q = q_ref[0, :, sl].astype(F32)
        q = q * lax.rsqrt(jnp.mean(q * q, axis=-1, keepdims=True) + NORM_EPS) * qg_ref[...] * (dh ** -0.5)
        k = k_ref[0, :, sl].astype(F32)
        k = k * lax.rsqrt(jnp.mean(k * k, axis=-1, keepdims=True) + NORM_EPS) * kg_ref[...]
        s = lax.dot_general(q.astype(BF16), k.astype(BF16), nt, preferred_element_type=F32)
        m = jnp.max(s, axis=-1, keepdims=True)
        p = jnp.exp(s - m)
        p = p / jnp.sum(p, axis=-1, keepdims=True)
        o = jnp.dot(p.astype(BF16), v_ref[0, :, sl].astype(BF16), preferred_element_type=F32)
        o_ref[0, :, sl] = o.astype(o_ref.dtype)


def _mem_attention(z3, kv, q_norm, k_norm, *, tq):
    b, s, _ = z3.shape
    m = kv.shape[1]
    w = MEM_HEADS * MEM_HEAD_DIM
    const = lambda shape: pl.BlockSpec(shape, lambda bi, i: (0,) * len(shape))
    return pl.pallas_call(
        _mem_attn_kernel,
        grid=(b, s // tq),
        in_specs=[
            pl.BlockSpec((1, tq, w), lambda bi, i: (bi, i, COL_MQ // w)),
            pl.BlockSpec((1, m, w), lambda bi, i: (bi, 0, 0)),
            pl.BlockSpec((1, m, w), lambda bi, i: (bi, 0, 1)),
            const((1, MEM_HEAD_DIM)), const((1, MEM_HEAD_DIM)),
        ],
        out_specs=pl.BlockSpec((1, tq, w), lambda bi, i: (bi, i, 0)),
        out_shape=jax.ShapeDtypeStruct((b, s, w), BF16),
        compiler_params=_cparams("parallel", "parallel"),
        name="mem_attention",
    )(z3, kv, kv, q_norm.reshape(1, MEM_HEAD_DIM).astype(F32), k_norm.reshape(1, MEM_HEAD_DIM).astype(F32))


def _merge_kernel(x_ref, on_ref, og_ref, om_ref, m0_ref, m1_ref, m2_ref, bm_ref, wb_ref, wo_ref, o_ref):
    merged = None
    for br, (ref, mg_ref) in enumerate(((on_ref, m0_ref), (og_ref, m1_ref), (om_ref, m2_ref))):
        y = jnp.dot(ref[...], wb_ref[br], preferred_element_type=F32)
        gate = jax.nn.sigmoid(mg_ref[...].astype(F32) + bm_ref[br:br + 1, :])
        merged = gate * y if merged is None else merged + gate * y
    o_ref[...] = x_ref[...] + jnp.dot(merged.astype(BF16), wo_ref[...], preferred_element_type=F32)


def _merge_out(x2, o_nsa, o_gla, o_mem, z2, b_merge, w_branch, w_out, *, tm):
    t, d = x2.shape
    bw = BRANCH_WIDTH
    row = lambda w: pl.BlockSpec((tm, w), lambda i: (i, 0))
    gate_cols = lambda br: pl.BlockSpec((tm, d), lambda i: (i, COL_MERGE // d + br))
    return pl.pallas_call(
        _merge_kernel,
        grid=(t // tm,),
        in_specs=[
            row(d), row(bw), row(bw), row(bw),
            gate_cols(0), gate_cols(1), gate_cols(2),
            pl.BlockSpec((N_BRANCH, d), lambda i: (0, 0)),
            pl.BlockSpec((N_BRANCH, bw, d), lambda i: (0, 0, 0)),
            pl.BlockSpec((d, d), lambda i: (0, 0)),
        ],
        out_specs=row(d),
        out_shape=jax.ShapeDtypeStruct((t, d), F32),
        compiler_params=_cparams("parallel"),
        name="merge_out",
    )(x2, o_nsa, o_gla, o_mem, z2, z2, z2, b_merge.astype(F32), w_branch, w_out)


def _mlp_kernel(x_ref, g_ref, wu_ref, wd_ref, o_ref, h_ref, acc_ref):
    j = pl.program_id(1)

    @pl.when(j == 0)
    def _():
        x = x_ref[...]
        ms = jnp.mean(x * x, axis=-1, keepdims=True)
        h_ref[...] = (x * lax.rsqrt(ms + NORM_EPS) * g_ref[...]).astype(h_ref.dtype)
        acc_ref[...] = jnp.zeros(acc_ref.shape, F32)

    u = jnp.dot(h_ref[...], wu_ref[...], preferred_element_type=F32)
    u = jnp.square(jnp.maximum(u, 0.0)).astype(BF16)
    acc_ref[...] += jnp.dot(u, wd_ref[...], preferred_element_type=F32)

    @pl.when(j == pl.num_programs(1) - 1)
    def _():
        o_ref[...] = x_ref[...] + acc_ref[...]


def _mlp(x2, g, w_up, w_down, *, tm, th):
    t, d = x2.shape
    hid = w_up.shape[1]
    return pl.pallas_call(
        _mlp_kernel,
        grid=(t // tm, hid // th),
        in_specs=[
            pl.BlockSpec((tm, d), lambda i, j: (i, 0)),
            pl.BlockSpec((1, d), lambda i, j: (0, 0)),
            pl.BlockSpec((d, th), lambda i, j: (0, j)),
            pl.BlockSpec((th, d), lambda i, j: (j, 0)),
        ],
        out_specs=pl.BlockSpec((tm, d), lambda i, j: (i, 0)),
        out_shape=jax.ShapeDtypeStruct((t, d), F32),
        scratch_shapes=[pltpu.VMEM((tm, d), BF16), pltpu.VMEM((tm, d), F32)],
        compiler_params=_cparams("parallel", "arbitrary"),
        name="mlp",
    )(x2, g.reshape(1, d).astype(F32), w_up, w_down)


def _tile(n, pref):
    t = min(n, pref)
    assert n % t == 0, (n, pref)
    return t


def _layer(x, mem2, pos3, pos_cmp, perm, p):
    b, s, d = x.shape
    t = b * s
    hd, g = NSA_HEAD_DIM, NSA_GROUPS
    x2 = x.reshape(t, d)

    w_in = jnp.take(jnp.pad(p["w_in"], ((0, 0), (0, 1))), perm, axis=1).astype(BF16)
    z2 = _norm_matmul(x2, p["ln_mix"].astype(F32), w_in, tm=_tile(t, 512), tn=512, out_dtype=F32, name="in_proj")
    z3 = z2.reshape(b, s, D_IN_PAD)

    q, ks, vs, kw, vw = _nsa_prep(z3, pos3, p["nsa_q_norm"].astype(F32), p["nsa_k_norm"][1].astype(F32),
                                  p["nsa_k_norm"][2].astype(F32), tp=_tile(s, 256))
    nc = s // CMP_STRIDE
    kvc = z3[:, :, COL_KV:COL_KV + 2 * g * hd].reshape(b, s, 2, g, hd)
    xc = kvc.transpose(2, 0, 3, 1, 4).reshape(2, b, g, nc, CMP_STRIDE * hd)
    w1 = p["cmp_w1"]
    half = CMP_STRIDE * hd
    w1cat = jnp.concatenate([w1[:, :half], w1[:, half:]], axis=-1).astype(BF16)
    pe2 = jnp.pad(p["cmp_pe"].reshape(2, 2, half), ((0, 0), (0, 6), (0, 0))).astype(F32)
    cmp_kv = _nsa_compress(xc, w1cat, p["cmp_w2"].astype(BF16), pe2, pos_cmp, p["nsa_k_norm"][0].astype(F32))
    o_nsa = _nsa_attention(q, cmp_kv, ks, vs, kw, vw, z3, tq=_tile(s, 128), tk=_tile(s, 512))

    o_gla = _gla(z3, p["gla_w_gate"], p["gla_b_gate"], p["gla_norm"])

    kv = _norm_matmul(mem2, p["mem_norm"].astype(F32), p["mem_w_kv"].astype(BF16),
                      tm=_tile(mem2.shape[0], 512), tn=512, out_dtype=BF16, name="mem_kv")
    kv = kv.reshape(b, mem2.shape[0] // b, 2 * MEM_HEADS * MEM_HEAD_DIM)
    o_mem = _mem_attention(z3, kv, p["mem_q_norm"], p["mem_k_norm"], tq=_tile(s, 512))

    x2 = _merge_out(x2, o_nsa.reshape(t, -1), o_gla.reshape(t, -1), o_mem.reshape(t, -1), z2,
                    p["b_merge"], p["w_branch"].astype(BF16), p["w_out"].astype(BF16), tm=_tile(t, 256))
    x2 = _mlp(x2, p["ln_mlp"], p["w_up"].astype(BF16), p["w_down"].astype(BF16), tm=_tile(t, 512), th=512)
    return x2.reshape(b, s, d)


def kernel(x, mem, positions, ln_mix, w_in, b_merge, nsa_q_norm, nsa_k_norm, cmp_pe, cmp_w1, cmp_w2,
           gla_w_gate, gla_b_gate, gla_norm, mem_norm, mem_w_kv, mem_q_norm, mem_k_norm, w_branch, w_out,
           ln_mlp, w_up, w_down):
    b, s, d = x.shape
    assert d == 1024 and s % WINDOW == 0 and s >= 2 * WINDOW
    depth = w_in.shape[0]
    perm_np, d_in = _in_proj_permutation(d)
    assert w_in.shape[2] == d_in
    perm = jnp.asarray(perm_np)
    pos3 = positions.astype(jnp.int32).reshape(b, s, 1)
    nc = s // CMP_STRIDE
    cmp_end = np.minimum(np.arange(nc) * CMP_STRIDE + CMP_BLOCK - 1, s - 1)
    pos_cmp = pos3[:, cmp_end, :]
    mem2 = mem.reshape(b * mem.shape[1], d)
    names = ("ln_mix", "w_in", "b_merge", "nsa_q_norm", "nsa_k_norm", "cmp_pe", "cmp_w1", "cmp_w2",
             "gla_w_gate", "gla_b_gate", "gla_norm", "mem_norm", "mem_w_kv", "mem_q_norm", "mem_k_norm",
             "w_branch", "w_out", "ln_mlp", "w_up", "w_down")
    stacked = (ln_mix, w_in, b_merge, nsa_q_norm, nsa_k_norm, cmp_pe, cmp_w1, cmp_w2, gla_w_gate, gla_b_gate,
               gla_norm, mem_norm, mem_w_kv, mem_q_norm, mem_k_norm, w_branch, w_out, ln_mlp, w_up, w_down)
    for l in range(depth):
        x = _layer(x, mem2, pos3, pos_cmp, perm, {n: a[l] for n, a in zip(names, stacked)})
    return x
```

```python
import functools

import numpy as np
import jax
import jax.numpy as jnp
from jax import lax
from jax.experimental import pallas as pl
from jax.experimental.pallas import tpu as pltpu

NSA_HEADS = 8
NSA_GROUPS = 2
NSA_REP = NSA_HEADS // NSA_GROUPS
NSA_HEAD_DIM = 64
CMP_BLOCK = 32
CMP_STRIDE = 16
CMP_HIDDEN = 4 * NSA_HEAD_DIM
SEL_BLOCK = 64
N_SEL = 16
WINDOW = 512
FORCE_SCORE = 1e4
GLA_HEADS = 4
GLA_HEAD_DK = 64
GLA_HEAD_DV = 128
GLA_RANK = 16
GLA_TAU = 16.0
GLA_CHUNK = 64
GLA_SUB = 16
MEM_HEADS = 4
MEM_HEAD_DIM = 128
N_BRANCH = 3
BRANCH_WIDTH = 512
ROPE_THETA = 500000.0
ROPE_ROT = NSA_HEAD_DIM // 4
ROPE_HALF = ROPE_ROT // 2
NORM_EPS = 1e-6

LANES = 128
VMEM_LIMIT_BYTES = 48 * 1024 * 1024

F32 = jnp.float32
BF16 = jnp.bfloat16
HIGHEST = lax.Precision.HIGHEST
NEG_BIG = -1e30
MAX_CONST_SHIFT = 40.0

COL_NQ = 0
COL_GV = 512
COL_GR = 1024
COL_MQ = 1536
COL_MERGE = 2048
COL_GQ = 5120
COL_GK = 5376
COL_KV = 5632
COL_SMALL = 6400
D_IN_PAD = 6656
SMALL_GLOW_LANE = 12


def _in_proj_permutation(d_model):
    sizes = (512, 128, 128, 128, 128, 128, 128, 24, 256, 256, 512, 512, 16, 512, 3 * d_model)
    off = np.concatenate([[0], np.cumsum(sizes)])
    (o_nq, o_kc, o_vc, o_ks, o_vs, o_kw, o_vw, o_ng, o_gq, o_gk, o_gv, o_gr, o_gl, o_mq, o_mg) = off[:-1]
    d_in = int(off[-1])
    perm = np.full((D_IN_PAD,), d_in, np.int32)

    def put(new, old, n):
        perm[new:new + n] = np.arange(old, old + n)

    put(COL_NQ, o_nq, 512)
    put(COL_GV, o_gv, 512)
    put(COL_GR, o_gr, 512)
    put(COL_MQ, o_mq, 512)
    put(COL_MERGE, o_mg, 3 * d_model)
    put(COL_GQ, o_gq, 256)
    put(COL_GK, o_gk, 256)
    put(COL_KV, o_kc, 768)
    for g in range(NSA_GROUPS):
        for br in range(3):
            for r in range(NSA_REP):
                perm[COL_SMALL + g * LANES + br * NSA_REP + r] = o_ng + (g * NSA_REP + r) * 3 + br
    put(COL_SMALL + SMALL_GLOW_LANE, o_gl, GLA_RANK)
    return perm, d_in


def _cparams(*sem):
    return pltpu.CompilerParams(dimension_semantics=sem, vmem_limit_bytes=VMEM_LIMIT_BYTES)


def _norm_matmul_kernel(x_ref, g_ref, w_ref, o_ref, h_ref):
    @pl.when(pl.program_id(1) == 0)
    def _():
        x = x_ref[...].astype(F32)
        ms = jnp.mean(x * x, axis=-1, keepdims=True)
        h_ref[...] = (x * lax.rsqrt(ms + NORM_EPS) * g_ref[...]).astype(h_ref.dtype)

    o_ref[...] = jnp.dot(h_ref[...], w_ref[...], preferred_element_type=F32).astype(o_ref.dtype)


def _norm_matmul(x, g, w, *, tm, tn, out_dtype, name):
    m, k = x.shape
    n = w.shape[1]
    return pl.pallas_call(
        _norm_matmul_kernel,
        grid=(m // tm, n // tn),
        in_specs=[
            pl.BlockSpec((tm, k), lambda i, j: (i, 0)),
            pl.BlockSpec((1, k), lambda i, j: (0, 0)),
            pl.BlockSpec((k, tn), lambda i, j: (0, j)),
        ],
        out_specs=pl.BlockSpec((tm, tn), lambda i, j: (i, j)),
        out_shape=jax.ShapeDtypeStruct((m, n), out_dtype),
        scratch_shapes=[pltpu.VMEM((tm, k), BF16)],
        compiler_params=_cparams("parallel", "arbitrary"),
        name=name,
    )(x, g.reshape(1, k), w)


def _rope_tables(pos, freq):
    ang = pos * freq
    lane = lax.broadcasted_iota(jnp.int32, (1, LANES), 1)
    first = lane < ROPE_HALF

    def spread(a):
        a = a + pltpu.roll(a, ROPE_HALF, axis=1)
        return a + pltpu.roll(a, NSA_HEAD_DIM, axis=1)

    c = spread(jnp.where(first, jnp.cos(ang), 0.0)) + jnp.where((lane & (NSA_HEAD_DIM - 1)) >= ROPE_ROT, 1.0, 0.0)
    s = spread(jnp.where(first, jnp.sin(ang), 0.0))
    return c, s


def _rope_lanes(x, c, s):
    n = x.shape[-1]
    if n > LANES:
        c = jnp.concatenate([c] * (n // LANES), axis=1)
        s = jnp.concatenate([s] * (n // LANES), axis=1)
    lane = lax.broadcasted_iota(jnp.int32, (1, n), 1) & (NSA_HEAD_DIM - 1)
    up = pltpu.roll(x, n - ROPE_HALF, axis=1)
    dn = pltpu.roll(x, ROPE_HALF, axis=1)
    y = jnp.where(lane < ROPE_HALF, -up, jnp.where(lane < ROPE_ROT, dn, 0.0))
    return x * c + y * s


def _head_rms(x, bd, g):
    x2 = x * x
    hi = x2.astype(BF16)
    lo = (x2 - hi.astype(F32)).astype(BF16)
    ms = (jnp.dot(hi, bd, preferred_element_type=F32) + jnp.dot(lo, bd, preferred_element_type=F32))
    return x * lax.rsqrt(ms * (1.0 / NSA_HEAD_DIM) + NORM_EPS) * g


def _nsa_prep_kernel(q_ref, ks_ref, vs_ref, kw_ref, vw_ref, pos_ref, qg_ref, ksg_ref, kwg_ref,
                     bdq_ref, bdk_ref, f_ref,
                     qo_ref, kso_ref, vso_ref, kwo_ref, vwo_ref):
    hd = NSA_HEAD_DIM
    tp = q_ref.shape[1]
    pos = pos_ref[0].astype(F32)
    c, s = _rope_tables(pos, f_ref[...])
    q = _head_rms(q_ref[0].astype(F32), bdq_ref[...], qg_ref[...])
    q = _rope_lanes(q, c, s) * (hd ** -0.5)
    for h in range(NSA_HEADS):
        qo_ref[0, h] = q[:, h * hd:(h + 1) * hd].astype(qo_ref.dtype)
    ks = _rope_lanes(_head_rms(ks_ref[0].astype(F32), bdk_ref[...], ksg_ref[...]), c, s)
    kw = _rope_lanes(_head_rms(kw_ref[0].astype(F32), bdk_ref[...], kwg_ref[...]), c, s)
    vs = vs_ref[0]
    vw = vw_ref[0]
    tok = pl.program_id(1) * tp + lax.broadcasted_iota(jnp.int32, (tp, hd), 0)
    lane = lax.broadcasted_iota(jnp.int32, (tp, hd), 1)
    blk_onehot = jnp.where((tok >> 6) == lane, 1.0, 0.0).astype(kso_ref.dtype)
    ones_col = jnp.where(lane == 0, 1.0, 0.0).astype(vso_ref.dtype)
    for g in range(NSA_GROUPS):
        sl = slice(g * hd, (g + 1) * hd)
        kso_ref[0, g, :, 0:hd] = ks[:, sl].astype(kso_ref.dtype)
        kso_ref[0, g, :, hd:2 * hd] = blk_onehot
        kwo_ref[0, g] = kw[:, sl].astype(kwo_ref.dtype)
        vso_ref[0, g, :, 0:hd] = vs[:, sl].astype(vso_ref.dtype)
        vso_ref[0, g, :, hd:2 * hd] = ones_col
        vwo_ref[0, g, :, 0:hd] = vw[:, sl].astype(vwo_ref.dtype)
        vwo_ref[0, g, :, hd:2 * hd] = ones_col


def _block_diag_ones(n, width):
    i = np.arange(n)
    return jnp.asarray((i[:, None] // width == i[None, :] // width).astype(np.float32)).astype(BF16)


def _rope_freq_lanes(n):
    inv = ROPE_THETA ** (-(np.arange(ROPE_HALF, dtype=np.float64)) / ROPE_HALF)
    f = np.zeros((n,), np.float64)
    f[:ROPE_HALF] = inv
    return jnp.asarray(f.astype(np.float32)).reshape(1, n)


def _nsa_prep(z3, pos3, q_norm, ks_norm, kw_norm, *, tp):
    b, s, _ = z3.shape
    hd, g = NSA_HEAD_DIM, NSA_GROUPS
    assert SEL_BLOCK == 64 and s // SEL_BLOCK <= hd
    kvb = COL_KV // LANES
    qg = jnp.tile(q_norm, NSA_HEADS).reshape(1, NSA_HEADS * hd)
    ksg = jnp.tile(ks_norm, g).reshape(1, g * hd)
    kwg = jnp.tile(kw_norm, g).reshape(1, g * hd)
    const = lambda shape: pl.BlockSpec(shape, lambda bi, i: (0,) * len(shape))
    plain_out = jax.ShapeDtypeStruct((b, g, s, hd), BF16)
    plain_spec = pl.BlockSpec((1, g, tp, hd), lambda bi, i: (bi, 0, i, 0))
    aug_out = jax.ShapeDtypeStruct((b, g, s, 2 * hd), BF16)
    aug_spec = pl.BlockSpec((1, g, tp, 2 * hd), lambda bi, i: (bi, 0, i, 0))
    return pl.pallas_call(
        _nsa_prep_kernel,
        grid=(b, s // tp),
        in_specs=[
            pl.BlockSpec((1, tp, 512), lambda bi, i: (bi, i, COL_NQ // 512)),
            pl.BlockSpec((1, tp, LANES), lambda bi, i: (bi, i, kvb + 2)),
            pl.BlockSpec((1, tp, LANES), lambda bi, i: (bi, i, kvb + 3)),
            pl.BlockSpec((1, tp, LANES), lambda bi, i: (bi, i, kvb + 4)),
            pl.BlockSpec((1, tp, LANES), lambda bi, i: (bi, i, kvb + 5)),
            pl.BlockSpec((1, tp, 1), lambda bi, i: (bi, i, 0)),
            const((1, 512)), const((1, LANES)), const((1, LANES)),
            const((512, 512)), const((LANES, LANES)),
            const((1, LANES)),
        ],
        out_specs=[
            pl.BlockSpec((1, NSA_HEADS, tp, hd), lambda bi, i: (bi, 0, i, 0)),
            aug_spec, aug_spec, plain_spec, aug_spec,
        ],
        out_shape=[jax.ShapeDtypeStruct((b, NSA_HEADS, s, hd), BF16), aug_out, aug_out, plain_out, aug_out],
        compiler_params=_cparams("parallel", "parallel"),
        name="nsa_prep",
    )(z3, z3, z3, z3, z3, pos3, qg, ksg, kwg,
      _block_diag_ones(512, hd), _block_diag_ones(LANES, hd), _rope_freq_lanes(LANES))


def _gelu_tanh(x):
    return 0.5 * x * (1.0 + jnp.tanh(np.sqrt(2.0 / np.pi) * (x + 0.044715 * x * x * x)))


def _nsa_cmp_kernel(x_ref, w1_ref, w2_ref, pe_ref, pos_ref, g_ref, rot_ref, f_ref, o_ref):
    kind = pl.program_id(0)
    nc = x_ref.shape[3]
    x = x_ref[0, 0, 0].astype(BF16)
    w1 = w1_ref[0]
    ab = jnp.dot(x, w1, preferred_element_type=F32)
    r = jnp.dot(pe_ref[0].astype(BF16), w1, preferred_element_type=F32)
    a = ab[:, :CMP_HIDDEN] + r[0:1, :CMP_HIDDEN]
    bm = ab[:, CMP_HIDDEN:] + r[1:2, CMP_HIDDEN:]
    hid = _gelu_tanh(a + pltpu.roll(bm, nc - 1, axis=0))
    comp = jnp.dot(hid.astype(BF16), w2_ref[0], preferred_element_type=F32)

    hd = NSA_HEAD_DIM
    lane = lax.broadcasted_iota(jnp.int32, (nc, hd), 1)

    @pl.when(kind == 0)
    def _():
        ms = jnp.mean(comp * comp, axis=-1, keepdims=True)
        kn = comp * lax.rsqrt(ms + NORM_EPS) * g_ref[...]
        ang = pos_ref[0].astype(F32) * f_ref[...]
        y = jnp.dot(kn, rot_ref[...], preferred_element_type=F32, precision=HIGHEST)
        o_ref[0, 0, 0, :, 0:hd] = (kn * jnp.cos(ang) + y * jnp.sin(ang)).astype(o_ref.dtype)
        o_ref[0, 0, 0, :, hd:2 * hd] = jnp.zeros((nc, hd), o_ref.dtype)

    @pl.when(kind != 0)
    def _():
        o_ref[0, 0, 0, :, 0:hd] = comp.astype(o_ref.dtype)
        o_ref[0, 0, 0, :, hd:2 * hd] = jnp.where(lane == 0, 1.0, 0.0).astype(o_ref.dtype)


def _rope_freq_head():
    p = np.arange(NSA_HEAD_DIM)
    inv = ROPE_THETA ** (-(np.arange(ROPE_HALF, dtype=np.float64)) / ROPE_HALF)
    f = np.where(p < ROPE_ROT, inv[p % ROPE_HALF], 0.0)
    return jnp.asarray(f.astype(np.float32)).reshape(1, NSA_HEAD_DIM)


def _rope_rot_matrix():
    m = np.zeros((NSA_HEAD_DIM, NSA_HEAD_DIM), np.float32)
    for l in range(ROPE_HALF):
        m[l + ROPE_HALF, l] = -1.0
        m[l, l + ROPE_HALF] = 1.0
    return jnp.asarray(m)


def _nsa_compress(xc, w1cat, w2, pe2, pos_cmp, k_norm0):
    _, b, g, nc, _ = xc.shape
    hd = NSA_HEAD_DIM
    const = lambda shape: pl.BlockSpec(shape, lambda k, bi, gi: (0,) * len(shape))
    return pl.pallas_call(
        _nsa_cmp_kernel,
        grid=(2, b, g),
        in_specs=[
            pl.BlockSpec((1, 1, 1, nc, CMP_STRIDE * hd), lambda k, bi, gi: (k, bi, gi, 0, 0)),
            pl.BlockSpec((1, CMP_STRIDE * hd, 2 * CMP_HIDDEN), lambda k, bi, gi: (k, 0, 0)),
            pl.BlockSpec((1, CMP_HIDDEN, hd), lambda k, bi, gi: (k, 0, 0)),
            pl.BlockSpec((1, 8, CMP_STRIDE * hd), lambda k, bi, gi: (k, 0, 0)),
            pl.BlockSpec((1, nc, 1), lambda k, bi, gi: (bi, 0, 0)),
            const((1, hd)), const((hd, hd)), const((1, hd)),
        ],
        out_specs=pl.BlockSpec((1, 1, 1, nc, 2 * hd), lambda k, bi, gi: (k, bi, gi, 0, 0)),
        out_shape=jax.ShapeDtypeStruct((2, b, g, nc, 2 * hd), BF16),
        compiler_params=_cparams("parallel", "parallel", "parallel"),
        name="nsa_compress",
    )(xc, w1cat, w2, pe2, pos_cmp, k_norm0.reshape(1, hd), _rope_rot_matrix(), _rope_freq_head())


def _softmax_rows(s, mask):
    s = jnp.where(mask, s, NEG_BIG)
    m = jnp.max(s, axis=-1, keepdims=True)
    p = jnp.where(mask, jnp.exp(s - m), 0.0)
    d = jnp.sum(p, axis=-1, keepdims=True)
    return p / jnp.where(d > 0, d, 1.0)


_NT = (((1,), (1,)), ((), ()))


def _select_blocks(p_sum, ov_ref, t0, tq, n_sel):
    n_blk = ov_ref.shape[0]
    imp_t = lax.dot_general(ov_ref[...], p_sum, _NT, preferred_element_type=F32, precision=HIGHEST)
    j_col = lax.broadcasted_iota(jnp.int32, (n_blk, 1), 0)
    t_lane = t0 + lax.broadcasted_iota(jnp.int32, (1, tq), 1)
    causal = j_col * SEL_BLOCK <= t_lane
    cur = t_lane >> 6
    forced = causal & ((j_col == 0) | (j_col == cur) | (j_col == cur - 1))
    score = jnp.where(forced, FORCE_SCORE, jnp.where(causal, imp_t, -FORCE_SCORE))
    ng = n_blk // 8
    groups = [score[8 * v:8 * v + 8] for v in range(ng)]
    ranks = [jnp.zeros((8, tq), F32) for _ in range(ng)]
    sub = lax.broadcasted_iota(jnp.int32, (8, tq), 0)
    for jp in range(n_blk):
        row = jnp.broadcast_to(score[jp:jp + 1, :], (8, tq))
        vj = jp // 8
        for v in range(ng):
            if v < vj:
                ahead = row > groups[v]
            elif v > vj:
                ahead = row >= groups[v]
            else:
                ahead = (row > groups[v]) | ((row == groups[v]) & (sub > jp % 8))
            ranks[v] = ranks[v] + jnp.where(ahead, 1.0, 0.0)
    sel_t = jnp.where(jnp.concatenate(ranks, axis=0) < n_sel, 1.0, 0.0).astype(BF16)
    eye = (lax.broadcasted_iota(jnp.int32, (tq, tq), 0) == lax.broadcasted_iota(jnp.int32, (tq, tq), 1)).astype(BF16)
    return lax.dot_general(eye, sel_t, _NT, preferred_element_type=F32)


def _sum_heads(p, tq):
    out = p[0:tq]
    for r in range(1, NSA_REP):
        out = out + p[r * tq:(r + 1) * tq]
    return out


def _gated_store(gl_ref, o_ref, o_c, o_s, o_w, tq):
    rep, hd = NSA_REP, NSA_HEAD_DIM
    gates = jax.nn.sigmoid(gl_ref[0].astype(F32))
    for r in range(rep):
        rs = slice(r * tq, (r + 1) * tq)
        out = (gates[:, r:r + 1] * o_c[rs] + gates[:, rep + r:rep + r + 1] * o_s[rs]
               + gates[:, 2 * rep + r:2 * rep + r + 1] * o_w[rs])
        o_ref[0, :, r * hd:(r + 1) * hd] = out.astype(o_ref.dtype)


def _nsa_fast(sh_ref, q_ref, kc_ref, vc_ref, ks_ref, vs_ref, kw_ref, vw_ref, gl_ref, ov_ref, o_ref,
              lhs_ref, acc_ref, pre_ref, *, tq, tk, n_sel):
    rep, hd = NSA_REP, NSA_HEAD_DIM
    rows = rep * tq
    t0 = pl.program_id(2) * tq
    c_c, c_s, c_w = sh_ref[0], sh_ref[1], sh_ref[2]
    q4 = q_ref[0].reshape(rows, hd)
    t_q = t0 + lax.broadcasted_iota(jnp.int32, (tq, 1), 0)


    span = WINDOW + tq
    w0 = pl.multiple_of(jnp.maximum(t0 - WINDOW, 0), tq)
    kp = w0 + lax.broadcasted_iota(jnp.int32, (1, span), 1)
    bias_w = jnp.where((kp <= t_q) & (kp > t_q - WINDOW), c_w, NEG_BIG)
    s_w = lax.dot_general(q4, kw_ref[0, 0, pl.ds(w0, span), :], _NT, preferred_element_type=F32)
    p_w = jnp.exp(s_w.reshape(rep, tq, span) + bias_w[None]).reshape(rows, span)
    pre_ref[0] = jnp.dot(p_w.astype(BF16), vw_ref[0, 0, pl.ds(w0, span), :], preferred_element_type=F32)

    d0 = pl.multiple_of(t0, tq)
    kcol = t0 + lax.broadcasted_iota(jnp.int32, (1, tq), 1)
    bias_d = jnp.where(kcol <= t_q, c_s, NEG_BIG)
    s_d = lax.dot_general(q4, ks_ref[0, 0, pl.ds(d0, tq), :][:, 0:hd], _NT, preferred_element_type=F32)
    p_d = jnp.exp(s_d.reshape(rep, tq, tq) + bias_d[None]).reshape(rows, tq)
    acc_ref[...] = jnp.dot(p_d.astype(BF16), vs_ref[0, 0, pl.ds(d0, tq), :], preferred_element_type=F32)

    kc = kc_ref[0, 0, 0][:, 0:hd]
    ncp = kc.shape[0]
    n_idx = lax.broadcasted_iota(jnp.int32, (1, ncp), 1)
    valid_c = (n_idx * CMP_STRIDE + (CMP_BLOCK - 1) <= t_q) & (n_idx < ncp - 1)
    bias_c = jnp.where(valid_c, c_c, NEG_BIG)
    s_c = lax.dot_general(q4, kc, _NT, preferred_element_type=F32).reshape(rep, tq, ncp)
    p_c = jnp.exp(s_c + bias_c[None]).reshape(rows, ncp)
    ol_c = jnp.dot(p_c.astype(BF16), vc_ref[0, 0, 0], preferred_element_type=F32)
    inv_c = 1.0 / jnp.where(ol_c[:, hd:hd + 1] > 0, ol_c[:, hd:hd + 1], 1.0)
    pre_ref[1] = ol_c * inv_c
    sel = _select_blocks(_sum_heads(p_c * inv_c, tq), ov_ref, t0, tq, n_sel)

    n_blk = sel.shape[1]
    j_lane = lax.broadcasted_iota(jnp.int32, (1, n_blk), 1)
    shift = jnp.where((sel > 0.5) & (j_lane * SEL_BLOCK < t0), c_s, NEG_BIG).astype(BF16)
    if n_blk < hd:
        shift = jnp.concatenate([shift, jnp.zeros((tq, hd - n_blk), BF16)], axis=1)
    lhs_ref[:, 0:hd] = q4
    for r in range(rep):
        lhs_ref[r * tq:(r + 1) * tq, hd:2 * hd] = shift

    def kv_step(c, carry):
        k0 = pl.multiple_of(c * tk, tk)
        s = lax.dot_general(lhs_ref[...], ks_ref[0, 0, pl.ds(k0, tk), :], _NT, preferred_element_type=F32)
        acc_ref[...] += jnp.dot(jnp.exp(s).astype(BF16), vs_ref[0, 0, pl.ds(k0, tk), :],
                                preferred_element_type=F32)
        return carry

    lax.fori_loop(0, (t0 + tk - 1) // tk, kv_step, 0)

    ol_s = acc_ref[...]
    ol_w = pre_ref[0]
    _gated_store(gl_ref, o_ref, pre_ref[1][:, 0:hd], ol_s[:, 0:hd] / ol_s[:, hd:hd + 1],
                 ol_w[:, 0:hd] / ol_w[:, hd:hd + 1], tq)


def _nsa_slow(q_ref, kc_ref, vc_ref, ks_ref, vs_ref, kw_ref, vw_ref, gl_ref, ov_ref, o_ref,
              acc_ref, m_ref, l_ref, *, tq, tk, n_sel):
    rep, hd = NSA_REP, NSA_HEAD_DIM
    rows = rep * tq
    t0 = pl.program_id(2) * tq
    q4 = q_ref[0].reshape(rows, hd)
    t_row = t0 + (lax.broadcasted_iota(jnp.int32, (rows, 1), 0) & (tq - 1))

    kc = kc_ref[0, 0, 0][:, 0:hd]
    ncp = kc.shape[0]
    s_c = lax.dot_general(q4, kc, _NT, preferred_element_type=F32)
    n_idx = lax.broadcasted_iota(jnp.int32, (1, ncp), 1)
    p_c = _softmax_rows(s_c, (n_idx * CMP_STRIDE + (CMP_BLOCK - 1) <= t_row) & (n_idx < ncp - 1))
    o_c = jnp.dot(p_c.astype(BF16), vc_ref[0, 0, 0], preferred_element_type=F32)[:, 0:hd]
    sel = _select_blocks(_sum_heads(p_c, tq), ov_ref, t0, tq, n_sel).astype(BF16)
    n_blk = sel.shape[1]

    m_ref[...] = jnp.full(m_ref.shape, NEG_BIG, F32)
    l_ref[...] = jnp.zeros(l_ref.shape, F32)
    acc_ref[...] = jnp.zeros(acc_ref.shape, F32)
    t_q = t0 + lax.broadcasted_iota(jnp.int32, (tq, 1), 0)

    def kv_step(c, carry):
        k0 = pl.multiple_of(c * tk, tk)
        kt = ks_ref[0, 0, pl.ds(k0, tk), :][:, 0:hd]
        s = lax.dot_general(q4, kt, _NT, preferred_element_type=F32).reshape(rep, tq, tk)
        kk = lax.broadcasted_iota(jnp.int32, (1, tk), 1)
        blk = (k0 >> 6) + (kk >> 6)
        expand = jnp.where(lax.broadcasted_iota(jnp.int32, (n_blk, 1), 0) == blk, 1.0, 0.0).astype(BF16)
        chosen = jnp.dot(sel, expand, preferred_element_type=F32)
        mask = ((chosen > 0.5) & (k0 + kk <= t_q))[None]
        s = jnp.where(mask, s, NEG_BIG)
        m_old = m_ref[...]
        m_new = jnp.maximum(m_old, jnp.max(s, axis=-1, keepdims=True))
        p = jnp.where(mask, jnp.exp(s - m_new), 0.0)
        alpha = jnp.exp(m_old - m_new)
        l_ref[...] = alpha * l_ref[...] + jnp.sum(p, axis=-1, keepdims=True)
        pv = jnp.dot(p.reshape(rows, tk).astype(BF16), vs_ref[0, 0, pl.ds(k0, tk), :], preferred_element_type=F32)
        acc_ref[...] = alpha.reshape(rows, 1) * acc_ref[...] + pv
        m_ref[...] = m_new
        return carry

    lax.fori_loop(0, (t0 + tq + tk - 1) // tk, kv_step, 0)
    o_s = acc_ref[:, 0:hd] / l_ref[...].reshape(rows, 1)

    span = WINDOW + tq
    w0 = pl.multiple_of(jnp.maximum(t0 - WINDOW, 0), tq)
    s_w = lax.dot_general(q4, kw_ref[0, 0, pl.ds(w0, span), :], _NT, preferred_element_type=F32)
    kp = w0 + lax.broadcasted_iota(jnp.int32, (1, span), 1)
    p_w = _softmax_rows(s_w, (kp <= t_row) & (kp > t_row - WINDOW))
    o_w = jnp.dot(p_w.astype(BF16), vw_ref[0, 0, pl.ds(w0, span), :], preferred_element_type=F32)[:, 0:hd]

    _gated_store(gl_ref, o_ref, o_c, o_s, o_w, tq)


def _nsa_attn_kernel(sh_ref, q_ref, kc_ref, vc_ref, ks_ref, vs_ref, kw_ref, vw_ref, gl_ref, ov_ref, o_ref,
                     lhs_ref, acc_ref, pre_ref, m_ref, l_ref, *, tq, tk, n_sel):
    data = (q_ref, kc_ref, vc_ref, ks_ref, vs_ref, kw_ref, vw_ref, gl_ref, ov_ref, o_ref)

    @pl.when(sh_ref[3] > 0.5)
    def _():
        _nsa_fast(sh_ref, *data, lhs_ref, acc_ref, pre_ref, tq=tq, tk=tk, n_sel=n_sel)

    @pl.when(sh_ref[3] <= 0.5)
    def _():
        _nsa_slow(*data, acc_ref, m_ref, l_ref, tq=tq, tk=tk, n_sel=n_sel)


def _overlap_t(n_blk, ncp):
    c_start = np.arange(ncp) * CMP_STRIDE
    b_start = np.arange(n_blk) * SEL_BLOCK
    ov = ((c_start[None, :] < b_start[:, None] + SEL_BLOCK) & (b_start[:, None] < c_start[None, :] + CMP_BLOCK))
    ov[:, ncp - 1] = False
    return jnp.asarray(ov.astype(np.float32))


def _nsa_shifts(q_norm, k_norm):
    bound = (NSA_HEAD_DIM ** 0.5) * jnp.max(jnp.abs(q_norm)) * jnp.max(jnp.abs(k_norm), axis=-1)
    bound = bound.astype(BF16).astype(F32)
    fast = jnp.all(bound <= MAX_CONST_SHIFT).astype(F32)
    return jnp.concatenate([-bound, fast[None]])


def _nsa_attention(shifts, q, cmp_kv, ks, vs, kw, vw, z3, *, tq, tk):
    b, h, s, hd = q.shape
    g, rep = NSA_GROUPS, NSA_REP
    ncp = cmp_kv.shape[3]
    n_blk = s // SEL_BLOCK
    rows = rep * tq
    assert n_blk % 8 == 0 and n_blk <= hd and WINDOW % tq == 0
    full = lambda w: pl.BlockSpec((1, 1, s, w), lambda bi, gi, i: (bi, gi, 0, 0))
    kernel = functools.partial(_nsa_attn_kernel, tq=tq, tk=tk, n_sel=min(N_SEL, n_blk))
    return pl.pallas_call(
        kernel,
        grid=(b, g, s // tq),
        in_specs=[
            pl.BlockSpec(memory_space=pltpu.SMEM),
            pl.BlockSpec((1, rep, tq, hd), lambda bi, gi, i: (bi, gi, i, 0)),
            pl.BlockSpec((1, 1, 1, ncp, 2 * hd), lambda bi, gi, i: (0, bi, gi, 0, 0)),
            pl.BlockSpec((1, 1, 1, ncp, 2 * hd), lambda bi, gi, i: (1, bi, gi, 0, 0)),
            full(2 * hd), full(2 * hd), full(hd), full(2 * hd),
            pl.BlockSpec((1, tq, LANES), lambda bi, gi, i: (bi, i, COL_SMALL // LANES + gi)),
            pl.BlockSpec((n_blk, ncp), lambda bi, gi, i: (0, 0)),
        ],
        out_specs=pl.BlockSpec((1, tq, rep * hd), lambda bi, gi, i: (bi, i, gi)),
        out_shape=jax.ShapeDtypeStruct((b, s, h * hd), BF16),
        scratch_shapes=[pltpu.VMEM((rows, 2 * hd), BF16), pltpu.VMEM((rows, 2 * hd), F32),
                        pltpu.VMEM((2, rows, 2 * hd), F32),
                        pltpu.VMEM((rep, tq, 1), F32), pltpu.VMEM((rep, tq, 1), F32)],
        compiler_params=_cparams("parallel", "parallel", "arbitrary"),
        name="nsa_attention",
    )(shifts, q, cmp_kv, cmp_kv, ks, vs, kw, vw, z3, _overlap_t(n_blk, ncp))


def _gla_kernel(q_ref, k_ref, v_ref, r_ref, sm_ref, wg_ref, bg_ref, ng_ref, o_ref, st_ref):
    c, sub = GLA_CHUNK, GLA_SUB
    dk, dv, nh = GLA_HEAD_DK, GLA_HEAD_DV, GLA_HEADS
    nt = (((1,), (1,)), ((), ()))

    @pl.when(pl.program_id(1) == 0)
    def _():
        st_ref[...] = jnp.zeros(st_ref.shape, F32)

    x = jnp.dot(sm_ref[0].astype(F32), wg_ref[...], preferred_element_type=F32, precision=HIGHEST) + bg_ref[...]
    log_a = (jnp.minimum(x, 0.0) - jnp.log1p(jnp.exp(-jnp.abs(x)))) / GLA_TAU
    ri = lax.broadcasted_iota(jnp.int32, (c, c), 0)
    ci = lax.broadcasted_iota(jnp.int32, (c, c), 1)
    bcum = jnp.dot((ci <= ri).astype(F32), log_a, preferred_element_type=F32, precision=HIGHEST)
    q = q_ref[0].astype(F32) * (dk ** -0.5)
    k = k_ref[0].astype(F32)
    v = v_ref[0].astype(BF16)
    b_last = bcum[c - 1:c, :]
    q_in = (q * jnp.exp(bcum)).astype(BF16)
    k_out = (k * jnp.exp(b_last - bcum)).astype(BF16)
    decay = jnp.exp(b_last)

    o_heads = [[] for _ in range(nh)]
    for i in range(c // sub):
        hi = (i + 1) * sub
        ref = bcum[i * sub - 1:i * sub, :] if i > 0 else jnp.zeros((1, nh * dk), F32)
        q_i = (q[i * sub:hi] * jnp.exp(bcum[i * sub:hi] - ref)).astype(BF16)
        k_i = (k[0:hi] * jnp.exp(ref - bcum[0:hi])).astype(BF16)
        tri = (lax.broadcasted_iota(jnp.int32, (sub, hi), 1)
               <= lax.broadcasted_iota(jnp.int32, (sub, hi), 0) + i * sub)
        for h in range(nh):
            ks_ = slice(h * dk, (h + 1) * dk)
            a = lax.dot_general(q_i[:, ks_], k_i[:, ks_], nt, preferred_element_type=F32)
            a = jnp.where(tri, a, 0.0).astype(BF16)
            o_heads[h].append(jnp.dot(a, v[0:hi, h * dv:(h + 1) * dv], preferred_element_type=F32))

    r_gate = r_ref[0].astype(F32)
    for h in range(nh):
        ks_ = slice(h * dk, (h + 1) * dk)
        vs_ = slice(h * dv, (h + 1) * dv)
        st = st_ref[h]
        o_h = jnp.concatenate(o_heads[h], axis=0)
        o_h = o_h + lax.dot_general(q_in[:, ks_], st.astype(BF16), nt, preferred_element_type=F32)
        v_t = v_ref[0, :, vs_].astype(F32).T.astype(BF16)
        st_ref[h] = st * decay[:, ks_] + jnp.dot(v_t, k_out[:, ks_], preferred_element_type=F32)
        ms = jnp.mean(o_h * o_h, axis=-1, keepdims=True)
        o_n = o_h * lax.rsqrt(ms + NORM_EPS) * ng_ref[...]
        rg = r_gate[:, vs_]
        o_ref[0, :, vs_] = (o_n * (rg * jax.nn.sigmoid(rg))).astype(o_ref.dtype)


def _gla(z3, w_gate, b_gate, norm_g):
    b, s, _ = z3.shape
    c = GLA_CHUNK
    nk = GLA_HEADS * GLA_HEAD_DK
    nv = GLA_HEADS * GLA_HEAD_DV
    wg = jnp.zeros((LANES, nk), F32).at[SMALL_GLOW_LANE:SMALL_GLOW_LANE + GLA_RANK].set(w_gate.astype(F32))
    const = lambda shape: pl.BlockSpec(shape, lambda bi, i: (0,) * len(shape))
    return pl.pallas_call(
        _gla_kernel,
        grid=(b, s // c),
        in_specs=[
            pl.BlockSpec((1, c, nk), lambda bi, i: (bi, i, COL_GQ // nk)),
            pl.BlockSpec((1, c, nk), lambda bi, i: (bi, i, COL_GK // nk)),
            pl.BlockSpec((1, c, nv), lambda bi, i: (bi, i, COL_GV // nv)),
            pl.BlockSpec((1, c, nv), lambda bi, i: (bi, i, COL_GR // nv)),
            pl.BlockSpec((1, c, LANES), lambda bi, i: (bi, i, COL_SMALL // LANES)),
            const((LANES, nk)), const((1, nk)), const((1, GLA_HEAD_DV)),
        ],
        out_specs=pl.BlockSpec((1, c, nv), lambda bi, i: (bi, i, 0)),
        out_shape=jax.ShapeDtypeStruct((b, s, nv), BF16),
        scratch_shapes=[pltpu.VMEM((GLA_HEADS, GLA_HEAD_DV, GLA_HEAD_DK), F32)],
        compiler_params=_cparams("parallel", "arbitrary"),
        name="gla",
    )(z3, z3, z3, z3, z3, wg, b_gate.reshape(1, nk).astype(F32), norm_g.reshape(1, GLA_HEAD_DV).astype(F32))


def _mem_attn_kernel(q_ref, k_ref, v_ref, qg_ref, kg_ref, o_ref):
    dh = MEM_HEAD_DIM
    nt = (((1,), (1,)), ((), ()))
    for h in range(MEM_HEADS):
        sl = slice(h * dh, (h + 1) * dh)
        q = q_ref[0, :, sl].astype(F32)
        q = q * lax.rsqrt(jnp.mean(q * q, axis=-1, keepdims=True) + NORM_EPS) * qg_ref[...] * (dh ** -0.5)
        k = k_ref[0, :, sl].astype(F32)
        k = k * lax.rsqrt(jnp.mean(k * k, axis=-1, keepdims=True) + NORM_EPS) * kg_ref[...]
        s = lax.dot_general(q.astype(BF16), k.astype(BF16), nt, preferred_element_type=F32)
        m = jnp.max(s, axis=-1, keepdims=True)
        p = jnp.exp(s - m)
        p = p / jnp.sum(p, axis=-1, keepdims=True)
        o = jnp.dot(p.astype(BF16), v_ref[0, :, sl].astype(BF16), preferred_element_type=F32)
        o_ref[0, :, sl] = o.astype(o_ref.dtype)


def _mem_attention(z3, kv, q_norm, k_norm, *, tq):
    b, s, _ = z3.shape
    m = kv.shape[1]
    w = MEM_HEADS * MEM_HEAD_DIM
    const = lambda shape: pl.BlockSpec(shape, lambda bi, i: (0,) * len(shape))
    return pl.pallas_call(
        _mem_attn_kernel,
        grid=(b, s // tq),
        in_specs=[
            pl.BlockSpec((1, tq, w), lambda bi, i: (bi, i, COL_MQ // w)),
            pl.BlockSpec((1, m, w), lambda bi, i: (bi, 0, 0)),
            pl.BlockSpec((1, m, w), lambda bi, i: (bi, 0, 1)),
            const((1, MEM_HEAD_DIM)), const((1, MEM_HEAD_DIM)),
        ],
        out_specs=pl.BlockSpec((1, tq, w), lambda bi, i: (bi, i, 0)),
        out_shape=jax.ShapeDtypeStruct((b, s, w), BF16),
        compiler_params=_cparams("parallel", "parallel"),
        name="mem_attention",
    )(z3, kv, kv, q_norm.reshape(1, MEM_HEAD_DIM).astype(F32), k_norm.reshape(1, MEM_HEAD_DIM).astype(F32))


def _merge_kernel(x_ref, on_ref, og_ref, om_ref, m0_ref, m1_ref, m2_ref, bm_ref, wb_ref, wo_ref, o_ref):
    merged = None
    for br, (ref, mg_ref) in enumerate(((on_ref, m0_ref), (og_ref, m1_ref), (om_ref, m2_ref))):
        y = jnp.dot(ref[...], wb_ref[br], preferred_element_type=F32)
        gate = jax.nn.sigmoid(mg_ref[...].astype(F32) + bm_ref[br:br + 1, :])
        merged = gate * y if merged is None else merged + gate * y
    o_ref[...] = x_ref[...] + jnp.dot(merged.astype(BF16), wo_ref[...], preferred_element_type=F32)


def _merge_out(x2, o_nsa, o_gla, o_mem, z2, b_merge, w_branch, w_out, *, tm):
    t, d = x2.shape
    bw = BRANCH_WIDTH
    row = lambda w: pl.BlockSpec((tm, w), lambda i: (i, 0))
    gate_cols = lambda br: pl.BlockSpec((tm, d), lambda i: (i, COL_MERGE // d + br))
    return pl.pallas_call(
        _merge_kernel,
        grid=(t // tm,),
        in_specs=[
            row(d), row(bw), row(bw), row(bw),
            gate_cols(0), gate_cols(1), gate_cols(2),
            pl.BlockSpec((N_BRANCH, d), lambda i: (0, 0)),
            pl.BlockSpec((N_BRANCH, bw, d), lambda i: (0, 0, 0)),
            pl.BlockSpec((d, d), lambda i: (0, 0)),
        ],
        out_specs=row(d),
        out_shape=jax.ShapeDtypeStruct((t, d), F32),
        compiler_params=_cparams("parallel"),
        name="merge_out",
    )(x2, o_nsa, o_gla, o_mem, z2, z2, z2, b_merge.astype(F32), w_branch, w_out)


def _mlp_kernel(x_ref, g_ref, wu_ref, wd_ref, o_ref, h_ref, acc_ref):
    j = pl.program_id(1)

    @pl.when(j == 0)
    def _():
        x = x_ref[...]
        ms = jnp.mean(x * x, axis=-1, keepdims=True)
        h_ref[...] = (x * lax.rsqrt(ms + NORM_EPS) * g_ref[...]).astype(h_ref.dtype)
        acc_ref[...] = jnp.zeros(acc_ref.shape, F32)

    u = jnp.dot(h_ref[...], wu_ref[...], preferred_element_type=F32)
    u = jnp.square(jnp.maximum(u, 0.0)).astype(BF16)
    acc_ref[...] += jnp.dot(u, wd_ref[...], preferred_element_type=F32)

    @pl.when(j == pl.num_programs(1) - 1)
    def _():
        o_ref[...] = x_ref[...] + acc_ref[...]


def _mlp(x2, g, w_up, w_down, *, tm, th):
    t, d = x2.shape
    hid = w_up.shape[1]
    return pl.pallas_call(
        _mlp_kernel,
        grid=(t // tm, hid // th),
        in_specs=[
            pl.BlockSpec((tm, d), lambda i, j: (i, 0)),
            pl.BlockSpec((1, d), lambda i, j: (0, 0)),
            pl.BlockSpec((d, th), lambda i, j: (0, j)),
            pl.BlockSpec((th, d), lambda i, j: (j, 0)),
        ],
        out_specs=pl.BlockSpec((tm, d), lambda i, j: (i, 0)),
        out_shape=jax.ShapeDtypeStruct((t, d), F32),
        scratch_shapes=[pltpu.VMEM((tm, d), BF16), pltpu.VMEM((tm, d), F32)],
        compiler_params=_cparams("parallel", "arbitrary"),
        name="mlp",
    )(x2, g.reshape(1, d).astype(F32), w_up, w_down)


def _tile(n, pref):
    t = min(n, pref)
    assert n % t == 0, (n, pref)
    return t


def _layer(x, mem2, pos3, pos_cmp, perm, p):
    b, s, d = x.shape
    t = b * s
    hd, g = NSA_HEAD_DIM, NSA_GROUPS
    x2 = x.reshape(t, d)

    w_in = jnp.take(jnp.pad(p["w_in"], ((0, 0), (0, 1))), perm, axis=1).astype(BF16)
    z2 = _norm_matmul(x2, p["ln_mix"].astype(F32), w_in, tm=_tile(t, 512), tn=512, out_dtype=F32, name="in_proj")
    z3 = z2.reshape(b, s, D_IN_PAD)

    q, ks, vs, kw, vw = _nsa_prep(z3, pos3, p["nsa_q_norm"].astype(F32), p["nsa_k_norm"][1].astype(F32),
                                  p["nsa_k_norm"][2].astype(F32), tp=_tile(s, 256))
    nc = s // CMP_STRIDE
    kvc = z3[:, :, COL_KV:COL_KV + 2 * g * hd].reshape(b, s, 2, g, hd)
    xc = kvc.transpose(2, 0, 3, 1, 4).reshape(2, b, g, nc, CMP_STRIDE * hd)
    w1 = p["cmp_w1"]
    half = CMP_STRIDE * hd
    w1cat = jnp.concatenate([w1[:, :half], w1[:, half:]], axis=-1).astype(BF16)
    pe2 = jnp.pad(p["cmp_pe"].reshape(2, 2, half), ((0, 0), (0, 6), (0, 0))).astype(F32)
    cmp_kv = _nsa_compress(xc, w1cat, p["cmp_w2"].astype(BF16), pe2, pos_cmp, p["nsa_k_norm"][0].astype(F32))
    shifts = _nsa_shifts(p["nsa_q_norm"].astype(F32), p["nsa_k_norm"].astype(F32))
    o_nsa = _nsa_attention(shifts, q, cmp_kv, ks, vs, kw, vw, z3, tq=_tile(s, 256), tk=_tile(s, 512))

    o_gla = _gla(z3, p["gla_w_gate"], p["gla_b_gate"], p["gla_norm"])

    kv = _norm_matmul(mem2, p["mem_norm"].astype(F32), p["mem_w_kv"].astype(BF16),
                      tm=_tile(mem2.shape[0], 512), tn=512, out_dtype=BF16, name="mem_kv")
    kv = kv.reshape(b, mem2.shape[0] // b, 2 * MEM_HEADS * MEM_HEAD_DIM)
    o_mem = _mem_attention(z3, kv, p["mem_q_norm"], p["mem_k_norm"], tq=_tile(s, 512))

    x2 = _merge_out(x2, o_nsa.reshape(t, -1), o_gla.reshape(t, -1), o_mem.reshape(t, -1), z2,
                    p["b_merge"], p["w_branch"].astype(BF16), p["w_out"].astype(BF16), tm=_tile(t, 256))
    x2 = _mlp(x2, p["ln_mlp"], p["w_up"].astype(BF16), p["w_down"].astype(BF16), tm=_tile(t, 512), th=512)
    return x2.reshape(b, s, d)


def kernel(x, mem, positions, ln_mix, w_in, b_merge, nsa_q_norm, nsa_k_norm, cmp_pe, cmp_w1, cmp_w2,
           gla_w_gate, gla_b_gate, gla_norm, mem_norm, mem_w_kv, mem_q_norm, mem_k_norm, w_branch, w_out,
           ln_mlp, w_up, w_down):
    b, s, d = x.shape
    assert d == 1024 and s % WINDOW == 0 and s >= 2 * WINDOW
    depth = w_in.shape[0]
    perm_np, d_in = _in_proj_permutation(d)
    assert w_in.shape[2] == d_in
    perm = jnp.asarray(perm_np)
    pos3 = positions.astype(jnp.int32).reshape(b, s, 1)
    nc = s // CMP_STRIDE
    cmp_end = np.minimum(np.arange(nc) * CMP_STRIDE + CMP_BLOCK - 1, s - 1)
    pos_cmp = pos3[:, cmp_end, :]
    mem2 = mem.reshape(b * mem.shape[1], d)
    names = ("ln_mix", "w_in", "b_merge", "nsa_q_norm", "nsa_k_norm", "cmp_pe", "cmp_w1", "cmp_w2",
             "gla_w_gate", "gla_b_gate", "gla_norm", "mem_norm", "mem_w_kv", "mem_q_norm", "mem_k_norm",
             "w_branch", "w_out", "ln_mlp", "w_up", "w_down")
    stacked = (ln_mix, w_in, b_merge, nsa_q_norm, nsa_k_norm, cmp_pe, cmp_w1, cmp_w2, gla_w_gate, gla_b_gate,
               gla_norm, mem_norm, mem_w_kv, mem_q_norm, mem_k_norm, w_branch, w_out, ln_mlp, w_up, w_down)
    for l in range(depth):
        x = _layer(x, mem2, pos3, pos_cmp, perm, {n: a[l] for n, a in zip(names, stacked)})
    return x
```

```python
import functools

import numpy as np
import jax
import jax.numpy as jnp
from jax import lax
from jax.experimental import pallas as pl
from jax.experimental.pallas import tpu as pltpu

NSA_HEADS = 8
NSA_GROUPS = 2
NSA_REP = NSA_HEADS // NSA_GROUPS
NSA_HEAD_DIM = 64
CMP_BLOCK = 32
CMP_STRIDE = 16
CMP_HIDDEN = 4 * NSA_HEAD_DIM
SEL_BLOCK = 64
N_SEL = 16
WINDOW = 512
FORCE_SCORE = 1e4
GLA_HEADS = 4
GLA_HEAD_DK = 64
GLA_HEAD_DV = 128
GLA_RANK = 16
GLA_TAU = 16.0
GLA_CHUNK = 64
GLA_SUB = 16
MEM_HEADS = 4
MEM_HEAD_DIM = 128
N_BRANCH = 3
BRANCH_WIDTH = 512
ROPE_THETA = 500000.0
ROPE_ROT = NSA_HEAD_DIM // 4
ROPE_HALF = ROPE_ROT // 2
NORM_EPS = 1e-6

LANES = 128
VMEM_LIMIT_BYTES = 48 * 1024 * 1024

F32 = jnp.float32
BF16 = jnp.bfloat16
HIGHEST = lax.Precision.HIGHEST
NEG_BIG = -1e30
MAX_CONST_SHIFT = 40.0

COL_NQ = 0
COL_GV = 512
COL_GR = 1024
COL_MQ = 1536
COL_MERGE = 2048
COL_GQ = 5120
COL_GK = 5376
COL_KV = 5632
COL_SMALL = 6400
D_IN_PAD = 6656
SMALL_GLOW_LANE = 12


def _in_proj_permutation(d_model):
    sizes = (512, 128, 128, 128, 128, 128, 128, 24, 256, 256, 512, 512, 16, 512, 3 * d_model)
    off = np.concatenate([[0], np.cumsum(sizes)])
    (o_nq, o_kc, o_vc, o_ks, o_vs, o_kw, o_vw, o_ng, o_gq, o_gk, o_gv, o_gr, o_gl, o_mq, o_mg) = off[:-1]
    d_in = int(off[-1])
    perm = np.full((D_IN_PAD,), d_in, np.int32)

    def put(new, old, n):
        perm[new:new + n] = np.arange(old, old + n)

    put(COL_NQ, o_nq, 512)
    put(COL_GV, o_gv, 512)
    put(COL_GR, o_gr, 512)
    put(COL_MQ, o_mq, 512)
    put(COL_MERGE, o_mg, 3 * d_model)
    put(COL_GQ, o_gq, 256)
    put(COL_GK, o_gk, 256)
    put(COL_KV, o_kc, 768)
    for g in range(NSA_GROUPS):
        for br in range(3):
            for r in range(NSA_REP):
                perm[COL_SMALL + g * LANES + br * NSA_REP + r] = o_ng + (g * NSA_REP + r) * 3 + br
    put(COL_SMALL + SMALL_GLOW_LANE, o_gl, GLA_RANK)
    return perm, d_in


def _cparams(*sem):
    return pltpu.CompilerParams(dimension_semantics=sem, vmem_limit_bytes=VMEM_LIMIT_BYTES)


def _norm_matmul_kernel(x_ref, g_ref, w_ref, o_ref, h_ref):
    @pl.when(pl.program_id(1) == 0)
    def _():
        x = x_ref[...].astype(F32)
        ms = jnp.mean(x * x, axis=-1, keepdims=True)
        h_ref[...] = (x * lax.rsqrt(ms + NORM_EPS) * g_ref[...]).astype(h_ref.dtype)

    o_ref[...] = jnp.dot(h_ref[...], w_ref[...], preferred_element_type=F32).astype(o_ref.dtype)


def _norm_matmul(x, g, w, *, tm, tn, out_dtype, name):
    m, k = x.shape
    n = w.shape[1]
    return pl.pallas_call(
        _norm_matmul_kernel,
        grid=(m // tm, n // tn),
        in_specs=[
            pl.BlockSpec((tm, k), lambda i, j: (i, 0)),
            pl.BlockSpec((1, k), lambda i, j: (0, 0)),
            pl.BlockSpec((k, tn), lambda i, j: (0, j)),
        ],
        out_specs=pl.BlockSpec((tm, tn), lambda i, j: (i, j)),
        out_shape=jax.ShapeDtypeStruct((m, n), out_dtype),
        scratch_shapes=[pltpu.VMEM((tm, k), BF16)],
        compiler_params=_cparams("parallel", "arbitrary"),
        name=name,
    )(x, g.reshape(1, k), w)


def _rope_tables(pos, freq):
    ang = pos * freq
    lane = lax.broadcasted_iota(jnp.int32, (1, LANES), 1)
    first = lane < ROPE_HALF

    def spread(a):
        a = a + pltpu.roll(a, ROPE_HALF, axis=1)
        return a + pltpu.roll(a, NSA_HEAD_DIM, axis=1)

    c = spread(jnp.where(first, jnp.cos(ang), 0.0)) + jnp.where((lane & (NSA_HEAD_DIM - 1)) >= ROPE_ROT, 1.0, 0.0)
    s = spread(jnp.where(first, jnp.sin(ang), 0.0))
    return c, s


def _rope_lanes(x, c, s):
    n = x.shape[-1]
    if n > LANES:
        c = jnp.concatenate([c] * (n // LANES), axis=1)
        s = jnp.concatenate([s] * (n // LANES), axis=1)
    lane = lax.broadcasted_iota(jnp.int32, (1, n), 1) & (NSA_HEAD_DIM - 1)
    up = pltpu.roll(x, n - ROPE_HALF, axis=1)
    dn = pltpu.roll(x, ROPE_HALF, axis=1)
    y = jnp.where(lane < ROPE_HALF, -up, jnp.where(lane < ROPE_ROT, dn, 0.0))
    return x * c + y * s


def _head_rms(x, bd, g):
    x2 = x * x
    hi = x2.astype(BF16)
    lo = (x2 - hi.astype(F32)).astype(BF16)
    ms = (jnp.dot(hi, bd, preferred_element_type=F32) + jnp.dot(lo, bd, preferred_element_type=F32))
    return x * lax.rsqrt(ms * (1.0 / NSA_HEAD_DIM) + NORM_EPS) * g


def _nsa_prep_kernel(q_ref, ks_ref, vs_ref, kw_ref, vw_ref, pos_ref, qg_ref, ksg_ref, kwg_ref,
                     bdq_ref, bdk_ref, f_ref,
                     qo_ref, kso_ref, vso_ref, kwo_ref, vwo_ref):
    hd = NSA_HEAD_DIM
    tp = q_ref.shape[1]
    pos = pos_ref[0].astype(F32)
    c, s = _rope_tables(pos, f_ref[...])
    q = _head_rms(q_ref[0].astype(F32), bdq_ref[...], qg_ref[...])
    q = _rope_lanes(q, c, s) * (hd ** -0.5)
    for h in range(NSA_HEADS):
        qo_ref[0, h] = q[:, h * hd:(h + 1) * hd].astype(qo_ref.dtype)
    ks = _rope_lanes(_head_rms(ks_ref[0].astype(F32), bdk_ref[...], ksg_ref[...]), c, s)
    kw = _rope_lanes(_head_rms(kw_ref[0].astype(F32), bdk_ref[...], kwg_ref[...]), c, s)
    vs = vs_ref[0]
    vw = vw_ref[0]
    tok = pl.program_id(1) * tp + lax.broadcasted_iota(jnp.int32, (tp, hd), 0)
    lane = lax.broadcasted_iota(jnp.int32, (tp, hd), 1)
    blk_onehot = jnp.where((tok >> 6) == lane, 1.0, 0.0).astype(kso_ref.dtype)
    ones_col = jnp.where(lane == 0, 1.0, 0.0).astype(vso_ref.dtype)
    for g in range(NSA_GROUPS):
        sl = slice(g * hd, (g + 1) * hd)
        kso_ref[0, g, :, 0:hd] = ks[:, sl].astype(kso_ref.dtype)
        kso_ref[0, g, :, hd:2 * hd] = blk_onehot
        kwo_ref[0, g] = kw[:, sl].astype(kwo_ref.dtype)
        vso_ref[0, g, :, 0:hd] = vs[:, sl].astype(vso_ref.dtype)
        vso_ref[0, g, :, hd:2 * hd] = ones_col
        vwo_ref[0, g, :, 0:hd] = vw[:, sl].astype(vwo_ref.dtype)
        vwo_ref[0, g, :, hd:2 * hd] = ones_col


def _block_diag_ones(n, width):
    i = np.arange(n)
    return jnp.asarray((i[:, None] // width == i[None, :] // width).astype(np.float32)).astype(BF16)


def _rope_freq_lanes(n):
    inv = ROPE_THETA ** (-(np.arange(ROPE_HALF, dtype=np.float64)) / ROPE_HALF)
    f = np.zeros((n,), np.float64)
    f[:ROPE_HALF] = inv
    return jnp.asarray(f.astype(np.float32)).reshape(1, n)


def _nsa_prep(z3, pos3, q_norm, ks_norm, kw_norm, *, tp):
    b, s, _ = z3.shape
    hd, g = NSA_HEAD_DIM, NSA_GROUPS
    assert SEL_BLOCK == 64 and s // SEL_BLOCK <= hd
    kvb = COL_KV // LANES
    qg = jnp.tile(q_norm, NSA_HEADS).reshape(1, NSA_HEADS * hd)
    ksg = jnp.tile(ks_norm, g).reshape(1, g * hd)
    kwg = jnp.tile(kw_norm, g).reshape(1, g * hd)
    const = lambda shape: pl.BlockSpec(shape, lambda bi, i: (0,) * len(shape))
    plain_out = jax.ShapeDtypeStruct((b, g, s, hd), BF16)
    plain_spec = pl.BlockSpec((1, g, tp, hd), lambda bi, i: (bi, 0, i, 0))
    aug_out = jax.ShapeDtypeStruct((b, g, s, 2 * hd), BF16)
    aug_spec = pl.BlockSpec((1, g, tp, 2 * hd), lambda bi, i: (bi, 0, i, 0))
    return pl.pallas_call(
        _nsa_prep_kernel,
        grid=(b, s // tp),
        in_specs=[
            pl.BlockSpec((1, tp, 512), lambda bi, i: (bi, i, COL_NQ // 512)),
            pl.BlockSpec((1, tp, LANES), lambda bi, i: (bi, i, kvb + 2)),
            pl.BlockSpec((1, tp, LANES), lambda bi, i: (bi, i, kvb + 3)),
            pl.BlockSpec((1, tp, LANES), lambda bi, i: (bi, i, kvb + 4)),
            pl.BlockSpec((1, tp, LANES), lambda bi, i: (bi, i, kvb + 5)),
            pl.BlockSpec((1, tp, 1), lambda bi, i: (bi, i, 0)),
            const((1, 512)), const((1, LANES)), const((1, LANES)),
            const((512, 512)), const((LANES, LANES)),
            const((1, LANES)),
        ],
        out_specs=[
            pl.BlockSpec((1, NSA_HEADS, tp, hd), lambda bi, i: (bi, 0, i, 0)),
            aug_spec, aug_spec, plain_spec, aug_spec,
        ],
        out_shape=[jax.ShapeDtypeStruct((b, NSA_HEADS, s, hd), BF16), aug_out, aug_out, plain_out, aug_out],
        compiler_params=_cparams("parallel", "parallel"),
        name="nsa_prep",
    )(z3, z3, z3, z3, z3, pos3, qg, ksg, kwg,
      _block_diag_ones(512, hd), _block_diag_ones(LANES, hd), _rope_freq_lanes(LANES))


def _gelu_tanh(x):
    return 0.5 * x * (1.0 + jnp.tanh(np.sqrt(2.0 / np.pi) * (x + 0.044715 * x * x * x)))


def _nsa_cmp_kernel(x_ref, w1_ref, w2_ref, pe_ref, pos_ref, g_ref, rot_ref, f_ref, o_ref):
    kind = pl.program_id(0)
    nc = x_ref.shape[3]
    x = x_ref[0, 0, 0].astype(BF16)
    w1 = w1_ref[0]
    ab = jnp.dot(x, w1, preferred_element_type=F32)
    r = jnp.dot(pe_ref[0].astype(BF16), w1, preferred_element_type=F32)
    a = ab[:, :CMP_HIDDEN] + r[0:1, :CMP_HIDDEN]
    bm = ab[:, CMP_HIDDEN:] + r[1:2, CMP_HIDDEN:]
    hid = _gelu_tanh(a + pltpu.roll(bm, nc - 1, axis=0))
    comp = jnp.dot(hid.astype(BF16), w2_ref[0], preferred_element_type=F32)

    hd = NSA_HEAD_DIM
    lane = lax.broadcasted_iota(jnp.int32, (nc, hd), 1)

    @pl.when(kind == 0)
    def _():
        ms = jnp.mean(comp * comp, axis=-1, keepdims=True)
        kn = comp * lax.rsqrt(ms + NORM_EPS) * g_ref[...]
        ang = pos_ref[0].astype(F32) * f_ref[...]
        y = jnp.dot(kn, rot_ref[...], preferred_element_type=F32, precision=HIGHEST)
        o_ref[0, 0, 0, :, 0:hd] = (kn * jnp.cos(ang) + y * jnp.sin(ang)).astype(o_ref.dtype)
        o_ref[0, 0, 0, :, hd:2 * hd] = jnp.zeros((nc, hd), o_ref.dtype)

    @pl.when(kind != 0)
    def _():
        o_ref[0, 0, 0, :, 0:hd] = comp.astype(o_ref.dtype)
        o_ref[0, 0, 0, :, hd:2 * hd] = jnp.where(lane == 0, 1.0, 0.0).astype(o_ref.dtype)


def _rope_freq_head():
    p = np.arange(NSA_HEAD_DIM)
    inv = ROPE_THETA ** (-(np.arange(ROPE_HALF, dtype=np.float64)) / ROPE_HALF)
    f = np.where(p < ROPE_ROT, inv[p % ROPE_HALF], 0.0)
    return jnp.asarray(f.astype(np.float32)).reshape(1, NSA_HEAD_DIM)


def _rope_rot_matrix():
    m = np.zeros((NSA_HEAD_DIM, NSA_HEAD_DIM), np.float32)
    for l in range(ROPE_HALF):
        m[l + ROPE_HALF, l] = -1.0
        m[l, l + ROPE_HALF] = 1.0
    return jnp.asarray(m)


def _nsa_compress(xc, w1cat, w2, pe2, pos_cmp, k_norm0):
    _, b, g, nc, _ = xc.shape
    hd = NSA_HEAD_DIM
    const = lambda shape: pl.BlockSpec(shape, lambda k, bi, gi: (0,) * len(shape))
    return pl.pallas_call(
        _nsa_cmp_kernel,
        grid=(2, b, g),
        in_specs=[
            pl.BlockSpec((1, 1, 1, nc, CMP_STRIDE * hd), lambda k, bi, gi: (k, bi, gi, 0, 0)),
            pl.BlockSpec((1, CMP_STRIDE * hd, 2 * CMP_HIDDEN), lambda k, bi, gi: (k, 0, 0)),
            pl.BlockSpec((1, CMP_HIDDEN, hd), lambda k, bi, gi: (k, 0, 0)),
            pl.BlockSpec((1, 8, CMP_STRIDE * hd), lambda k, bi, gi: (k, 0, 0)),
            pl.BlockSpec((1, nc, 1), lambda k, bi, gi: (bi, 0, 0)),
            const((1, hd)), const((hd, hd)), const((1, hd)),
        ],
        out_specs=pl.BlockSpec((1, 1, 1, nc, 2 * hd), lambda k, bi, gi: (k, bi, gi, 0, 0)),
        out_shape=jax.ShapeDtypeStruct((2, b, g, nc, 2 * hd), BF16),
        compiler_params=_cparams("parallel", "parallel", "parallel"),
        name="nsa_compress",
    )(xc, w1cat, w2, pe2, pos_cmp, k_norm0.reshape(1, hd), _rope_rot_matrix(), _rope_freq_head())


def _softmax_rows(s, mask):
    s = jnp.where(mask, s, NEG_BIG)
    m = jnp.max(s, axis=-1, keepdims=True)
    p = jnp.where(mask, jnp.exp(s - m), 0.0)
    d = jnp.sum(p, axis=-1, keepdims=True)
    return p / jnp.where(d > 0, d, 1.0)


_NT = (((1,), (1,)), ((), ()))


def _select_blocks(p_sum, ov_ref, t0, tq, n_sel):
    n_blk = ov_ref.shape[0]
    imp_t = lax.dot_general(ov_ref[...], p_sum, _NT, preferred_element_type=F32, precision=HIGHEST)
    j_col = lax.broadcasted_iota(jnp.int32, (n_blk, 1), 0)
    t_lane = t0 + lax.broadcasted_iota(jnp.int32, (1, tq), 1)
    causal = j_col * SEL_BLOCK <= t_lane
    cur = t_lane >> 6
    forced = causal & ((j_col == 0) | (j_col == cur) | (j_col == cur - 1))
    score = jnp.where(forced, FORCE_SCORE, jnp.where(causal, imp_t, -FORCE_SCORE))
    ng = n_blk // 8
    groups = [score[8 * v:8 * v + 8] for v in range(ng)]
    ranks = [jnp.zeros((8, tq), F32) for _ in range(ng)]
    sub = lax.broadcasted_iota(jnp.int32, (8, tq), 0)
    for jp in range(n_blk):
        row = jnp.broadcast_to(score[jp:jp + 1, :], (8, tq))
        vj = jp // 8
        for v in range(ng):
            if v < vj:
                ahead = row > groups[v]
            elif v > vj:
                ahead = row >= groups[v]
            else:
                ahead = (row > groups[v]) | ((row == groups[v]) & (sub > jp % 8))
            ranks[v] = ranks[v] + jnp.where(ahead, 1.0, 0.0)
    sel_t = jnp.where(jnp.concatenate(ranks, axis=0) < n_sel, 1.0, 0.0).astype(BF16)
    eye = (lax.broadcasted_iota(jnp.int32, (tq, tq), 0) == lax.broadcasted_iota(jnp.int32, (tq, tq), 1)).astype(BF16)
    return lax.dot_general(eye, sel_t, _NT, preferred_element_type=F32)


def _sum_heads(p, tq):
    out = p[0:tq]
    for r in range(1, NSA_REP):
        out = out + p[r * tq:(r + 1) * tq]
    return out


def _gated_store(gl_ref, o_ref, o_c, o_s, o_w, tq):
    rep, hd = NSA_REP, NSA_HEAD_DIM
    gates = jax.nn.sigmoid(gl_ref[0].astype(F32))
    for r in range(rep):
        rs = slice(r * tq, (r + 1) * tq)
        out = (gates[:, r:r + 1] * o_c[rs] + gates[:, rep + r:rep + r + 1] * o_s[rs]
               + gates[:, 2 * rep + r:2 * rep + r + 1] * o_w[rs])
        o_ref[0, :, r * hd:(r + 1) * hd] = out.astype(o_ref.dtype)


def _nsa_fast(sh_ref, q_ref, kc_ref, vc_ref, ks_ref, vs_ref, kw_ref, vw_ref, gl_ref, ov_ref, o_ref,
              lhs_ref, acc_ref, pre_ref, *, tq, tk, n_sel):
    rep, hd = NSA_REP, NSA_HEAD_DIM
    rows = rep * tq
    t0 = pl.program_id(2) * tq
    c_c, c_s, c_w = sh_ref[0], sh_ref[1], sh_ref[2]
    q4 = q_ref[0].reshape(rows, hd)
    t_q = t0 + lax.broadcasted_iota(jnp.int32, (tq, 1), 0)


    span = WINDOW + tq
    w0 = pl.multiple_of(jnp.maximum(t0 - WINDOW, 0), tq)
    kp = w0 + lax.broadcasted_iota(jnp.int32, (1, span), 1)
    bias_w = jnp.where((kp <= t_q) & (kp > t_q - WINDOW), c_w, NEG_BIG)
    s_w = lax.dot_general(q4, kw_ref[0, 0, pl.ds(w0, span), :], _NT, preferred_element_type=F32)
    p_w = jnp.exp(s_w.reshape(rep, tq, span) + bias_w[None]).reshape(rows, span)
    pre_ref[0] = jnp.dot(p_w.astype(BF16), vw_ref[0, 0, pl.ds(w0, span), :], preferred_element_type=F32)

    d0 = pl.multiple_of(t0, tq)
    kcol = t0 + lax.broadcasted_iota(jnp.int32, (1, tq), 1)
    bias_d = jnp.where(kcol <= t_q, c_s, NEG_BIG)
    s_d = lax.dot_general(q4, ks_ref[0, 0, pl.ds(d0, tq), :][:, 0:hd], _NT, preferred_element_type=F32)
    p_d = jnp.exp(s_d.reshape(rep, tq, tq) + bias_d[None]).reshape(rows, tq)
    acc_ref[...] = jnp.dot(p_d.astype(BF16), vs_ref[0, 0, pl.ds(d0, tq), :], preferred_element_type=F32)

    kc = kc_ref[0, 0, 0][:, 0:hd]
    ncp = kc.shape[0]
    n_idx = lax.broadcasted_iota(jnp.int32, (1, ncp), 1)
    valid_c = (n_idx * CMP_STRIDE + (CMP_BLOCK - 1) <= t_q) & (n_idx < ncp - 1)
    bias_c = jnp.where(valid_c, c_c, NEG_BIG)
    s_c = lax.dot_general(q4, kc, _NT, preferred_element_type=F32).reshape(rep, tq, ncp)
    p_c = jnp.exp(s_c + bias_c[None]).reshape(rows, ncp)
    ol_c = jnp.dot(p_c.astype(BF16), vc_ref[0, 0, 0], preferred_element_type=F32)
    inv_c = 1.0 / jnp.where(ol_c[:, hd:hd + 1] > 0, ol_c[:, hd:hd + 1], 1.0)
    pre_ref[1] = ol_c * inv_c
    sel = _select_blocks(_sum_heads(p_c * inv_c, tq), ov_ref, t0, tq, n_sel)

    n_blk = sel.shape[1]
    j_lane = lax.broadcasted_iota(jnp.int32, (1, n_blk), 1)
    shift = jnp.where((sel > 0.5) & (j_lane * SEL_BLOCK < t0), c_s, NEG_BIG).astype(BF16)
    if n_blk < hd:
        shift = jnp.concatenate([shift, jnp.zeros((tq, hd - n_blk), BF16)], axis=1)
    lhs_ref[:, 0:hd] = q4
    for r in range(rep):
        lhs_ref[r * tq:(r + 1) * tq, hd:2 * hd] = shift

    def kv_step(c, carry):
        k0 = pl.multiple_of(c * tk, tk)
        s = lax.dot_general(lhs_ref[...], ks_ref[0, 0, pl.ds(k0, tk), :], _NT, preferred_element_type=F32)
        acc_ref[...] += jnp.dot(jnp.exp(s).astype(BF16), vs_ref[0, 0, pl.ds(k0, tk), :],
                                preferred_element_type=F32)
        return carry

    lax.fori_loop(0, (t0 + tk - 1) // tk, kv_step, 0)

    ol_s = acc_ref[...]
    ol_w = pre_ref[0]
    _gated_store(gl_ref, o_ref, pre_ref[1][:, 0:hd], ol_s[:, 0:hd] / ol_s[:, hd:hd + 1],
                 ol_w[:, 0:hd] / ol_w[:, hd:hd + 1], tq)


def _nsa_slow(q_ref, kc_ref, vc_ref, ks_ref, vs_ref, kw_ref, vw_ref, gl_ref, ov_ref, o_ref,
              acc_ref, m_ref, l_ref, *, tq, tk, n_sel):
    rep, hd = NSA_REP, NSA_HEAD_DIM
    rows = rep * tq
    t0 = pl.program_id(2) * tq
    q4 = q_ref[0].reshape(rows, hd)
    t_row = t0 + (lax.broadcasted_iota(jnp.int32, (rows, 1), 0) & (tq - 1))

    kc = kc_ref[0, 0, 0][:, 0:hd]
    ncp = kc.shape[0]
    s_c = lax.dot_general(q4, kc, _NT, preferred_element_type=F32)
    n_idx = lax.broadcasted_iota(jnp.int32, (1, ncp), 1)
    p_c = _softmax_rows(s_c, (n_idx * CMP_STRIDE + (CMP_BLOCK - 1) <= t_row) & (n_idx < ncp - 1))
    o_c = jnp.dot(p_c.astype(BF16), vc_ref[0, 0, 0], preferred_element_type=F32)[:, 0:hd]
    sel = _select_blocks(_sum_heads(p_c, tq), ov_ref, t0, tq, n_sel).astype(BF16)
    n_blk = sel.shape[1]

    m_ref[...] = jnp.full(m_ref.shape, NEG_BIG, F32)
    l_ref[...] = jnp.zeros(l_ref.shape, F32)
    acc_ref[...] = jnp.zeros(acc_ref.shape, F32)
    t_q = t0 + lax.broadcasted_iota(jnp.int32, (tq, 1), 0)

    def kv_step(c, carry):
        k0 = pl.multiple_of(c * tk, tk)
        kt = ks_ref[0, 0, pl.ds(k0, tk), :][:, 0:hd]
        s = lax.dot_general(q4, kt, _NT, preferred_element_type=F32).reshape(rep, tq, tk)
        kk = lax.broadcasted_iota(jnp.int32, (1, tk), 1)
        blk = (k0 >> 6) + (kk >> 6)
        expand = jnp.where(lax.broadcasted_iota(jnp.int32, (n_blk, 1), 0) == blk, 1.0, 0.0).astype(BF16)
        chosen = jnp.dot(sel, expand, preferred_element_type=F32)
        mask = ((chosen > 0.5) & (k0 + kk <= t_q))[None]
        s = jnp.where(mask, s, NEG_BIG)
        m_old = m_ref[...]
        m_new = jnp.maximum(m_old, jnp.max(s, axis=-1, keepdims=True))
        p = jnp.where(mask, jnp.exp(s - m_new), 0.0)
        alpha = jnp.exp(m_old - m_new)
        l_ref[...] = alpha * l_ref[...] + jnp.sum(p, axis=-1, keepdims=True)
        pv = jnp.dot(p.reshape(rows, tk).astype(BF16), vs_ref[0, 0, pl.ds(k0, tk), :], preferred_element_type=F32)
        acc_ref[...] = alpha.reshape(rows, 1) * acc_ref[...] + pv
        m_ref[...] = m_new
        return carry

    lax.fori_loop(0, (t0 + tq + tk - 1) // tk, kv_step, 0)
    o_s = acc_ref[:, 0:hd] / l_ref[...].reshape(rows, 1)

    span = WINDOW + tq
    w0 = pl.multiple_of(jnp.maximum(t0 - WINDOW, 0), tq)
    s_w = lax.dot_general(q4, kw_ref[0, 0, pl.ds(w0, span), :], _NT, preferred_element_type=F32)
    kp = w0 + lax.broadcasted_iota(jnp.int32, (1, span), 1)
    p_w = _softmax_rows(s_w, (kp <= t_row) & (kp > t_row - WINDOW))
    o_w = jnp.dot(p_w.astype(BF16), vw_ref[0, 0, pl.ds(w0, span), :], preferred_element_type=F32)[:, 0:hd]

    _gated_store(gl_ref, o_ref, o_c, o_s, o_w, tq)


def _nsa_attn_kernel(sh_ref, q_ref, kc_ref, vc_ref, ks_ref, vs_ref, kw_ref, vw_ref, gl_ref, ov_ref, o_ref,
                     lhs_ref, acc_ref, pre_ref, m_ref, l_ref, *, tq, tk, n_sel):
    data = (q_ref, kc_ref, vc_ref, ks_ref, vs_ref, kw_ref, vw_ref, gl_ref, ov_ref, o_ref)

    @pl.when(sh_ref[3] > 0.5)
    def _():
        _nsa_fast(sh_ref, *data, lhs_ref, acc_ref, pre_ref, tq=tq, tk=tk, n_sel=n_sel)

    @pl.when(sh_ref[3] <= 0.5)
    def _():
        _nsa_slow(*data, acc_ref, m_ref, l_ref, tq=tq, tk=tk, n_sel=n_sel)


def _overlap_t(n_blk, ncp):
    c_start = np.arange(ncp) * CMP_STRIDE
    b_start = np.arange(n_blk) * SEL_BLOCK
    ov = ((c_start[None, :] < b_start[:, None] + SEL_BLOCK) & (b_start[:, None] < c_start[None, :] + CMP_BLOCK))
    ov[:, ncp - 1] = False
    return jnp.asarray(ov.astype(np.float32))


def _nsa_shifts(q_norm, k_norm):
    bound = (NSA_HEAD_DIM ** 0.5) * jnp.max(jnp.abs(q_norm)) * jnp.max(jnp.abs(k_norm), axis=-1)
    bound = bound.astype(BF16).astype(F32)
    fast = jnp.all(bound <= MAX_CONST_SHIFT).astype(F32)
    return jnp.concatenate([-bound, fast[None]])


def _nsa_attention(shifts, q, cmp_kv, ks, vs, kw, vw, z3, *, tq, tk):
    b, h, s, hd = q.shape
    g, rep = NSA_GROUPS, NSA_REP
    ncp = cmp_kv.shape[3]
    n_blk = s // SEL_BLOCK
    rows = rep * tq
    assert n_blk % 8 == 0 and n_blk <= hd and WINDOW % tq == 0
    full = lambda w: pl.BlockSpec((1, 1, s, w), lambda bi, gi, i: (bi, gi, 0, 0))
    kernel = functools.partial(_nsa_attn_kernel, tq=tq, tk=tk, n_sel=min(N_SEL, n_blk))
    return pl.pallas_call(
        kernel,
        grid=(b, g, s // tq),
        in_specs=[
            pl.BlockSpec(memory_space=pltpu.SMEM),
            pl.BlockSpec((1, rep, tq, hd), lambda bi, gi, i: (bi, gi, i, 0)),
            pl.BlockSpec((1, 1, 1, ncp, 2 * hd), lambda bi, gi, i: (0, bi, gi, 0, 0)),
            pl.BlockSpec((1, 1, 1, ncp, 2 * hd), lambda bi, gi, i: (1, bi, gi, 0, 0)),
            full(2 * hd), full(2 * hd), full(hd), full(2 * hd),
            pl.BlockSpec((1, tq, LANES), lambda bi, gi, i: (bi, i, COL_SMALL // LANES + gi)),
            pl.BlockSpec((n_blk, ncp), lambda bi, gi, i: (0, 0)),
        ],
        out_specs=pl.BlockSpec((1, tq, rep * hd), lambda bi, gi, i: (bi, i, gi)),
        out_shape=jax.ShapeDtypeStruct((b, s, h * hd), BF16),
        scratch_shapes=[pltpu.VMEM((rows, 2 * hd), BF16), pltpu.VMEM((rows, 2 * hd), F32),
                        pltpu.VMEM((2, rows, 2 * hd), F32),
                        pltpu.VMEM((rep, tq, 1), F32), pltpu.VMEM((rep, tq, 1), F32)],
        compiler_params=_cparams("parallel", "parallel", "arbitrary"),
        name="nsa_attention",
    )(shifts, q, cmp_kv, cmp_kv, ks, vs, kw, vw, z3, _overlap_t(n_blk, ncp))


def _head_block_diag(x, n_rows, head_of_lane, dtype):
    return jnp.concatenate([jnp.where(head_of_lane == h, x, 0.0) for h in range(GLA_HEADS)], axis=0).astype(dtype)


def _gla_kernel(q_ref, k_ref, v_ref, r_ref, sm_ref, wg_ref, bg_ref, ng_ref, o_ref, st_ref, *, n_chunks):
    c, sub = GLA_CHUNK, GLA_SUB
    dk, dv, nh = GLA_HEAD_DK, GLA_HEAD_DV, GLA_HEADS
    nk, nv = nh * dk, nh * dv

    @pl.when(pl.program_id(1) == 0)
    def _():
        st_ref[...] = jnp.zeros(st_ref.shape, F32)

    x = jnp.dot(sm_ref[0].astype(F32), wg_ref[...], preferred_element_type=F32, precision=HIGHEST) + bg_ref[...]
    log_a_all = (jnp.minimum(x, 0.0) - jnp.log1p(jnp.exp(-jnp.abs(x)))) / GLA_TAU
    tril = (lax.broadcasted_iota(jnp.int32, (c, c), 1) <= lax.broadcasted_iota(jnp.int32, (c, c), 0)).astype(F32)
    head_k = lax.broadcasted_iota(jnp.int32, (1, nk), 1) >> 6
    head_v = lax.broadcasted_iota(jnp.int32, (1, nv), 1) >> 7
    state_mask = (lax.broadcasted_iota(jnp.int32, (nk, 1), 0) >> 6) == head_v
    eye_k = (lax.broadcasted_iota(jnp.int32, (nk, nk), 0) == lax.broadcasted_iota(jnp.int32, (nk, nk), 1))

    for cc in range(n_chunks):
        rs = slice(cc * c, (cc + 1) * c)
        bcum = jnp.dot(tril, log_a_all[rs], preferred_element_type=F32, precision=HIGHEST)
        q = q_ref[0, rs].astype(F32) * (dk ** -0.5)
        k = k_ref[0, rs].astype(F32)
        v = v_ref[0, rs].astype(F32)
        b_last = bcum[c - 1:c, :]

        parts = []
        for i in range(c // sub):
            lo, hi = i * sub, (i + 1) * sub
            ref = bcum[lo - 1:lo, :] if i > 0 else jnp.zeros((1, nk), F32)
            q_i = (q[lo:hi] * jnp.exp(bcum[lo:hi] - ref)).astype(BF16)
            k_i = k[0:hi] * jnp.exp(ref - bcum[0:hi])
            a = lax.dot_general(q_i, _head_block_diag(k_i, hi, head_k, BF16), _NT,
                                preferred_element_type=F32)
            col = lax.broadcasted_iota(jnp.int32, (sub, nh * hi), 1)
            key = col
            for h in range(1, nh):
                key = key - jnp.where(col >= h * hi, hi, 0)
            a = jnp.where(key <= lax.broadcasted_iota(jnp.int32, (sub, nh * hi), 0) + lo, a, 0.0).astype(BF16)
            parts.append(jnp.dot(a, _head_block_diag(v[0:hi], hi, head_v, BF16), preferred_element_type=F32))

        st = st_ref[...]
        q_in = (q * jnp.exp(bcum)).astype(BF16)
        o = jnp.concatenate(parts, axis=0) + jnp.dot(q_in, st.astype(BF16), preferred_element_type=F32)

        k_out = (k * jnp.exp(b_last - bcum)).astype(BF16)
        upd = lax.dot_general(k_out, v.astype(BF16), (((0,), (0,)), ((), ())), preferred_element_type=F32)
        decay_col = jnp.sum(jnp.where(eye_k, jnp.exp(b_last), 0.0), axis=1, keepdims=True)
        st_ref[...] = st * decay_col + jnp.where(state_mask, upd, 0.0)

        r_gate = r_ref[0, rs].astype(F32)
        for h in range(nh):
            vs_ = slice(h * dv, (h + 1) * dv)
            o_h = o[:, vs_]
            ms = jnp.mean(o_h * o_h, axis=-1, keepdims=True)
            rg = r_gate[:, vs_]
            o_ref[0, rs, vs_] = (o_h * lax.rsqrt(ms + NORM_EPS) * ng_ref[...] * (rg * jax.nn.sigmoid(rg))).astype(o_ref.dtype)


def _gla(z3, w_gate, b_gate, norm_g, *, n_chunks):
    b, s, _ = z3.shape
    c = GLA_CHUNK * n_chunks
    nk = GLA_HEADS * GLA_HEAD_DK
    nv = GLA_HEADS * GLA_HEAD_DV
    wg = jnp.zeros((LANES, nk), F32).at[SMALL_GLOW_LANE:SMALL_GLOW_LANE + GLA_RANK].set(w_gate.astype(F32))
    const = lambda shape: pl.BlockSpec(shape, lambda bi, i: (0,) * len(shape))
    return pl.pallas_call(
        functools.partial(_gla_kernel, n_chunks=n_chunks),
        grid=(b, s // c),
        in_specs=[
            pl.BlockSpec((1, c, nk), lambda bi, i: (bi, i, COL_GQ // nk)),
            pl.BlockSpec((1, c, nk), lambda bi, i: (bi, i, COL_GK // nk)),
            pl.BlockSpec((1, c, nv), lambda bi, i: (bi, i, COL_GV // nv)),
            pl.BlockSpec((1, c, nv), lambda bi, i: (bi, i, COL_GR // nv)),
            pl.BlockSpec((1, c, LANES), lambda bi, i: (bi, i, COL_SMALL // LANES)),
            const((LANES, nk)), const((1, nk)), const((1, GLA_HEAD_DV)),
        ],
        out_specs=pl.BlockSpec((1, c, nv), lambda bi, i: (bi, i, 0)),
        out_shape=jax.ShapeDtypeStruct((b, s, nv), BF16),
        scratch_shapes=[pltpu.VMEM((nk, nv), F32)],
        compiler_params=_cparams("parallel", "arbitrary"),
        name="gla",
    )(z3, z3, z3, z3, z3, wg, b_gate.reshape(1, nk).astype(F32), norm_g.reshape(1, GLA_HEAD_DV).astype(F32))


def _mem_attn_kernel(q_ref, k_ref, v_ref, qg_ref, kg_ref, o_ref):
    dh = MEM_HEAD_DIM
    nt = (((1,), (1,)), ((), ()))
    for h in range(MEM_HEADS):
        sl = slice(h * dh, (h + 1) * dh)
        q = q_ref[0, :, sl].astype(F32)
        q = q * lax.rsqrt(jnp.mean(q * q, axis=-1, keepdims=True) + NORM_EPS) * qg_ref[...] * (dh ** -0.5)
        k = k_ref[0, :, sl].astype(F32)
        k = k * lax.rsqrt(jnp.mean(k * k, axis=-1, keepdims=True) + NORM_EPS) * kg_ref[...]
        s = lax.dot_general(q.astype(BF16), k.astype(BF16), nt, preferred_element_type=F32)
        m = jnp.max(s, axis=-1, keepdims=True)
        p = jnp.exp(s - m)
        p = p / jnp.sum(p, axis=-1, keepdims=True)
        o = jnp.dot(p.astype(BF16), v_ref[0, :, sl].astype(BF16), preferred_element_type=F32)
        o_ref[0, :, sl] = o.astype(o_ref.dtype)


def _mem_attention(z3, kv, q_norm, k_norm, *, tq):
    b, s, _ = z3.shape
    m = kv.shape[1]
    w = MEM_HEADS * MEM_HEAD_DIM
    const = lambda shape: pl.BlockSpec(shape, lambda bi, i: (0,) * len(shape))
    return pl.pallas_call(
        _mem_attn_kernel,
        grid=(b, s // tq),
        in_specs=[
            pl.BlockSpec((1, tq, w), lambda bi, i: (bi, i, COL_MQ // w)),
            pl.BlockSpec((1, m, w), lambda bi, i: (bi, 0, 0)),
            pl.BlockSpec((1, m, w), lambda bi, i: (bi, 0, 1)),
            const((1, MEM_HEAD_DIM)), const((1, MEM_HEAD_DIM)),
        ],
        out_specs=pl.BlockSpec((1, tq, w), lambda bi, i: (bi, i, 0)),
        out_shape=jax.ShapeDtypeStruct((b, s, w), BF16),
        compiler_params=_cparams("parallel", "parallel"),
        name="mem_attention",
    )(z3, kv, kv, q_norm.reshape(1, MEM_HEAD_DIM).astype(F32), k_norm.reshape(1, MEM_HEAD_DIM).astype(F32))


def _merge_kernel(x_ref, on_ref, og_ref, om_ref, m0_ref, m1_ref, m2_ref, bm_ref, wb_ref, wo_ref, o_ref):
    merged = None
    for br, (ref, mg_ref) in enumerate(((on_ref, m0_ref), (og_ref, m1_ref), (om_ref, m2_ref))):
        y = jnp.dot(ref[...], wb_ref[br], preferred_element_type=F32)
        gate = jax.nn.sigmoid(mg_ref[...].astype(F32) + bm_ref[br:br + 1, :])
        merged = gate * y if merged is None else merged + gate * y
    o_ref[...] = x_ref[...] + jnp.dot(merged.astype(BF16), wo_ref[...], preferred_element_type=F32)


def _merge_out(x2, o_nsa, o_gla, o_mem, z2, b_merge, w_branch, w_out, *, tm):
    t, d = x2.shape
    bw = BRANCH_WIDTH
    row = lambda w: pl.BlockSpec((tm, w), lambda i: (i, 0))
    gate_cols = lambda br: pl.BlockSpec((tm, d), lambda i: (i, COL_MERGE // d + br))
    return pl.pallas_call(
        _merge_kernel,
        grid=(t // tm,),
        in_specs=[
            row(d), row(bw), row(bw), row(bw),
            gate_cols(0), gate_cols(1), gate_cols(2),
            pl.BlockSpec((N_BRANCH, d), lambda i: (0, 0)),
            pl.BlockSpec((N_BRANCH, bw, d), lambda i: (0, 0, 0)),
            pl.BlockSpec((d, d), lambda i: (0, 0)),
        ],
        out_specs=row(d),
        out_shape=jax.ShapeDtypeStruct((t, d), F32),
        compiler_params=_cparams("parallel"),
        name="merge_out",
    )(x2, o_nsa, o_gla, o_mem, z2, z2, z2, b_merge.astype(F32), w_branch, w_out)


def _mlp_kernel(x_ref, g_ref, wu_ref, wd_ref, o_ref, h_ref, acc_ref):
    j = pl.program_id(1)

    @pl.when(j == 0)
    def _():
        x = x_ref[...]
        ms = jnp.mean(x * x, axis=-1, keepdims=True)
        h_ref[...] = (x * lax.rsqrt(ms + NORM_EPS) * g_ref[...]).astype(h_ref.dtype)
        acc_ref[...] = jnp.zeros(acc_ref.shape, F32)

    u = jnp.dot(h_ref[...], wu_ref[...], preferred_element_type=F32)
    u = jnp.square(jnp.maximum(u, 0.0)).astype(BF16)
    acc_ref[...] += jnp.dot(u, wd_ref[...], preferred_element_type=F32)

    @pl.when(j == pl.num_programs(1) - 1)
    def _():
        o_ref[...] = x_ref[...] + acc_ref[...]


def _mlp(x2, g, w_up, w_down, *, tm, th):
    t, d = x2.shape
    hid = w_up.shape[1]
    return pl.pallas_call(
        _mlp_kernel,
        grid=(t // tm, hid // th),
        in_specs=[
            pl.BlockSpec((tm, d), lambda i, j: (i, 0)),
            pl.BlockSpec((1, d), lambda i, j: (0, 0)),
            pl.BlockSpec((d, th), lambda i, j: (0, j)),
            pl.BlockSpec((th, d), lambda i, j: (j, 0)),
        ],
        out_specs=pl.BlockSpec((tm, d), lambda i, j: (i, 0)),
        out_shape=jax.ShapeDtypeStruct((t, d), F32),
        scratch_shapes=[pltpu.VMEM((tm, d), BF16), pltpu.VMEM((tm, d), F32)],
        compiler_params=_cparams("parallel", "arbitrary"),
        name="mlp",
    )(x2, g.reshape(1, d).astype(F32), w_up, w_down)


def _tile(n, pref):
    t = min(n, pref)
    assert n % t == 0, (n, pref)
    return t


def _layer(x, mem2, pos3, pos_cmp, perm, p):
    b, s, d = x.shape
    t = b * s
    hd, g = NSA_HEAD_DIM, NSA_GROUPS
    x2 = x.reshape(t, d)

    w_in = jnp.take(jnp.pad(p["w_in"], ((0, 0), (0, 1))), perm, axis=1).astype(BF16)
    z2 = _norm_matmul(x2, p["ln_mix"].astype(F32), w_in, tm=_tile(t, 1024), tn=D_IN_PAD // 4, out_dtype=BF16, name="in_proj")
    z3 = z2.reshape(b, s, D_IN_PAD)

    q, ks, vs, kw, vw = _nsa_prep(z3, pos3, p["nsa_q_norm"].astype(F32), p["nsa_k_norm"][1].astype(F32),
                                  p["nsa_k_norm"][2].astype(F32), tp=_tile(s, 256))
    nc = s // CMP_STRIDE
    kvc = z3[:, :, COL_KV:COL_KV + 2 * g * hd].reshape(b, s, 2, g, hd)
    xc = kvc.transpose(2, 0, 3, 1, 4).reshape(2, b, g, nc, CMP_STRIDE * hd)
    w1 = p["cmp_w1"]
    half = CMP_STRIDE * hd
    w1cat = jnp.concatenate([w1[:, :half], w1[:, half:]], axis=-1).astype(BF16)
    pe2 = jnp.pad(p["cmp_pe"].reshape(2, 2, half), ((0, 0), (0, 6), (0, 0))).astype(F32)
    cmp_kv = _nsa_compress(xc, w1cat, p["cmp_w2"].astype(BF16), pe2, pos_cmp, p["nsa_k_norm"][0].astype(F32))
    shifts = _nsa_shifts(p["nsa_q_norm"].astype(F32), p["nsa_k_norm"].astype(F32))
    o_nsa = _nsa_attention(shifts, q, cmp_kv, ks, vs, kw, vw, z3, tq=_tile(s, 256), tk=_tile(s, 512))

    o_gla = _gla(z3, p["gla_w_gate"], p["gla_b_gate"], p["gla_norm"], n_chunks=4)

    kv = _norm_matmul(mem2, p["mem_norm"].astype(F32), p["mem_w_kv"].astype(BF16),
                      tm=_tile(mem2.shape[0], 512), tn=512, out_dtype=BF16, name="mem_kv")
    kv = kv.reshape(b, mem2.shape[0] // b, 2 * MEM_HEADS * MEM_HEAD_DIM)
    o_mem = _mem_attention(z3, kv, p["mem_q_norm"], p["mem_k_norm"], tq=_tile(s, 512))

    x2 = _merge_out(x2, o_nsa.reshape(t, -1), o_gla.reshape(t, -1), o_mem.reshape(t, -1), z2,
                    p["b_merge"], p["w_branch"].astype(BF16), p["w_out"].astype(BF16), tm=_tile(t, 512))
    x2 = _mlp(x2, p["ln_mlp"], p["w_up"].astype(BF16), p["w_down"].astype(BF16), tm=_tile(t, 1024), th=512)
    return x2.reshape(b, s, d)


def kernel(x, mem, positions, ln_mix, w_in, b_merge, nsa_q_norm, nsa_k_norm, cmp_pe, cmp_w1, cmp_w2,
           gla_w_gate, gla_b_gate, gla_norm, mem_norm, mem_w_kv, mem_q_norm, mem_k_norm, w_branch, w_out,
           ln_mlp, w_up, w_down):
    b, s, d = x.shape
    assert d == 1024 and s % WINDOW == 0 and s >= 2 * WINDOW
    depth = w_in.shape[0]
    perm_np, d_in = _in_proj_permutation(d)
    assert w_in.shape[2] == d_in
    perm = jnp.asarray(perm_np)
    pos3 = positions.astype(jnp.int32).reshape(b, s, 1)
    nc = s // CMP_STRIDE
    cmp_end = np.minimum(np.arange(nc) * CMP_STRIDE + CMP_BLOCK - 1, s - 1)
    pos_cmp = pos3[:, cmp_end, :]
    mem2 = mem.reshape(b * mem.shape[1], d)
    names = ("ln_mix", "w_in", "b_merge", "nsa_q_norm", "nsa_k_norm", "cmp_pe", "cmp_w1", "cmp_w2",
             "gla_w_gate", "gla_b_gate", "gla_norm", "mem_norm", "mem_w_kv", "mem_q_norm", "mem_k_norm",
             "w_branch", "w_out", "ln_mlp", "w_up", "w_down")
    stacked = (ln_mix, w_in, b_merge, nsa_q_norm, nsa_k_norm, cmp_pe, cmp_w1, cmp_w2, gla_w_gate, gla_b_gate,
               gla_norm, mem_norm, mem_w_kv, mem_q_norm, mem_k_norm, w_branch, w_out, ln_mlp, w_up, w_down)
    for l in range(depth):
        x = _layer(x, mem2, pos3, pos_cmp, perm, {n: a[l] for n, a in zip(names, stacked)})
    return x
```

```python
import functools

import numpy as np
import jax
import jax.numpy as jnp
from jax import lax
from jax.experimental import pallas as pl
from jax.experimental.pallas import tpu as pltpu

NSA_HEADS = 8
NSA_GROUPS = 2
NSA_REP = NSA_HEADS // NSA_GROUPS
NSA_HEAD_DIM = 64
CMP_BLOCK = 32
CMP_STRIDE = 16
CMP_HIDDEN = 4 * NSA_HEAD_DIM
SEL_BLOCK = 64
N_SEL = 16
WINDOW = 512
FORCE_SCORE = 1e4
GLA_HEADS = 4
GLA_HEAD_DK = 64
GLA_HEAD_DV = 128
GLA_RANK = 16
GLA_TAU = 16.0
GLA_CHUNK = 64
GLA_SUB = 16
MEM_HEADS = 4
MEM_HEAD_DIM = 128
N_BRANCH = 3
BRANCH_WIDTH = 512
ROPE_THETA = 500000.0
ROPE_ROT = NSA_HEAD_DIM // 4
ROPE_HALF = ROPE_ROT // 2
NORM_EPS = 1e-6

LANES = 128
VMEM_LIMIT_BYTES = 48 * 1024 * 1024

F32 = jnp.float32
BF16 = jnp.bfloat16
HIGHEST = lax.Precision.HIGHEST
NEG_BIG = -1e30
MAX_CONST_SHIFT = 40.0

COL_NQ = 0
COL_GV = 512
COL_GR = 1024
COL_MQ = 1536
COL_MERGE = 2048
COL_GQ = 5120
COL_GK = 5376
COL_KV = 5632
COL_SMALL = 6400
D_IN_PAD = 6656
SMALL_GLOW_LANE = 12


def _in_proj_permutation(d_model):
    sizes = (512, 128, 128, 128, 128, 128, 128, 24, 256, 256, 512, 512, 16, 512, 3 * d_model)
    off = np.concatenate([[0], np.cumsum(sizes)])
    (o_nq, o_kc, o_vc, o_ks, o_vs, o_kw, o_vw, o_ng, o_gq, o_gk, o_gv, o_gr, o_gl, o_mq, o_mg) = off[:-1]
    d_in = int(off[-1])
    perm = np.full((D_IN_PAD,), d_in, np.int32)

    def put(new, old, n):
        perm[new:new + n] = np.arange(old, old + n)

    put(COL_NQ, o_nq, 512)
    put(COL_GV, o_gv, 512)
    put(COL_GR, o_gr, 512)
    put(COL_MQ, o_mq, 512)
    put(COL_MERGE, o_mg, 3 * d_model)
    put(COL_GQ, o_gq, 256)
    put(COL_GK, o_gk, 256)
    put(COL_KV, o_kc, 768)
    for g in range(NSA_GROUPS):
        for br in range(3):
            for r in range(NSA_REP):
                perm[COL_SMALL + g * LANES + br * NSA_REP + r] = o_ng + (g * NSA_REP + r) * 3 + br
    put(COL_SMALL + SMALL_GLOW_LANE, o_gl, GLA_RANK)
    return perm, d_in


def _cparams(*sem):
    return pltpu.CompilerParams(dimension_semantics=sem, vmem_limit_bytes=VMEM_LIMIT_BYTES)


def _norm_matmul_kernel(x_ref, g_ref, w_ref, o_ref, h_ref):
    @pl.when(pl.program_id(1) == 0)
    def _():
        x = x_ref[...].astype(F32)
        ms = jnp.mean(x * x, axis=-1, keepdims=True)
        h_ref[...] = (x * lax.rsqrt(ms + NORM_EPS) * g_ref[...]).astype(h_ref.dtype)

    o_ref[...] = jnp.dot(h_ref[...], w_ref[...], preferred_element_type=F32).astype(o_ref.dtype)


def _norm_matmul(x, g, w, *, tm, tn, out_dtype, name):
    m, k = x.shape
    n = w.shape[1]
    return pl.pallas_call(
        _norm_matmul_kernel,
        grid=(m // tm, n // tn),
        in_specs=[
            pl.BlockSpec((tm, k), lambda i, j: (i, 0)),
            pl.BlockSpec((1, k), lambda i, j: (0, 0)),
            pl.BlockSpec((k, tn), lambda i, j: (0, j)),
        ],
        out_specs=pl.BlockSpec((tm, tn), lambda i, j: (i, j)),
        out_shape=jax.ShapeDtypeStruct((m, n), out_dtype),
        scratch_shapes=[pltpu.VMEM((tm, k), BF16)],
        compiler_params=_cparams("parallel", "arbitrary"),
        name=name,
    )(x, g.reshape(1, k), w)


def _rope_tables(pos, freq):
    ang = pos * freq
    lane = lax.broadcasted_iota(jnp.int32, (1, LANES), 1)
    first = lane < ROPE_HALF

    def spread(a):
        a = a + pltpu.roll(a, ROPE_HALF, axis=1)
        return a + pltpu.roll(a, NSA_HEAD_DIM, axis=1)

    c = spread(jnp.where(first, jnp.cos(ang), 0.0)) + jnp.where((lane & (NSA_HEAD_DIM - 1)) >= ROPE_ROT, 1.0, 0.0)
    s = spread(jnp.where(first, jnp.sin(ang), 0.0))
    return c, s


def _rope_lanes(x, c, s):
    n = x.shape[-1]
    if n > LANES:
        c = jnp.concatenate([c] * (n // LANES), axis=1)
        s = jnp.concatenate([s] * (n // LANES), axis=1)
    lane = lax.broadcasted_iota(jnp.int32, (1, n), 1) & (NSA_HEAD_DIM - 1)
    up = pltpu.roll(x, n - ROPE_HALF, axis=1)
    dn = pltpu.roll(x, ROPE_HALF, axis=1)
    y = jnp.where(lane < ROPE_HALF, -up, jnp.where(lane < ROPE_ROT, dn, 0.0))
    return x * c + y * s


def _head_rms(x, bd, g):
    x2 = x * x
    hi = x2.astype(BF16)
    lo = (x2 - hi.astype(F32)).astype(BF16)
    ms = (jnp.dot(hi, bd, preferred_element_type=F32) + jnp.dot(lo, bd, preferred_element_type=F32))
    return x * lax.rsqrt(ms * (1.0 / NSA_HEAD_DIM) + NORM_EPS) * g


def _nsa_prep_kernel(q_ref, ks_ref, vs_ref, kw_ref, vw_ref, pos_ref, qg_ref, ksg_ref, kwg_ref,
                     bdq_ref, bdk_ref, f_ref,
                     qo_ref, kso_ref, vso_ref, kwo_ref, vwo_ref):
    hd = NSA_HEAD_DIM
    tp = q_ref.shape[1]
    pos = pos_ref[0].astype(F32)
    c, s = _rope_tables(pos, f_ref[...])
    q = _head_rms(q_ref[0].astype(F32), bdq_ref[...], qg_ref[...])
    q = _rope_lanes(q, c, s) * (hd ** -0.5)
    for h in range(NSA_HEADS):
        qo_ref[0, h] = q[:, h * hd:(h + 1) * hd].astype(qo_ref.dtype)
    ks = _rope_lanes(_head_rms(ks_ref[0].astype(F32), bdk_ref[...], ksg_ref[...]), c, s)
    kw = _rope_lanes(_head_rms(kw_ref[0].astype(F32), bdk_ref[...], kwg_ref[...]), c, s)
    vs = vs_ref[0]
    vw = vw_ref[0]
    tok = pl.program_id(1) * tp + lax.broadcasted_iota(jnp.int32, (tp, hd), 0)
    lane = lax.broadcasted_iota(jnp.int32, (tp, hd), 1)
    blk_onehot = jnp.where((tok >> 6) == lane, 1.0, 0.0).astype(kso_ref.dtype)
    ones_col = jnp.ones((tp, hd), vso_ref.dtype)
    for g in range(NSA_GROUPS):
        sl = slice(g * hd, (g + 1) * hd)
        kso_ref[0, g, :, 0:hd] = ks[:, sl].astype(kso_ref.dtype)
        kso_ref[0, g, :, hd:2 * hd] = blk_onehot
        kwo_ref[0, g] = kw[:, sl].astype(kwo_ref.dtype)
        vso_ref[0, g, :, 0:hd] = vs[:, sl].astype(vso_ref.dtype)
        vso_ref[0, g, :, hd:2 * hd] = ones_col
        vwo_ref[0, g, :, 0:hd] = vw[:, sl].astype(vwo_ref.dtype)
        vwo_ref[0, g, :, hd:2 * hd] = ones_col


def _block_diag_ones(n, width):
    i = np.arange(n)
    return jnp.asarray((i[:, None] // width == i[None, :] // width).astype(np.float32)).astype(BF16)


def _rope_freq_lanes(n):
    inv = ROPE_THETA ** (-(np.arange(ROPE_HALF, dtype=np.float64)) / ROPE_HALF)
    f = np.zeros((n,), np.float64)
    f[:ROPE_HALF] = inv
    return jnp.asarray(f.astype(np.float32)).reshape(1, n)


def _nsa_prep(z3, pos3, q_norm, ks_norm, kw_norm, *, tp):
    b, s, _ = z3.shape
    hd, g = NSA_HEAD_DIM, NSA_GROUPS
    assert SEL_BLOCK == 64 and s // SEL_BLOCK <= hd
    kvb = COL_KV // LANES
    qg = jnp.tile(q_norm, NSA_HEADS).reshape(1, NSA_HEADS * hd)
    ksg = jnp.tile(ks_norm, g).reshape(1, g * hd)
    kwg = jnp.tile(kw_norm, g).reshape(1, g * hd)
    const = lambda shape: pl.BlockSpec(shape, lambda bi, i: (0,) * len(shape))
    plain_out = jax.ShapeDtypeStruct((b, g, s, hd), BF16)
    plain_spec = pl.BlockSpec((1, g, tp, hd), lambda bi, i: (bi, 0, i, 0))
    aug_out = jax.ShapeDtypeStruct((b, g, s, 2 * hd), BF16)
    aug_spec = pl.BlockSpec((1, g, tp, 2 * hd), lambda bi, i: (bi, 0, i, 0))
    return pl.pallas_call(
        _nsa_prep_kernel,
        grid=(b, s // tp),
        in_specs=[
            pl.BlockSpec((1, tp, 512), lambda bi, i: (bi, i, COL_NQ // 512)),
            pl.BlockSpec((1, tp, LANES), lambda bi, i: (bi, i, kvb + 2)),
            pl.BlockSpec((1, tp, LANES), lambda bi, i: (bi, i, kvb + 3)),
            pl.BlockSpec((1, tp, LANES), lambda bi, i: (bi, i, kvb + 4)),
            pl.BlockSpec((1, tp, LANES), lambda bi, i: (bi, i, kvb + 5)),
            pl.BlockSpec((1, tp, 1), lambda bi, i: (bi, i, 0)),
            const((1, 512)), const((1, LANES)), const((1, LANES)),
            const((512, 512)), const((LANES, LANES)),
            const((1, LANES)),
        ],
        out_specs=[
            pl.BlockSpec((1, NSA_HEADS, tp, hd), lambda bi, i: (bi, 0, i, 0)),
            aug_spec, aug_spec, plain_spec, aug_spec,
        ],
        out_shape=[jax.ShapeDtypeStruct((b, NSA_HEADS, s, hd), BF16), aug_out, aug_out, plain_out, aug_out],
        compiler_params=_cparams("parallel", "parallel"),
        name="nsa_prep",
    )(z3, z3, z3, z3, z3, pos3, qg, ksg, kwg,
      _block_diag_ones(512, hd), _block_diag_ones(LANES, hd), _rope_freq_lanes(LANES))


def _gelu_tanh(x):
    return 0.5 * x * (1.0 + jnp.tanh(np.sqrt(2.0 / np.pi) * (x + 0.044715 * x * x * x)))


def _nsa_cmp_kernel(x_ref, w1_ref, w2_ref, pe_ref, pos_ref, g_ref, rot_ref, f_ref, o_ref):
    kind = pl.program_id(0)
    nc = x_ref.shape[3]
    x = x_ref[0, 0, 0].astype(BF16)
    w1 = w1_ref[0]
    ab = jnp.dot(x, w1, preferred_element_type=F32)
    r = jnp.dot(pe_ref[0].astype(BF16), w1, preferred_element_type=F32)
    a = ab[:, :CMP_HIDDEN] + r[0:1, :CMP_HIDDEN]
    bm = ab[:, CMP_HIDDEN:] + r[1:2, CMP_HIDDEN:]
    hid = _gelu_tanh(a + pltpu.roll(bm, nc - 1, axis=0))
    comp = jnp.dot(hid.astype(BF16), w2_ref[0], preferred_element_type=F32)

    hd = NSA_HEAD_DIM
    lane = lax.broadcasted_iota(jnp.int32, (nc, hd), 1)

    @pl.when(kind == 0)
    def _():
        ms = jnp.mean(comp * comp, axis=-1, keepdims=True)
        kn = comp * lax.rsqrt(ms + NORM_EPS) * g_ref[...]
        ang = pos_ref[0].astype(F32) * f_ref[...]
        y = jnp.dot(kn, rot_ref[...], preferred_element_type=F32, precision=HIGHEST)
        o_ref[0, 0, 0, :, 0:hd] = (kn * jnp.cos(ang) + y * jnp.sin(ang)).astype(o_ref.dtype)
        o_ref[0, 0, 0, :, hd:2 * hd] = jnp.zeros((nc, hd), o_ref.dtype)

    @pl.when(kind != 0)
    def _():
        o_ref[0, 0, 0, :, 0:hd] = comp.astype(o_ref.dtype)
        o_ref[0, 0, 0, :, hd:2 * hd] = jnp.ones((nc, hd), o_ref.dtype)


def _rope_freq_head():
    p = np.arange(NSA_HEAD_DIM)
    inv = ROPE_THETA ** (-(np.arange(ROPE_HALF, dtype=np.float64)) / ROPE_HALF)
    f = np.where(p < ROPE_ROT, inv[p % ROPE_HALF], 0.0)
    return jnp.asarray(f.astype(np.float32)).reshape(1, NSA_HEAD_DIM)


def _rope_rot_matrix():
    m = np.zeros((NSA_HEAD_DIM, NSA_HEAD_DIM), np.float32)
    for l in range(ROPE_HALF):
        m[l + ROPE_HALF, l] = -1.0
        m[l, l + ROPE_HALF] = 1.0
    return jnp.asarray(m)


def _nsa_compress(xc, w1cat, w2, pe2, pos_cmp, k_norm0):
    _, b, g, nc, _ = xc.shape
    hd = NSA_HEAD_DIM
    const = lambda shape: pl.BlockSpec(shape, lambda k, bi, gi: (0,) * len(shape))
    return pl.pallas_call(
        _nsa_cmp_kernel,
        grid=(2, b, g),
        in_specs=[
            pl.BlockSpec((1, 1, 1, nc, CMP_STRIDE * hd), lambda k, bi, gi: (k, bi, gi, 0, 0)),
            pl.BlockSpec((1, CMP_STRIDE * hd, 2 * CMP_HIDDEN), lambda k, bi, gi: (k, 0, 0)),
            pl.BlockSpec((1, CMP_HIDDEN, hd), lambda k, bi, gi: (k, 0, 0)),
            pl.BlockSpec((1, 8, CMP_STRIDE * hd), lambda k, bi, gi: (k, 0, 0)),
            pl.BlockSpec((1, nc, 1), lambda k, bi, gi: (bi, 0, 0)),
            const((1, hd)), const((hd, hd)), const((1, hd)),
        ],
        out_specs=pl.BlockSpec((1, 1, 1, nc, 2 * hd), lambda k, bi, gi: (k, bi, gi, 0, 0)),
        out_shape=jax.ShapeDtypeStruct((2, b, g, nc, 2 * hd), BF16),
        compiler_params=_cparams("parallel", "parallel", "parallel"),
        name="nsa_compress",
    )(xc, w1cat, w2, pe2, pos_cmp, k_norm0.reshape(1, hd), _rope_rot_matrix(), _rope_freq_head())


def _softmax_rows(s, mask):
    s = jnp.where(mask, s, NEG_BIG)
    m = jnp.max(s, axis=-1, keepdims=True)
    p = jnp.where(mask, jnp.exp(s - m), 0.0)
    d = jnp.sum(p, axis=-1, keepdims=True)
    return p / jnp.where(d > 0, d, 1.0)


_NT = (((1,), (1,)), ((), ()))


def _select_blocks(p_sum, ov_ref, t0, tq, n_sel):
    n_blk = ov_ref.shape[0]
    p_hi = p_sum.astype(BF16)
    p_lo = (p_sum - p_hi.astype(F32)).astype(BF16)
    imp_t = (lax.dot_general(ov_ref[...], p_hi, _NT, preferred_element_type=F32)
             + lax.dot_general(ov_ref[...], p_lo, _NT, preferred_element_type=F32))
    j_col = lax.broadcasted_iota(jnp.int32, (n_blk, 1), 0)
    t_lane = t0 + lax.broadcasted_iota(jnp.int32, (1, tq), 1)
    causal = j_col * SEL_BLOCK <= t_lane
    cur = t_lane >> 6
    forced = causal & ((j_col == 0) | (j_col == cur) | (j_col == cur - 1))
    score = jnp.where(forced, FORCE_SCORE, jnp.where(causal, imp_t, -FORCE_SCORE))
    ng = n_blk // 8
    groups = [score[8 * v:8 * v + 8] for v in range(ng)]
    ranks = [jnp.zeros((8, tq), F32) for _ in range(ng)]
    sub = lax.broadcasted_iota(jnp.int32, (8, tq), 0)
    for jp in range(n_blk):
        row = jnp.broadcast_to(score[jp:jp + 1, :], (8, tq))
        vj = jp // 8
        for v in range(ng):
            if v < vj:
                ahead = row > groups[v]
            elif v > vj:
                ahead = row >= groups[v]
            else:
                ahead = (row > groups[v]) | ((row == groups[v]) & (sub > jp % 8))
            ranks[v] = ranks[v] + jnp.where(ahead, 1.0, 0.0)
    sel_t = jnp.where(jnp.concatenate(ranks, axis=0) < n_sel, 1.0, 0.0).astype(BF16)
    eye = (lax.broadcasted_iota(jnp.int32, (tq, tq), 0) == lax.broadcasted_iota(jnp.int32, (tq, tq), 1)).astype(BF16)
    return lax.dot_general(eye, sel_t, _NT, preferred_element_type=F32)


def _sum_heads(p, tq):
    out = p[0:tq]
    for r in range(1, NSA_REP):
        out = out + p[r * tq:(r + 1) * tq]
    return out


def _store_heads(o_ref, o, tq, row0):
    hd = NSA_HEAD_DIM
    for r in range(NSA_REP):
        o_ref[0, row0:row0 + tq, r * hd:(r + 1) * hd] = o[r * tq:(r + 1) * tq].astype(o_ref.dtype)


def _store_selection(sel_ref, sel, row0):
    tq, n_blk = sel.shape
    sel = sel.astype(sel_ref.dtype)
    if n_blk < NSA_HEAD_DIM:
        sel = jnp.concatenate([sel, jnp.zeros((tq, NSA_HEAD_DIM - n_blk), sel_ref.dtype)], axis=1)
    sel_ref[0, 0, row0:row0 + tq] = sel


def _normalize_aug(ol):
    return ol * pltpu.roll(1.0 / ol, NSA_HEAD_DIM, axis=1)


def _gated_store(gl_ref, gp_ref, oc_ref, o_ref, o_s, o_w, lay_ref, tq):
    rep, hd = NSA_REP, NSA_HEAD_DIM
    w = rep * hd
    gates = jax.nn.sigmoid(gl_ref[0].astype(F32))
    g_hi = gates.astype(BF16)
    g_lo = (gates - g_hi.astype(F32)).astype(BF16)
    gmap = (jnp.dot(g_hi, gp_ref[...], preferred_element_type=F32)
            + jnp.dot(g_lo, gp_ref[...], preferred_element_type=F32))
    for r in range(rep):
        lay_ref[0, :, r * hd:(r + 1) * hd] = o_s[r * tq:(r + 1) * tq, 0:hd]
        lay_ref[1, :, r * hd:(r + 1) * hd] = o_w[r * tq:(r + 1) * tq, 0:hd]
    out = (gmap[:, 0:w] * oc_ref[0].astype(F32) + gmap[:, w:2 * w] * lay_ref[0] + gmap[:, 2 * w:3 * w] * lay_ref[1])
    o_ref[0] = out.astype(o_ref.dtype)


def _gate_placement():
    m = np.zeros((LANES, 3 * NSA_REP * NSA_HEAD_DIM), np.float32)
    for br in range(3):
        for r in range(NSA_REP):
            c0 = br * NSA_REP * NSA_HEAD_DIM + r * NSA_HEAD_DIM
            m[br * NSA_REP + r, c0:c0 + NSA_HEAD_DIM] = 1.0
    return jnp.asarray(m).astype(BF16)


def _sel_fast(sh_ref, q_ref, kc_ref, vc_ref, ov_ref, oc_ref, sel_ref, *, tq, n_sel):
    rep, hd = NSA_REP, NSA_HEAD_DIM
    rows = rep * tq
    kc = kc_ref[0, 0, 0][:, 0:hd]
    ncp = kc.shape[0]
    n_idx = lax.broadcasted_iota(jnp.int32, (1, ncp), 1)
    for part in range(q_ref.shape[2] // tq):
        row0 = part * tq
        t0 = pl.program_id(2) * q_ref.shape[2] + row0
        q4 = q_ref[0, :, row0:row0 + tq, :].reshape(rows, hd)
        t_q = t0 + lax.broadcasted_iota(jnp.int32, (tq, 1), 0)
        valid_c = (n_idx * CMP_STRIDE + (CMP_BLOCK - 1) <= t_q) & (n_idx < ncp - 1)
        bias_c = jnp.where(valid_c, sh_ref[0], NEG_BIG)
        s_c = lax.dot_general(q4, kc, _NT, preferred_element_type=F32).reshape(rep, tq, ncp)
        p_c = jnp.exp(s_c + bias_c[None]).reshape(rows, ncp)
        ol_c = jnp.dot(p_c.astype(BF16), vc_ref[0, 0, 0], preferred_element_type=F32)
        inv_c = 1.0 / jnp.where(ol_c[:, hd:hd + 1] > 0, ol_c[:, hd:hd + 1], 1.0)
        _store_heads(oc_ref, ol_c[:, 0:hd] * inv_c, tq, row0)
        _store_selection(sel_ref, _select_blocks(_sum_heads(p_c * inv_c, tq), ov_ref, t0, tq, n_sel), row0)


def _sel_slow(q_ref, kc_ref, vc_ref, ov_ref, oc_ref, sel_ref, *, tq, n_sel):
    rep, hd = NSA_REP, NSA_HEAD_DIM
    rows = rep * tq
    kc = kc_ref[0, 0, 0][:, 0:hd]
    ncp = kc.shape[0]
    n_idx = lax.broadcasted_iota(jnp.int32, (1, ncp), 1)
    for part in range(q_ref.shape[2] // tq):
        row0 = part * tq
        t0 = pl.program_id(2) * q_ref.shape[2] + row0
        q4 = q_ref[0, :, row0:row0 + tq, :].reshape(rows, hd)
        t_row = t0 + (lax.broadcasted_iota(jnp.int32, (rows, 1), 0) & (tq - 1))
        s_c = lax.dot_general(q4, kc, _NT, preferred_element_type=F32)
        p_c = _softmax_rows(s_c, (n_idx * CMP_STRIDE + (CMP_BLOCK - 1) <= t_row) & (n_idx < ncp - 1))
        o_c = jnp.dot(p_c.astype(BF16), vc_ref[0, 0, 0], preferred_element_type=F32)[:, 0:hd]
        _store_heads(oc_ref, o_c, tq, row0)
        _store_selection(sel_ref, _select_blocks(_sum_heads(p_c, tq), ov_ref, t0, tq, n_sel), row0)


def _nsa_select_kernel(sh_ref, q_ref, kc_ref, vc_ref, ov_ref, oc_ref, sel_ref, *, tq, n_sel):
    data = (q_ref, kc_ref, vc_ref, ov_ref, oc_ref, sel_ref)

    @pl.when(sh_ref[3] > 0.5)
    def _():
        _sel_fast(sh_ref, *data, tq=tq, n_sel=n_sel)

    @pl.when(sh_ref[3] <= 0.5)
    def _():
        _sel_slow(*data, tq=tq, n_sel=n_sel)


def _att_fast(sh_ref, q_ref, ks_ref, vs_ref, kw_ref, vw_ref, gl_ref, gp_ref, oc_ref, sel_ref, o_ref,
              lhs_ref, acc_ref, pre_ref, lay_ref, *, tq, tk):
    rep, hd = NSA_REP, NSA_HEAD_DIM
    rows = rep * tq
    t0 = pl.program_id(2) * tq
    c_s, c_w = sh_ref[1], sh_ref[2]
    q4 = q_ref[0].reshape(rows, hd)
    t_q = t0 + lax.broadcasted_iota(jnp.int32, (tq, 1), 0)

    j_lane = lax.broadcasted_iota(jnp.int32, (1, hd), 1)
    shift = jnp.where((sel_ref[0, 0].astype(F32) > 0.5) & (j_lane * SEL_BLOCK < t0), c_s, NEG_BIG).astype(BF16)
    lhs_ref[:, 0:hd] = q4
    for r in range(rep):
        lhs_ref[r * tq:(r + 1) * tq, hd:2 * hd] = shift

    span = WINDOW + tq
    w0 = pl.multiple_of(jnp.maximum(t0 - WINDOW, 0), tq)
    kp = w0 + lax.broadcasted_iota(jnp.int32, (1, span), 1)
    bias_w = jnp.where((kp <= t_q) & (kp > t_q - WINDOW), c_w, NEG_BIG)
    s_w = lax.dot_general(q4, kw_ref[0, 0, pl.ds(w0, span), :], _NT, preferred_element_type=F32)
    p_w = jnp.exp(s_w.reshape(rep, tq, span) + bias_w[None]).reshape(rows, span)
    pre_ref[...] = jnp.dot(p_w.astype(BF16), vw_ref[0, 0, pl.ds(w0, span), :], preferred_element_type=F32)

    d0 = pl.multiple_of(t0, tq)
    kcol = t0 + lax.broadcasted_iota(jnp.int32, (1, tq), 1)
    bias_d = jnp.where(kcol <= t_q, c_s, NEG_BIG)
    s_d = lax.dot_general(q4, ks_ref[0, 0, pl.ds(d0, tq), :][:, 0:hd], _NT, preferred_element_type=F32)
    p_d = jnp.exp(s_d.reshape(rep, tq, tq) + bias_d[None]).reshape(rows, tq)
    acc_ref[...] = jnp.dot(p_d.astype(BF16), vs_ref[0, 0, pl.ds(d0, tq), :], preferred_element_type=F32)

    def kv_step(c, carry):
        k0 = pl.multiple_of(c * tk, tk)
        s = lax.dot_general(lhs_ref[...], ks_ref[0, 0, pl.ds(k0, tk), :], _NT, preferred_element_type=F32)
        acc_ref[...] += jnp.dot(jnp.exp(s).astype(BF16), vs_ref[0, 0, pl.ds(k0, tk), :],
                                preferred_element_type=F32)
        return carry

    lax.fori_loop(0, (t0 + tk - 1) // tk, kv_step, 0)

    _gated_store(gl_ref, gp_ref, oc_ref, o_ref, _normalize_aug(acc_ref[...]), _normalize_aug(pre_ref[...]),
                 lay_ref, tq)


def _att_slow(q_ref, ks_ref, vs_ref, kw_ref, vw_ref, gl_ref, gp_ref, oc_ref, sel_ref, o_ref,
              acc_ref, lay_ref, m_ref, l_ref, *, tq, tk):
    rep, hd = NSA_REP, NSA_HEAD_DIM
    rows = rep * tq
    t0 = pl.program_id(2) * tq
    q4 = q_ref[0].reshape(rows, hd)
    t_row = t0 + (lax.broadcasted_iota(jnp.int32, (rows, 1), 0) & (tq - 1))
    sel = sel_ref[0, 0]
    n_blk = sel.shape[1]

    m_ref[...] = jnp.full(m_ref.shape, NEG_BIG, F32)
    l_ref[...] = jnp.zeros(l_ref.shape, F32)
    acc_ref[...] = jnp.zeros(acc_ref.shape, F32)
    t_q = t0 + lax.broadcasted_iota(jnp.int32, (tq, 1), 0)

    def kv_step(c, carry):
        k0 = pl.multiple_of(c * tk, tk)
        kt = ks_ref[0, 0, pl.ds(k0, tk), :][:, 0:hd]
        s = lax.dot_general(q4, kt, _NT, preferred_element_type=F32).reshape(rep, tq, tk)
        kk = lax.broadcasted_iota(jnp.int32, (1, tk), 1)
        blk = (k0 >> 6) + (kk >> 6)
        expand = jnp.where(lax.broadcasted_iota(jnp.int32, (n_blk, 1), 0) == blk, 1.0, 0.0).astype(BF16)
        chosen = jnp.dot(sel, expand, preferred_element_type=F32)
        mask = ((chosen > 0.5) & (k0 + kk <= t_q))[None]
        s = jnp.where(mask, s, NEG_BIG)
        m_old = m_ref[...]
        m_new = jnp.maximum(m_old, jnp.max(s, axis=-1, keepdims=True))
        p = jnp.where(mask, jnp.exp(s - m_new), 0.0)
        alpha = jnp.exp(m_old - m_new)
        l_ref[...] = alpha * l_ref[...] + jnp.sum(p, axis=-1, keepdims=True)
        pv = jnp.dot(p.reshape(rows, tk).astype(BF16), vs_ref[0, 0, pl.ds(k0, tk), :], preferred_element_type=F32)
        acc_ref[...] = alpha.reshape(rows, 1) * acc_ref[...] + pv
        m_ref[...] = m_new
        return carry

    lax.fori_loop(0, (t0 + tq + tk - 1) // tk, kv_step, 0)
    o_s = acc_ref[:, 0:hd] / l_ref[...].reshape(rows, 1)

    span = WINDOW + tq
    w0 = pl.multiple_of(jnp.maximum(t0 - WINDOW, 0), tq)
    s_w = lax.dot_general(q4, kw_ref[0, 0, pl.ds(w0, span), :], _NT, preferred_element_type=F32)
    kp = w0 + lax.broadcasted_iota(jnp.int32, (1, span), 1)
    p_w = _softmax_rows(s_w, (kp <= t_row) & (kp > t_row - WINDOW))
    o_w = jnp.dot(p_w.astype(BF16), vw_ref[0, 0, pl.ds(w0, span), :], preferred_element_type=F32)[:, 0:hd]

    _gated_store(gl_ref, gp_ref, oc_ref, o_ref, o_s, o_w, lay_ref, tq)


def _nsa_attend_kernel(sh_ref, q_ref, ks_ref, vs_ref, kw_ref, vw_ref, gl_ref, gp_ref, oc_ref, sel_ref, o_ref,
                       lhs_ref, acc_ref, pre_ref, lay_ref, m_ref, l_ref, *, tq, tk):
    data = (q_ref, ks_ref, vs_ref, kw_ref, vw_ref, gl_ref, gp_ref, oc_ref, sel_ref, o_ref)

    @pl.when(sh_ref[3] > 0.5)
    def _():
        _att_fast(sh_ref, *data, lhs_ref, acc_ref, pre_ref, lay_ref, tq=tq, tk=tk)

    @pl.when(sh_ref[3] <= 0.5)
    def _():
        _att_slow(*data, acc_ref, lay_ref, m_ref, l_ref, tq=tq, tk=tk)


def _overlap_t(n_blk, ncp):
    c_start = np.arange(ncp) * CMP_STRIDE
    b_start = np.arange(n_blk) * SEL_BLOCK
    ov = ((c_start[None, :] < b_start[:, None] + SEL_BLOCK) & (b_start[:, None] < c_start[None, :] + CMP_BLOCK))
    ov[:, ncp - 1] = False
    return jnp.asarray(ov.astype(np.float32)).astype(BF16)


def _nsa_shifts(q_norm, k_norm):
    bound = (NSA_HEAD_DIM ** 0.5) * jnp.max(jnp.abs(q_norm)) * jnp.max(jnp.abs(k_norm), axis=-1)
    bound = bound.astype(BF16).astype(F32)
    fast = jnp.all(bound <= MAX_CONST_SHIFT).astype(F32)
    return jnp.concatenate([-bound, fast[None]])


def _nsa_select(shifts, q, cmp_kv, *, tq, parts):
    b, h, s, hd = q.shape
    g, rep = NSA_GROUPS, NSA_REP
    ncp = cmp_kv.shape[3]
    n_blk = s // SEL_BLOCK
    tb = tq * parts
    assert n_blk % 8 == 0 and n_blk <= hd and s % tb == 0
    kernel = functools.partial(_nsa_select_kernel, tq=tq, n_sel=min(N_SEL, n_blk))
    return pl.pallas_call(
        kernel,
        grid=(b, g, s // tb),
        in_specs=[
            pl.BlockSpec(memory_space=pltpu.SMEM),
            pl.BlockSpec((1, rep, tb, hd), lambda bi, gi, i: (bi, gi, i, 0)),
            pl.BlockSpec((1, 1, 1, ncp, 2 * hd), lambda bi, gi, i: (0, bi, gi, 0, 0)),
            pl.BlockSpec((1, 1, 1, ncp, 2 * hd), lambda bi, gi, i: (1, bi, gi, 0, 0)),
            pl.BlockSpec((n_blk, ncp), lambda bi, gi, i: (0, 0)),
        ],
        out_specs=[pl.BlockSpec((1, tb, rep * hd), lambda bi, gi, i: (bi, i, gi)),
                   pl.BlockSpec((1, 1, tb, hd), lambda bi, gi, i: (bi, gi, i, 0))],
        out_shape=[jax.ShapeDtypeStruct((b, s, h * hd), BF16), jax.ShapeDtypeStruct((b, g, s, hd), BF16)],
        compiler_params=_cparams("parallel", "parallel", "parallel"),
        name="nsa_select",
    )(shifts, q, cmp_kv, cmp_kv, _overlap_t(n_blk, ncp))


def _nsa_attend(shifts, q, ks, vs, kw, vw, z3, o_cmp, sel, *, tq, tk):
    b, h, s, hd = q.shape
    g, rep = NSA_GROUPS, NSA_REP
    rows = rep * tq
    assert WINDOW % tq == 0
    full = lambda w: pl.BlockSpec((1, 1, s, w), lambda bi, gi, i: (bi, gi, 0, 0))
    return pl.pallas_call(
        functools.partial(_nsa_attend_kernel, tq=tq, tk=tk),
        grid=(b, g, s // tq),
        in_specs=[
            pl.BlockSpec(memory_space=pltpu.SMEM),
            pl.BlockSpec((1, rep, tq, hd), lambda bi, gi, i: (bi, gi, i, 0)),
            full(2 * hd), full(2 * hd), full(hd), full(2 * hd),
            pl.BlockSpec((1, tq, LANES), lambda bi, gi, i: (bi, i, COL_SMALL // LANES + gi)),
            pl.BlockSpec((LANES, 3 * rep * hd), lambda bi, gi, i: (0, 0)),
            pl.BlockSpec((1, tq, rep * hd), lambda bi, gi, i: (bi, i, gi)),
            pl.BlockSpec((1, 1, tq, hd), lambda bi, gi, i: (bi, gi, i, 0)),
        ],
        out_specs=pl.BlockSpec((1, tq, rep * hd), lambda bi, gi, i: (bi, i, gi)),
        out_shape=jax.ShapeDtypeStruct((b, s, h * hd), BF16),
        scratch_shapes=[pltpu.VMEM((rows, 2 * hd), BF16), pltpu.VMEM((rows, 2 * hd), F32),
                        pltpu.VMEM((rows, 2 * hd), F32), pltpu.VMEM((2, tq, rep * hd), F32),
                        pltpu.VMEM((rep, tq, 1), F32), pltpu.VMEM((rep, tq, 1), F32)],
        compiler_params=_cparams("parallel", "parallel", "arbitrary"),
        name="nsa_attend",
    )(shifts, q, ks, vs, kw, vw, z3, _gate_placement(), o_cmp, sel)


def _head_block_diag(x, n_rows, head_of_lane, dtype):
    return jnp.concatenate([jnp.where(head_of_lane == h, x, 0.0) for h in range(GLA_HEADS)], axis=0).astype(dtype)


def _gla_kernel(q_ref, k_ref, v_ref, r_ref, sm_ref, wg_ref, bg_ref, ng_ref, tril_ref, o_ref,
                st_ref, upd_ref, oin_ref, qin_ref, *, n_chunks):
    c, sub = GLA_CHUNK, GLA_SUB
    dk, dv, nh = GLA_HEAD_DK, GLA_HEAD_DV, GLA_HEADS
    nk, nv = nh * dk, nh * dv

    @pl.when(pl.program_id(1) == 0)
    def _():
        st_ref[...] = jnp.zeros(st_ref.shape, F32)

    sm = sm_ref[0]
    x = (jnp.dot(sm, wg_ref[0], preferred_element_type=F32) + jnp.dot(sm, wg_ref[1], preferred_element_type=F32)
         + bg_ref[...])
    log_a = (jnp.minimum(x, 0.0) - jnp.log1p(jnp.exp(-jnp.abs(x)))) / GLA_TAU
    la_hi = log_a.astype(BF16)
    la_lo = (log_a - la_hi.astype(F32)).astype(BF16)
    bcum_all = (jnp.dot(tril_ref[...], la_hi, preferred_element_type=F32)
                + jnp.dot(tril_ref[...], la_lo, preferred_element_type=F32))
    head_k = lax.broadcasted_iota(jnp.int32, (1, nk), 1) >> 6
    head_v = lax.broadcasted_iota(jnp.int32, (1, nv), 1) >> 7
    state_mask = (lax.broadcasted_iota(jnp.int32, (nv, 1), 0) >> 7) == head_k
    causal = ((lax.broadcasted_iota(jnp.int32, (c, nh * c), 1) & (c - 1))
              <= lax.broadcasted_iota(jnp.int32, (c, nh * c), 0))

    decays = []
    for cc in range(n_chunks):
        rs = slice(cc * c, (cc + 1) * c)
        bcum = bcum_all[rs]
        q = q_ref[0, rs].astype(F32) * (dk ** -0.5)
        k = k_ref[0, rs].astype(F32)
        v = v_ref[0, rs].astype(F32)
        b_last = bcum[c - 1:c, :]

        score_rows = []
        for i in range(c // sub):
            lo, hi = i * sub, (i + 1) * sub
            ref = bcum[lo - 1:lo, :] if i > 0 else jnp.zeros((1, nk), F32)
            q_i = (q[lo:hi] * jnp.exp(bcum[lo:hi] - ref)).astype(BF16)
            k_i = k[0:hi] * jnp.exp(ref - bcum[0:hi])
            if hi < c:
                k_i = jnp.concatenate([k_i, jnp.zeros((c - hi, nk), F32)], axis=0)
            score_rows.append(lax.dot_general(q_i, _head_block_diag(k_i, c, head_k, BF16), _NT,
                                              preferred_element_type=F32))
        a = jnp.where(causal, jnp.concatenate(score_rows, axis=0), 0.0).astype(BF16)
        oin_ref[rs] = jnp.dot(a, _head_block_diag(v, c, head_v, BF16), preferred_element_type=F32)
        qin_ref[rs] = (q * jnp.exp(bcum)).astype(BF16)
        k_out = (k * jnp.exp(b_last - bcum)).astype(BF16)
        upd = lax.dot_general(v.astype(BF16), k_out, (((0,), (0,)), ((), ())), preferred_element_type=F32)
        upd_ref[cc] = jnp.where(state_mask, upd, 0.0)
        decays.append(jnp.exp(b_last))

    for cc in range(n_chunks):
        rs = slice(cc * c, (cc + 1) * c)
        st = st_ref[...]
        o = oin_ref[rs] + lax.dot_general(qin_ref[rs], st.astype(BF16), _NT, preferred_element_type=F32)
        st_ref[...] = st * decays[cc] + upd_ref[cc]
        r_gate = r_ref[0, rs].astype(F32)
        for h in range(nh):
            vs_ = slice(h * dv, (h + 1) * dv)
            o_h = o[:, vs_]
            ms = jnp.mean(o_h * o_h, axis=-1, keepdims=True)
            rg = r_gate[:, vs_]
            o_ref[0, rs, vs_] = (o_h * lax.rsqrt(ms + NORM_EPS) * ng_ref[...] * (rg * jax.nn.sigmoid(rg))).astype(o_ref.dtype)


def _gla(z3, w_gate, b_gate, norm_g, *, n_chunks):
    b, s, _ = z3.shape
    c = GLA_CHUNK * n_chunks
    nk = GLA_HEADS * GLA_HEAD_DK
    nv = GLA_HEADS * GLA_HEAD_DV
    wg = jnp.zeros((LANES, nk), F32).at[SMALL_GLOW_LANE:SMALL_GLOW_LANE + GLA_RANK].set(w_gate.astype(F32))
    wg_hi = wg.astype(BF16)
    wg = jnp.stack([wg_hi, (wg - wg_hi.astype(F32)).astype(BF16)])
    idx = np.arange(c)
    tril = jnp.asarray(((idx[:, None] >= idx[None, :])
                        & (idx[:, None] // GLA_CHUNK == idx[None, :] // GLA_CHUNK)).astype(np.float32)).astype(BF16)
    const = lambda shape: pl.BlockSpec(shape, lambda bi, i: (0,) * len(shape))
    return pl.pallas_call(
        functools.partial(_gla_kernel, n_chunks=n_chunks),
        grid=(b, s // c),
        in_specs=[
            pl.BlockSpec((1, c, nk), lambda bi, i: (bi, i, COL_GQ // nk)),
            pl.BlockSpec((1, c, nk), lambda bi, i: (bi, i, COL_GK // nk)),
            pl.BlockSpec((1, c, nv), lambda bi, i: (bi, i, COL_GV // nv)),
            pl.BlockSpec((1, c, nv), lambda bi, i: (bi, i, COL_GR // nv)),
            pl.BlockSpec((1, c, LANES), lambda bi, i: (bi, i, COL_SMALL // LANES)),
            const((2, LANES, nk)), const((1, nk)), const((1, GLA_HEAD_DV)), const((c, c)),
        ],
        out_specs=pl.BlockSpec((1, c, nv), lambda bi, i: (bi, i, 0)),
        out_shape=jax.ShapeDtypeStruct((b, s, nv), BF16),
        scratch_shapes=[pltpu.VMEM((nv, nk), F32), pltpu.VMEM((n_chunks, nv, nk), F32),
                        pltpu.VMEM((c, nv), F32), pltpu.VMEM((c, nk), BF16)],
        compiler_params=_cparams("parallel", "arbitrary"),
        name="gla",
    )(z3, z3, z3, z3, z3, wg, b_gate.reshape(1, nk).astype(F32), norm_g.reshape(1, GLA_HEAD_DV).astype(F32), tril)


def _mem_attn_kernel(q_ref, k_ref, v_ref, qg_ref, kg_ref, o_ref):
    dh = MEM_HEAD_DIM
    nt = (((1,), (1,)), ((), ()))
    for h in range(MEM_HEADS):
        sl = slice(h * dh, (h + 1) * dh)
        q = q_ref[0, :, sl].astype(F32)
        q = q * lax.rsqrt(jnp.mean(q * q, axis=-1, keepdims=True) + NORM_EPS) * qg_ref[...] * (dh ** -0.5)
        k = k_ref[0, :, sl].astype(F32)
        k = k * lax.rsqrt(jnp.mean(k * k, axis=-1, keepdims=True) + NORM_EPS) * kg_ref[...]
        s = lax.dot_general(q.astype(BF16), k.astype(BF16), nt, preferred_element_type=F32)
        m = jnp.max(s, axis=-1, keepdims=True)
        p = jnp.exp(s - m)
        p = p / jnp.sum(p, axis=-1, keepdims=True)
        o = jnp.dot(p.astype(BF16), v_ref[0, :, sl].astype(BF16), preferred_element_type=F32)
        o_ref[0, :, sl] = o.astype(o_ref.dtype)


def _mem_attention(z3, kv, q_norm, k_norm, *, tq):
    b, s, _ = z3.shape
    m = kv.shape[1]
    w = MEM_HEADS * MEM_HEAD_DIM
    const = lambda shape: pl.BlockSpec(shape, lambda bi, i: (0,) * len(shape))
    return pl.pallas_call(
        _mem_attn_kernel,
        grid=(b, s // tq),
        in_specs=[
            pl.BlockSpec((1, tq, w), lambda bi, i: (bi, i, COL_MQ // w)),
            pl.BlockSpec((1, m, w), lambda bi, i: (bi, 0, 0)),
            pl.BlockSpec((1, m, w), lambda bi, i: (bi, 0, 1)),
            const((1, MEM_HEAD_DIM)), const((1, MEM_HEAD_DIM)),
        ],
        out_specs=pl.BlockSpec((1, tq, w), lambda bi, i: (bi, i, 0)),
        out_shape=jax.ShapeDtypeStruct((b, s, w), BF16),
        compiler_params=_cparams("parallel", "parallel"),
        name="mem_attention",
    )(z3, kv, kv, q_norm.reshape(1, MEM_HEAD_DIM).astype(F32), k_norm.reshape(1, MEM_HEAD_DIM).astype(F32))


def _merge_kernel(x_ref, on_ref, og_ref, om_ref, m0_ref, m1_ref, m2_ref, bm_ref, wb_ref, wo_ref, o_ref):
    merged = None
    for br, (ref, mg_ref) in enumerate(((on_ref, m0_ref), (og_ref, m1_ref), (om_ref, m2_ref))):
        y = jnp.dot(ref[...], wb_ref[br], preferred_element_type=F32)
        gate = jax.nn.sigmoid(mg_ref[...].astype(F32) + bm_ref[br:br + 1, :])
        merged = gate * y if merged is None else merged + gate * y
    o_ref[...] = x_ref[...] + jnp.dot(merged.astype(BF16), wo_ref[...], preferred_element_type=F32)


def _merge_out(x2, o_nsa, o_gla, o_mem, z2, b_merge, w_branch, w_out, *, tm):
    t, d = x2.shape
    bw = BRANCH_WIDTH
    row = lambda w: pl.BlockSpec((tm, w), lambda i: (i, 0))
    gate_cols = lambda br: pl.BlockSpec((tm, d), lambda i: (i, COL_MERGE // d + br))
    return pl.pallas_call(
        _merge_kernel,
        grid=(t // tm,),
        in_specs=[
            row(d), row(bw), row(bw), row(bw),
            gate_cols(0), gate_cols(1), gate_cols(2),
            pl.BlockSpec((N_BRANCH, d), lambda i: (0, 0)),
            pl.BlockSpec((N_BRANCH, bw, d), lambda i: (0, 0, 0)),
            pl.BlockSpec((d, d), lambda i: (0, 0)),
        ],
        out_specs=row(d),
        out_shape=jax.ShapeDtypeStruct((t, d), F32),
        compiler_params=_cparams("parallel"),
        name="merge_out",
    )(x2, o_nsa, o_gla, o_mem, z2, z2, z2, b_merge.astype(F32), w_branch, w_out)


def _mlp_kernel(x_ref, g_ref, wu_ref, wd_ref, o_ref, h_ref, acc_ref):
    j = pl.program_id(1)

    @pl.when(j == 0)
    def _():
        x = x_ref[...]
        ms = jnp.mean(x * x, axis=-1, keepdims=True)
        h_ref[...] = (x * lax.rsqrt(ms + NORM_EPS) * g_ref[...]).astype(h_ref.dtype)
        acc_ref[...] = jnp.zeros(acc_ref.shape, F32)

    u = jnp.dot(h_ref[...], wu_ref[...], preferred_element_type=F32)
    u = jnp.square(jnp.maximum(u, 0.0)).astype(BF16)
    acc_ref[...] += jnp.dot(u, wd_ref[...], preferred_element_type=F32)

    @pl.when(j == pl.num_programs(1) - 1)
    def _():
        o_ref[...] = x_ref[...] + acc_ref[...]


def _mlp(x2, g, w_up, w_down, *, tm, th):
    t, d = x2.shape
    hid = w_up.shape[1]
    return pl.pallas_call(
        _mlp_kernel,
        grid=(t // tm, hid // th),
        in_specs=[
            pl.BlockSpec((tm, d), lambda i, j: (i, 0)),
            pl.BlockSpec((1, d), lambda i, j: (0, 0)),
            pl.BlockSpec((d, th), lambda i, j: (0, j)),
            pl.BlockSpec((th, d), lambda i, j: (j, 0)),
        ],
        out_specs=pl.BlockSpec((tm, d), lambda i, j: (i, 0)),
        out_shape=jax.ShapeDtypeStruct((t, d), F32),
        scratch_shapes=[pltpu.VMEM((tm, d), BF16), pltpu.VMEM((tm, d), F32)],
        compiler_params=_cparams("parallel", "arbitrary"),
        name="mlp",
    )(x2, g.reshape(1, d).astype(F32), w_up, w_down)


def _tile(n, pref):
    t = min(n, pref)
    assert n % t == 0, (n, pref)
    return t


def _layer(x, mem2, pos3, pos_cmp, perm, p):
    b, s, d = x.shape
    t = b * s
    hd, g = NSA_HEAD_DIM, NSA_GROUPS
    x2 = x.reshape(t, d)

    w_in = jnp.take(jnp.pad(p["w_in"], ((0, 0), (0, 1))), perm, axis=1).astype(BF16)
    z2 = _norm_matmul(x2, p["ln_mix"].astype(F32), w_in, tm=_tile(t, 1024), tn=D_IN_PAD // 4, out_dtype=BF16, name="in_proj")
    z3 = z2.reshape(b, s, D_IN_PAD)

    q, ks, vs, kw, vw = _nsa_prep(z3, pos3, p["nsa_q_norm"].astype(F32), p["nsa_k_norm"][1].astype(F32),
                                  p["nsa_k_norm"][2].astype(F32), tp=_tile(s, 256))
    nc = s // CMP_STRIDE
    kvc = z3[:, :, COL_KV:COL_KV + 2 * g * hd].reshape(b, s, 2, g, hd)
    xc = kvc.transpose(2, 0, 3, 1, 4).reshape(2, b, g, nc, CMP_STRIDE * hd)
    w1 = p["cmp_w1"]
    half = CMP_STRIDE * hd
    w1cat = jnp.concatenate([w1[:, :half], w1[:, half:]], axis=-1).astype(BF16)
    pe2 = jnp.pad(p["cmp_pe"].reshape(2, 2, half), ((0, 0), (0, 6), (0, 0))).astype(F32)
    cmp_kv = _nsa_compress(xc, w1cat, p["cmp_w2"].astype(BF16), pe2, pos_cmp, p["nsa_k_norm"][0].astype(F32))
    shifts = _nsa_shifts(p["nsa_q_norm"].astype(F32), p["nsa_k_norm"].astype(F32))
    o_cmp, sel = _nsa_select(shifts, q, cmp_kv, tq=_tile(s, 256), parts=2)
    o_nsa = _nsa_attend(shifts, q, ks, vs, kw, vw, z3, o_cmp, sel, tq=_tile(s, 256), tk=_tile(s, 512))

    o_gla = _gla(z3, p["gla_w_gate"], p["gla_b_gate"], p["gla_norm"], n_chunks=8)

    kv = _norm_matmul(mem2, p["mem_norm"].astype(F32), p["mem_w_kv"].astype(BF16),
                      tm=_tile(mem2.shape[0], 512), tn=512, out_dtype=BF16, name="mem_kv")
    kv = kv.reshape(b, mem2.shape[0] // b, 2 * MEM_HEADS * MEM_HEAD_DIM)
    o_mem = _mem_attention(z3, kv, p["mem_q_norm"], p["mem_k_norm"], tq=_tile(s, 512))

    x2 = _merge_out(x2, o_nsa.reshape(t, -1), o_gla.reshape(t, -1), o_mem.reshape(t, -1), z2,
                    p["b_merge"], p["w_branch"].astype(BF16), p["w_out"].astype(BF16), tm=_tile(t, 512))
    x2 = _mlp(x2, p["ln_mlp"], p["w_up"].astype(BF16), p["w_down"].astype(BF16), tm=_tile(t, 1024), th=512)
    return x2.reshape(b, s, d)


def kernel(x, mem, positions, ln_mix, w_in, b_merge, nsa_q_norm, nsa_k_norm, cmp_pe, cmp_w1, cmp_w2,
           gla_w_gate, gla_b_gate, gla_norm, mem_norm, mem_w_kv, mem_q_norm, mem_k_norm, w_branch, w_out,
           ln_mlp, w_up, w_down):
    b, s, d = x.shape
    assert d == 1024 and s % WINDOW == 0 and s >= 2 * WINDOW
    depth = w_in.shape[0]
    perm_np, d_in = _in_proj_permutation(d)
    assert w_in.shape[2] == d_in
    perm = jnp.asarray(perm_np)
    pos3 = positions.astype(jnp.int32).reshape(b, s, 1)
    nc = s // CMP_STRIDE
    cmp_end = np.minimum(np.arange(nc) * CMP_STRIDE + CMP_BLOCK - 1, s - 1)
    pos_cmp = pos3[:, cmp_end, :]
    mem2 = mem.reshape(b * mem.shape[1], d)
    names = ("ln_mix", "w_in", "b_merge", "nsa_q_norm", "nsa_k_norm", "cmp_pe", "cmp_w1", "cmp_w2",
             "gla_w_gate", "gla_b_gate", "gla_norm", "mem_norm", "mem_w_kv", "mem_q_norm", "mem_k_norm",
             "w_branch", "w_out", "ln_mlp", "w_up", "w_down")
    stacked = (ln_mix, w_in, b_merge, nsa_q_norm, nsa_k_norm, cmp_pe, cmp_w1, cmp_w2, gla_w_gate, gla_b_gate,
               gla_norm, mem_norm, mem_w_kv, mem_q_norm, mem_k_norm, w_branch, w_out, ln_mlp, w_up, w_down)
    for l in range(depth):
        x = _layer(x, mem2, pos3, pos_cmp, perm, {n: a[l] for n, a in zip(names, stacked)})
    return x
```

```python
import functools

import numpy as np
import jax
import jax.numpy as jnp
from jax import lax
from jax.experimental import pallas as pl
from jax.experimental.pallas import tpu as pltpu

NSA_HEADS = 8
NSA_GROUPS = 2
NSA_REP = NSA_HEADS // NSA_GROUPS
NSA_HEAD_DIM = 64
CMP_BLOCK = 32
CMP_STRIDE = 16
CMP_HIDDEN = 4 * NSA_HEAD_DIM
SEL_BLOCK = 64
N_SEL = 16
WINDOW = 512
FORCE_SCORE = 1e4
GLA_HEADS = 4
GLA_HEAD_DK = 64
GLA_HEAD_DV = 128
GLA_RANK = 16
GLA_TAU = 16.0
GLA_CHUNK = 64
GLA_SUB = 16
MEM_HEADS = 4
MEM_HEAD_DIM = 128
N_BRANCH = 3
BRANCH_WIDTH = 512
ROPE_THETA = 500000.0
ROPE_ROT = NSA_HEAD_DIM // 4
ROPE_HALF = ROPE_ROT // 2
NORM_EPS = 1e-6

LANES = 128
VMEM_LIMIT_BYTES = 48 * 1024 * 1024

F32 = jnp.float32
BF16 = jnp.bfloat16
HIGHEST = lax.Precision.HIGHEST
NEG_BIG = -1e30
MAX_CONST_SHIFT = 40.0

COL_NQ = 0
COL_GV = 512
COL_GR = 1024
COL_MQ = 1536
COL_MERGE = 2048
COL_GQ = 5120
COL_GK = 5376
COL_KV = 5632
COL_SMALL = 6400
D_IN_PAD = 6656
SMALL_GLOW_LANE = 12


def _in_proj_permutation(d_model):
    sizes = (512, 128, 128, 128, 128, 128, 128, 24, 256, 256, 512, 512, 16, 512, 3 * d_model)
    off = np.concatenate([[0], np.cumsum(sizes)])
    (o_nq, o_kc, o_vc, o_ks, o_vs, o_kw, o_vw, o_ng, o_gq, o_gk, o_gv, o_gr, o_gl, o_mq, o_mg) = off[:-1]
    d_in = int(off[-1])
    perm = np.full((D_IN_PAD,), d_in, np.int32)

    def put(new, old, n):
        perm[new:new + n] = np.arange(old, old + n)

    put(COL_NQ, o_nq, 512)
    put(COL_GV, o_gv, 512)
    put(COL_GR, o_gr, 512)
    put(COL_MQ, o_mq, 512)
    put(COL_MERGE, o_mg, 3 * d_model)
    put(COL_GQ, o_gq, 256)
    put(COL_GK, o_gk, 256)
    put(COL_KV, o_kc, 768)
    for g in range(NSA_GROUPS):
        for br in range(3):
            for r in range(NSA_REP):
                perm[COL_SMALL + g * LANES + br * NSA_REP + r] = o_ng + (g * NSA_REP + r) * 3 + br
    put(COL_SMALL + SMALL_GLOW_LANE, o_gl, GLA_RANK)
    return perm, d_in


def _cparams(*sem):
    return pltpu.CompilerParams(dimension_semantics=sem, vmem_limit_bytes=VMEM_LIMIT_BYTES)


def _norm_matmul_kernel(x_ref, g_ref, w_ref, o_ref, h_ref):
    @pl.when(pl.program_id(1) == 0)
    def _():
        x = x_ref[...].astype(F32)
        ms = jnp.mean(x * x, axis=-1, keepdims=True)
        h_ref[...] = (x * lax.rsqrt(ms + NORM_EPS) * g_ref[...]).astype(h_ref.dtype)

    o_ref[...] = jnp.dot(h_ref[...], w_ref[...], preferred_element_type=F32).astype(o_ref.dtype)


def _norm_matmul(x, g, w, *, tm, tn, out_dtype, name):
    m, k = x.shape
    n = w.shape[1]
    return pl.pallas_call(
        _norm_matmul_kernel,
        grid=(m // tm, n // tn),
        in_specs=[
            pl.BlockSpec((tm, k), lambda i, j: (i, 0)),
            pl.BlockSpec((1, k), lambda i, j: (0, 0)),
            pl.BlockSpec((k, tn), lambda i, j: (0, j)),
        ],
        out_specs=pl.BlockSpec((tm, tn), lambda i, j: (i, j)),
        out_shape=jax.ShapeDtypeStruct((m, n), out_dtype),
        scratch_shapes=[pltpu.VMEM((tm, k), BF16)],
        compiler_params=_cparams("parallel", "arbitrary"),
        name=name,
    )(x, g.reshape(1, k), w)


def _rope_tables(pos_row, freq_col, place):
    ang = freq_col * pos_row
    tn = (((0,), (0,)), ((), ()))
    lane = lax.broadcasted_iota(jnp.int32, (1, LANES), 1)
    c = lax.dot_general(jnp.cos(ang), place, tn, preferred_element_type=F32, precision=HIGHEST)
    s = lax.dot_general(jnp.sin(ang), place, tn, preferred_element_type=F32, precision=HIGHEST)
    return c + jnp.where((lane & (NSA_HEAD_DIM - 1)) >= ROPE_ROT, 1.0, 0.0), s


def _rope_lanes(x, c, s):
    n = x.shape[-1]
    if n > LANES:
        c = jnp.concatenate([c] * (n // LANES), axis=1)
        s = jnp.concatenate([s] * (n // LANES), axis=1)
    lane = lax.broadcasted_iota(jnp.int32, (1, n), 1) & (NSA_HEAD_DIM - 1)
    up = pltpu.roll(x, n - ROPE_HALF, axis=1)
    dn = pltpu.roll(x, ROPE_HALF, axis=1)
    y = jnp.where(lane < ROPE_HALF, -up, jnp.where(lane < ROPE_ROT, dn, 0.0))
    return x * c + y * s


def _head_rms(x, bd, g):
    x2 = x * x
    hi = x2.astype(BF16)
    lo = (x2 - hi.astype(F32)).astype(BF16)
    ms = (jnp.dot(hi, bd, preferred_element_type=F32) + jnp.dot(lo, bd, preferred_element_type=F32))
    return x * lax.rsqrt(ms * (1.0 / NSA_HEAD_DIM) + NORM_EPS) * g


def _nsa_prep_kernel(q_ref, ks_ref, vs_ref, kw_ref, vw_ref, pos_ref, qg_ref, ksg_ref, kwg_ref,
                     bdq_ref, bdk_ref, f_ref, place_ref,
                     qo_ref, kso_ref, vso_ref, kwo_ref, vwo_ref):
    hd = NSA_HEAD_DIM
    tp = q_ref.shape[1]
    c, s = _rope_tables(pos_ref[0].astype(F32), f_ref[...], place_ref[...])
    q = _head_rms(q_ref[0].astype(F32), bdq_ref[...], qg_ref[...])
    q = (_rope_lanes(q, c, s) * (hd ** -0.5)).astype(qo_ref.dtype)
    for h in range(NSA_HEADS):
        qo_ref[0, h] = q[:, h * hd:(h + 1) * hd]
    ks = _rope_lanes(_head_rms(ks_ref[0].astype(F32), bdk_ref[...], ksg_ref[...]), c, s)
    kw = _rope_lanes(_head_rms(kw_ref[0].astype(F32), bdk_ref[...], kwg_ref[...]), c, s)
    vs = vs_ref[0]
    vw = vw_ref[0]
    tok = pl.program_id(1) * tp + lax.broadcasted_iota(jnp.int32, (tp, hd), 0)
    lane = lax.broadcasted_iota(jnp.int32, (tp, hd), 1)
    blk_onehot = jnp.where((tok >> 6) == lane, 1.0, 0.0).astype(kso_ref.dtype)
    ones_col = jnp.ones((tp, hd), vso_ref.dtype)
    for g in range(NSA_GROUPS):
        sl = slice(g * hd, (g + 1) * hd)
        kso_ref[0, g, :, 0:hd] = ks[:, sl].astype(kso_ref.dtype)
        kso_ref[0, g, :, hd:2 * hd] = blk_onehot
        kwo_ref[0, g] = kw[:, sl].astype(kwo_ref.dtype)
        vso_ref[0, g, :, 0:hd] = vs[:, sl].astype(vso_ref.dtype)
        vso_ref[0, g, :, hd:2 * hd] = ones_col
        vwo_ref[0, g, :, 0:hd] = vw[:, sl].astype(vwo_ref.dtype)
        vwo_ref[0, g, :, hd:2 * hd] = ones_col


def _block_diag_ones(n, width):
    i = np.arange(n)
    return jnp.asarray((i[:, None] // width == i[None, :] // width).astype(np.float32)).astype(BF16)


def _rope_freq_col():
    inv = ROPE_THETA ** (-(np.arange(ROPE_HALF, dtype=np.float64)) / ROPE_HALF)
    return jnp.asarray(inv.astype(np.float32)).reshape(ROPE_HALF, 1)


def _rope_placement():
    m = np.zeros((ROPE_HALF, LANES), np.float32)
    for f in range(ROPE_HALF):
        for base in range(0, LANES, NSA_HEAD_DIM):
            m[f, base + f] = 1.0
            m[f, base + f + ROPE_HALF] = 1.0
    return jnp.asarray(m)


def _nsa_prep(z3, pos_rows, q_norm, ks_norm, kw_norm, *, tp):
    b, s, _ = z3.shape
    hd, g = NSA_HEAD_DIM, NSA_GROUPS
    assert SEL_BLOCK == 64 and s // SEL_BLOCK <= hd
    kvb = COL_KV // LANES
    qg = jnp.tile(q_norm, NSA_HEADS).reshape(1, NSA_HEADS * hd)
    ksg = jnp.tile(ks_norm, g).reshape(1, g * hd)
    kwg = jnp.tile(kw_norm, g).reshape(1, g * hd)
    const = lambda shape: pl.BlockSpec(shape, lambda bi, i: (0,) * len(shape))
    plain_out = jax.ShapeDtypeStruct((b, g, s, hd), BF16)
    plain_spec = pl.BlockSpec((1, g, tp, hd), lambda bi, i: (bi, 0, i, 0))
    aug_out = jax.ShapeDtypeStruct((b, g, s, 2 * hd), BF16)
    aug_spec = pl.BlockSpec((1, g, tp, 2 * hd), lambda bi, i: (bi, 0, i, 0))
    return pl.pallas_call(
        _nsa_prep_kernel,
        grid=(b, s // tp),
        in_specs=[
            pl.BlockSpec((1, tp, 512), lambda bi, i: (bi, i, COL_NQ // 512)),
            pl.BlockSpec((1, tp, LANES), lambda bi, i: (bi, i, kvb + 2)),
            pl.BlockSpec((1, tp, LANES), lambda bi, i: (bi, i, kvb + 3)),
            pl.BlockSpec((1, tp, LANES), lambda bi, i: (bi, i, kvb + 4)),
            pl.BlockSpec((1, tp, LANES), lambda bi, i: (bi, i, kvb + 5)),
            pl.BlockSpec((1, 1, tp), lambda bi, i: (bi, 0, i)),
            const((1, 512)), const((1, LANES)), const((1, LANES)),
            const((512, 512)), const((LANES, LANES)),
            const((ROPE_HALF, 1)), const((ROPE_HALF, LANES)),
        ],
        out_specs=[
            pl.BlockSpec((1, NSA_HEADS, tp, hd), lambda bi, i: (bi, 0, i, 0)),
            aug_spec, aug_spec, plain_spec, aug_spec,
        ],
        out_shape=[jax.ShapeDtypeStruct((b, NSA_HEADS, s, hd), BF16), aug_out, aug_out, plain_out, aug_out],
        compiler_params=_cparams("parallel", "parallel"),
        name="nsa_prep",
    )(z3, z3, z3, z3, z3, pos_rows, qg, ksg, kwg,
      _block_diag_ones(512, hd), _block_diag_ones(LANES, hd), _rope_freq_col(), _rope_placement())


def _gelu_tanh(x):
    return 0.5 * x * (1.0 + jnp.tanh(np.sqrt(2.0 / np.pi) * (x + 0.044715 * x * x * x)))


def _nsa_cmp_kernel(x_ref, w1_ref, w2_ref, pe_ref, pos_ref, g_ref, rot_ref, f_ref, o_ref):
    kind = pl.program_id(0)
    nc = x_ref.shape[3]
    x = x_ref[0, 0, 0].astype(BF16)
    w1 = w1_ref[0]
    ab = jnp.dot(x, w1, preferred_element_type=F32)
    r = jnp.dot(pe_ref[0].astype(BF16), w1, preferred_element_type=F32)
    a = ab[:, :CMP_HIDDEN] + r[0:1, :CMP_HIDDEN]
    bm = ab[:, CMP_HIDDEN:] + r[1:2, CMP_HIDDEN:]
    hid = _gelu_tanh(a + pltpu.roll(bm, nc - 1, axis=0))
    comp = jnp.dot(hid.astype(BF16), w2_ref[0], preferred_element_type=F32)

    hd = NSA_HEAD_DIM
    lane = lax.broadcasted_iota(jnp.int32, (nc, hd), 1)

    @pl.when(kind == 0)
    def _():
        ms = jnp.mean(comp * comp, axis=-1, keepdims=True)
        kn = comp * lax.rsqrt(ms + NORM_EPS) * g_ref[...]
        ang = pos_ref[0].astype(F32) * f_ref[...]
        y = jnp.dot(kn, rot_ref[...], preferred_element_type=F32, precision=HIGHEST)
        o_ref[0, 0, 0, :, 0:hd] = (kn * jnp.cos(ang) + y * jnp.sin(ang)).astype(o_ref.dtype)
        o_ref[0, 0, 0, :, hd:2 * hd] = jnp.zeros((nc, hd), o_ref.dtype)

    @pl.when(kind != 0)
    def _():
        o_ref[0, 0, 0, :, 0:hd] = comp.astype(o_ref.dtype)
        o_ref[0, 0, 0, :, hd:2 * hd] = jnp.ones((nc, hd), o_ref.dtype)


def _rope_freq_head():
    p = np.arange(NSA_HEAD_DIM)
    inv = ROPE_THETA ** (-(np.arange(ROPE_HALF, dtype=np.float64)) / ROPE_HALF)
    f = np.where(p < ROPE_ROT, inv[p % ROPE_HALF], 0.0)
    return jnp.asarray(f.astype(np.float32)).reshape(1, NSA_HEAD_DIM)


def _rope_rot_matrix(n):
    m = np.zeros((n, n), np.float32)
    for base in range(0, n, NSA_HEAD_DIM):
        for l in range(ROPE_HALF):
            m[base + l + ROPE_HALF, base + l] = -1.0
            m[base + l, base + l + ROPE_HALF] = 1.0
    return jnp.asarray(m)


def _nsa_compress(xc, w1cat, w2, pe2, pos_cmp, k_norm0):
    _, b, g, nc, _ = xc.shape
    hd = NSA_HEAD_DIM
    const = lambda shape: pl.BlockSpec(shape, lambda k, bi, gi: (0,) * len(shape))
    return pl.pallas_call(
        _nsa_cmp_kernel,
        grid=(2, b, g),
        in_specs=[
            pl.BlockSpec((1, 1, 1, nc, CMP_STRIDE * hd), lambda k, bi, gi: (k, bi, gi, 0, 0)),
            pl.BlockSpec((1, CMP_STRIDE * hd, 2 * CMP_HIDDEN), lambda k, bi, gi: (k, 0, 0)),
            pl.BlockSpec((1, CMP_HIDDEN, hd), lambda k, bi, gi: (k, 0, 0)),
            pl.BlockSpec((1, 8, CMP_STRIDE * hd), lambda k, bi, gi: (k, 0, 0)),
            pl.BlockSpec((1, nc, 1), lambda k, bi, gi: (bi, 0, 0)),
            const((1, hd)), const((hd, hd)), const((1, hd)),
        ],
        out_specs=pl.BlockSpec((1, 1, 1, nc, 2 * hd), lambda k, bi, gi: (k, bi, gi, 0, 0)),
        out_shape=jax.ShapeDtypeStruct((2, b, g, nc, 2 * hd), BF16),
        compiler_params=_cparams("parallel", "parallel", "parallel"),
        name="nsa_compress",
    )(xc, w1cat, w2, pe2, pos_cmp, k_norm0.reshape(1, hd), _rope_rot_matrix(hd), _rope_freq_head())


def _softmax_rows(s, mask):
    s = jnp.where(mask, s, NEG_BIG)
    m = jnp.max(s, axis=-1, keepdims=True)
    p = jnp.where(mask, jnp.exp(s - m), 0.0)
    d = jnp.sum(p, axis=-1, keepdims=True)
    return p / jnp.where(d > 0, d, 1.0)


_NT = (((1,), (1,)), ((), ()))


def _select_blocks(p_sum, ov_ref, t0, tq, n_sel):
    n_blk = ov_ref.shape[0]
    p_hi = p_sum.astype(BF16)
    p_lo = (p_sum - p_hi.astype(F32)).astype(BF16)
    imp_t = (lax.dot_general(ov_ref[...], p_hi, _NT, preferred_element_type=F32)
             + lax.dot_general(ov_ref[...], p_lo, _NT, preferred_element_type=F32))
    j_col = lax.broadcasted_iota(jnp.int32, (n_blk, 1), 0)
    t_lane = t0 + lax.broadcasted_iota(jnp.int32, (1, tq), 1)
    causal = j_col * SEL_BLOCK <= t_lane
    cur = t_lane >> 6
    forced = causal & ((j_col == 0) | (j_col == cur) | (j_col == cur - 1))
    score = jnp.where(forced, FORCE_SCORE, jnp.where(causal, imp_t, -FORCE_SCORE))
    ng = n_blk // 8
    groups = [score[8 * v:8 * v + 8] for v in range(ng)]
    ranks = [jnp.zeros((8, tq), F32) for _ in range(ng)]
    sub = lax.broadcasted_iota(jnp.int32, (8, tq), 0)
    for jp in range(n_blk):
        row = jnp.broadcast_to(score[jp:jp + 1, :], (8, tq))
        vj = jp // 8
        for v in range(ng):
            if v < vj:
                ahead = row > groups[v]
            elif v > vj:
                ahead = row >= groups[v]
            else:
                ahead = (row > groups[v]) | ((row == groups[v]) & (sub > jp % 8))
            ranks[v] = ranks[v] + jnp.where(ahead, 1.0, 0.0)
    sel_t = jnp.where(jnp.concatenate(ranks, axis=0) < n_sel, 1.0, 0.0).astype(BF16)
    eye = (lax.broadcasted_iota(jnp.int32, (tq, tq), 0) == lax.broadcasted_iota(jnp.int32, (tq, tq), 1)).astype(BF16)
    return lax.dot_general(eye, sel_t, _NT, preferred_element_type=F32)


def _sum_heads(p, tq):
    out = p[0:tq]
    for r in range(1, NSA_REP):
        out = out + p[r * tq:(r + 1) * tq]
    return out


def _store_heads(o_ref, o, tq, row0):
    hd = NSA_HEAD_DIM
    for r in range(NSA_REP):
        o_ref[0, row0:row0 + tq, r * hd:(r + 1) * hd] = o[r * tq:(r + 1) * tq].astype(o_ref.dtype)


def _store_selection(sel_ref, sel, row0):
    tq, n_blk = sel.shape
    sel = sel.astype(sel_ref.dtype)
    if n_blk < NSA_HEAD_DIM:
        sel = jnp.concatenate([sel, jnp.zeros((tq, NSA_HEAD_DIM - n_blk), sel_ref.dtype)], axis=1)
    sel_ref[0, 0, row0:row0 + tq] = sel


def _normalize_aug(ol):
    return ol * pltpu.roll(1.0 / ol, NSA_HEAD_DIM, axis=1)


def _gate_maps(gl_ref, gp_ref, gm_ref):
    gates = jax.nn.sigmoid(gl_ref[0].astype(F32))
    g_hi = gates.astype(BF16)
    g_lo = (gates - g_hi.astype(F32)).astype(BF16)
    gm_ref[...] = (jnp.dot(g_hi, gp_ref[...], preferred_element_type=F32)
                   + jnp.dot(g_lo, gp_ref[...], preferred_element_type=F32))


def _to_lane_layout(lay_ref, idx, o, tq):
    hd = NSA_HEAD_DIM
    for r in range(NSA_REP):
        lay_ref[idx, :, r * hd:(r + 1) * hd] = o[r * tq:(r + 1) * tq, 0:hd]


def _gated_store(gm_ref, oc_ref, lay_ref, o_ref):
    w = NSA_REP * NSA_HEAD_DIM
    out = (gm_ref[:, 0:w] * oc_ref[0].astype(F32) + gm_ref[:, w:2 * w] * lay_ref[0]
           + gm_ref[:, 2 * w:3 * w] * lay_ref[1])
    o_ref[0] = out.astype(o_ref.dtype)


def _gate_placement():
    m = np.zeros((LANES, 3 * NSA_REP * NSA_HEAD_DIM), np.float32)
    for br in range(3):
        for r in range(NSA_REP):
            c0 = br * NSA_REP * NSA_HEAD_DIM + r * NSA_HEAD_DIM
            m[br * NSA_REP + r, c0:c0 + NSA_HEAD_DIM] = 1.0
    return jnp.asarray(m).astype(BF16)


def _sel_fast(sh_ref, q_ref, kc_ref, vc_ref, ov_ref, oc_ref, sel_ref, *, tq, n_sel):
    rep, hd = NSA_REP, NSA_HEAD_DIM
    rows = rep * tq
    kc = kc_ref[0, 0, 0][:, 0:hd]
    ncp = kc.shape[0]
    n_idx = lax.broadcasted_iota(jnp.int32, (1, ncp), 1)
    for part in range(q_ref.shape[2] // tq):
        row0 = part * tq
        t0 = pl.program_id(2) * q_ref.shape[2] + row0
        q4 = q_ref[0, :, row0:row0 + tq, :].reshape(rows, hd)
        t_q = t0 + lax.broadcasted_iota(jnp.int32, (tq, 1), 0)
        valid_c = (n_idx * CMP_STRIDE + (CMP_BLOCK - 1) <= t_q) & (n_idx < ncp - 1)
        bias_c = jnp.where(valid_c, sh_ref[0], NEG_BIG)
        s_c = lax.dot_general(q4, kc, _NT, preferred_element_type=F32).reshape(rep, tq, ncp)
        p_c = jnp.exp(s_c + bias_c[None]).reshape(rows, ncp)
        ol_c = jnp.dot(p_c.astype(BF16), vc_ref[0, 0, 0], preferred_element_type=F32)
        inv_c = 1.0 / jnp.where(ol_c[:, hd:hd + 1] > 0, ol_c[:, hd:hd + 1], 1.0)
        _store_heads(oc_ref, ol_c[:, 0:hd] * inv_c, tq, row0)
        _store_selection(sel_ref, _select_blocks(_sum_heads(p_c * inv_c, tq), ov_ref, t0, tq, n_sel), row0)


def _sel_slow(q_ref, kc_ref, vc_ref, ov_ref, oc_ref, sel_ref, *, tq, n_sel):
    rep, hd = NSA_REP, NSA_HEAD_DIM
    rows = rep * tq
    kc = kc_ref[0, 0, 0][:, 0:hd]
    ncp = kc.shape[0]
    n_idx = lax.broadcasted_iota(jnp.int32, (1, ncp), 1)
    for part in range(q_ref.shape[2] // tq):
        row0 = part * tq
        t0 = pl.program_id(2) * q_ref.shape[2] + row0
        q4 = q_ref[0, :, row0:row0 + tq, :].reshape(rows, hd)
        t_row = t0 + (lax.broadcasted_iota(jnp.int32, (rows, 1), 0) & (tq - 1))
        s_c = lax.dot_general(q4, kc, _NT, preferred_element_type=F32)
        p_c = _softmax_rows(s_c, (n_idx * CMP_STRIDE + (CMP_BLOCK - 1) <= t_row) & (n_idx < ncp - 1))
        o_c = jnp.dot(p_c.astype(BF16), vc_ref[0, 0, 0], preferred_element_type=F32)[:, 0:hd]
        _store_heads(oc_ref, o_c, tq, row0)
        _store_selection(sel_ref, _select_blocks(_sum_heads(p_c, tq), ov_ref, t0, tq, n_sel), row0)


def _nsa_select_kernel(sh_ref, q_ref, kc_ref, vc_ref, ov_ref, oc_ref, sel_ref, *, tq, n_sel):
    data = (q_ref, kc_ref, vc_ref, ov_ref, oc_ref, sel_ref)

    @pl.when(sh_ref[3] > 0.5)
    def _():
        _sel_fast(sh_ref, *data, tq=tq, n_sel=n_sel)

    @pl.when(sh_ref[3] <= 0.5)
    def _():
        _sel_slow(*data, tq=tq, n_sel=n_sel)


def _att_fast(sh_ref, q_ref, ks_ref, vs_ref, kw_ref, vw_ref, gl_ref, gp_ref, oc_ref, sel_ref, o_ref,
              lhs_ref, acc_ref, pre_ref, gm_ref, lay_ref, *, tq, tk):
    rep, hd = NSA_REP, NSA_HEAD_DIM
    rows = rep * tq
    t0 = pl.program_id(2) * tq
    c_s, c_w = sh_ref[1], sh_ref[2]
    q4 = q_ref[0].reshape(rows, hd)
    t_q = t0 + lax.broadcasted_iota(jnp.int32, (tq, 1), 0)

    j_lane = lax.broadcasted_iota(jnp.int32, (1, hd), 1)
    shift = jnp.where((sel_ref[0, 0].astype(F32) > 0.5) & (j_lane * SEL_BLOCK < t0), c_s, NEG_BIG).astype(BF16)
    lhs_ref[:, 0:hd] = q4
    for r in range(rep):
        lhs_ref[r * tq:(r + 1) * tq, hd:2 * hd] = shift

    span = WINDOW + tq
    w0 = pl.multiple_of(jnp.maximum(t0 - WINDOW, 0), tq)
    kp = w0 + lax.broadcasted_iota(jnp.int32, (1, span), 1)
    bias_w = jnp.where((kp <= t_q) & (kp > t_q - WINDOW), c_w, NEG_BIG)
    s_w = lax.dot_general(q4, kw_ref[0, 0, pl.ds(w0, span), :], _NT, preferred_element_type=F32)
    p_w = jnp.exp(s_w.reshape(rep, tq, span) + bias_w[None]).reshape(rows, span)
    pre_ref[...] = jnp.dot(p_w.astype(BF16), vw_ref[0, 0, pl.ds(w0, span), :], preferred_element_type=F32)

    d0 = pl.multiple_of(t0, tq)
    kcol = t0 + lax.broadcasted_iota(jnp.int32, (1, tq), 1)
    bias_d = jnp.where(kcol <= t_q, c_s, NEG_BIG)
    s_d = lax.dot_general(q4, ks_ref[0, 0, pl.ds(d0, tq), :][:, 0:hd], _NT, preferred_element_type=F32)
    p_d = jnp.exp(s_d.reshape(rep, tq, tq) + bias_d[None]).reshape(rows, tq)
    acc_ref[...] = jnp.dot(p_d.astype(BF16), vs_ref[0, 0, pl.ds(d0, tq), :], preferred_element_type=F32)

    def kv_step(c, carry):
        k0 = pl.multiple_of(c * tk, tk)
        s = lax.dot_general(lhs_ref[...], ks_ref[0, 0, pl.ds(k0, tk), :], _NT, preferred_element_type=F32)
        acc_ref[...] += jnp.dot(jnp.exp(s).astype(BF16), vs_ref[0, 0, pl.ds(k0, tk), :],
                                preferred_element_type=F32)
        return carry

    lax.fori_loop(0, (t0 + tk - 1) // tk, kv_step, 0)

    _to_lane_layout(lay_ref, 0, _normalize_aug(acc_ref[...]), tq)
    _to_lane_layout(lay_ref, 1, _normalize_aug(pre_ref[...]), tq)
    _gate_maps(gl_ref, gp_ref, gm_ref)
    _gated_store(gm_ref, oc_ref, lay_ref, o_ref)


def _att_slow(q_ref, ks_ref, vs_ref, kw_ref, vw_ref, gl_ref, gp_ref, oc_ref, sel_ref, o_ref,
              acc_ref, gm_ref, lay_ref, m_ref, l_ref, *, tq, tk):
    rep, hd = NSA_REP, NSA_HEAD_DIM
    rows = rep * tq
    t0 = pl.program_id(2) * tq
    q4 = q_ref[0].reshape(rows, hd)
    t_row = t0 + (lax.broadcasted_iota(jnp.int32, (rows, 1), 0) & (tq - 1))
    sel = sel_ref[0, 0]
    n_blk = sel.shape[1]

    m_ref[...] = jnp.full(m_ref.shape, NEG_BIG, F32)
    l_ref[...] = jnp.zeros(l_ref.shape, F32)
    acc_ref[...] = jnp.zeros(acc_ref.shape, F32)
    t_q = t0 + lax.broadcasted_iota(jnp.int32, (tq, 1), 0)

    def kv_step(c, carry):
        k0 = pl.multiple_of(c * tk, tk)
        kt = ks_ref[0, 0, pl.ds(k0, tk), :][:, 0:hd]
        s = lax.dot_general(q4, kt, _NT, preferred_element_type=F32).reshape(rep, tq, tk)
        kk = lax.broadcasted_iota(jnp.int32, (1, tk), 1)
        blk = (k0 >> 6) + (kk >> 6)
        expand = jnp.where(lax.broadcasted_iota(jnp.int32, (n_blk, 1), 0) == blk, 1.0, 0.0).astype(BF16)
        chosen = jnp.dot(sel, expand, preferred_element_type=F32)
        mask = ((chosen > 0.5) & (k0 + kk <= t_q))[None]
        s = jnp.where(mask, s, NEG_BIG)
        m_old = m_ref[...]
        m_new = jnp.maximum(m_old, jnp.max(s, axis=-1, keepdims=True))
        p = jnp.where(mask, jnp.exp(s - m_new), 0.0)
        alpha = jnp.exp(m_old - m_new)
        l_ref[...] = alpha * l_ref[...] + jnp.sum(p, axis=-1, keepdims=True)
        pv = jnp.dot(p.reshape(rows, tk).astype(BF16), vs_ref[0, 0, pl.ds(k0, tk), :], preferred_element_type=F32)
        acc_ref[...] = alpha.reshape(rows, 1) * acc_ref[...] + pv
        m_ref[...] = m_new
        return carry

    lax.fori_loop(0, (t0 + tq + tk - 1) // tk, kv_step, 0)
    o_s = acc_ref[:, 0:hd] / l_ref[...].reshape(rows, 1)

    span = WINDOW + tq
    w0 = pl.multiple_of(jnp.maximum(t0 - WINDOW, 0), tq)
    s_w = lax.dot_general(q4, kw_ref[0, 0, pl.ds(w0, span), :], _NT, preferred_element_type=F32)
    kp = w0 + lax.broadcasted_iota(jnp.int32, (1, span), 1)
    p_w = _softmax_rows(s_w, (kp <= t_row) & (kp > t_row - WINDOW))
    o_w = jnp.dot(p_w.astype(BF16), vw_ref[0, 0, pl.ds(w0, span), :], preferred_element_type=F32)[:, 0:hd]

    _to_lane_layout(lay_ref, 0, o_s, tq)
    _to_lane_layout(lay_ref, 1, o_w, tq)
    _gate_maps(gl_ref, gp_ref, gm_ref)
    _gated_store(gm_ref, oc_ref, lay_ref, o_ref)


def _nsa_attend_kernel(sh_ref, q_ref, ks_ref, vs_ref, kw_ref, vw_ref, gl_ref, gp_ref, oc_ref, sel_ref, o_ref,
                       lhs_ref, acc_ref, pre_ref, gm_ref, lay_ref, m_ref, l_ref, *, tq, tk):
    data = (q_ref, ks_ref, vs_ref, kw_ref, vw_ref, gl_ref, gp_ref, oc_ref, sel_ref, o_ref)

    @pl.when(sh_ref[3] > 0.5)
    def _():
        _att_fast(sh_ref, *data, lhs_ref, acc_ref, pre_ref, gm_ref, lay_ref, tq=tq, tk=tk)

    @pl.when(sh_ref[3] <= 0.5)
    def _():
        _att_slow(*data, acc_ref, gm_ref, lay_ref, m_ref, l_ref, tq=tq, tk=tk)


def _overlap_t(n_blk, ncp):
    c_start = np.arange(ncp) * CMP_STRIDE
    b_start = np.arange(n_blk) * SEL_BLOCK
    ov = ((c_start[None, :] < b_start[:, None] + SEL_BLOCK) & (b_start[:, None] < c_start[None, :] + CMP_BLOCK))
    ov[:, ncp - 1] = False
    return jnp.asarray(ov.astype(np.float32)).astype(BF16)


def _nsa_shifts(q_norm, k_norm):
    bound = (NSA_HEAD_DIM ** 0.5) * jnp.max(jnp.abs(q_norm)) * jnp.max(jnp.abs(k_norm), axis=-1)
    bound = bound.astype(BF16).astype(F32)
    fast = jnp.all(bound <= MAX_CONST_SHIFT).astype(F32)
    return jnp.concatenate([-bound, fast[None]])


def _nsa_select(shifts, q, cmp_kv, *, tq, parts):
    b, h, s, hd = q.shape
    g, rep = NSA_GROUPS, NSA_REP
    ncp = cmp_kv.shape[3]
    n_blk = s // SEL_BLOCK
    tb = tq * parts
    assert n_blk % 8 == 0 and n_blk <= hd and s % tb == 0
    kernel = functools.partial(_nsa_select_kernel, tq=tq, n_sel=min(N_SEL, n_blk))
    return pl.pallas_call(
        kernel,
        grid=(b, g, s // tb),
        in_specs=[
            pl.BlockSpec(memory_space=pltpu.SMEM),
            pl.BlockSpec((1, rep, tb, hd), lambda bi, gi, i: (bi, gi, i, 0)),
            pl.BlockSpec((1, 1, 1, ncp, 2 * hd), lambda bi, gi, i: (0, bi, gi, 0, 0)),
            pl.BlockSpec((1, 1, 1, ncp, 2 * hd), lambda bi, gi, i: (1, bi, gi, 0, 0)),
            pl.BlockSpec((n_blk, ncp), lambda bi, gi, i: (0, 0)),
        ],
        out_specs=[pl.BlockSpec((1, tb, rep * hd), lambda bi, gi, i: (bi, i, gi)),
                   pl.BlockSpec((1, 1, tb, hd), lambda bi, gi, i: (bi, gi, i, 0))],
        out_shape=[jax.ShapeDtypeStruct((b, s, h * hd), BF16), jax.ShapeDtypeStruct((b, g, s, hd), BF16)],
        compiler_params=_cparams("parallel", "parallel", "parallel"),
        name="nsa_select",
    )(shifts, q, cmp_kv, cmp_kv, _overlap_t(n_blk, ncp))


def _nsa_attend(shifts, q, ks, vs, kw, vw, z3, o_cmp, sel, *, tq, tk):
    b, h, s, hd = q.shape
    g, rep = NSA_GROUPS, NSA_REP
    rows = rep * tq
    assert WINDOW % tq == 0
    full = lambda w: pl.BlockSpec((1, 1, s, w), lambda bi, gi, i: (bi, gi, 0, 0))
    return pl.pallas_call(
        functools.partial(_nsa_attend_kernel, tq=tq, tk=tk),
        grid=(b, g, s // tq),
        in_specs=[
            pl.BlockSpec(memory_space=pltpu.SMEM),
            pl.BlockSpec((1, rep, tq, hd), lambda bi, gi, i: (bi, gi, i, 0)),
            full(2 * hd), full(2 * hd), full(hd), full(2 * hd),
            pl.BlockSpec((1, tq, LANES), lambda bi, gi, i: (bi, i, COL_SMALL // LANES + gi)),
            pl.BlockSpec((LANES, 3 * rep * hd), lambda bi, gi, i: (0, 0)),
            pl.BlockSpec((1, tq, rep * hd), lambda bi, gi, i: (bi, i, gi)),
            pl.BlockSpec((1, 1, tq, hd), lambda bi, gi, i: (bi, gi, i, 0)),
        ],
        out_specs=pl.BlockSpec((1, tq, rep * hd), lambda bi, gi, i: (bi, i, gi)),
        out_shape=jax.ShapeDtypeStruct((b, s, h * hd), BF16),
        scratch_shapes=[pltpu.VMEM((rows, 2 * hd), BF16), pltpu.VMEM((rows, 2 * hd), F32),
                        pltpu.VMEM((rows, 2 * hd), F32), pltpu.VMEM((tq, 3 * rep * hd), F32),
                        pltpu.VMEM((2, tq, rep * hd), F32),
                        pltpu.VMEM((rep, tq, 1), F32), pltpu.VMEM((rep, tq, 1), F32)],
        compiler_params=_cparams("parallel", "parallel", "arbitrary"),
        name="nsa_attend",
    )(shifts, q, ks, vs, kw, vw, z3, _gate_placement(), o_cmp, sel)


def _head_block_diag(x, n_rows, head_of_lane, dtype):
    return jnp.concatenate([jnp.where(head_of_lane == h, x, 0.0) for h in range(GLA_HEADS)], axis=0).astype(dtype)


def _gla_kernel(q_ref, k_ref, v_ref, r_ref, sm_ref, wg_ref, bg_ref, ng_ref, tril_ref, o_ref,
                st_ref, upd_ref, oin_ref, qin_ref, *, n_chunks):
    c, sub = GLA_CHUNK, GLA_SUB
    dk, dv, nh = GLA_HEAD_DK, GLA_HEAD_DV, GLA_HEADS
    nk, nv = nh * dk, nh * dv

    @pl.when(pl.program_id(1) == 0)
    def _():
        st_ref[...] = jnp.zeros(st_ref.shape, F32)

    sm = sm_ref[0]
    x = (jnp.dot(sm, wg_ref[0], preferred_element_type=F32) + jnp.dot(sm, wg_ref[1], preferred_element_type=F32)
         + bg_ref[...])
    log_a = (jnp.minimum(x, 0.0) - jnp.log1p(jnp.exp(-jnp.abs(x)))) / GLA_TAU
    la_hi = log_a.astype(BF16)
    la_lo = (log_a - la_hi.astype(F32)).astype(BF16)
    bcum_all = (jnp.dot(tril_ref[...], la_hi, preferred_element_type=F32)
                + jnp.dot(tril_ref[...], la_lo, preferred_element_type=F32))
    head_k = lax.broadcasted_iota(jnp.int32, (1, nk), 1) >> 6
    head_v = lax.broadcasted_iota(jnp.int32, (1, nv), 1) >> 7
    state_mask = (lax.broadcasted_iota(jnp.int32, (nv, 1), 0) >> 7) == head_k
    causal = ((lax.broadcasted_iota(jnp.int32, (c, nh * c), 1) & (c - 1))
              <= lax.broadcasted_iota(jnp.int32, (c, nh * c), 0))

    decays = []
    for cc in range(n_chunks):
        rs = slice(cc * c, (cc + 1) * c)
        bcum = bcum_all[rs]
        q = q_ref[0, rs].astype(F32) * (dk ** -0.5)
        k = k_ref[0, rs].astype(F32)
        v = v_ref[0, rs].astype(F32)
        b_last = bcum[c - 1:c, :]

        score_rows = []
        for i in range(c // sub):
            lo, hi = i * sub, (i + 1) * sub
            ref = bcum[lo - 1:lo, :] if i > 0 else jnp.zeros((1, nk), F32)
            q_i = (q[lo:hi] * jnp.exp(bcum[lo:hi] - ref)).astype(BF16)
            k_i = k[0:hi] * jnp.exp(ref - bcum[0:hi])
            if hi < c:
                k_i = jnp.concatenate([k_i, jnp.zeros((c - hi, nk), F32)], axis=0)
            score_rows.append(lax.dot_general(q_i, _head_block_diag(k_i, c, head_k, BF16), _NT,
                                              preferred_element_type=F32))
        a = jnp.where(causal, jnp.concatenate(score_rows, axis=0), 0.0).astype(BF16)
        oin_ref[rs] = jnp.dot(a, _head_block_diag(v, c, head_v, BF16), preferred_element_type=F32)
        qin_ref[rs] = (q * jnp.exp(bcum)).astype(BF16)
        k_out = (k * jnp.exp(b_last - bcum)).astype(BF16)
        upd = lax.dot_general(v.astype(BF16), k_out, (((0,), (0,)), ((), ())), preferred_element_type=F32)
        upd_ref[cc] = jnp.where(state_mask, upd, 0.0)
        decays.append(jnp.exp(b_last))

    for cc in range(n_chunks):
        rs = slice(cc * c, (cc + 1) * c)
        st = st_ref[...]
        o = oin_ref[rs] + lax.dot_general(qin_ref[rs], st.astype(BF16), _NT, preferred_element_type=F32)
        st_ref[...] = st * decays[cc] + upd_ref[cc]
        r_gate = r_ref[0, rs].astype(F32)
        for h in range(nh):
            vs_ = slice(h * dv, (h + 1) * dv)
            o_h = o[:, vs_]
            ms = jnp.mean(o_h * o_h, axis=-1, keepdims=True)
            rg = r_gate[:, vs_]
            o_ref[0, rs, vs_] = (o_h * lax.rsqrt(ms + NORM_EPS) * ng_ref[...] * (rg * jax.nn.sigmoid(rg))).astype(o_ref.dtype)


def _gla(z3, w_gate, b_gate, norm_g, *, n_chunks):
    b, s, _ = z3.shape
    c = GLA_CHUNK * n_chunks
    nk = GLA_HEADS * GLA_HEAD_DK
    nv = GLA_HEADS * GLA_HEAD_DV
    wg = jnp.zeros((LANES, nk), F32).at[SMALL_GLOW_LANE:SMALL_GLOW_LANE + GLA_RANK].set(w_gate.astype(F32))
    wg_hi = wg.astype(BF16)
    wg = jnp.stack([wg_hi, (wg - wg_hi.astype(F32)).astype(BF16)])
    idx = np.arange(c)
    tril = jnp.asarray(((idx[:, None] >= idx[None, :])
                        & (idx[:, None] // GLA_CHUNK == idx[None, :] // GLA_CHUNK)).astype(np.float32)).astype(BF16)
    const = lambda shape: pl.BlockSpec(shape, lambda bi, i: (0,) * len(shape))
    return pl.pallas_call(
        functools.partial(_gla_kernel, n_chunks=n_chunks),
        grid=(b, s // c),
        in_specs=[
            pl.BlockSpec((1, c, nk), lambda bi, i: (bi, i, COL_GQ // nk)),
            pl.BlockSpec((1, c, nk), lambda bi, i: (bi, i, COL_GK // nk)),
            pl.BlockSpec((1, c, nv), lambda bi, i: (bi, i, COL_GV // nv)),
            pl.BlockSpec((1, c, nv), lambda bi, i: (bi, i, COL_GR // nv)),
            pl.BlockSpec((1, c, LANES), lambda bi, i: (bi, i, COL_SMALL // LANES)),
            const((2, LANES, nk)), const((1, nk)), const((1, GLA_HEAD_DV)), const((c, c)),
        ],
        out_specs=pl.BlockSpec((1, c, nv), lambda bi, i: (bi, i, 0)),
        out_shape=jax.ShapeDtypeStruct((b, s, nv), BF16),
        scratch_shapes=[pltpu.VMEM((nv, nk), F32), pltpu.VMEM((n_chunks, nv, nk), F32),
                        pltpu.VMEM((c, nv), F32), pltpu.VMEM((c, nk), BF16)],
        compiler_params=_cparams("parallel", "arbitrary"),
        name="gla",
    )(z3, z3, z3, z3, z3, wg, b_gate.reshape(1, nk).astype(F32), norm_g.reshape(1, GLA_HEAD_DV).astype(F32), tril)


def _mem_attn_kernel(q_ref, k_ref, v_ref, qg_ref, kg_ref, o_ref):
    dh = MEM_HEAD_DIM
    nt = (((1,), (1,)), ((), ()))
    for h in range(MEM_HEADS):
        sl = slice(h * dh, (h + 1) * dh)
        q = q_ref[0, :, sl].astype(F32)
        q = q * lax.rsqrt(jnp.mean(q * q, axis=-1, keepdims=True) + NORM_EPS) * qg_ref[...] * (dh ** -0.5)
        k = k_ref[0, :, sl].astype(F32)
        k = k * lax.rsqrt(jnp.mean(k * k, axis=-1, keepdims=True) + NORM_EPS) * kg_ref[...]
        s = lax.dot_general(q.astype(BF16), k.astype(BF16), nt, preferred_element_type=F32)
        m = jnp.max(s, axis=-1, keepdims=True)
        p = jnp.exp(s - m)
        p = p / jnp.sum(p, axis=-1, keepdims=True)
        o = jnp.dot(p.astype(BF16), v_ref[0, :, sl].astype(BF16), preferred_element_type=F32)
        o_ref[0, :, sl] = o.astype(o_ref.dtype)


def _mem_attention(z3, kv, q_norm, k_norm, *, tq):
    b, s, _ = z3.shape
    m = kv.shape[1]
    w = MEM_HEADS * MEM_HEAD_DIM
    const = lambda shape: pl.BlockSpec(shape, lambda bi, i: (0,) * len(shape))
    return pl.pallas_call(
        _mem_attn_kernel,
        grid=(b, s // tq),
        in_specs=[
            pl.BlockSpec((1, tq, w), lambda bi, i: (bi, i, COL_MQ // w)),
            pl.BlockSpec((1, m, w), lambda bi, i: (bi, 0, 0)),
            pl.BlockSpec((1, m, w), lambda bi, i: (bi, 0, 1)),
            const((1, MEM_HEAD_DIM)), const((1, MEM_HEAD_DIM)),
        ],
        out_specs=pl.BlockSpec((1, tq, w), lambda bi, i: (bi, i, 0)),
        out_shape=jax.ShapeDtypeStruct((b, s, w), BF16),
        compiler_params=_cparams("parallel", "parallel"),
        name="mem_attention",
    )(z3, kv, kv, q_norm.reshape(1, MEM_HEAD_DIM).astype(F32), k_norm.reshape(1, MEM_HEAD_DIM).astype(F32))


def _merge_kernel(x_ref, on_ref, og_ref, om_ref, m0_ref, m1_ref, m2_ref, bm_ref, wb_ref, wo_ref, o_ref):
    merged = None
    for br, (ref, mg_ref) in enumerate(((on_ref, m0_ref), (og_ref, m1_ref), (om_ref, m2_ref))):
        y = jnp.dot(ref[...], wb_ref[br], preferred_element_type=F32)
        gate = jax.nn.sigmoid(mg_ref[...].astype(F32) + bm_ref[br:br + 1, :])
        merged = gate * y if merged is None else merged + gate * y
    o_ref[...] = x_ref[...] + jnp.dot(merged.astype(BF16), wo_ref[...], preferred_element_type=F32)


def _merge_out(x2, o_nsa, o_gla, o_mem, z2, b_merge, w_branch, w_out, *, tm):
    t, d = x2.shape
    bw = BRANCH_WIDTH
    row = lambda w: pl.BlockSpec((tm, w), lambda i: (i, 0))
    gate_cols = lambda br: pl.BlockSpec((tm, d), lambda i: (i, COL_MERGE // d + br))
    return pl.pallas_call(
        _merge_kernel,
        grid=(t // tm,),
        in_specs=[
            row(d), row(bw), row(bw), row(bw),
            gate_cols(0), gate_cols(1), gate_cols(2),
            pl.BlockSpec((N_BRANCH, d), lambda i: (0, 0)),
            pl.BlockSpec((N_BRANCH, bw, d), lambda i: (0, 0, 0)),
            pl.BlockSpec((d, d), lambda i: (0, 0)),
        ],
        out_specs=row(d),
        out_shape=jax.ShapeDtypeStruct((t, d), F32),
        compiler_params=_cparams("parallel"),
        name="merge_out",
    )(x2, o_nsa, o_gla, o_mem, z2, z2, z2, b_merge.astype(F32), w_branch, w_out)


def _mlp_kernel(x_ref, g_ref, wu_ref, wd_ref, o_ref, h_ref, acc_ref):
    j = pl.program_id(1)

    @pl.when(j == 0)
    def _():
        x = x_ref[...]
        ms = jnp.mean(x * x, axis=-1, keepdims=True)
        h_ref[...] = (x * lax.rsqrt(ms + NORM_EPS) * g_ref[...]).astype(h_ref.dtype)
        acc_ref[...] = jnp.zeros(acc_ref.shape, F32)

    u = jnp.dot(h_ref[...], wu_ref[...], preferred_element_type=F32)
    u = jnp.square(jnp.maximum(u, 0.0)).astype(BF16)
    acc_ref[...] += jnp.dot(u, wd_ref[...], preferred_element_type=F32)

    @pl.when(j == pl.num_programs(1) - 1)
    def _():
        o_ref[...] = x_ref[...] + acc_ref[...]


def _mlp(x2, g, w_up, w_down, *, tm, th):
    t, d = x2.shape
    hid = w_up.shape[1]
    return pl.pallas_call(
        _mlp_kernel,
        grid=(t // tm, hid // th),
        in_specs=[
            pl.BlockSpec((tm, d), lambda i, j: (i, 0)),
            pl.BlockSpec((1, d), lambda i, j: (0, 0)),
            pl.BlockSpec((d, th), lambda i, j: (0, j)),
            pl.BlockSpec((th, d), lambda i, j: (j, 0)),
        ],
        out_specs=pl.BlockSpec((tm, d), lambda i, j: (i, 0)),
        out_shape=jax.ShapeDtypeStruct((t, d), F32),
        scratch_shapes=[pltpu.VMEM((tm, d), BF16), pltpu.VMEM((tm, d), F32)],
        compiler_params=_cparams("parallel", "arbitrary"),
        name="mlp",
    )(x2, g.reshape(1, d).astype(F32), w_up, w_down)


def _tile(n, pref):
    t = min(n, pref)
    assert n % t == 0, (n, pref)
    return t


def _layer(x, mem2, pos_rows, pos_cmp, perm, p):
    b, s, d = x.shape
    t = b * s
    hd, g = NSA_HEAD_DIM, NSA_GROUPS
    x2 = x.reshape(t, d)

    w_in = jnp.take(jnp.pad(p["w_in"], ((0, 0), (0, 1))), perm, axis=1).astype(BF16)
    z2 = _norm_matmul(x2, p["ln_mix"].astype(F32), w_in, tm=_tile(t, 1024), tn=D_IN_PAD // 2, out_dtype=BF16, name="in_proj")
    z3 = z2.reshape(b, s, D_IN_PAD)

    q, ks, vs, kw, vw = _nsa_prep(z3, pos_rows, p["nsa_q_norm"].astype(F32), p["nsa_k_norm"][1].astype(F32),
                                  p["nsa_k_norm"][2].astype(F32), tp=_tile(s, 256))
    nc = s // CMP_STRIDE
    kvc = z3[:, :, COL_KV:COL_KV + 2 * g * hd].reshape(b, s, 2, g, hd)
    xc = kvc.transpose(2, 0, 3, 1, 4).reshape(2, b, g, nc, CMP_STRIDE * hd)
    w1 = p["cmp_w1"]
    half = CMP_STRIDE * hd
    w1cat = jnp.concatenate([w1[:, :half], w1[:, half:]], axis=-1).astype(BF16)
    pe2 = jnp.pad(p["cmp_pe"].reshape(2, 2, half), ((0, 0), (0, 6), (0, 0))).astype(F32)
    cmp_kv = _nsa_compress(xc, w1cat, p["cmp_w2"].astype(BF16), pe2, pos_cmp, p["nsa_k_norm"][0].astype(F32))
    shifts = _nsa_shifts(p["nsa_q_norm"].astype(F32), p["nsa_k_norm"].astype(F32))
    o_cmp, sel = _nsa_select(shifts, q, cmp_kv, tq=_tile(s, 256), parts=2)
    o_nsa = _nsa_attend(shifts, q, ks, vs, kw, vw, z3, o_cmp, sel, tq=_tile(s, 256), tk=_tile(s, 512))

    o_gla = _gla(z3, p["gla_w_gate"], p["gla_b_gate"], p["gla_norm"], n_chunks=8)

    kv = _norm_matmul(mem2, p["mem_norm"].astype(F32), p["mem_w_kv"].astype(BF16),
                      tm=_tile(mem2.shape[0], 512), tn=512, out_dtype=BF16, name="mem_kv")
    kv = kv.reshape(b, mem2.shape[0] // b, 2 * MEM_HEADS * MEM_HEAD_DIM)
    o_mem = _mem_attention(z3, kv, p["mem_q_norm"], p["mem_k_norm"], tq=_tile(s, 512))

    x2 = _merge_out(x2, o_nsa.reshape(t, -1), o_gla.reshape(t, -1), o_mem.reshape(t, -1), z2,
                    p["b_merge"], p["w_branch"].astype(BF16), p["w_out"].astype(BF16), tm=_tile(t, 512))
    x2 = _mlp(x2, p["ln_mlp"], p["w_up"].astype(BF16), p["w_down"].astype(BF16), tm=_tile(t, 1024), th=1024)
    return x2.reshape(b, s, d)


def kernel(x, mem, positions, ln_mix, w_in, b_merge, nsa_q_norm, nsa_k_norm, cmp_pe, cmp_w1, cmp_w2,
           gla_w_gate, gla_b_gate, gla_norm, mem_norm, mem_w_kv, mem_q_norm, mem_k_norm, w_branch, w_out,
           ln_mlp, w_up, w_down):
    b, s, d = x.shape
    assert d == 1024 and s % WINDOW == 0 and s >= 2 * WINDOW
    depth = w_in.shape[0]
    perm_np, d_in = _in_proj_permutation(d)
    assert w_in.shape[2] == d_in
    perm = jnp.asarray(perm_np)
    pos3 = positions.astype(jnp.int32).reshape(b, s, 1)
    nc = s // CMP_STRIDE
    cmp_end = np.minimum(np.arange(nc) * CMP_STRIDE + CMP_BLOCK - 1, s - 1)
    pos_cmp = pos3[:, cmp_end, :]
    pos_rows = positions.astype(jnp.int32).reshape(b, 1, s)
    mem2 = mem.reshape(b * mem.shape[1], d)
    names = ("ln_mix", "w_in", "b_merge", "nsa_q_norm", "nsa_k_norm", "cmp_pe", "cmp_w1", "cmp_w2",
             "gla_w_gate", "gla_b_gate", "gla_norm", "mem_norm", "mem_w_kv", "mem_q_norm", "mem_k_norm",
             "w_branch", "w_out", "ln_mlp", "w_up", "w_down")
    stacked = (ln_mix, w_in, b_merge, nsa_q_norm, nsa_k_norm, cmp_pe, cmp_w1, cmp_w2, gla_w_gate, gla_b_gate,
               gla_norm, mem_norm, mem_w_kv, mem_q_norm, mem_k_norm, w_branch, w_out, ln_mlp, w_up, w_down)
    for l in range(depth):
        x = _layer(x, mem2, pos_rows, pos_cmp, perm, {n: a[l] for n, a in zip(names, stacked)})
    return x
```

```python
import functools

import numpy as np
import jax
import jax.numpy as jnp
from jax import lax
from jax.experimental import pallas as pl
from jax.experimental.pallas import tpu as pltpu

NSA_HEADS = 8
NSA_GROUPS = 2
NSA_REP = NSA_HEADS // NSA_GROUPS
NSA_HEAD_DIM = 64
CMP_BLOCK = 32
CMP_STRIDE = 16
CMP_HIDDEN = 4 * NSA_HEAD_DIM
SEL_BLOCK = 64
N_SEL = 16
WINDOW = 512
FORCE_SCORE = 1e4
GLA_HEADS = 4
GLA_HEAD_DK = 64
GLA_HEAD_DV = 128
GLA_RANK = 16
GLA_TAU = 16.0
GLA_CHUNK = 64
GLA_SUB = 16
MEM_HEADS = 4
MEM_HEAD_DIM = 128
N_BRANCH = 3
BRANCH_WIDTH = 512
ROPE_THETA = 500000.0
ROPE_ROT = NSA_HEAD_DIM // 4
ROPE_HALF = ROPE_ROT // 2
NORM_EPS = 1e-6

LANES = 128
VMEM_LIMIT_BYTES = 48 * 1024 * 1024

F32 = jnp.float32
BF16 = jnp.bfloat16
HIGHEST = lax.Precision.HIGHEST
NEG_BIG = -1e30
MAX_CONST_SHIFT = 40.0

COL_NQ = 0
COL_GV = 512
COL_GR = 1024
COL_MQ = 1536
COL_MERGE = 2048
COL_GQ = 5120
COL_GK = 5376
COL_KV = 5632
COL_SMALL = 6400
D_IN_PAD = 6656
SMALL_GLOW_LANE = 12


def _in_proj_permutation(d_model):
    sizes = (512, 128, 128, 128, 128, 128, 128, 24, 256, 256, 512, 512, 16, 512, 3 * d_model)
    off = np.concatenate([[0], np.cumsum(sizes)])
    (o_nq, o_kc, o_vc, o_ks, o_vs, o_kw, o_vw, o_ng, o_gq, o_gk, o_gv, o_gr, o_gl, o_mq, o_mg) = off[:-1]
    d_in = int(off[-1])
    perm = np.full((D_IN_PAD,), d_in, np.int32)

    def put(new, old, n):
        perm[new:new + n] = np.arange(old, old + n)

    put(COL_NQ, o_nq, 512)
    put(COL_GV, o_gv, 512)
    put(COL_GR, o_gr, 512)
    put(COL_MQ, o_mq, 512)
    put(COL_MERGE, o_mg, 3 * d_model)
    put(COL_GQ, o_gq, 256)
    put(COL_GK, o_gk, 256)
    put(COL_KV, o_kc, 768)
    for g in range(NSA_GROUPS):
        for br in range(3):
            for r in range(NSA_REP):
                perm[COL_SMALL + g * LANES + br * NSA_REP + r] = o_ng + (g * NSA_REP + r) * 3 + br
    put(COL_SMALL + SMALL_GLOW_LANE, o_gl, GLA_RANK)
    return perm, d_in


def _cparams(*sem):
    return pltpu.CompilerParams(dimension_semantics=sem, vmem_limit_bytes=VMEM_LIMIT_BYTES)


def _norm_matmul_kernel(x_ref, g_ref, w_ref, o_ref, h_ref):
    @pl.when(pl.program_id(1) == 0)
    def _():
        x = x_ref[...].astype(F32)
        ms = jnp.mean(x * x, axis=-1, keepdims=True)
        h_ref[...] = (x * lax.rsqrt(ms + NORM_EPS) * g_ref[...]).astype(h_ref.dtype)

    o_ref[...] = jnp.dot(h_ref[...], w_ref[...], preferred_element_type=F32).astype(o_ref.dtype)


def _norm_matmul(x, g, w, *, tm, tn, out_dtype, name):
    m, k = x.shape
    n = w.shape[1]
    return pl.pallas_call(
        _norm_matmul_kernel,
        grid=(m // tm, n // tn),
        in_specs=[
            pl.BlockSpec((tm, k), lambda i, j: (i, 0)),
            pl.BlockSpec((1, k), lambda i, j: (0, 0)),
            pl.BlockSpec((k, tn), lambda i, j: (0, j)),
        ],
        out_specs=pl.BlockSpec((tm, tn), lambda i, j: (i, j)),
        out_shape=jax.ShapeDtypeStruct((m, n), out_dtype),
        scratch_shapes=[pltpu.VMEM((tm, k), BF16)],
        compiler_params=_cparams("parallel", "arbitrary"),
        name=name,
    )(x, g.reshape(1, k), w)


def _rope_tables(pos_row, freq_col, place):
    ang = freq_col * pos_row
    tn = (((0,), (0,)), ((), ()))
    lane = lax.broadcasted_iota(jnp.int32, (1, LANES), 1)
    c = lax.dot_general(jnp.cos(ang), place, tn, preferred_element_type=F32, precision=HIGHEST)
    s = lax.dot_general(jnp.sin(ang), place, tn, preferred_element_type=F32, precision=HIGHEST)
    return c + jnp.where((lane & (NSA_HEAD_DIM - 1)) >= ROPE_ROT, 1.0, 0.0), s


def _rope_lanes(x, c, s):
    n = x.shape[-1]
    if n > LANES:
        c = jnp.concatenate([c] * (n // LANES), axis=1)
        s = jnp.concatenate([s] * (n // LANES), axis=1)
    lane = lax.broadcasted_iota(jnp.int32, (1, n), 1) & (NSA_HEAD_DIM - 1)
    up = pltpu.roll(x, n - ROPE_HALF, axis=1)
    dn = pltpu.roll(x, ROPE_HALF, axis=1)
    y = jnp.where(lane < ROPE_HALF, -up, jnp.where(lane < ROPE_ROT, dn, 0.0))
    return x * c + y * s


def _head_rms(x, bd, g):
    x2 = x * x
    hi = x2.astype(BF16)
    lo = (x2 - hi.astype(F32)).astype(BF16)
    ms = (jnp.dot(hi, bd, preferred_element_type=F32) + jnp.dot(lo, bd, preferred_element_type=F32))
    return x * lax.rsqrt(ms * (1.0 / NSA_HEAD_DIM) + NORM_EPS) * g


def _nsa_prep_kernel(q_ref, ks_ref, vs_ref, kw_ref, vw_ref, pos_ref, qg_ref, ksg_ref, kwg_ref,
                     bdq_ref, bdk_ref, f_ref, place_ref,
                     qo_ref, kso_ref, vso_ref, kwo_ref, vwo_ref):
    hd = NSA_HEAD_DIM
    tp = q_ref.shape[1]
    c, s = _rope_tables(pos_ref[0].astype(F32), f_ref[...], place_ref[...])
    q = _head_rms(q_ref[0].astype(F32), bdq_ref[...], qg_ref[...])
    q = (_rope_lanes(q, c, s) * (hd ** -0.5)).astype(qo_ref.dtype)
    for h in range(NSA_HEADS):
        qo_ref[0, h] = q[:, h * hd:(h + 1) * hd]
    ks = _rope_lanes(_head_rms(ks_ref[0].astype(F32), bdk_ref[...], ksg_ref[...]), c, s)
    kw = _rope_lanes(_head_rms(kw_ref[0].astype(F32), bdk_ref[...], kwg_ref[...]), c, s)
    vs = vs_ref[0]
    vw = vw_ref[0]
    tok = pl.program_id(1) * tp + lax.broadcasted_iota(jnp.int32, (tp, hd), 0)
    lane = lax.broadcasted_iota(jnp.int32, (tp, hd), 1)
    blk_onehot = jnp.where((tok >> 6) == lane, 1.0, 0.0).astype(kso_ref.dtype)
    ones_col = jnp.ones((tp, hd), vso_ref.dtype)
    for g in range(NSA_GROUPS):
        sl = slice(g * hd, (g + 1) * hd)
        kso_ref[0, g, :, 0:hd] = ks[:, sl].astype(kso_ref.dtype)
        kso_ref[0, g, :, hd:2 * hd] = blk_onehot
        kwo_ref[0, g] = kw[:, sl].astype(kwo_ref.dtype)
        vso_ref[0, g, :, 0:hd] = vs[:, sl].astype(vso_ref.dtype)
        vso_ref[0, g, :, hd:2 * hd] = ones_col
        vwo_ref[0, g, :, 0:hd] = vw[:, sl].astype(vwo_ref.dtype)
        vwo_ref[0, g, :, hd:2 * hd] = ones_col


def _block_diag_ones(n, width):
    i = np.arange(n)
    return jnp.asarray((i[:, None] // width == i[None, :] // width).astype(np.float32)).astype(BF16)


def _rope_freq_col():
    inv = ROPE_THETA ** (-(np.arange(ROPE_HALF, dtype=np.float64)) / ROPE_HALF)
    return jnp.asarray(inv.astype(np.float32)).reshape(ROPE_HALF, 1)


def _rope_placement():
    m = np.zeros((ROPE_HALF, LANES), np.float32)
    for f in range(ROPE_HALF):
        for base in range(0, LANES, NSA_HEAD_DIM):
            m[f, base + f] = 1.0
            m[f, base + f + ROPE_HALF] = 1.0
    return jnp.asarray(m)


def _nsa_prep(z3, pos_rows, q_norm, ks_norm, kw_norm, *, tp):
    b, s, _ = z3.shape
    hd, g = NSA_HEAD_DIM, NSA_GROUPS
    assert SEL_BLOCK == 64 and s // SEL_BLOCK <= hd
    kvb = COL_KV // LANES
    qg = jnp.tile(q_norm, NSA_HEADS).reshape(1, NSA_HEADS * hd)
    ksg = jnp.tile(ks_norm, g).reshape(1, g * hd)
    kwg = jnp.tile(kw_norm, g).reshape(1, g * hd)
    const = lambda shape: pl.BlockSpec(shape, lambda bi, i: (0,) * len(shape))
    plain_out = jax.ShapeDtypeStruct((b, g, s, hd), BF16)
    plain_spec = pl.BlockSpec((1, g, tp, hd), lambda bi, i: (bi, 0, i, 0))
    aug_out = jax.ShapeDtypeStruct((b, g, s, 2 * hd), BF16)
    aug_spec = pl.BlockSpec((1, g, tp, 2 * hd), lambda bi, i: (bi, 0, i, 0))
    return pl.pallas_call(
        _nsa_prep_kernel,
        grid=(b, s // tp),
        in_specs=[
            pl.BlockSpec((1, tp, 512), lambda bi, i: (bi, i, COL_NQ // 512)),
            pl.BlockSpec((1, tp, LANES), lambda bi, i: (bi, i, kvb + 2)),
            pl.BlockSpec((1, tp, LANES), lambda bi, i: (bi, i, kvb + 3)),
            pl.BlockSpec((1, tp, LANES), lambda bi, i: (bi, i, kvb + 4)),
            pl.BlockSpec((1, tp, LANES), lambda bi, i: (bi, i, kvb + 5)),
            pl.BlockSpec((1, 1, tp), lambda bi, i: (bi, 0, i)),
            const((1, 512)), const((1, LANES)), const((1, LANES)),
            const((512, 512)), const((LANES, LANES)),
            const((ROPE_HALF, 1)), const((ROPE_HALF, LANES)),
        ],
        out_specs=[
            pl.BlockSpec((1, NSA_HEADS, tp, hd), lambda bi, i: (bi, 0, i, 0)),
            aug_spec, aug_spec, plain_spec, aug_spec,
        ],
        out_shape=[jax.ShapeDtypeStruct((b, NSA_HEADS, s, hd), BF16), aug_out, aug_out, plain_out, aug_out],
        compiler_params=_cparams("parallel", "parallel"),
        name="nsa_prep",
    )(z3, z3, z3, z3, z3, pos_rows, qg, ksg, kwg,
      _block_diag_ones(512, hd), _block_diag_ones(LANES, hd), _rope_freq_col(), _rope_placement())


def _gelu_tanh(x):
    return 0.5 * x * (1.0 + jnp.tanh(np.sqrt(2.0 / np.pi) * (x + 0.044715 * x * x * x)))


def _nsa_cmp_kernel(x_ref, w1_ref, w2_ref, pe_ref, pos_ref, g_ref, rot_ref, f_ref, o_ref, xs_ref):
    kind = pl.program_id(0)
    hd = NSA_HEAD_DIM
    nc = x_ref.shape[1] // CMP_STRIDE
    xs_ref[...] = x_ref[0].astype(F32)
    r = jnp.dot(pe_ref[0].astype(BF16), w1_ref[0], preferred_element_type=F32)
    ab = [jnp.zeros((nc, 2 * CMP_HIDDEN), F32) for _ in range(NSA_GROUPS)]
    for p in range(CMP_STRIDE):
        xp = xs_ref[pl.ds(p, nc, stride=CMP_STRIDE), :].astype(BF16)
        w1p = w1_ref[0, p * hd:(p + 1) * hd, :]
        for g in range(NSA_GROUPS):
            ab[g] = ab[g] + jnp.dot(xp[:, g * hd:(g + 1) * hd], w1p, preferred_element_type=F32)

    for g in range(NSA_GROUPS):
        a = ab[g][:, :CMP_HIDDEN] + r[0:1, :CMP_HIDDEN]
        bm = ab[g][:, CMP_HIDDEN:] + r[1:2, CMP_HIDDEN:]
        hid = _gelu_tanh(a + pltpu.roll(bm, nc - 1, axis=0))
        comp = jnp.dot(hid.astype(BF16), w2_ref[0], preferred_element_type=F32)

        @pl.when(kind == 0)
        def _():
            ms = jnp.mean(comp * comp, axis=-1, keepdims=True)
            kn = comp * lax.rsqrt(ms + NORM_EPS) * g_ref[...]
            ang = pos_ref[0].astype(F32) * f_ref[...]
            y = jnp.dot(kn, rot_ref[...], preferred_element_type=F32, precision=HIGHEST)
            o_ref[0, 0, g, :, 0:hd] = (kn * jnp.cos(ang) + y * jnp.sin(ang)).astype(o_ref.dtype)
            o_ref[0, 0, g, :, hd:2 * hd] = jnp.zeros((nc, hd), o_ref.dtype)

        @pl.when(kind != 0)
        def _():
            o_ref[0, 0, g, :, 0:hd] = comp.astype(o_ref.dtype)
            o_ref[0, 0, g, :, hd:2 * hd] = jnp.ones((nc, hd), o_ref.dtype)


def _rope_freq_head():
    p = np.arange(NSA_HEAD_DIM)
    inv = ROPE_THETA ** (-(np.arange(ROPE_HALF, dtype=np.float64)) / ROPE_HALF)
    f = np.where(p < ROPE_ROT, inv[p % ROPE_HALF], 0.0)
    return jnp.asarray(f.astype(np.float32)).reshape(1, NSA_HEAD_DIM)


def _rope_rot_matrix(n):
    m = np.zeros((n, n), np.float32)
    for base in range(0, n, NSA_HEAD_DIM):
        for l in range(ROPE_HALF):
            m[base + l + ROPE_HALF, base + l] = -1.0
            m[base + l, base + l + ROPE_HALF] = 1.0
    return jnp.asarray(m)


def _nsa_compress(z3, w1cat, w2, pe2, pos_cmp, k_norm0):
    b, s, _ = z3.shape
    hd, g = NSA_HEAD_DIM, NSA_GROUPS
    nc = s // CMP_STRIDE
    assert g * hd == LANES
    const = lambda shape: pl.BlockSpec(shape, lambda k, bi: (0,) * len(shape))
    return pl.pallas_call(
        _nsa_cmp_kernel,
        grid=(2, b),
        in_specs=[
            pl.BlockSpec((1, s, LANES), lambda k, bi: (bi, 0, COL_KV // LANES + k)),
            pl.BlockSpec((1, CMP_STRIDE * hd, 2 * CMP_HIDDEN), lambda k, bi: (k, 0, 0)),
            pl.BlockSpec((1, CMP_HIDDEN, hd), lambda k, bi: (k, 0, 0)),
            pl.BlockSpec((1, 8, CMP_STRIDE * hd), lambda k, bi: (k, 0, 0)),
            pl.BlockSpec((1, nc, 1), lambda k, bi: (bi, 0, 0)),
            const((1, hd)), const((hd, hd)), const((1, hd)),
        ],
        out_specs=pl.BlockSpec((1, 1, g, nc, 2 * hd), lambda k, bi: (k, bi, 0, 0, 0)),
        out_shape=jax.ShapeDtypeStruct((2, b, g, nc, 2 * hd), BF16),
        scratch_shapes=[pltpu.VMEM((s, LANES), F32)],
        compiler_params=_cparams("parallel", "parallel"),
        name="nsa_compress",
    )(z3, w1cat, w2, pe2, pos_cmp, k_norm0.reshape(1, hd), _rope_rot_matrix(hd), _rope_freq_head())


def _softmax_rows(s, mask):
    s = jnp.where(mask, s, NEG_BIG)
    m = jnp.max(s, axis=-1, keepdims=True)
    p = jnp.where(mask, jnp.exp(s - m), 0.0)
    d = jnp.sum(p, axis=-1, keepdims=True)
    return p / jnp.where(d > 0, d, 1.0)


_NT = (((1,), (1,)), ((), ()))


def _select_blocks(p_sum, ov_ref, t0, tq, n_sel):
    n_blk = ov_ref.shape[0]
    p_hi = p_sum.astype(BF16)
    p_lo = (p_sum - p_hi.astype(F32)).astype(BF16)
    imp_t = (lax.dot_general(ov_ref[...], p_hi, _NT, preferred_element_type=F32)
             + lax.dot_general(ov_ref[...], p_lo, _NT, preferred_element_type=F32))
    j_col = lax.broadcasted_iota(jnp.int32, (n_blk, 1), 0)
    t_lane = t0 + lax.broadcasted_iota(jnp.int32, (1, tq), 1)
    causal = j_col * SEL_BLOCK <= t_lane
    cur = t_lane >> 6
    forced = causal & ((j_col == 0) | (j_col == cur) | (j_col == cur - 1))
    score = jnp.where(forced, FORCE_SCORE, jnp.where(causal, imp_t, -FORCE_SCORE))
    ng = n_blk // 8
    groups = [score[8 * v:8 * v + 8] for v in range(ng)]
    ranks = [jnp.zeros((8, tq), F32) for _ in range(ng)]
    sub = lax.broadcasted_iota(jnp.int32, (8, tq), 0)
    for jp in range(n_blk):
        row = jnp.broadcast_to(score[jp:jp + 1, :], (8, tq))
        vj = jp // 8
        for v in range(ng):
            if v < vj:
                ahead = row > groups[v]
            elif v > vj:
                ahead = row >= groups[v]
            else:
                ahead = (row > groups[v]) | ((row == groups[v]) & (sub > jp % 8))
            ranks[v] = ranks[v] + jnp.where(ahead, 1.0, 0.0)
    sel_t = jnp.where(jnp.concatenate(ranks, axis=0) < n_sel, 1.0, 0.0).astype(BF16)
    eye = (lax.broadcasted_iota(jnp.int32, (tq, tq), 0) == lax.broadcasted_iota(jnp.int32, (tq, tq), 1)).astype(BF16)
    return lax.dot_general(eye, sel_t, _NT, preferred_element_type=F32)


def _sum_heads(p, tq):
    out = p[0:tq]
    for r in range(1, NSA_REP):
        out = out + p[r * tq:(r + 1) * tq]
    return out


def _store_heads(o_ref, o, tq, row0):
    hd = NSA_HEAD_DIM
    for r in range(NSA_REP):
        o_ref[0, row0:row0 + tq, r * hd:(r + 1) * hd] = o[r * tq:(r + 1) * tq].astype(o_ref.dtype)


def _store_selection(sel_ref, sel, row0):
    tq, n_blk = sel.shape
    sel = sel.astype(sel_ref.dtype)
    if n_blk < NSA_HEAD_DIM:
        sel = jnp.concatenate([sel, jnp.zeros((tq, NSA_HEAD_DIM - n_blk), sel_ref.dtype)], axis=1)
    sel_ref[0, 0, row0:row0 + tq] = sel


def _normalize_aug(ol):
    return ol * pltpu.roll(1.0 / ol, NSA_HEAD_DIM, axis=1)


def _gate_maps(gl_ref, gp_ref, gm_ref):
    gates = jax.nn.sigmoid(gl_ref[0].astype(F32))
    g_hi = gates.astype(BF16)
    g_lo = (gates - g_hi.astype(F32)).astype(BF16)
    gm_ref[...] = (jnp.dot(g_hi, gp_ref[...], preferred_element_type=F32)
                   + jnp.dot(g_lo, gp_ref[...], preferred_element_type=F32))


def _to_lane_layout(lay_ref, idx, o, tq):
    hd = NSA_HEAD_DIM
    for r in range(NSA_REP):
        lay_ref[idx, :, r * hd:(r + 1) * hd] = o[r * tq:(r + 1) * tq, 0:hd]


def _gated_store(gm_ref, oc_ref, lay_ref, o_ref):
    w = NSA_REP * NSA_HEAD_DIM
    out = (gm_ref[:, 0:w] * oc_ref[0].astype(F32) + gm_ref[:, w:2 * w] * lay_ref[0]
           + gm_ref[:, 2 * w:3 * w] * lay_ref[1])
    o_ref[0] = out.astype(o_ref.dtype)


def _gate_placement():
    m = np.zeros((LANES, 3 * NSA_REP * NSA_HEAD_DIM), np.float32)
    for br in range(3):
        for r in range(NSA_REP):
            c0 = br * NSA_REP * NSA_HEAD_DIM + r * NSA_HEAD_DIM
            m[br * NSA_REP + r, c0:c0 + NSA_HEAD_DIM] = 1.0
    return jnp.asarray(m).astype(BF16)


def _sel_fast(sh_ref, q_ref, kc_ref, vc_ref, ov_ref, oc_ref, sel_ref, *, tq, n_sel):
    rep, hd = NSA_REP, NSA_HEAD_DIM
    rows = rep * tq
    kc = kc_ref[0, 0, 0][:, 0:hd]
    ncp = kc.shape[0]
    n_idx = lax.broadcasted_iota(jnp.int32, (1, ncp), 1)
    for part in range(q_ref.shape[2] // tq):
        row0 = part * tq
        t0 = pl.program_id(2) * q_ref.shape[2] + row0
        q4 = q_ref[0, :, row0:row0 + tq, :].reshape(rows, hd)
        t_q = t0 + lax.broadcasted_iota(jnp.int32, (tq, 1), 0)
        valid_c = (n_idx * CMP_STRIDE + (CMP_BLOCK - 1) <= t_q) & (n_idx < ncp - 1)
        bias_c = jnp.where(valid_c, sh_ref[0], NEG_BIG)
        s_c = lax.dot_general(q4, kc, _NT, preferred_element_type=F32).reshape(rep, tq, ncp)
        p_c = jnp.exp(s_c + bias_c[None]).reshape(rows, ncp)
        ol_c = jnp.dot(p_c.astype(BF16), vc_ref[0, 0, 0], preferred_element_type=F32)
        inv_c = 1.0 / jnp.where(ol_c[:, hd:hd + 1] > 0, ol_c[:, hd:hd + 1], 1.0)
        _store_heads(oc_ref, ol_c[:, 0:hd] * inv_c, tq, row0)
        _store_selection(sel_ref, _select_blocks(_sum_heads(p_c * inv_c, tq), ov_ref, t0, tq, n_sel), row0)


def _sel_slow(q_ref, kc_ref, vc_ref, ov_ref, oc_ref, sel_ref, *, tq, n_sel):
    rep, hd = NSA_REP, NSA_HEAD_DIM
    rows = rep * tq
    kc = kc_ref[0, 0, 0][:, 0:hd]
    ncp = kc.shape[0]
    n_idx = lax.broadcasted_iota(jnp.int32, (1, ncp), 1)
    for part in range(q_ref.shape[2] // tq):
        row0 = part * tq
        t0 = pl.program_id(2) * q_ref.shape[2] + row0
        q4 = q_ref[0, :, row0:row0 + tq, :].reshape(rows, hd)
        t_row = t0 + (lax.broadcasted_iota(jnp.int32, (rows, 1), 0) & (tq - 1))
        s_c = lax.dot_general(q4, kc, _NT, preferred_element_type=F32)
        p_c = _softmax_rows(s_c, (n_idx * CMP_STRIDE + (CMP_BLOCK - 1) <= t_row) & (n_idx < ncp - 1))
        o_c = jnp.dot(p_c.astype(BF16), vc_ref[0, 0, 0], preferred_element_type=F32)[:, 0:hd]
        _store_heads(oc_ref, o_c, tq, row0)
        _store_selection(sel_ref, _select_blocks(_sum_heads(p_c, tq), ov_ref, t0, tq, n_sel), row0)


def _nsa_select_kernel(sh_ref, q_ref, kc_ref, vc_ref, ov_ref, oc_ref, sel_ref, *, tq, n_sel):
    data = (q_ref, kc_ref, vc_ref, ov_ref, oc_ref, sel_ref)

    @pl.when(sh_ref[3] > 0.5)
    def _():
        _sel_fast(sh_ref, *data, tq=tq, n_sel=n_sel)

    @pl.when(sh_ref[3] <= 0.5)
    def _():
        _sel_slow(*data, tq=tq, n_sel=n_sel)


def _att_fast(sh_ref, q_ref, ks_ref, vs_ref, kw_ref, vw_ref, gl_ref, gp_ref, oc_ref, sel_ref, o_ref,
              lhs_ref, acc_ref, pre_ref, gm_ref, lay_ref, *, tq, tk):
    rep, hd = NSA_REP, NSA_HEAD_DIM
    rows = rep * tq
    t0 = pl.program_id(2) * tq
    c_s, c_w = sh_ref[1], sh_ref[2]
    q4 = q_ref[0].reshape(rows, hd)
    t_q = t0 + lax.broadcasted_iota(jnp.int32, (tq, 1), 0)

    j_lane = lax.broadcasted_iota(jnp.int32, (1, hd), 1)
    shift = jnp.where((sel_ref[0, 0].astype(F32) > 0.5) & (j_lane * SEL_BLOCK < t0), c_s, NEG_BIG).astype(BF16)
    lhs_ref[:, 0:hd] = q4
    for r in range(rep):
        lhs_ref[r * tq:(r + 1) * tq, hd:2 * hd] = shift

    span = WINDOW + tq
    w0 = pl.multiple_of(jnp.maximum(t0 - WINDOW, 0), tq)
    kp = w0 + lax.broadcasted_iota(jnp.int32, (1, span), 1)
    bias_w = jnp.where((kp <= t_q) & (kp > t_q - WINDOW), c_w, NEG_BIG)
    s_w = lax.dot_general(q4, kw_ref[0, 0, pl.ds(w0, span), :], _NT, preferred_element_type=F32)
    p_w = jnp.exp(s_w.reshape(rep, tq, span) + bias_w[None]).reshape(rows, span)
    pre_ref[...] = jnp.dot(p_w.astype(BF16), vw_ref[0, 0, pl.ds(w0, span), :], preferred_element_type=F32)

    d0 = pl.multiple_of(t0, tq)
    kcol = t0 + lax.broadcasted_iota(jnp.int32, (1, tq), 1)
    bias_d = jnp.where(kcol <= t_q, c_s, NEG_BIG)
    s_d = lax.dot_general(q4, ks_ref[0, 0, pl.ds(d0, tq), :][:, 0:hd], _NT, preferred_element_type=F32)
    p_d = jnp.exp(s_d.reshape(rep, tq, tq) + bias_d[None]).reshape(rows, tq)
    acc_ref[...] = jnp.dot(p_d.astype(BF16), vs_ref[0, 0, pl.ds(d0, tq), :], preferred_element_type=F32)

    def kv_step(c, carry):
        k0 = pl.multiple_of(c * tk, tk)
        s = lax.dot_general(lhs_ref[...], ks_ref[0, 0, pl.ds(k0, tk), :], _NT, preferred_element_type=F32)
        acc_ref[...] += jnp.dot(jnp.exp(s).astype(BF16), vs_ref[0, 0, pl.ds(k0, tk), :],
                                preferred_element_type=F32)
        return carry

    lax.fori_loop(0, (t0 + tk - 1) // tk, kv_step, 0)

    _to_lane_layout(lay_ref, 0, _normalize_aug(acc_ref[...]), tq)
    _to_lane_layout(lay_ref, 1, _normalize_aug(pre_ref[...]), tq)
    _gate_maps(gl_ref, gp_ref, gm_ref)
    _gated_store(gm_ref, oc_ref, lay_ref, o_ref)


def _att_slow(q_ref, ks_ref, vs_ref, kw_ref, vw_ref, gl_ref, gp_ref, oc_ref, sel_ref, o_ref,
              acc_ref, gm_ref, lay_ref, m_ref, l_ref, *, tq, tk):
    rep, hd = NSA_REP, NSA_HEAD_DIM
    rows = rep * tq
    t0 = pl.program_id(2) * tq
    q4 = q_ref[0].reshape(rows, hd)
    t_row = t0 + (lax.broadcasted_iota(jnp.int32, (rows, 1), 0) & (tq - 1))
    sel = sel_ref[0, 0]
    n_blk = sel.shape[1]

    m_ref[...] = jnp.full(m_ref.shape, NEG_BIG, F32)
    l_ref[...] = jnp.zeros(l_ref.shape, F32)
    acc_ref[...] = jnp.zeros(acc_ref.shape, F32)
    t_q = t0 + lax.broadcasted_iota(jnp.int32, (tq, 1), 0)

    def kv_step(c, carry):
        k0 = pl.multiple_of(c * tk, tk)
        kt = ks_ref[0, 0, pl.ds(k0, tk), :][:, 0:hd]
        s = lax.dot_general(q4, kt, _NT, preferred_element_type=F32).reshape(rep, tq, tk)
        kk = lax.broadcasted_iota(jnp.int32, (1, tk), 1)
        blk = (k0 >> 6) + (kk >> 6)
        expand = jnp.where(lax.broadcasted_iota(jnp.int32, (n_blk, 1), 0) == blk, 1.0, 0.0).astype(BF16)
        chosen = jnp.dot(sel, expand, preferred_element_type=F32)
        mask = ((chosen > 0.5) & (k0 + kk <= t_q))[None]
        s = jnp.where(mask, s, NEG_BIG)
        m_old = m_ref[...]
        m_new = jnp.maximum(m_old, jnp.max(s, axis=-1, keepdims=True))
        p = jnp.where(mask, jnp.exp(s - m_new), 0.0)
        alpha = jnp.exp(m_old - m_new)
        l_ref[...] = alpha * l_ref[...] + jnp.sum(p, axis=-1, keepdims=True)
        pv = jnp.dot(p.reshape(rows, tk).astype(BF16), vs_ref[0, 0, pl.ds(k0, tk), :], preferred_element_type=F32)
        acc_ref[...] = alpha.reshape(rows, 1) * acc_ref[...] + pv
        m_ref[...] = m_new
        return carry

    lax.fori_loop(0, (t0 + tq + tk - 1) // tk, kv_step, 0)
    o_s = acc_ref[:, 0:hd] / l_ref[...].reshape(rows, 1)

    span = WINDOW + tq
    w0 = pl.multiple_of(jnp.maximum(t0 - WINDOW, 0), tq)
    s_w = lax.dot_general(q4, kw_ref[0, 0, pl.ds(w0, span), :], _NT, preferred_element_type=F32)
    kp = w0 + lax.broadcasted_iota(jnp.int32, (1, span), 1)
    p_w = _softmax_rows(s_w, (kp <= t_row) & (kp > t_row - WINDOW))
    o_w = jnp.dot(p_w.astype(BF16), vw_ref[0, 0, pl.ds(w0, span), :], preferred_element_type=F32)[:, 0:hd]

    _to_lane_layout(lay_ref, 0, o_s, tq)
    _to_lane_layout(lay_ref, 1, o_w, tq)
    _gate_maps(gl_ref, gp_ref, gm_ref)
    _gated_store(gm_ref, oc_ref, lay_ref, o_ref)


def _nsa_attend_kernel(sh_ref, q_ref, ks_ref, vs_ref, kw_ref, vw_ref, gl_ref, gp_ref, oc_ref, sel_ref, o_ref,
                       lhs_ref, acc_ref, pre_ref, gm_ref, lay_ref, m_ref, l_ref, *, tq, tk):
    data = (q_ref, ks_ref, vs_ref, kw_ref, vw_ref, gl_ref, gp_ref, oc_ref, sel_ref, o_ref)

    @pl.when(sh_ref[3] > 0.5)
    def _():
        _att_fast(sh_ref, *data, lhs_ref, acc_ref, pre_ref, gm_ref, lay_ref, tq=tq, tk=tk)

    @pl.when(sh_ref[3] <= 0.5)
    def _():
        _att_slow(*data, acc_ref, gm_ref, lay_ref, m_ref, l_ref, tq=tq, tk=tk)


def _overlap_t(n_blk, ncp):
    c_start = np.arange(ncp) * CMP_STRIDE
    b_start = np.arange(n_blk) * SEL_BLOCK
    ov = ((c_start[None, :] < b_start[:, None] + SEL_BLOCK) & (b_start[:, None] < c_start[None, :] + CMP_BLOCK))
    ov[:, ncp - 1] = False
    return jnp.asarray(ov.astype(np.float32)).astype(BF16)


def _nsa_shifts(q_norm, k_norm):
    bound = (NSA_HEAD_DIM ** 0.5) * jnp.max(jnp.abs(q_norm)) * jnp.max(jnp.abs(k_norm), axis=-1)
    bound = bound.astype(BF16).astype(F32)
    fast = jnp.all(bound <= MAX_CONST_SHIFT).astype(F32)
    return jnp.concatenate([-bound, fast[None]])


def _nsa_select(shifts, q, cmp_kv, *, tq, parts):
    b, h, s, hd = q.shape
    g, rep = NSA_GROUPS, NSA_REP
    ncp = cmp_kv.shape[3]
    n_blk = s // SEL_BLOCK
    tb = tq * parts
    assert n_blk % 8 == 0 and n_blk <= hd and s % tb == 0
    kernel = functools.partial(_nsa_select_kernel, tq=tq, n_sel=min(N_SEL, n_blk))
    return pl.pallas_call(
        kernel,
        grid=(b, g, s // tb),
        in_specs=[
            pl.BlockSpec(memory_space=pltpu.SMEM),
            pl.BlockSpec((1, rep, tb, hd), lambda bi, gi, i: (bi, gi, i, 0)),
            pl.BlockSpec((1, 1, 1, ncp, 2 * hd), lambda bi, gi, i: (0, bi, gi, 0, 0)),
            pl.BlockSpec((1, 1, 1, ncp, 2 * hd), lambda bi, gi, i: (1, bi, gi, 0, 0)),
            pl.BlockSpec((n_blk, ncp), lambda bi, gi, i: (0, 0)),
        ],
        out_specs=[pl.BlockSpec((1, tb, rep * hd), lambda bi, gi, i: (bi, i, gi)),
                   pl.BlockSpec((1, 1, tb, hd), lambda bi, gi, i: (bi, gi, i, 0))],
        out_shape=[jax.ShapeDtypeStruct((b, s, h * hd), BF16), jax.ShapeDtypeStruct((b, g, s, hd), BF16)],
        compiler_params=_cparams("parallel", "parallel", "parallel"),
        name="nsa_select",
    )(shifts, q, cmp_kv, cmp_kv, _overlap_t(n_blk, ncp))


def _nsa_attend(shifts, q, ks, vs, kw, vw, z3, o_cmp, sel, *, tq, tk):
    b, h, s, hd = q.shape
    g, rep = NSA_GROUPS, NSA_REP
    rows = rep * tq
    assert WINDOW % tq == 0
    full = lambda w: pl.BlockSpec((1, 1, s, w), lambda bi, gi, i: (bi, gi, 0, 0))
    return pl.pallas_call(
        functools.partial(_nsa_attend_kernel, tq=tq, tk=tk),
        grid=(b, g, s // tq),
        in_specs=[
            pl.BlockSpec(memory_space=pltpu.SMEM),
            pl.BlockSpec((1, rep, tq, hd), lambda bi, gi, i: (bi, gi, i, 0)),
            full(2 * hd), full(2 * hd), full(hd), full(2 * hd),
            pl.BlockSpec((1, tq, LANES), lambda bi, gi, i: (bi, i, COL_SMALL // LANES + gi)),
            pl.BlockSpec((LANES, 3 * rep * hd), lambda bi, gi, i: (0, 0)),
            pl.BlockSpec((1, tq, rep * hd), lambda bi, gi, i: (bi, i, gi)),
            pl.BlockSpec((1, 1, tq, hd), lambda bi, gi, i: (bi, gi, i, 0)),
        ],
        out_specs=pl.BlockSpec((1, tq, rep * hd), lambda bi, gi, i: (bi, i, gi)),
        out_shape=jax.ShapeDtypeStruct((b, s, h * hd), BF16),
        scratch_shapes=[pltpu.VMEM((rows, 2 * hd), BF16), pltpu.VMEM((rows, 2 * hd), F32),
                        pltpu.VMEM((rows, 2 * hd), F32), pltpu.VMEM((tq, 3 * rep * hd), F32),
                        pltpu.VMEM((2, tq, rep * hd), F32),
                        pltpu.VMEM((rep, tq, 1), F32), pltpu.VMEM((rep, tq, 1), F32)],
        compiler_params=_cparams("parallel", "parallel", "arbitrary"),
        name="nsa_attend",
    )(shifts, q, ks, vs, kw, vw, z3, _gate_placement(), o_cmp, sel)


def _head_block_diag(x, n_rows, head_of_lane, dtype):
    return jnp.concatenate([jnp.where(head_of_lane == h, x, 0.0) for h in range(GLA_HEADS)], axis=0).astype(dtype)


def _gla_kernel(q_ref, k_ref, v_ref, r_ref, sm_ref, wg_ref, bg_ref, ng_ref, tril_ref, o_ref,
                st_ref, upd_ref, oin_ref, qin_ref, *, n_chunks):
    c, sub = GLA_CHUNK, GLA_SUB
    dk, dv, nh = GLA_HEAD_DK, GLA_HEAD_DV, GLA_HEADS
    nk, nv = nh * dk, nh * dv

    @pl.when(pl.program_id(1) == 0)
    def _():
        st_ref[...] = jnp.zeros(st_ref.shape, F32)

    sm = sm_ref[0]
    x = (jnp.dot(sm, wg_ref[0], preferred_element_type=F32) + jnp.dot(sm, wg_ref[1], preferred_element_type=F32)
         + bg_ref[...])
    log_a = (jnp.minimum(x, 0.0) - jnp.log1p(jnp.exp(-jnp.abs(x)))) / GLA_TAU
    la_hi = log_a.astype(BF16)
    la_lo = (log_a - la_hi.astype(F32)).astype(BF16)
    bcum_all = (jnp.dot(tril_ref[...], la_hi, preferred_element_type=F32)
                + jnp.dot(tril_ref[...], la_lo, preferred_element_type=F32))
    head_k = lax.broadcasted_iota(jnp.int32, (1, nk), 1) >> 6
    head_v = lax.broadcasted_iota(jnp.int32, (1, nv), 1) >> 7
    state_mask = (lax.broadcasted_iota(jnp.int32, (nv, 1), 0) >> 7) == head_k
    causal = ((lax.broadcasted_iota(jnp.int32, (c, nh * c), 1) & (c - 1))
              <= lax.broadcasted_iota(jnp.int32, (c, nh * c), 0))

    decays = []
    for cc in range(n_chunks):
        rs = slice(cc * c, (cc + 1) * c)
        bcum = bcum_all[rs]
        q = q_ref[0, rs].astype(F32) * (dk ** -0.5)
        k = k_ref[0, rs].astype(F32)
        v = v_ref[0, rs].astype(F32)
        b_last = bcum[c - 1:c, :]

        score_rows = []
        for i in range(c // sub):
            lo, hi = i * sub, (i + 1) * sub
            ref = bcum[lo - 1:lo, :] if i > 0 else jnp.zeros((1, nk), F32)
            q_i = (q[lo:hi] * jnp.exp(bcum[lo:hi] - ref)).astype(BF16)
            k_i = k[0:hi] * jnp.exp(ref - bcum[0:hi])
            if hi < c:
                k_i = jnp.concatenate([k_i, jnp.zeros((c - hi, nk), F32)], axis=0)
            score_rows.append(lax.dot_general(q_i, _head_block_diag(k_i, c, head_k, BF16), _NT,
                                              preferred_element_type=F32))
        a = jnp.where(causal, jnp.concatenate(score_rows, axis=0), 0.0).astype(BF16)
        oin_ref[rs] = jnp.dot(a, _head_block_diag(v, c, head_v, BF16), preferred_element_type=F32)
        qin_ref[rs] = (q * jnp.exp(bcum)).astype(BF16)
        k_out = (k * jnp.exp(b_last - bcum)).astype(BF16)
        upd = lax.dot_general(v.astype(BF16), k_out, (((0,), (0,)), ((), ())), preferred_element_type=F32)
        upd_ref[cc] = jnp.where(state_mask, upd, 0.0)
        decays.append(jnp.exp(b_last))

    for cc in range(n_chunks):
        rs = slice(cc * c, (cc + 1) * c)
        st = st_ref[...]
        o = oin_ref[rs] + lax.dot_general(qin_ref[rs], st.astype(BF16), _NT, preferred_element_type=F32)
        st_ref[...] = st * decays[cc] + upd_ref[cc]
        r_gate = r_ref[0, rs].astype(F32)
        for h in range(nh):
            vs_ = slice(h * dv, (h + 1) * dv)
            o_h = o[:, vs_]
            ms = jnp.mean(o_h * o_h, axis=-1, keepdims=True)
            rg = r_gate[:, vs_]
            o_ref[0, rs, vs_] = (o_h * lax.rsqrt(ms + NORM_EPS) * ng_ref[...] * (rg * jax.nn.sigmoid(rg))).astype(o_ref.dtype)


def _gla(z3, w_gate, b_gate, norm_g, *, n_chunks):
    b, s, _ = z3.shape
    c = GLA_CHUNK * n_chunks
    nk = GLA_HEADS * GLA_HEAD_DK
    nv = GLA_HEADS * GLA_HEAD_DV
    wg = jnp.zeros((LANES, nk), F32).at[SMALL_GLOW_LANE:SMALL_GLOW_LANE + GLA_RANK].set(w_gate.astype(F32))
    wg_hi = wg.astype(BF16)
    wg = jnp.stack([wg_hi, (wg - wg_hi.astype(F32)).astype(BF16)])
    idx = np.arange(c)
    tril = jnp.asarray(((idx[:, None] >= idx[None, :])
                        & (idx[:, None] // GLA_CHUNK == idx[None, :] // GLA_CHUNK)).astype(np.float32)).astype(BF16)
    const = lambda shape: pl.BlockSpec(shape, lambda bi, i: (0,) * len(shape))
    return pl.pallas_call(
        functools.partial(_gla_kernel, n_chunks=n_chunks),
        grid=(b, s // c),
        in_specs=[
            pl.BlockSpec((1, c, nk), lambda bi, i: (bi, i, COL_GQ // nk)),
            pl.BlockSpec((1, c, nk), lambda bi, i: (bi, i, COL_GK // nk)),
            pl.BlockSpec((1, c, nv), lambda bi, i: (bi, i, COL_GV // nv)),
            pl.BlockSpec((1, c, nv), lambda bi, i: (bi, i, COL_GR // nv)),
            pl.BlockSpec((1, c, LANES), lambda bi, i: (bi, i, COL_SMALL // LANES)),
            const((2, LANES, nk)), const((1, nk)), const((1, GLA_HEAD_DV)), const((c, c)),
        ],
        out_specs=pl.BlockSpec((1, c, nv), lambda bi, i: (bi, i, 0)),
        out_shape=jax.ShapeDtypeStruct((b, s, nv), BF16),
        scratch_shapes=[pltpu.VMEM((nv, nk), F32), pltpu.VMEM((n_chunks, nv, nk), F32),
                        pltpu.VMEM((c, nv), F32), pltpu.VMEM((c, nk), BF16)],
        compiler_params=_cparams("parallel", "arbitrary"),
        name="gla",
    )(z3, z3, z3, z3, z3, wg, b_gate.reshape(1, nk).astype(F32), norm_g.reshape(1, GLA_HEAD_DV).astype(F32), tril)


def _mem_attn_kernel(q_ref, k_ref, v_ref, qg_ref, kg_ref, o_ref):
    dh = MEM_HEAD_DIM
    nt = (((1,), (1,)), ((), ()))
    for h in range(MEM_HEADS):
        sl = slice(h * dh, (h + 1) * dh)
        q = q_ref[0, :, sl].astype(F32)
        q = q * lax.rsqrt(jnp.mean(q * q, axis=-1, keepdims=True) + NORM_EPS) * qg_ref[...] * (dh ** -0.5)
        k = k_ref[0, :, sl].astype(F32)
        k = k * lax.rsqrt(jnp.mean(k * k, axis=-1, keepdims=True) + NORM_EPS) * kg_ref[...]
        s = lax.dot_general(q.astype(BF16), k.astype(BF16), nt, preferred_element_type=F32)
        m = jnp.max(s, axis=-1, keepdims=True)
        p = jnp.exp(s - m)
        p = p / jnp.sum(p, axis=-1, keepdims=True)
        o = jnp.dot(p.astype(BF16), v_ref[0, :, sl].astype(BF16), preferred_element_type=F32)
        o_ref[0, :, sl] = o.astype(o_ref.dtype)


def _mem_attention(z3, kv, q_norm, k_norm, *, tq):
    b, s, _ = z3.shape
    m = kv.shape[1]
    w = MEM_HEADS * MEM_HEAD_DIM
    const = lambda shape: pl.BlockSpec(shape, lambda bi, i: (0,) * len(shape))
    return pl.pallas_call(
        _mem_attn_kernel,
        grid=(b, s // tq),
        in_specs=[
            pl.BlockSpec((1, tq, w), lambda bi, i: (bi, i, COL_MQ // w)),
            pl.BlockSpec((1, m, w), lambda bi, i: (bi, 0, 0)),
            pl.BlockSpec((1, m, w), lambda bi, i: (bi, 0, 1)),
            const((1, MEM_HEAD_DIM)), const((1, MEM_HEAD_DIM)),
        ],
        out_specs=pl.BlockSpec((1, tq, w), lambda bi, i: (bi, i, 0)),
        out_shape=jax.ShapeDtypeStruct((b, s, w), BF16),
        compiler_params=_cparams("parallel", "parallel"),
        name="mem_attention",
    )(z3, kv, kv, q_norm.reshape(1, MEM_HEAD_DIM).astype(F32), k_norm.reshape(1, MEM_HEAD_DIM).astype(F32))


def _merge_kernel(x_ref, on_ref, og_ref, om_ref, m0_ref, m1_ref, m2_ref, bm_ref, wb_ref, wo_ref, o_ref):
    merged = None
    for br, (ref, mg_ref) in enumerate(((on_ref, m0_ref), (og_ref, m1_ref), (om_ref, m2_ref))):
        y = jnp.dot(ref[...], wb_ref[br], preferred_element_type=F32)
        gate = jax.nn.sigmoid(mg_ref[...].astype(F32) + bm_ref[br:br + 1, :])
        merged = gate * y if merged is None else merged + gate * y
    o_ref[...] = x_ref[...] + jnp.dot(merged.astype(BF16), wo_ref[...], preferred_element_type=F32)


def _merge_out(x2, o_nsa, o_gla, o_mem, z2, b_merge, w_branch, w_out, *, tm):
    t, d = x2.shape
    bw = BRANCH_WIDTH
    row = lambda w: pl.BlockSpec((tm, w), lambda i: (i, 0))
    gate_cols = lambda br: pl.BlockSpec((tm, d), lambda i: (i, COL_MERGE // d + br))
    return pl.pallas_call(
        _merge_kernel,
        grid=(t // tm,),
        in_specs=[
            row(d), row(bw), row(bw), row(bw),
            gate_cols(0), gate_cols(1), gate_cols(2),
            pl.BlockSpec((N_BRANCH, d), lambda i: (0, 0)),
            pl.BlockSpec((N_BRANCH, bw, d), lambda i: (0, 0, 0)),
            pl.BlockSpec((d, d), lambda i: (0, 0)),
        ],
        out_specs=row(d),
        out_shape=jax.ShapeDtypeStruct((t, d), F32),
        compiler_params=_cparams("parallel"),
        name="merge_out",
    )(x2, o_nsa, o_gla, o_mem, z2, z2, z2, b_merge.astype(F32), w_branch, w_out)


def _mlp_kernel(x_ref, g_ref, wu_ref, wd_ref, o_ref, h_ref, acc_ref):
    j = pl.program_id(1)

    @pl.when(j == 0)
    def _():
        x = x_ref[...]
        ms = jnp.mean(x * x, axis=-1, keepdims=True)
        h_ref[...] = (x * lax.rsqrt(ms + NORM_EPS) * g_ref[...]).astype(h_ref.dtype)
        acc_ref[...] = jnp.zeros(acc_ref.shape, F32)

    u = jnp.dot(h_ref[...], wu_ref[...], preferred_element_type=F32)
    u = jnp.square(jnp.maximum(u, 0.0)).astype(BF16)
    acc_ref[...] += jnp.dot(u, wd_ref[...], preferred_element_type=F32)

    @pl.when(j == pl.num_programs(1) - 1)
    def _():
        o_ref[...] = x_ref[...] + acc_ref[...]


def _mlp(x2, g, w_up, w_down, *, tm, th):
    t, d = x2.shape
    hid = w_up.shape[1]
    return pl.pallas_call(
        _mlp_kernel,
        grid=(t // tm, hid // th),
        in_specs=[
            pl.BlockSpec((tm, d), lambda i, j: (i, 0)),
            pl.BlockSpec((1, d), lambda i, j: (0, 0)),
            pl.BlockSpec((d, th), lambda i, j: (0, j)),
            pl.BlockSpec((th, d), lambda i, j: (j, 0)),
        ],
        out_specs=pl.BlockSpec((tm, d), lambda i, j: (i, 0)),
        out_shape=jax.ShapeDtypeStruct((t, d), F32),
        scratch_shapes=[pltpu.VMEM((tm, d), BF16), pltpu.VMEM((tm, d), F32)],
        compiler_params=_cparams("parallel", "arbitrary"),
        name="mlp",
    )(x2, g.reshape(1, d).astype(F32), w_up, w_down)


def _tile(n, pref):
    t = min(n, pref)
    assert n % t == 0, (n, pref)
    return t


def _layer(x, mem2, pos_rows, pos_cmp, p):
    b, s, d = x.shape
    t = b * s
    hd, g = NSA_HEAD_DIM, NSA_GROUPS
    x2 = x.reshape(t, d)

    z2 = _norm_matmul(x2, p["ln_mix"].astype(F32), p["w_in"], tm=_tile(t, 1024), tn=D_IN_PAD // 2, out_dtype=BF16, name="in_proj")
    z3 = z2.reshape(b, s, D_IN_PAD)

    q, ks, vs, kw, vw = _nsa_prep(z3, pos_rows, p["nsa_q_norm"].astype(F32), p["nsa_k_norm"][1].astype(F32),
                                  p["nsa_k_norm"][2].astype(F32), tp=_tile(s, 256))
    pe2 = jnp.pad(p["cmp_pe"].reshape(2, 2, CMP_STRIDE * hd), ((0, 0), (0, 6), (0, 0))).astype(F32)
    cmp_kv = _nsa_compress(z3, p["cmp_w1"], p["cmp_w2"], pe2, pos_cmp, p["nsa_k_norm"][0].astype(F32))
    shifts = _nsa_shifts(p["nsa_q_norm"].astype(F32), p["nsa_k_norm"].astype(F32))
    o_cmp, sel = _nsa_select(shifts, q, cmp_kv, tq=_tile(s, 256), parts=2)
    o_nsa = _nsa_attend(shifts, q, ks, vs, kw, vw, z3, o_cmp, sel, tq=_tile(s, 256), tk=_tile(s, 512))

    o_gla = _gla(z3, p["gla_w_gate"], p["gla_b_gate"], p["gla_norm"], n_chunks=8)

    kv = _norm_matmul(mem2, p["mem_norm"].astype(F32), p["mem_w_kv"].astype(BF16),
                      tm=_tile(mem2.shape[0], 512), tn=512, out_dtype=BF16, name="mem_kv")
    kv = kv.reshape(b, mem2.shape[0] // b, 2 * MEM_HEADS * MEM_HEAD_DIM)
    o_mem = _mem_attention(z3, kv, p["mem_q_norm"], p["mem_k_norm"], tq=_tile(s, 512))

    x2 = _merge_out(x2, o_nsa.reshape(t, -1), o_gla.reshape(t, -1), o_mem.reshape(t, -1), z2,
                    p["b_merge"], p["w_branch"].astype(BF16), p["w_out"].astype(BF16), tm=_tile(t, 512))
    x2 = _mlp(x2, p["ln_mlp"], p["w_up"].astype(BF16), p["w_down"].astype(BF16), tm=_tile(t, 1024), th=1024)
    return x2.reshape(b, s, d)


def kernel(x, mem, positions, ln_mix, w_in, b_merge, nsa_q_norm, nsa_k_norm, cmp_pe, cmp_w1, cmp_w2,
           gla_w_gate, gla_b_gate, gla_norm, mem_norm, mem_w_kv, mem_q_norm, mem_k_norm, w_branch, w_out,
           ln_mlp, w_up, w_down):
    b, s, d = x.shape
    assert d == 1024 and s % WINDOW == 0 and s >= 2 * WINDOW
    depth = w_in.shape[0]
    perm_np, d_in = _in_proj_permutation(d)
    assert w_in.shape[2] == d_in
    w_in = jnp.take(jnp.pad(w_in, ((0, 0), (0, 0), (0, 1))), jnp.asarray(perm_np), axis=2).astype(BF16)
    half = CMP_STRIDE * NSA_HEAD_DIM
    cmp_w1 = jnp.concatenate([cmp_w1[:, :, :half], cmp_w1[:, :, half:]], axis=-1).astype(BF16)
    cmp_w2, mem_w_kv, w_branch, w_out, w_up, w_down = (
        a.astype(BF16) for a in (cmp_w2, mem_w_kv, w_branch, w_out, w_up, w_down))
    pos3 = positions.astype(jnp.int32).reshape(b, s, 1)
    nc = s // CMP_STRIDE
    cmp_end = np.minimum(np.arange(nc) * CMP_STRIDE + CMP_BLOCK - 1, s - 1)
    pos_cmp = pos3[:, cmp_end, :]
    pos_rows = positions.astype(jnp.int32).reshape(b, 1, s)
    mem2 = mem.reshape(b * mem.shape[1], d)
    names = ("ln_mix", "w_in", "b_merge", "nsa_q_norm", "nsa_k_norm", "cmp_pe", "cmp_w1", "cmp_w2",
             "gla_w_gate", "gla_b_gate", "gla_norm", "mem_norm", "mem_w_kv", "mem_q_norm", "mem_k_norm",
             "w_branch", "w_out", "ln_mlp", "w_up", "w_down")
    stacked = (ln_mix, w_in, b_merge, nsa_q_norm, nsa_k_norm, cmp_pe, cmp_w1, cmp_w2, gla_w_gate, gla_b_gate,
               gla_norm, mem_norm, mem_w_kv, mem_q_norm, mem_k_norm, w_branch, w_out, ln_mlp, w_up, w_down)
    for l in range(depth):
        x = _layer(x, mem2, pos_rows, pos_cmp, {n: a[l] for n, a in zip(names, stacked)})
    return x
```

```python
import functools

import numpy as np
import jax
import jax.numpy as jnp
from jax import lax
from jax.experimental import pallas as pl
from jax.experimental.pallas import tpu as pltpu

NSA_HEADS = 8
NSA_GROUPS = 2
NSA_REP = NSA_HEADS // NSA_GROUPS
NSA_HEAD_DIM = 64
CMP_BLOCK = 32
CMP_STRIDE = 16
CMP_HIDDEN = 4 * NSA_HEAD_DIM
SEL_BLOCK = 64
N_SEL = 16
WINDOW = 512
FORCE_SCORE = 1e4
GLA_HEADS = 4
GLA_HEAD_DK = 64
GLA_HEAD_DV = 128
GLA_RANK = 16
GLA_TAU = 16.0
GLA_CHUNK = 64
GLA_SUB = 16
MEM_HEADS = 4
MEM_HEAD_DIM = 128
N_BRANCH = 3
BRANCH_WIDTH = 512
ROPE_THETA = 500000.0
ROPE_ROT = NSA_HEAD_DIM // 4
ROPE_HALF = ROPE_ROT // 2
NORM_EPS = 1e-6

LANES = 128
VMEM_LIMIT_BYTES = 48 * 1024 * 1024

F32 = jnp.float32
BF16 = jnp.bfloat16
HIGHEST = lax.Precision.HIGHEST
NEG_BIG = -1e30
MAX_CONST_SHIFT = 40.0

COL_NQ = 0
COL_GV = 512
COL_GR = 1024
COL_MQ = 1536
COL_MERGE = 2048
COL_GQ = 5120
COL_GK = 5376
COL_KV = 5632
COL_SMALL = 6400
D_IN_PAD = 6656
SMALL_GLOW_LANE = 12


def _in_proj_permutation(d_model):
    sizes = (512, 128, 128, 128, 128, 128, 128, 24, 256, 256, 512, 512, 16, 512, 3 * d_model)
    off = np.concatenate([[0], np.cumsum(sizes)])
    (o_nq, o_kc, o_vc, o_ks, o_vs, o_kw, o_vw, o_ng, o_gq, o_gk, o_gv, o_gr, o_gl, o_mq, o_mg) = off[:-1]
    d_in = int(off[-1])
    perm = np.full((D_IN_PAD,), d_in, np.int32)

    def put(new, old, n):
        perm[new:new + n] = np.arange(old, old + n)

    put(COL_NQ, o_nq, 512)
    put(COL_GV, o_gv, 512)
    put(COL_GR, o_gr, 512)
    put(COL_MQ, o_mq, 512)
    put(COL_MERGE, o_mg, 3 * d_model)
    put(COL_GQ, o_gq, 256)
    put(COL_GK, o_gk, 256)
    put(COL_KV, o_kc, 768)
    for g in range(NSA_GROUPS):
        for br in range(3):
            for r in range(NSA_REP):
                perm[COL_SMALL + g * LANES + br * NSA_REP + r] = o_ng + (g * NSA_REP + r) * 3 + br
    put(COL_SMALL + SMALL_GLOW_LANE, o_gl, GLA_RANK)
    return perm, d_in


def _relayout_in_proj(w, perm, d_in):
    pieces, start = [], 0
    for i in range(1, len(perm) + 1):
        prev = perm[i - 1]
        if i < len(perm) and (perm[i] == prev + 1 if prev != d_in else perm[i] == d_in):
            continue
        n = i - start
        if prev == d_in:
            pieces.append(jnp.zeros(w.shape[:-1] + (n,), w.dtype))
        else:
            pieces.append(w[..., perm[start]:perm[start] + n])
        start = i
    return jnp.concatenate(pieces, axis=-1)


def _cparams(*sem):
    return pltpu.CompilerParams(dimension_semantics=sem, vmem_limit_bytes=VMEM_LIMIT_BYTES)


def _norm_matmul_kernel(x_ref, g_ref, w_ref, o_ref, h_ref):
    @pl.when(pl.program_id(1) == 0)
    def _():
        x = x_ref[...].astype(F32)
        ms = jnp.mean(x * x, axis=-1, keepdims=True)
        h_ref[...] = (x * lax.rsqrt(ms + NORM_EPS) * g_ref[...]).astype(h_ref.dtype)

    o_ref[...] = jnp.dot(h_ref[...], w_ref[...], preferred_element_type=F32).astype(o_ref.dtype)


def _norm_matmul(x, g, w, layer, *, tm, tn, out_dtype, name):
    m, k = x.shape
    n = w.shape[2]
    return pl.pallas_call(
        _norm_matmul_kernel,
        grid=(m // tm, n // tn),
        in_specs=[
            pl.BlockSpec((tm, k), lambda i, j: (i, 0)),
            pl.BlockSpec((1, k), lambda i, j: (0, 0)),
            pl.BlockSpec((None, k, tn), lambda i, j: (layer, 0, j)),
        ],
        out_specs=pl.BlockSpec((tm, tn), lambda i, j: (i, j)),
        out_shape=jax.ShapeDtypeStruct((m, n), out_dtype),
        scratch_shapes=[pltpu.VMEM((tm, k), BF16)],
        compiler_params=_cparams("parallel", "arbitrary"),
        name=name,
    )(x, g.reshape(1, k), w)


def _rope_tables(pos_row, freq_col, place):
    ang = freq_col * pos_row
    tn = (((0,), (0,)), ((), ()))
    lane = lax.broadcasted_iota(jnp.int32, (1, LANES), 1)
    c = lax.dot_general(jnp.cos(ang), place, tn, preferred_element_type=F32, precision=HIGHEST)
    s = lax.dot_general(jnp.sin(ang), place, tn, preferred_element_type=F32, precision=HIGHEST)
    return c + jnp.where((lane & (NSA_HEAD_DIM - 1)) >= ROPE_ROT, 1.0, 0.0), s


def _rope_lanes(x, c, s):
    n = x.shape[-1]
    if n > LANES:
        c = jnp.concatenate([c] * (n // LANES), axis=1)
        s = jnp.concatenate([s] * (n // LANES), axis=1)
    lane = lax.broadcasted_iota(jnp.int32, (1, n), 1) & (NSA_HEAD_DIM - 1)
    up = pltpu.roll(x, n - ROPE_HALF, axis=1)
    dn = pltpu.roll(x, ROPE_HALF, axis=1)
    y = jnp.where(lane < ROPE_HALF, -up, jnp.where(lane < ROPE_ROT, dn, 0.0))
    return x * c + y * s


def _head_rms(x, bd, g):
    x2 = x * x
    hi = x2.astype(BF16)
    lo = (x2 - hi.astype(F32)).astype(BF16)
    ms = (jnp.dot(hi, bd, preferred_element_type=F32) + jnp.dot(lo, bd, preferred_element_type=F32))
    return x * lax.rsqrt(ms * (1.0 / NSA_HEAD_DIM) + NORM_EPS) * g


def _nsa_prep_kernel(q_ref, ks_ref, vs_ref, kw_ref, vw_ref, pos_ref, qg_ref, ksg_ref, kwg_ref,
                     bdq_ref, bdk_ref, f_ref, place_ref,
                     qo_ref, kso_ref, vso_ref, kwo_ref, vwo_ref):
    hd = NSA_HEAD_DIM
    tp = q_ref.shape[1]
    c, s = _rope_tables(pos_ref[0].astype(F32), f_ref[...], place_ref[...])
    q = _head_rms(q_ref[0].astype(F32), bdq_ref[...], qg_ref[...])
    q = (_rope_lanes(q, c, s) * (hd ** -0.5)).astype(qo_ref.dtype)
    for h in range(NSA_HEADS):
        qo_ref[0, h] = q[:, h * hd:(h + 1) * hd]
    ks = _rope_lanes(_head_rms(ks_ref[0].astype(F32), bdk_ref[...], ksg_ref[...]), c, s)
    kw = _rope_lanes(_head_rms(kw_ref[0].astype(F32), bdk_ref[...], kwg_ref[...]), c, s)
    vs = vs_ref[0]
    vw = vw_ref[0]
    tok = pl.program_id(1) * tp + lax.broadcasted_iota(jnp.int32, (tp, hd), 0)
    lane = lax.broadcasted_iota(jnp.int32, (tp, hd), 1)
    blk_onehot = jnp.where((tok >> 6) == lane, 1.0, 0.0).astype(kso_ref.dtype)
    ones_col = jnp.ones((tp, hd), vso_ref.dtype)
    for g in range(NSA_GROUPS):
        sl = slice(g * hd, (g + 1) * hd)
        kso_ref[0, g, :, 0:hd] = ks[:, sl].astype(kso_ref.dtype)
        kso_ref[0, g, :, hd:2 * hd] = blk_onehot
        kwo_ref[0, g] = kw[:, sl].astype(kwo_ref.dtype)
        vso_ref[0, g, :, 0:hd] = vs[:, sl].astype(vso_ref.dtype)
        vso_ref[0, g, :, hd:2 * hd] = ones_col
        vwo_ref[0, g, :, 0:hd] = vw[:, sl].astype(vwo_ref.dtype)
        vwo_ref[0, g, :, hd:2 * hd] = ones_col


def _block_diag_ones(n, width):
    i = np.arange(n)
    return jnp.asarray((i[:, None] // width == i[None, :] // width).astype(np.float32)).astype(BF16)


def _rope_freq_col():
    inv = ROPE_THETA ** (-(np.arange(ROPE_HALF, dtype=np.float64)) / ROPE_HALF)
    return jnp.asarray(inv.astype(np.float32)).reshape(ROPE_HALF, 1)


def _rope_placement():
    m = np.zeros((ROPE_HALF, LANES), np.float32)
    for f in range(ROPE_HALF):
        for base in range(0, LANES, NSA_HEAD_DIM):
            m[f, base + f] = 1.0
            m[f, base + f + ROPE_HALF] = 1.0
    return jnp.asarray(m)


def _nsa_prep(z3, pos_rows, q_norm, ks_norm, kw_norm, *, tp):
    b, s, _ = z3.shape
    hd, g = NSA_HEAD_DIM, NSA_GROUPS
    assert SEL_BLOCK == 64 and s // SEL_BLOCK <= hd
    kvb = COL_KV // LANES
    qg = jnp.tile(q_norm, NSA_HEADS).reshape(1, NSA_HEADS * hd)
    ksg = jnp.tile(ks_norm, g).reshape(1, g * hd)
    kwg = jnp.tile(kw_norm, g).reshape(1, g * hd)
    const = lambda shape: pl.BlockSpec(shape, lambda bi, i: (0,) * len(shape))
    plain_out = jax.ShapeDtypeStruct((b, g, s, hd), BF16)
    plain_spec = pl.BlockSpec((1, g, tp, hd), lambda bi, i: (bi, 0, i, 0))
    aug_out = jax.ShapeDtypeStruct((b, g, s, 2 * hd), BF16)
    aug_spec = pl.BlockSpec((1, g, tp, 2 * hd), lambda bi, i: (bi, 0, i, 0))
    return pl.pallas_call(
        _nsa_prep_kernel,
        grid=(b, s // tp),
        in_specs=[
            pl.BlockSpec((1, tp, 512), lambda bi, i: (bi, i, COL_NQ // 512)),
            pl.BlockSpec((1, tp, LANES), lambda bi, i: (bi, i, kvb + 2)),
            pl.BlockSpec((1, tp, LANES), lambda bi, i: (bi, i, kvb + 3)),
            pl.BlockSpec((1, tp, LANES), lambda bi, i: (bi, i, kvb + 4)),
            pl.BlockSpec((1, tp, LANES), lambda bi, i: (bi, i, kvb + 5)),
            pl.BlockSpec((1, 1, tp), lambda bi, i: (bi, 0, i)),
            const((1, 512)), const((1, LANES)), const((1, LANES)),
            const((512, 512)), const((LANES, LANES)),
            const((ROPE_HALF, 1)), const((ROPE_HALF, LANES)),
        ],
        out_specs=[
            pl.BlockSpec((1, NSA_HEADS, tp, hd), lambda bi, i: (bi, 0, i, 0)),
            aug_spec, aug_spec, plain_spec, aug_spec,
        ],
        out_shape=[jax.ShapeDtypeStruct((b, NSA_HEADS, s, hd), BF16), aug_out, aug_out, plain_out, aug_out],
        compiler_params=_cparams("parallel", "parallel"),
        name="nsa_prep",
    )(z3, z3, z3, z3, z3, pos_rows, qg, ksg, kwg,
      _block_diag_ones(512, hd), _block_diag_ones(LANES, hd), _rope_freq_col(), _rope_placement())


def _gelu_tanh(x):
    return 0.5 * x * (1.0 + jnp.tanh(np.sqrt(2.0 / np.pi) * (x + 0.044715 * x * x * x)))


def _nsa_cmp_kernel(x_ref, w1_ref, w2_ref, pe_ref, pos_ref, g_ref, rot_ref, f_ref, o_ref, xs_ref):
    kind = pl.program_id(0)
    hd = NSA_HEAD_DIM
    nc = x_ref.shape[1] // CMP_STRIDE
    xs_ref[...] = x_ref[0].astype(F32)
    r = jnp.dot(pe_ref[0].astype(BF16), w1_ref[0], preferred_element_type=F32)
    ab = [jnp.zeros((nc, 2 * CMP_HIDDEN), F32) for _ in range(NSA_GROUPS)]
    for p in range(CMP_STRIDE):
        xp = xs_ref[pl.ds(p, nc, stride=CMP_STRIDE), :].astype(BF16)
        w1p = w1_ref[0, p * hd:(p + 1) * hd, :]
        for g in range(NSA_GROUPS):
            ab[g] = ab[g] + jnp.dot(xp[:, g * hd:(g + 1) * hd], w1p, preferred_element_type=F32)

    for g in range(NSA_GROUPS):
        a = ab[g][:, :CMP_HIDDEN] + r[0:1, :CMP_HIDDEN]
        bm = ab[g][:, CMP_HIDDEN:] + r[1:2, CMP_HIDDEN:]
        hid = _gelu_tanh(a + pltpu.roll(bm, nc - 1, axis=0))
        comp = jnp.dot(hid.astype(BF16), w2_ref[0], preferred_element_type=F32)

        @pl.when(kind == 0)
        def _():
            ms = jnp.mean(comp * comp, axis=-1, keepdims=True)
            kn = comp * lax.rsqrt(ms + NORM_EPS) * g_ref[...]
            ang = pos_ref[0].astype(F32) * f_ref[...]
            y = jnp.dot(kn, rot_ref[...], preferred_element_type=F32, precision=HIGHEST)
            o_ref[0, 0, g, :, 0:hd] = (kn * jnp.cos(ang) + y * jnp.sin(ang)).astype(o_ref.dtype)
            o_ref[0, 0, g, :, hd:2 * hd] = jnp.zeros((nc, hd), o_ref.dtype)

        @pl.when(kind != 0)
        def _():
            o_ref[0, 0, g, :, 0:hd] = comp.astype(o_ref.dtype)
            o_ref[0, 0, g, :, hd:2 * hd] = jnp.ones((nc, hd), o_ref.dtype)


def _rope_freq_head():
    p = np.arange(NSA_HEAD_DIM)
    inv = ROPE_THETA ** (-(np.arange(ROPE_HALF, dtype=np.float64)) / ROPE_HALF)
    f = np.where(p < ROPE_ROT, inv[p % ROPE_HALF], 0.0)
    return jnp.asarray(f.astype(np.float32)).reshape(1, NSA_HEAD_DIM)


def _rope_rot_matrix(n):
    m = np.zeros((n, n), np.float32)
    for base in range(0, n, NSA_HEAD_DIM):
        for l in range(ROPE_HALF):
            m[base + l + ROPE_HALF, base + l] = -1.0
            m[base + l, base + l + ROPE_HALF] = 1.0
    return jnp.asarray(m)


def _nsa_compress(z3, w1cat, w2, pe2, pos_cmp, k_norm0):
    b, s, _ = z3.shape
    hd, g = NSA_HEAD_DIM, NSA_GROUPS
    nc = s // CMP_STRIDE
    assert g * hd == LANES
    const = lambda shape: pl.BlockSpec(shape, lambda k, bi: (0,) * len(shape))
    return pl.pallas_call(
        _nsa_cmp_kernel,
        grid=(2, b),
        in_specs=[
            pl.BlockSpec((1, s, LANES), lambda k, bi: (bi, 0, COL_KV // LANES + k)),
            pl.BlockSpec((1, CMP_STRIDE * hd, 2 * CMP_HIDDEN), lambda k, bi: (k, 0, 0)),
            pl.BlockSpec((1, CMP_HIDDEN, hd), lambda k, bi: (k, 0, 0)),
            pl.BlockSpec((1, 8, CMP_STRIDE * hd), lambda k, bi: (k, 0, 0)),
            pl.BlockSpec((1, nc, 1), lambda k, bi: (bi, 0, 0)),
            const((1, hd)), const((hd, hd)), const((1, hd)),
        ],
        out_specs=pl.BlockSpec((1, 1, g, nc, 2 * hd), lambda k, bi: (k, bi, 0, 0, 0)),
        out_shape=jax.ShapeDtypeStruct((2, b, g, nc, 2 * hd), BF16),
        scratch_shapes=[pltpu.VMEM((s, LANES), F32)],
        compiler_params=_cparams("parallel", "parallel"),
        name="nsa_compress",
    )(z3, w1cat, w2, pe2, pos_cmp, k_norm0.reshape(1, hd), _rope_rot_matrix(hd), _rope_freq_head())


def _softmax_rows(s, mask):
    s = jnp.where(mask, s, NEG_BIG)
    m = jnp.max(s, axis=-1, keepdims=True)
    p = jnp.where(mask, jnp.exp(s - m), 0.0)
    d = jnp.sum(p, axis=-1, keepdims=True)
    return p / jnp.where(d > 0, d, 1.0)


_NT = (((1,), (1,)), ((), ()))


def _select_blocks(p_sum, ov_ref, t0, tq, n_sel):
    n_blk = ov_ref.shape[0]
    p_hi = p_sum.astype(BF16)
    p_lo = (p_sum - p_hi.astype(F32)).astype(BF16)
    imp_t = (lax.dot_general(ov_ref[...], p_hi, _NT, preferred_element_type=F32)
             + lax.dot_general(ov_ref[...], p_lo, _NT, preferred_element_type=F32))
    j_col = lax.broadcasted_iota(jnp.int32, (n_blk, 1), 0)
    t_lane = t0 + lax.broadcasted_iota(jnp.int32, (1, tq), 1)
    causal = j_col * SEL_BLOCK <= t_lane
    cur = t_lane >> 6
    forced = causal & ((j_col == 0) | (j_col == cur) | (j_col == cur - 1))
    score = jnp.where(forced, FORCE_SCORE, jnp.where(causal, imp_t, -FORCE_SCORE))
    ng = n_blk // 8
    groups = [score[8 * v:8 * v + 8] for v in range(ng)]
    ranks = [jnp.zeros((8, tq), F32) for _ in range(ng)]
    sub = lax.broadcasted_iota(jnp.int32, (8, tq), 0)
    for jp in range(n_blk):
        row = jnp.broadcast_to(score[jp:jp + 1, :], (8, tq))
        vj = jp // 8
        for v in range(ng):
            if v < vj:
                ahead = row > groups[v]
            elif v > vj:
                ahead = row >= groups[v]
            else:
                ahead = (row > groups[v]) | ((row == groups[v]) & (sub > jp % 8))
            ranks[v] = ranks[v] + jnp.where(ahead, 1.0, 0.0)
    sel_t = jnp.where(jnp.concatenate(ranks, axis=0) < n_sel, 1.0, 0.0).astype(BF16)
    eye = (lax.broadcasted_iota(jnp.int32, (tq, tq), 0) == lax.broadcasted_iota(jnp.int32, (tq, tq), 1)).astype(BF16)
    return lax.dot_general(eye, sel_t, _NT, preferred_element_type=F32)


def _sum_heads(p, tq):
    out = p[0:tq]
    for r in range(1, NSA_REP):
        out = out + p[r * tq:(r + 1) * tq]
    return out


def _store_heads(o_ref, o, tq, row0):
    hd = NSA_HEAD_DIM
    for r in range(NSA_REP):
        o_ref[0, row0:row0 + tq, r * hd:(r + 1) * hd] = o[r * tq:(r + 1) * tq].astype(o_ref.dtype)


def _store_selection(sel_ref, sel, row0):
    tq, n_blk = sel.shape
    sel = sel.astype(sel_ref.dtype)
    if n_blk < NSA_HEAD_DIM:
        sel = jnp.concatenate([sel, jnp.zeros((tq, NSA_HEAD_DIM - n_blk), sel_ref.dtype)], axis=1)
    sel_ref[0, 0, row0:row0 + tq] = sel


def _normalize_aug(ol):
    return ol * pltpu.roll(1.0 / ol, NSA_HEAD_DIM, axis=1)


def _gate_maps(gl_ref, gp_ref, gm_ref):
    gates = jax.nn.sigmoid(gl_ref[0].astype(F32))
    g_hi = gates.astype(BF16)
    g_lo = (gates - g_hi.astype(F32)).astype(BF16)
    gm_ref[...] = (jnp.dot(g_hi, gp_ref[...], preferred_element_type=F32)
                   + jnp.dot(g_lo, gp_ref[...], preferred_element_type=F32))


def _to_lane_layout(lay_ref, idx, o, tq):
    hd = NSA_HEAD_DIM
    for r in range(NSA_REP):
        lay_ref[idx, :, r * hd:(r + 1) * hd] = o[r * tq:(r + 1) * tq, 0:hd]


def _gated_store(gm_ref, oc_ref, lay_ref, o_ref):
    w = NSA_REP * NSA_HEAD_DIM
    out = (gm_ref[:, 0:w] * oc_ref[0].astype(F32) + gm_ref[:, w:2 * w] * lay_ref[0]
           + gm_ref[:, 2 * w:3 * w] * lay_ref[1])
    o_ref[0] = out.astype(o_ref.dtype)


def _gate_placement():
    m = np.zeros((LANES, 3 * NSA_REP * NSA_HEAD_DIM), np.float32)
    for br in range(3):
        for r in range(NSA_REP):
            c0 = br * NSA_REP * NSA_HEAD_DIM + r * NSA_HEAD_DIM
            m[br * NSA_REP + r, c0:c0 + NSA_HEAD_DIM] = 1.0
    return jnp.asarray(m).astype(BF16)


def _sel_fast(sh_ref, q_ref, kc_ref, vc_ref, ov_ref, oc_ref, sel_ref, *, tq, n_sel):
    rep, hd = NSA_REP, NSA_HEAD_DIM
    rows = rep * tq
    kc = kc_ref[0, 0, 0][:, 0:hd]
    ncp = kc.shape[0]
    n_idx = lax.broadcasted_iota(jnp.int32, (1, ncp), 1)
    for part in range(q_ref.shape[2] // tq):
        row0 = part * tq
        t0 = pl.program_id(2) * q_ref.shape[2] + row0
        q4 = q_ref[0, :, row0:row0 + tq, :].reshape(rows, hd)
        t_q = t0 + lax.broadcasted_iota(jnp.int32, (tq, 1), 0)
        valid_c = (n_idx * CMP_STRIDE + (CMP_BLOCK - 1) <= t_q) & (n_idx < ncp - 1)
        bias_c = jnp.where(valid_c, sh_ref[0], NEG_BIG)
        s_c = lax.dot_general(q4, kc, _NT, preferred_element_type=F32).reshape(rep, tq, ncp)
        p_c = jnp.exp(s_c + bias_c[None]).reshape(rows, ncp)
        ol_c = jnp.dot(p_c.astype(BF16), vc_ref[0, 0, 0], preferred_element_type=F32)
        inv_c = 1.0 / jnp.where(ol_c[:, hd:hd + 1] > 0, ol_c[:, hd:hd + 1], 1.0)
        _store_heads(oc_ref, ol_c[:, 0:hd] * inv_c, tq, row0)
        _store_selection(sel_ref, _select_blocks(_sum_heads(p_c * inv_c, tq), ov_ref, t0, tq, n_sel), row0)


def _sel_slow(q_ref, kc_ref, vc_ref, ov_ref, oc_ref, sel_ref, *, tq, n_sel):
    rep, hd = NSA_REP, NSA_HEAD_DIM
    rows = rep * tq
    kc = kc_ref[0, 0, 0][:, 0:hd]
    ncp = kc.shape[0]
    n_idx = lax.broadcasted_iota(jnp.int32, (1, ncp), 1)
    for part in range(q_ref.shape[2] // tq):
        row0 = part * tq
        t0 = pl.program_id(2) * q_ref.shape[2] + row0
        q4 = q_ref[0, :, row0:row0 + tq, :].reshape(rows, hd)
        t_row = t0 + (lax.broadcasted_iota(jnp.int32, (rows, 1), 0) & (tq - 1))
        s_c = lax.dot_general(q4, kc, _NT, preferred_element_type=F32)
        p_c = _softmax_rows(s_c, (n_idx * CMP_STRIDE + (CMP_BLOCK - 1) <= t_row) & (n_idx < ncp - 1))
        o_c = jnp.dot(p_c.astype(BF16), vc_ref[0, 0, 0], preferred_element_type=F32)[:, 0:hd]
        _store_heads(oc_ref, o_c, tq, row0)
        _store_selection(sel_ref, _select_blocks(_sum_heads(p_c, tq), ov_ref, t0, tq, n_sel), row0)


def _nsa_select_kernel(sh_ref, q_ref, kc_ref, vc_ref, ov_ref, oc_ref, sel_ref, *, tq, n_sel):
    data = (q_ref, kc_ref, vc_ref, ov_ref, oc_ref, sel_ref)

    @pl.when(sh_ref[3] > 0.5)
    def _():
        _sel_fast(sh_ref, *data, tq=tq, n_sel=n_sel)

    @pl.when(sh_ref[3] <= 0.5)
    def _():
        _sel_slow(*data, tq=tq, n_sel=n_sel)


def _att_fast(sh_ref, q_ref, ks_ref, vs_ref, kw_ref, vw_ref, gl_ref, gp_ref, oc_ref, sel_ref, o_ref,
              lhs_ref, acc_ref, pre_ref, gm_ref, lay_ref, *, tq, tk):
    rep, hd = NSA_REP, NSA_HEAD_DIM
    rows = rep * tq
    t0 = pl.program_id(2) * tq
    c_s, c_w = sh_ref[1], sh_ref[2]
    q4 = q_ref[0].reshape(rows, hd)
    t_q = t0 + lax.broadcasted_iota(jnp.int32, (tq, 1), 0)

    j_lane = lax.broadcasted_iota(jnp.int32, (1, hd), 1)
    shift = jnp.where((sel_ref[0, 0].astype(F32) > 0.5) & (j_lane * SEL_BLOCK < t0), c_s, NEG_BIG).astype(BF16)
    lhs_ref[:, 0:hd] = q4
    for r in range(rep):
        lhs_ref[r * tq:(r + 1) * tq, hd:2 * hd] = shift

    span = WINDOW + tq
    w0 = pl.multiple_of(jnp.maximum(t0 - WINDOW, 0), tq)
    kp = w0 + lax.broadcasted_iota(jnp.int32, (1, span), 1)
    bias_w = jnp.where((kp <= t_q) & (kp > t_q - WINDOW), c_w, NEG_BIG)
    s_w = lax.dot_general(q4, kw_ref[0, 0, pl.ds(w0, span), :], _NT, preferred_element_type=F32)
    p_w = jnp.exp(s_w.reshape(rep, tq, span) + bias_w[None]).reshape(rows, span)
    pre_ref[...] = jnp.dot(p_w.astype(BF16), vw_ref[0, 0, pl.ds(w0, span), :], preferred_element_type=F32)

    d0 = pl.multiple_of(t0, tq)
    kcol = t0 + lax.broadcasted_iota(jnp.int32, (1, tq), 1)
    bias_d = jnp.where(kcol <= t_q, c_s, NEG_BIG)
    s_d = lax.dot_general(q4, ks_ref[0, 0, pl.ds(d0, tq), :][:, 0:hd], _NT, preferred_element_type=F32)
    p_d = jnp.exp(s_d.reshape(rep, tq, tq) + bias_d[None]).reshape(rows, tq)
    acc_ref[...] = jnp.dot(p_d.astype(BF16), vs_ref[0, 0, pl.ds(d0, tq), :], preferred_element_type=F32)

    def kv_step(c, carry):
        k0 = pl.multiple_of(c * tk, tk)
        s = lax.dot_general(lhs_ref[...], ks_ref[0, 0, pl.ds(k0, tk), :], _NT, preferred_element_type=F32)
        acc_ref[...] += jnp.dot(jnp.exp(s).astype(BF16), vs_ref[0, 0, pl.ds(k0, tk), :],
                                preferred_element_type=F32)
        return carry

    lax.fori_loop(0, (t0 + tk - 1) // tk, kv_step, 0)

    _to_lane_layout(lay_ref, 0, _normalize_aug(acc_ref[...]), tq)
    _to_lane_layout(lay_ref, 1, _normalize_aug(pre_ref[...]), tq)
    _gate_maps(gl_ref, gp_ref, gm_ref)
    _gated_store(gm_ref, oc_ref, lay_ref, o_ref)


def _att_slow(q_ref, ks_ref, vs_ref, kw_ref, vw_ref, gl_ref, gp_ref, oc_ref, sel_ref, o_ref,
              acc_ref, gm_ref, lay_ref, m_ref, l_ref, *, tq, tk):
    rep, hd = NSA_REP, NSA_HEAD_DIM
    rows = rep * tq
    t0 = pl.program_id(2) * tq
    q4 = q_ref[0].reshape(rows, hd)
    t_row = t0 + (lax.broadcasted_iota(jnp.int32, (rows, 1), 0) & (tq - 1))
    sel = sel_ref[0, 0]
    n_blk = sel.shape[1]

    m_ref[...] = jnp.full(m_ref.shape, NEG_BIG, F32)
    l_ref[...] = jnp.zeros(l_ref.shape, F32)
    acc_ref[...] = jnp.zeros(acc_ref.shape, F32)
    t_q = t0 + lax.broadcasted_iota(jnp.int32, (tq, 1), 0)

    def kv_step(c, carry):
        k0 = pl.multiple_of(c * tk, tk)
        kt = ks_ref[0, 0, pl.ds(k0, tk), :][:, 0:hd]
        s = lax.dot_general(q4, kt, _NT, preferred_element_type=F32).reshape(rep, tq, tk)
        kk = lax.broadcasted_iota(jnp.int32, (1, tk), 1)
        blk = (k0 >> 6) + (kk >> 6)
        expand = jnp.where(lax.broadcasted_iota(jnp.int32, (n_blk, 1), 0) == blk, 1.0, 0.0).astype(BF16)
        chosen = jnp.dot(sel, expand, preferred_element_type=F32)
        mask = ((chosen > 0.5) & (k0 + kk <= t_q))[None]
        s = jnp.where(mask, s, NEG_BIG)
        m_old = m_ref[...]
        m_new = jnp.maximum(m_old, jnp.max(s, axis=-1, keepdims=True))
        p = jnp.where(mask, jnp.exp(s - m_new), 0.0)
        alpha = jnp.exp(m_old - m_new)
        l_ref[...] = alpha * l_ref[...] + jnp.sum(p, axis=-1, keepdims=True)
        pv = jnp.dot(p.reshape(rows, tk).astype(BF16), vs_ref[0, 0, pl.ds(k0, tk), :], preferred_element_type=F32)
        acc_ref[...] = alpha.reshape(rows, 1) * acc_ref[...] + pv
        m_ref[...] = m_new
        return carry

    lax.fori_loop(0, (t0 + tq + tk - 1) // tk, kv_step, 0)
    o_s = acc_ref[:, 0:hd] / l_ref[...].reshape(rows, 1)

    span = WINDOW + tq
    w0 = pl.multiple_of(jnp.maximum(t0 - WINDOW, 0), tq)
    s_w = lax.dot_general(q4, kw_ref[0, 0, pl.ds(w0, span), :], _NT, preferred_element_type=F32)
    kp = w0 + lax.broadcasted_iota(jnp.int32, (1, span), 1)
    p_w = _softmax_rows(s_w, (kp <= t_row) & (kp > t_row - WINDOW))
    o_w = jnp.dot(p_w.astype(BF16), vw_ref[0, 0, pl.ds(w0, span), :], preferred_element_type=F32)[:, 0:hd]

    _to_lane_layout(lay_ref, 0, o_s, tq)
    _to_lane_layout(lay_ref, 1, o_w, tq)
    _gate_maps(gl_ref, gp_ref, gm_ref)
    _gated_store(gm_ref, oc_ref, lay_ref, o_ref)


def _nsa_attend_kernel(sh_ref, q_ref, ks_ref, vs_ref, kw_ref, vw_ref, gl_ref, gp_ref, oc_ref, sel_ref, o_ref,
                       lhs_ref, acc_ref, pre_ref, gm_ref, lay_ref, m_ref, l_ref, *, tq, tk):
    data = (q_ref, ks_ref, vs_ref, kw_ref, vw_ref, gl_ref, gp_ref, oc_ref, sel_ref, o_ref)

    @pl.when(sh_ref[3] > 0.5)
    def _():
        _att_fast(sh_ref, *data, lhs_ref, acc_ref, pre_ref, gm_ref, lay_ref, tq=tq, tk=tk)

    @pl.when(sh_ref[3] <= 0.5)
    def _():
        _att_slow(*data, acc_ref, gm_ref, lay_ref, m_ref, l_ref, tq=tq, tk=tk)


def _overlap_t(n_blk, ncp):
    c_start = np.arange(ncp) * CMP_STRIDE
    b_start = np.arange(n_blk) * SEL_BLOCK
    ov = ((c_start[None, :] < b_start[:, None] + SEL_BLOCK) & (b_start[:, None] < c_start[None, :] + CMP_BLOCK))
    ov[:, ncp - 1] = False
    return jnp.asarray(ov.astype(np.float32)).astype(BF16)


def _nsa_shifts(q_norm, k_norm):
    bound = (NSA_HEAD_DIM ** 0.5) * jnp.max(jnp.abs(q_norm)) * jnp.max(jnp.abs(k_norm), axis=-1)
    bound = bound.astype(BF16).astype(F32)
    fast = jnp.all(bound <= MAX_CONST_SHIFT).astype(F32)
    return jnp.concatenate([-bound, fast[None]])


def _nsa_select(shifts, q, cmp_kv, *, tq, parts):
    b, h, s, hd = q.shape
    g, rep = NSA_GROUPS, NSA_REP
    ncp = cmp_kv.shape[3]
    n_blk = s // SEL_BLOCK
    tb = tq * parts
    assert n_blk % 8 == 0 and n_blk <= hd and s % tb == 0
    kernel = functools.partial(_nsa_select_kernel, tq=tq, n_sel=min(N_SEL, n_blk))
    return pl.pallas_call(
        kernel,
        grid=(b, g, s // tb),
        in_specs=[
            pl.BlockSpec(memory_space=pltpu.SMEM),
            pl.BlockSpec((1, rep, tb, hd), lambda bi, gi, i: (bi, gi, i, 0)),
            pl.BlockSpec((1, 1, 1, ncp, 2 * hd), lambda bi, gi, i: (0, bi, gi, 0, 0)),
            pl.BlockSpec((1, 1, 1, ncp, 2 * hd), lambda bi, gi, i: (1, bi, gi, 0, 0)),
            pl.BlockSpec((n_blk, ncp), lambda bi, gi, i: (0, 0)),
        ],
        out_specs=[pl.BlockSpec((1, tb, rep * hd), lambda bi, gi, i: (bi, i, gi)),
                   pl.BlockSpec((1, 1, tb, hd), lambda bi, gi, i: (bi, gi, i, 0))],
        out_shape=[jax.ShapeDtypeStruct((b, s, h * hd), BF16), jax.ShapeDtypeStruct((b, g, s, hd), BF16)],
        compiler_params=_cparams("parallel", "parallel", "parallel"),
        name="nsa_select",
    )(shifts, q, cmp_kv, cmp_kv, _overlap_t(n_blk, ncp))


def _nsa_attend(shifts, q, ks, vs, kw, vw, z3, o_cmp, sel, *, tq, tk):
    b, h, s, hd = q.shape
    g, rep = NSA_GROUPS, NSA_REP
    rows = rep * tq
    assert WINDOW % tq == 0
    full = lambda w: pl.BlockSpec((1, 1, s, w), lambda bi, gi, i: (bi, gi, 0, 0))
    return pl.pallas_call(
        functools.partial(_nsa_attend_kernel, tq=tq, tk=tk),
        grid=(b, g, s // tq),
        in_specs=[
            pl.BlockSpec(memory_space=pltpu.SMEM),
            pl.BlockSpec((1, rep, tq, hd), lambda bi, gi, i: (bi, gi, i, 0)),
            full(2 * hd), full(2 * hd), full(hd), full(2 * hd),
            pl.BlockSpec((1, tq, LANES), lambda bi, gi, i: (bi, i, COL_SMALL // LANES + gi)),
            pl.BlockSpec((LANES, 3 * rep * hd), lambda bi, gi, i: (0, 0)),
            pl.BlockSpec((1, tq, rep * hd), lambda bi, gi, i: (bi, i, gi)),
            pl.BlockSpec((1, 1, tq, hd), lambda bi, gi, i: (bi, gi, i, 0)),
        ],
        out_specs=pl.BlockSpec((1, tq, rep * hd), lambda bi, gi, i: (bi, i, gi)),
        out_shape=jax.ShapeDtypeStruct((b, s, h * hd), BF16),
        scratch_shapes=[pltpu.VMEM((rows, 2 * hd), BF16), pltpu.VMEM((rows, 2 * hd), F32),
                        pltpu.VMEM((rows, 2 * hd), F32), pltpu.VMEM((tq, 3 * rep * hd), F32),
                        pltpu.VMEM((2, tq, rep * hd), F32),
                        pltpu.VMEM((rep, tq, 1), F32), pltpu.VMEM((rep, tq, 1), F32)],
        compiler_params=_cparams("parallel", "parallel", "arbitrary"),
        name="nsa_attend",
    )(shifts, q, ks, vs, kw, vw, z3, _gate_placement(), o_cmp, sel)


def _head_block_diag(x, n_rows, head_of_lane, dtype):
    return jnp.concatenate([jnp.where(head_of_lane == h, x, 0.0) for h in range(GLA_HEADS)], axis=0).astype(dtype)


def _gla_kernel(q_ref, k_ref, v_ref, r_ref, sm_ref, wg_ref, bg_ref, ng_ref, tril_ref, o_ref,
                st_ref, upd_ref, oin_ref, qin_ref, *, n_chunks):
    c, sub = GLA_CHUNK, GLA_SUB
    dk, dv, nh = GLA_HEAD_DK, GLA_HEAD_DV, GLA_HEADS
    nk, nv = nh * dk, nh * dv

    @pl.when(pl.program_id(1) == 0)
    def _():
        st_ref[...] = jnp.zeros(st_ref.shape, F32)

    sm = sm_ref[0]
    x = (jnp.dot(sm, wg_ref[0], preferred_element_type=F32) + jnp.dot(sm, wg_ref[1], preferred_element_type=F32)
         + bg_ref[...])
    log_a = (jnp.minimum(x, 0.0) - jnp.log1p(jnp.exp(-jnp.abs(x)))) / GLA_TAU
    la_hi = log_a.astype(BF16)
    la_lo = (log_a - la_hi.astype(F32)).astype(BF16)
    bcum_all = (jnp.dot(tril_ref[...], la_hi, preferred_element_type=F32)
                + jnp.dot(tril_ref[...], la_lo, preferred_element_type=F32))
    head_k = lax.broadcasted_iota(jnp.int32, (1, nk), 1) >> 6
    head_v = lax.broadcasted_iota(jnp.int32, (1, nv), 1) >> 7
    state_mask = (lax.broadcasted_iota(jnp.int32, (nv, 1), 0) >> 7) == head_k
    causal = ((lax.broadcasted_iota(jnp.int32, (c, nh * c), 1) & (c - 1))
              <= lax.broadcasted_iota(jnp.int32, (c, nh * c), 0))

    decays = []
    for cc in range(n_chunks):
        rs = slice(cc * c, (cc + 1) * c)
        bcum = bcum_all[rs]
        q = q_ref[0, rs].astype(F32) * (dk ** -0.5)
        k = k_ref[0, rs].astype(F32)
        v = v_ref[0, rs].astype(F32)
        b_last = bcum[c - 1:c, :]

        score_rows = []
        for i in range(c // sub):
            lo, hi = i * sub, (i + 1) * sub
            ref = bcum[lo - 1:lo, :] if i > 0 else jnp.zeros((1, nk), F32)
            q_i = (q[lo:hi] * jnp.exp(bcum[lo:hi] - ref)).astype(BF16)
            k_i = k[0:hi] * jnp.exp(ref - bcum[0:hi])
            if hi < c:
                k_i = jnp.concatenate([k_i, jnp.zeros((c - hi, nk), F32)], axis=0)
            score_rows.append(lax.dot_general(q_i, _head_block_diag(k_i, c, head_k, BF16), _NT,
                                              preferred_element_type=F32))
        a = jnp.where(causal, jnp.concatenate(score_rows, axis=0), 0.0).astype(BF16)
        oin_ref[rs] = jnp.dot(a, _head_block_diag(v, c, head_v, BF16), preferred_element_type=F32)
        qin_ref[rs] = (q * jnp.exp(bcum)).astype(BF16)
        k_out = (k * jnp.exp(b_last - bcum)).astype(BF16)
        upd = lax.dot_general(v.astype(BF16), k_out, (((0,), (0,)), ((), ())), preferred_element_type=F32)
        upd_ref[cc] = jnp.where(state_mask, upd, 0.0)
        decays.append(jnp.exp(b_last))

    for cc in range(n_chunks):
        rs = slice(cc * c, (cc + 1) * c)
        st = st_ref[...]
        o = oin_ref[rs] + lax.dot_general(qin_ref[rs], st.astype(BF16), _NT, preferred_element_type=F32)
        st_ref[...] = st * decays[cc] + upd_ref[cc]
        r_gate = r_ref[0, rs].astype(F32)
        for h in range(nh):
            vs_ = slice(h * dv, (h + 1) * dv)
            o_h = o[:, vs_]
            ms = jnp.mean(o_h * o_h, axis=-1, keepdims=True)
            rg = r_gate[:, vs_]
            o_ref[0, rs, vs_] = (o_h * lax.rsqrt(ms + NORM_EPS) * ng_ref[...] * (rg * jax.nn.sigmoid(rg))).astype(o_ref.dtype)


def _gla(z3, w_gate, b_gate, norm_g, *, n_chunks):
    b, s, _ = z3.shape
    c = GLA_CHUNK * n_chunks
    nk = GLA_HEADS * GLA_HEAD_DK
    nv = GLA_HEADS * GLA_HEAD_DV
    wg = jnp.zeros((LANES, nk), F32).at[SMALL_GLOW_LANE:SMALL_GLOW_LANE + GLA_RANK].set(w_gate.astype(F32))
    wg_hi = wg.astype(BF16)
    wg = jnp.stack([wg_hi, (wg - wg_hi.astype(F32)).astype(BF16)])
    idx = np.arange(c)
    tril = jnp.asarray(((idx[:, None] >= idx[None, :])
                        & (idx[:, None] // GLA_CHUNK == idx[None, :] // GLA_CHUNK)).astype(np.float32)).astype(BF16)
    const = lambda shape: pl.BlockSpec(shape, lambda bi, i: (0,) * len(shape))
    return pl.pallas_call(
        functools.partial(_gla_kernel, n_chunks=n_chunks),
        grid=(b, s // c),
        in_specs=[
            pl.BlockSpec((1, c, nk), lambda bi, i: (bi, i, COL_GQ // nk)),
            pl.BlockSpec((1, c, nk), lambda bi, i: (bi, i, COL_GK // nk)),
            pl.BlockSpec((1, c, nv), lambda bi, i: (bi, i, COL_GV // nv)),
            pl.BlockSpec((1, c, nv), lambda bi, i: (bi, i, COL_GR // nv)),
            pl.BlockSpec((1, c, LANES), lambda bi, i: (bi, i, COL_SMALL // LANES)),
            const((2, LANES, nk)), const((1, nk)), const((1, GLA_HEAD_DV)), const((c, c)),
        ],
        out_specs=pl.BlockSpec((1, c, nv), lambda bi, i: (bi, i, 0)),
        out_shape=jax.ShapeDtypeStruct((b, s, nv), BF16),
        scratch_shapes=[pltpu.VMEM((nv, nk), F32), pltpu.VMEM((n_chunks, nv, nk), F32),
                        pltpu.VMEM((c, nv), F32), pltpu.VMEM((c, nk), BF16)],
        compiler_params=_cparams("parallel", "arbitrary"),
        name="gla",
    )(z3, z3, z3, z3, z3, wg, b_gate.reshape(1, nk).astype(F32), norm_g.reshape(1, GLA_HEAD_DV).astype(F32), tril)


def _mem_attn_kernel(q_ref, k_ref, v_ref, qg_ref, kg_ref, o_ref):
    dh = MEM_HEAD_DIM
    nt = (((1,), (1,)), ((), ()))
    for h in range(MEM_HEADS):
        sl = slice(h * dh, (h + 1) * dh)
        q = q_ref[0, :, sl].astype(F32)
        q = q * lax.rsqrt(jnp.mean(q * q, axis=-1, keepdims=True) + NORM_EPS) * qg_ref[...] * (dh ** -0.5)
        k = k_ref[0, :, sl].astype(F32)
        k = k * lax.rsqrt(jnp.mean(k * k, axis=-1, keepdims=True) + NORM_EPS) * kg_ref[...]
        s = lax.dot_general(q.astype(BF16), k.astype(BF16), nt, preferred_element_type=F32)
        m = jnp.max(s, axis=-1, keepdims=True)
        p = jnp.exp(s - m)
        p = p / jnp.sum(p, axis=-1, keepdims=True)
        o = jnp.dot(p.astype(BF16), v_ref[0, :, sl].astype(BF16), preferred_element_type=F32)
        o_ref[0, :, sl] = o.astype(o_ref.dtype)


def _mem_attention(z3, kv, q_norm, k_norm, *, tq):
    b, s, _ = z3.shape
    m = kv.shape[1]
    w = MEM_HEADS * MEM_HEAD_DIM
    const = lambda shape: pl.BlockSpec(shape, lambda bi, i: (0,) * len(shape))
    return pl.pallas_call(
        _mem_attn_kernel,
        grid=(b, s // tq),
        in_specs=[
            pl.BlockSpec((1, tq, w), lambda bi, i: (bi, i, COL_MQ // w)),
            pl.BlockSpec((1, m, w), lambda bi, i: (bi, 0, 0)),
            pl.BlockSpec((1, m, w), lambda bi, i: (bi, 0, 1)),
            const((1, MEM_HEAD_DIM)), const((1, MEM_HEAD_DIM)),
        ],
        out_specs=pl.BlockSpec((1, tq, w), lambda bi, i: (bi, i, 0)),
        out_shape=jax.ShapeDtypeStruct((b, s, w), BF16),
        compiler_params=_cparams("parallel", "parallel"),
        name="mem_attention",
    )(z3, kv, kv, q_norm.reshape(1, MEM_HEAD_DIM).astype(F32), k_norm.reshape(1, MEM_HEAD_DIM).astype(F32))


def _merge_kernel(x_ref, on_ref, og_ref, om_ref, m0_ref, m1_ref, m2_ref, bm_ref, wb_ref, wo_ref, o_ref):
    merged = None
    for br, (ref, mg_ref) in enumerate(((on_ref, m0_ref), (og_ref, m1_ref), (om_ref, m2_ref))):
        y = jnp.dot(ref[...], wb_ref[br], preferred_element_type=F32)
        gate = jax.nn.sigmoid(mg_ref[...].astype(F32) + bm_ref[br:br + 1, :])
        merged = gate * y if merged is None else merged + gate * y
    o_ref[...] = x_ref[...] + jnp.dot(merged.astype(BF16), wo_ref[...], preferred_element_type=F32)


def _merge_out(x2, o_nsa, o_gla, o_mem, z2, b_merge, w_branch, w_out, layer, *, tm):
    t, d = x2.shape
    bw = BRANCH_WIDTH
    row = lambda w: pl.BlockSpec((tm, w), lambda i: (i, 0))
    gate_cols = lambda br: pl.BlockSpec((tm, d), lambda i: (i, COL_MERGE // d + br))
    return pl.pallas_call(
        _merge_kernel,
        grid=(t // tm,),
        in_specs=[
            row(d), row(bw), row(bw), row(bw),
            gate_cols(0), gate_cols(1), gate_cols(2),
            pl.BlockSpec((N_BRANCH, d), lambda i: (0, 0)),
            pl.BlockSpec((None, N_BRANCH, bw, d), lambda i: (layer, 0, 0, 0)),
            pl.BlockSpec((None, d, d), lambda i: (layer, 0, 0)),
        ],
        out_specs=row(d),
        out_shape=jax.ShapeDtypeStruct((t, d), F32),
        compiler_params=_cparams("parallel"),
        name="merge_out",
    )(x2, o_nsa, o_gla, o_mem, z2, z2, z2, b_merge.astype(F32), w_branch, w_out)


def _mlp_kernel(x_ref, g_ref, wu_ref, wd_ref, o_ref, h_ref, acc_ref):
    j = pl.program_id(1)

    @pl.when(j == 0)
    def _():
        x = x_ref[...]
        ms = jnp.mean(x * x, axis=-1, keepdims=True)
        h_ref[...] = (x * lax.rsqrt(ms + NORM_EPS) * g_ref[...]).astype(h_ref.dtype)
        acc_ref[...] = jnp.zeros(acc_ref.shape, F32)

    u = jnp.dot(h_ref[...], wu_ref[...], preferred_element_type=F32)
    u = jnp.square(jnp.maximum(u, 0.0)).astype(BF16)
    acc_ref[...] += jnp.dot(u, wd_ref[...], preferred_element_type=F32)

    @pl.when(j == pl.num_programs(1) - 1)
    def _():
        o_ref[...] = x_ref[...] + acc_ref[...]


def _mlp(x2, g, w_up, w_down, layer, *, tm, th):
    t, d = x2.shape
    hid = w_up.shape[2]
    return pl.pallas_call(
        _mlp_kernel,
        grid=(t // tm, hid // th),
        in_specs=[
            pl.BlockSpec((tm, d), lambda i, j: (i, 0)),
            pl.BlockSpec((1, d), lambda i, j: (0, 0)),
            pl.BlockSpec((None, d, th), lambda i, j: (layer, 0, j)),
            pl.BlockSpec((None, th, d), lambda i, j: (layer, j, 0)),
        ],
        out_specs=pl.BlockSpec((tm, d), lambda i, j: (i, 0)),
        out_shape=jax.ShapeDtypeStruct((t, d), F32),
        scratch_shapes=[pltpu.VMEM((tm, d), BF16), pltpu.VMEM((tm, d), F32)],
        compiler_params=_cparams("parallel", "arbitrary"),
        name="mlp",
    )(x2, g.reshape(1, d).astype(F32), w_up, w_down)


def _tile(n, pref):
    t = min(n, pref)
    assert n % t == 0, (n, pref)
    return t


def _layer(x, mem2, pos_rows, pos_cmp, p, big, layer):
    b, s, d = x.shape
    t = b * s
    hd, g = NSA_HEAD_DIM, NSA_GROUPS
    x2 = x.reshape(t, d)

    z2 = _norm_matmul(x2, p["ln_mix"].astype(F32), big["w_in"], layer, tm=_tile(t, 1024), tn=D_IN_PAD // 2, out_dtype=BF16, name="in_proj")
    z3 = z2.reshape(b, s, D_IN_PAD)

    q, ks, vs, kw, vw = _nsa_prep(z3, pos_rows, p["nsa_q_norm"].astype(F32), p["nsa_k_norm"][1].astype(F32),
                                  p["nsa_k_norm"][2].astype(F32), tp=_tile(s, 256))
    pe2 = jnp.pad(p["cmp_pe"].reshape(2, 2, CMP_STRIDE * hd), ((0, 0), (0, 6), (0, 0))).astype(F32)
    cmp_kv = _nsa_compress(z3, p["cmp_w1"], p["cmp_w2"], pe2, pos_cmp, p["nsa_k_norm"][0].astype(F32))
    shifts = _nsa_shifts(p["nsa_q_norm"].astype(F32), p["nsa_k_norm"].astype(F32))
    o_cmp, sel = _nsa_select(shifts, q, cmp_kv, tq=_tile(s, 256), parts=2)
    o_nsa = _nsa_attend(shifts, q, ks, vs, kw, vw, z3, o_cmp, sel, tq=_tile(s, 256), tk=_tile(s, 512))

    o_gla = _gla(z3, p["gla_w_gate"], p["gla_b_gate"], p["gla_norm"], n_chunks=8)

    kv = _norm_matmul(mem2, p["mem_norm"].astype(F32), big["mem_w_kv"], layer,
                      tm=_tile(mem2.shape[0], 512), tn=512, out_dtype=BF16, name="mem_kv")
    kv = kv.reshape(b, mem2.shape[0] // b, 2 * MEM_HEADS * MEM_HEAD_DIM)
    o_mem = _mem_attention(z3, kv, p["mem_q_norm"], p["mem_k_norm"], tq=_tile(s, 512))

    x2 = _merge_out(x2, o_nsa.reshape(t, -1), o_gla.reshape(t, -1), o_mem.reshape(t, -1), z2,
                    p["b_merge"], big["w_branch"], big["w_out"], layer, tm=_tile(t, 512))
    x2 = _mlp(x2, p["ln_mlp"], big["w_up"], big["w_down"], layer, tm=_tile(t, 1024), th=1024)
    return x2.reshape(b, s, d)


def kernel(x, mem, positions, ln_mix, w_in, b_merge, nsa_q_norm, nsa_k_norm, cmp_pe, cmp_w1, cmp_w2,
           gla_w_gate, gla_b_gate, gla_norm, mem_norm, mem_w_kv, mem_q_norm, mem_k_norm, w_branch, w_out,
           ln_mlp, w_up, w_down):
    b, s, d = x.shape
    assert d == 1024 and s % WINDOW == 0 and s >= 2 * WINDOW
    depth = w_in.shape[0]
    perm_np, d_in = _in_proj_permutation(d)
    assert w_in.shape[2] == d_in
    w_in = _relayout_in_proj(w_in.astype(BF16), [int(c) for c in perm_np], d_in)
    half = CMP_STRIDE * NSA_HEAD_DIM
    cmp_w1 = jnp.concatenate([cmp_w1[:, :, :half], cmp_w1[:, :, half:]], axis=-1).astype(BF16)
    cmp_w2, mem_w_kv, w_branch, w_out, w_up, w_down = (
        a.astype(BF16) for a in (cmp_w2, mem_w_kv, w_branch, w_out, w_up, w_down))
    pos3 = positions.astype(jnp.int32).reshape(b, s, 1)
    nc = s // CMP_STRIDE
    cmp_end = np.minimum(np.arange(nc) * CMP_STRIDE + CMP_BLOCK - 1, s - 1)
    pos_cmp = pos3[:, cmp_end, :]
    pos_rows = positions.astype(jnp.int32).reshape(b, 1, s)
    mem2 = mem.reshape(b * mem.shape[1], d)
    names = ("ln_mix", "b_merge", "nsa_q_norm", "nsa_k_norm", "cmp_pe", "cmp_w1", "cmp_w2",
             "gla_w_gate", "gla_b_gate", "gla_norm", "mem_norm", "mem_q_norm", "mem_k_norm", "ln_mlp")
    stacked = (ln_mix, b_merge, nsa_q_norm, nsa_k_norm, cmp_pe, cmp_w1, cmp_w2, gla_w_gate, gla_b_gate,
               gla_norm, mem_norm, mem_q_norm, mem_k_norm, ln_mlp)
    big = dict(w_in=w_in, mem_w_kv=mem_w_kv, w_branch=w_branch, w_out=w_out, w_up=w_up, w_down=w_down)
    for l in range(depth):
        x = _layer(x, mem2, pos_rows, pos_cmp, {n: a[l] for n, a in zip(names, stacked)}, big, l)
    return x
```

```python
import functools

import numpy as np
import jax
import jax.numpy as jnp
from jax import lax
from jax.experimental import pallas as pl
from jax.experimental.pallas import tpu as pltpu

NSA_HEADS = 8
NSA_GROUPS = 2
NSA_REP = NSA_HEADS // NSA_GROUPS
NSA_HEAD_DIM = 64
CMP_BLOCK = 32
CMP_STRIDE = 16
CMP_HIDDEN = 4 * NSA_HEAD_DIM
SEL_BLOCK = 64
N_SEL = 16
WINDOW = 512
FORCE_SCORE = 1e4
GLA_HEADS = 4
GLA_HEAD_DK = 64
GLA_HEAD_DV = 128
GLA_RANK = 16
GLA_TAU = 16.0
GLA_CHUNK = 64
GLA_SUB = 16
MEM_HEADS = 4
MEM_HEAD_DIM = 128
N_BRANCH = 3
BRANCH_WIDTH = 512
ROPE_THETA = 500000.0
ROPE_ROT = NSA_HEAD_DIM // 4
ROPE_HALF = ROPE_ROT // 2
NORM_EPS = 1e-6

LANES = 128
VMEM_LIMIT_BYTES = 48 * 1024 * 1024

F32 = jnp.float32
BF16 = jnp.bfloat16
HIGHEST = lax.Precision.HIGHEST
NEG_BIG = -1e30
MAX_CONST_SHIFT = 40.0

COL_NQ = 0
COL_GV = 512
COL_GR = 1024
COL_MQ = 1536
COL_MERGE = 2048
COL_GQ = 5120
COL_GK = 5376
COL_KV = 5632
COL_SMALL = 6400
D_IN_PAD = 6656
SMALL_GLOW_LANE = 12


def _in_proj_permutation(d_model):
    sizes = (512, 128, 128, 128, 128, 128, 128, 24, 256, 256, 512, 512, 16, 512, 3 * d_model)
    off = np.concatenate([[0], np.cumsum(sizes)])
    (o_nq, o_kc, o_vc, o_ks, o_vs, o_kw, o_vw, o_ng, o_gq, o_gk, o_gv, o_gr, o_gl, o_mq, o_mg) = off[:-1]
    d_in = int(off[-1])
    perm = np.full((D_IN_PAD,), d_in, np.int32)

    def put(new, old, n):
        perm[new:new + n] = np.arange(old, old + n)

    put(COL_NQ, o_nq, 512)
    put(COL_GV, o_gv, 512)
    put(COL_GR, o_gr, 512)
    put(COL_MQ, o_mq, 512)
    put(COL_MERGE, o_mg, 3 * d_model)
    put(COL_GQ, o_gq, 256)
    put(COL_GK, o_gk, 256)
    put(COL_KV, o_kc, 768)
    for g in range(NSA_GROUPS):
        for br in range(3):
            for r in range(NSA_REP):
                perm[COL_SMALL + g * LANES + br * NSA_REP + r] = o_ng + (g * NSA_REP + r) * 3 + br
    put(COL_SMALL + SMALL_GLOW_LANE, o_gl, GLA_RANK)
    return perm, d_in


def _relayout_in_proj(w, perm, d_in):
    pieces, start = [], 0
    for i in range(1, len(perm) + 1):
        prev = perm[i - 1]
        if i < len(perm) and (perm[i] == prev + 1 if prev != d_in else perm[i] == d_in):
            continue
        n = i - start
        if prev == d_in:
            pieces.append(jnp.zeros(w.shape[:-1] + (n,), w.dtype))
        else:
            pieces.append(w[..., perm[start]:perm[start] + n])
        start = i
    return jnp.concatenate(pieces, axis=-1)


def _cparams(*sem):
    return pltpu.CompilerParams(dimension_semantics=sem, vmem_limit_bytes=VMEM_LIMIT_BYTES)


def _norm_matmul_kernel(x_ref, g_ref, w_ref, o_ref, h_ref):
    @pl.when(pl.program_id(1) == 0)
    def _():
        x = x_ref[...].astype(F32)
        ms = jnp.mean(x * x, axis=-1, keepdims=True)
        h_ref[...] = (x * lax.rsqrt(ms + NORM_EPS) * g_ref[...]).astype(h_ref.dtype)

    o_ref[...] = jnp.dot(h_ref[...], w_ref[...], preferred_element_type=F32).astype(o_ref.dtype)


def _norm_matmul(x, g, w, layer, *, tm, tn, out_dtype, name):
    m, k = x.shape
    n = w.shape[2]
    return pl.pallas_call(
        _norm_matmul_kernel,
        grid=(m // tm, n // tn),
        in_specs=[
            pl.BlockSpec((tm, k), lambda i, j: (i, 0)),
            pl.BlockSpec((1, k), lambda i, j: (0, 0)),
            pl.BlockSpec((None, k, tn), lambda i, j: (layer, 0, j)),
        ],
        out_specs=pl.BlockSpec((tm, tn), lambda i, j: (i, j)),
        out_shape=jax.ShapeDtypeStruct((m, n), out_dtype),
        scratch_shapes=[pltpu.VMEM((tm, k), BF16)],
        compiler_params=_cparams("parallel", "arbitrary"),
        name=name,
    )(x, g.reshape(1, k), w)


def _rope_tables(pos_row, freq_col, place):
    ang = freq_col * pos_row
    tn = (((0,), (0,)), ((), ()))
    lane = lax.broadcasted_iota(jnp.int32, (1, LANES), 1)
    c = lax.dot_general(jnp.cos(ang), place, tn, preferred_element_type=F32, precision=HIGHEST)
    s = lax.dot_general(jnp.sin(ang), place, tn, preferred_element_type=F32, precision=HIGHEST)
    return c + jnp.where((lane & (NSA_HEAD_DIM - 1)) >= ROPE_ROT, 1.0, 0.0), s


def _rope_lanes(x, c, s):
    n = x.shape[-1]
    if n > LANES:
        c = jnp.concatenate([c] * (n // LANES), axis=1)
        s = jnp.concatenate([s] * (n // LANES), axis=1)
    lane = lax.broadcasted_iota(jnp.int32, (1, n), 1) & (NSA_HEAD_DIM - 1)
    up = pltpu.roll(x, n - ROPE_HALF, axis=1)
    dn = pltpu.roll(x, ROPE_HALF, axis=1)
    y = jnp.where(lane < ROPE_HALF, -up, jnp.where(lane < ROPE_ROT, dn, 0.0))
    return x * c + y * s


def _head_rms(x, bd, g):
    x2 = x * x
    hi = x2.astype(BF16)
    lo = (x2 - hi.astype(F32)).astype(BF16)
    ms = (jnp.dot(hi, bd, preferred_element_type=F32) + jnp.dot(lo, bd, preferred_element_type=F32))
    return x * lax.rsqrt(ms * (1.0 / NSA_HEAD_DIM) + NORM_EPS) * g


def _nsa_prep_kernel(q_ref, ks_ref, vs_ref, kw_ref, vw_ref, pos_ref, qg_ref, ksg_ref, kwg_ref,
                     bdq_ref, bdk_ref, f_ref, place_ref,
                     qo_ref, kso_ref, vso_ref, kwo_ref, vwo_ref):
    hd = NSA_HEAD_DIM
    tp = q_ref.shape[1]
    c, s = _rope_tables(pos_ref[0].astype(F32), f_ref[...], place_ref[...])
    q = _head_rms(q_ref[0].astype(F32), bdq_ref[...], qg_ref[...])
    q = (_rope_lanes(q, c, s) * (hd ** -0.5)).astype(qo_ref.dtype)
    for h in range(NSA_HEADS):
        qo_ref[0, h] = q[:, h * hd:(h + 1) * hd]
    ks = _rope_lanes(_head_rms(ks_ref[0].astype(F32), bdk_ref[...], ksg_ref[...]), c, s)
    kw = _rope_lanes(_head_rms(kw_ref[0].astype(F32), bdk_ref[...], kwg_ref[...]), c, s)
    vs = vs_ref[0]
    vw = vw_ref[0]
    tok = pl.program_id(1) * tp + lax.broadcasted_iota(jnp.int32, (tp, hd), 0)
    lane = lax.broadcasted_iota(jnp.int32, (tp, hd), 1)
    blk_onehot = jnp.where((tok >> 6) == lane, 1.0, 0.0).astype(kso_ref.dtype)
    ones_col = jnp.ones((tp, hd), vso_ref.dtype)
    for g in range(NSA_GROUPS):
        sl = slice(g * hd, (g + 1) * hd)
        kso_ref[0, g, :, 0:hd] = ks[:, sl].astype(kso_ref.dtype)
        kso_ref[0, g, :, hd:2 * hd] = blk_onehot
        kwo_ref[0, g] = kw[:, sl].astype(kwo_ref.dtype)
        vso_ref[0, g, :, 0:hd] = vs[:, sl].astype(vso_ref.dtype)
        vso_ref[0, g, :, hd:2 * hd] = ones_col
        vwo_ref[0, g, :, 0:hd] = vw[:, sl].astype(vwo_ref.dtype)
        vwo_ref[0, g, :, hd:2 * hd] = ones_col


def _block_diag_ones(n, width):
    i = np.arange(n)
    return jnp.asarray((i[:, None] // width == i[None, :] // width).astype(np.float32)).astype(BF16)


def _rope_freq_col():
    inv = ROPE_THETA ** (-(np.arange(ROPE_HALF, dtype=np.float64)) / ROPE_HALF)
    return jnp.asarray(inv.astype(np.float32)).reshape(ROPE_HALF, 1)


def _rope_placement():
    m = np.zeros((ROPE_HALF, LANES), np.float32)
    for f in range(ROPE_HALF):
        for base in range(0, LANES, NSA_HEAD_DIM):
            m[f, base + f] = 1.0
            m[f, base + f + ROPE_HALF] = 1.0
    return jnp.asarray(m)


def _nsa_prep(z3, pos_rows, q_norm, ks_norm, kw_norm, *, tp):
    b, s, _ = z3.shape
    hd, g = NSA_HEAD_DIM, NSA_GROUPS
    assert SEL_BLOCK == 64 and s // SEL_BLOCK <= hd
    kvb = COL_KV // LANES
    qg = jnp.tile(q_norm, NSA_HEADS).reshape(1, NSA_HEADS * hd)
    ksg = jnp.tile(ks_norm, g).reshape(1, g * hd)
    kwg = jnp.tile(kw_norm, g).reshape(1, g * hd)
    const = lambda shape: pl.BlockSpec(shape, lambda bi, i: (0,) * len(shape))
    plain_out = jax.ShapeDtypeStruct((b, g, s, hd), BF16)
    plain_spec = pl.BlockSpec((1, g, tp, hd), lambda bi, i: (bi, 0, i, 0))
    aug_out = jax.ShapeDtypeStruct((b, g, s, 2 * hd), BF16)
    aug_spec = pl.BlockSpec((1, g, tp, 2 * hd), lambda bi, i: (bi, 0, i, 0))
    return pl.pallas_call(
        _nsa_prep_kernel,
        grid=(b, s // tp),
        in_specs=[
            pl.BlockSpec((1, tp, 512), lambda bi, i: (bi, i, COL_NQ // 512)),
            pl.BlockSpec((1, tp, LANES), lambda bi, i: (bi, i, kvb + 2)),
            pl.BlockSpec((1, tp, LANES), lambda bi, i: (bi, i, kvb + 3)),
            pl.BlockSpec((1, tp, LANES), lambda bi, i: (bi, i, kvb + 4)),
            pl.BlockSpec((1, tp, LANES), lambda bi, i: (bi, i, kvb + 5)),
            pl.BlockSpec((1, 1, tp), lambda bi, i: (bi, 0, i)),
            const((1, 512)), const((1, LANES)), const((1, LANES)),
            const((512, 512)), const((LANES, LANES)),
            const((ROPE_HALF, 1)), const((ROPE_HALF, LANES)),
        ],
        out_specs=[
            pl.BlockSpec((1, NSA_HEADS, tp, hd), lambda bi, i: (bi, 0, i, 0)),
            aug_spec, aug_spec, plain_spec, aug_spec,
        ],
        out_shape=[jax.ShapeDtypeStruct((b, NSA_HEADS, s, hd), BF16), aug_out, aug_out, plain_out, aug_out],
        compiler_params=_cparams("parallel", "parallel"),
        name="nsa_prep",
    )(z3, z3, z3, z3, z3, pos_rows, qg, ksg, kwg,
      _block_diag_ones(512, hd), _block_diag_ones(LANES, hd), _rope_freq_col(), _rope_placement())


def _gelu_tanh(x):
    return 0.5 * x * (1.0 + jnp.tanh(np.sqrt(2.0 / np.pi) * (x + 0.044715 * x * x * x)))


def _nsa_cmp_kernel(x_ref, w1_ref, w2_ref, pe_ref, pos_ref, g_ref, rot_ref, f_ref, o_ref, xs_ref):
    kind = pl.program_id(0)
    hd = NSA_HEAD_DIM
    nc = x_ref.shape[1] // CMP_STRIDE
    xs_ref[...] = x_ref[0].astype(F32)
    r = jnp.dot(pe_ref[0].astype(BF16), w1_ref[0], preferred_element_type=F32)
    ab = [jnp.zeros((nc, 2 * CMP_HIDDEN), F32) for _ in range(NSA_GROUPS)]
    for p in range(CMP_STRIDE):
        xp = xs_ref[pl.ds(p, nc, stride=CMP_STRIDE), :].astype(BF16)
        w1p = w1_ref[0, p * hd:(p + 1) * hd, :]
        for g in range(NSA_GROUPS):
            ab[g] = ab[g] + jnp.dot(xp[:, g * hd:(g + 1) * hd], w1p, preferred_element_type=F32)

    for g in range(NSA_GROUPS):
        a = ab[g][:, :CMP_HIDDEN] + r[0:1, :CMP_HIDDEN]
        bm = ab[g][:, CMP_HIDDEN:] + r[1:2, CMP_HIDDEN:]
        hid = _gelu_tanh(a + pltpu.roll(bm, nc - 1, axis=0))
        comp = jnp.dot(hid.astype(BF16), w2_ref[0], preferred_element_type=F32)

        @pl.when(kind == 0)
        def _():
            ms = jnp.mean(comp * comp, axis=-1, keepdims=True)
            kn = comp * lax.rsqrt(ms + NORM_EPS) * g_ref[...]
            ang = pos_ref[0].astype(F32) * f_ref[...]
            y = jnp.dot(kn, rot_ref[...], preferred_element_type=F32, precision=HIGHEST)
            o_ref[0, 0, g, :, 0:hd] = (kn * jnp.cos(ang) + y * jnp.sin(ang)).astype(o_ref.dtype)
            o_ref[0, 0, g, :, hd:2 * hd] = jnp.zeros((nc, hd), o_ref.dtype)

        @pl.when(kind != 0)
        def _():
            o_ref[0, 0, g, :, 0:hd] = comp.astype(o_ref.dtype)
            o_ref[0, 0, g, :, hd:2 * hd] = jnp.ones((nc, hd), o_ref.dtype)


def _rope_freq_head():
    p = np.arange(NSA_HEAD_DIM)
    inv = ROPE_THETA ** (-(np.arange(ROPE_HALF, dtype=np.float64)) / ROPE_HALF)
    f = np.where(p < ROPE_ROT, inv[p % ROPE_HALF], 0.0)
    return jnp.asarray(f.astype(np.float32)).reshape(1, NSA_HEAD_DIM)


def _rope_rot_matrix(n):
    m = np.zeros((n, n), np.float32)
    for base in range(0, n, NSA_HEAD_DIM):
        for l in range(ROPE_HALF):
            m[base + l + ROPE_HALF, base + l] = -1.0
            m[base + l, base + l + ROPE_HALF] = 1.0
    return jnp.asarray(m)


def _nsa_compress(z3, w1cat, w2, pe2, pos_cmp, k_norm0):
    b, s, _ = z3.shape
    hd, g = NSA_HEAD_DIM, NSA_GROUPS
    nc = s // CMP_STRIDE
    assert g * hd == LANES
    const = lambda shape: pl.BlockSpec(shape, lambda k, bi: (0,) * len(shape))
    return pl.pallas_call(
        _nsa_cmp_kernel,
        grid=(2, b),
        in_specs=[
            pl.BlockSpec((1, s, LANES), lambda k, bi: (bi, 0, COL_KV // LANES + k)),
            pl.BlockSpec((1, CMP_STRIDE * hd, 2 * CMP_HIDDEN), lambda k, bi: (k, 0, 0)),
            pl.BlockSpec((1, CMP_HIDDEN, hd), lambda k, bi: (k, 0, 0)),
            pl.BlockSpec((1, 8, CMP_STRIDE * hd), lambda k, bi: (k, 0, 0)),
            pl.BlockSpec((1, nc, 1), lambda k, bi: (bi, 0, 0)),
            const((1, hd)), const((hd, hd)), const((1, hd)),
        ],
        out_specs=pl.BlockSpec((1, 1, g, nc, 2 * hd), lambda k, bi: (k, bi, 0, 0, 0)),
        out_shape=jax.ShapeDtypeStruct((2, b, g, nc, 2 * hd), BF16),
        scratch_shapes=[pltpu.VMEM((s, LANES), F32)],
        compiler_params=_cparams("parallel", "parallel"),
        name="nsa_compress",
    )(z3, w1cat, w2, pe2, pos_cmp, k_norm0.reshape(1, hd), _rope_rot_matrix(hd), _rope_freq_head())


def _softmax_rows(s, mask):
    s = jnp.where(mask, s, NEG_BIG)
    m = jnp.max(s, axis=-1, keepdims=True)
    p = jnp.where(mask, jnp.exp(s - m), 0.0)
    d = jnp.sum(p, axis=-1, keepdims=True)
    return p / jnp.where(d > 0, d, 1.0)


_NT = (((1,), (1,)), ((), ()))


def _select_blocks(p_sum, ov_ref, t0, tq, n_sel):
    n_blk = ov_ref.shape[0]
    p_hi = p_sum.astype(BF16)
    p_lo = (p_sum - p_hi.astype(F32)).astype(BF16)
    imp_t = (lax.dot_general(ov_ref[...], p_hi, _NT, preferred_element_type=F32)
             + lax.dot_general(ov_ref[...], p_lo, _NT, preferred_element_type=F32))
    j_col = lax.broadcasted_iota(jnp.int32, (n_blk, 1), 0)
    t_lane = t0 + lax.broadcasted_iota(jnp.int32, (1, tq), 1)
    causal = j_col * SEL_BLOCK <= t_lane
    cur = t_lane >> 6
    forced = causal & ((j_col == 0) | (j_col == cur) | (j_col == cur - 1))
    score = jnp.where(forced, FORCE_SCORE, jnp.where(causal, imp_t, -FORCE_SCORE))
    ng = n_blk // 8
    groups = [score[8 * v:8 * v + 8] for v in range(ng)]
    ranks = [jnp.zeros((8, tq), F32) for _ in range(ng)]
    sub = lax.broadcasted_iota(jnp.int32, (8, tq), 0)
    for jp in range(n_blk):
        row = jnp.broadcast_to(score[jp:jp + 1, :], (8, tq))
        vj = jp // 8
        for v in range(ng):
            if v < vj:
                ahead = row > groups[v]
            elif v > vj:
                ahead = row >= groups[v]
            else:
                ahead = (row > groups[v]) | ((row == groups[v]) & (sub > jp % 8))
            ranks[v] = ranks[v] + jnp.where(ahead, 1.0, 0.0)
    sel_t = jnp.where(jnp.concatenate(ranks, axis=0) < n_sel, 1.0, 0.0).astype(BF16)
    eye = (lax.broadcasted_iota(jnp.int32, (tq, tq), 0) == lax.broadcasted_iota(jnp.int32, (tq, tq), 1)).astype(BF16)
    return lax.dot_general(eye, sel_t, _NT, preferred_element_type=F32)


def _sum_heads(p, tq):
    out = p[0:tq]
    for r in range(1, NSA_REP):
        out = out + p[r * tq:(r + 1) * tq]
    return out


def _store_heads(o_ref, o, tq, row0):
    hd = NSA_HEAD_DIM
    for r in range(NSA_REP):
        o_ref[0, row0:row0 + tq, r * hd:(r + 1) * hd] = o[r * tq:(r + 1) * tq].astype(o_ref.dtype)


def _store_selection(sel_ref, sel, row0):
    tq, n_blk = sel.shape
    sel = sel.astype(sel_ref.dtype)
    if n_blk < NSA_HEAD_DIM:
        sel = jnp.concatenate([sel, jnp.zeros((tq, NSA_HEAD_DIM - n_blk), sel_ref.dtype)], axis=1)
    sel_ref[0, 0, row0:row0 + tq] = sel


def _normalize_aug(ol):
    return ol * pltpu.roll(1.0 / ol, NSA_HEAD_DIM, axis=1)


def _gate_maps(gl_ref, gp_ref, gm_ref):
    gates = jax.nn.sigmoid(gl_ref[0].astype(F32))
    g_hi = gates.astype(BF16)
    g_lo = (gates - g_hi.astype(F32)).astype(BF16)
    gm_ref[...] = (jnp.dot(g_hi, gp_ref[...], preferred_element_type=F32)
                   + jnp.dot(g_lo, gp_ref[...], preferred_element_type=F32))


def _to_lane_layout(lay_ref, idx, o, tq):
    hd = NSA_HEAD_DIM
    for r in range(NSA_REP):
        lay_ref[idx, :, r * hd:(r + 1) * hd] = o[r * tq:(r + 1) * tq, 0:hd]


def _gated_store(gm_ref, oc_ref, lay_ref, o_ref):
    w = NSA_REP * NSA_HEAD_DIM
    out = (gm_ref[:, 0:w] * oc_ref[0].astype(F32) + gm_ref[:, w:2 * w] * lay_ref[0]
           + gm_ref[:, 2 * w:3 * w] * lay_ref[1])
    o_ref[0] = out.astype(o_ref.dtype)


def _gate_placement():
    m = np.zeros((LANES, 3 * NSA_REP * NSA_HEAD_DIM), np.float32)
    for br in range(3):
        for r in range(NSA_REP):
            c0 = br * NSA_REP * NSA_HEAD_DIM + r * NSA_HEAD_DIM
            m[br * NSA_REP + r, c0:c0 + NSA_HEAD_DIM] = 1.0
    return jnp.asarray(m).astype(BF16)


def _sel_fast(sh_ref, q_ref, kc_ref, vc_ref, ov_ref, oc_ref, sel_ref, *, tq, n_sel):
    rep, hd = NSA_REP, NSA_HEAD_DIM
    rows = rep * tq
    kc = kc_ref[0, 0, 0][:, 0:hd]
    ncp = kc.shape[0]
    n_idx = lax.broadcasted_iota(jnp.int32, (1, ncp), 1)
    for part in range(q_ref.shape[2] // tq):
        row0 = part * tq
        t0 = pl.program_id(2) * q_ref.shape[2] + row0
        q4 = q_ref[0, :, row0:row0 + tq, :].reshape(rows, hd)
        t_q = t0 + lax.broadcasted_iota(jnp.int32, (tq, 1), 0)
        valid_c = (n_idx * CMP_STRIDE + (CMP_BLOCK - 1) <= t_q) & (n_idx < ncp - 1)
        bias_c = jnp.where(valid_c, sh_ref[0], NEG_BIG)
        s_c = lax.dot_general(q4, kc, _NT, preferred_element_type=F32).reshape(rep, tq, ncp)
        p_c = jnp.exp(s_c + bias_c[None]).reshape(rows, ncp)
        ol_c = jnp.dot(p_c.astype(BF16), vc_ref[0, 0, 0], preferred_element_type=F32)
        inv_c = 1.0 / jnp.where(ol_c[:, hd:hd + 1] > 0, ol_c[:, hd:hd + 1], 1.0)
        _store_heads(oc_ref, ol_c[:, 0:hd] * inv_c, tq, row0)
        _store_selection(sel_ref, _select_blocks(_sum_heads(p_c * inv_c, tq), ov_ref, t0, tq, n_sel), row0)


def _sel_slow(q_ref, kc_ref, vc_ref, ov_ref, oc_ref, sel_ref, *, tq, n_sel):
    rep, hd = NSA_REP, NSA_HEAD_DIM
    rows = rep * tq
    kc = kc_ref[0, 0, 0][:, 0:hd]
    ncp = kc.shape[0]
    n_idx = lax.broadcasted_iota(jnp.int32, (1, ncp), 1)
    for part in range(q_ref.shape[2] // tq):
        row0 = part * tq
        t0 = pl.program_id(2) * q_ref.shape[2] + row0
        q4 = q_ref[0, :, row0:row0 + tq, :].reshape(rows, hd)
        t_row = t0 + (lax.broadcasted_iota(jnp.int32, (rows, 1), 0) & (tq - 1))
        s_c = lax.dot_general(q4, kc, _NT, preferred_element_type=F32)
        p_c = _softmax_rows(s_c, (n_idx * CMP_STRIDE + (CMP_BLOCK - 1) <= t_row) & (n_idx < ncp - 1))
        o_c = jnp.dot(p_c.astype(BF16), vc_ref[0, 0, 0], preferred_element_type=F32)[:, 0:hd]
        _store_heads(oc_ref, o_c, tq, row0)
        _store_selection(sel_ref, _select_blocks(_sum_heads(p_c, tq), ov_ref, t0, tq, n_sel), row0)


def _nsa_select_kernel(sh_ref, q_ref, kc_ref, vc_ref, ov_ref, oc_ref, sel_ref, *, tq, n_sel):
    data = (q_ref, kc_ref, vc_ref, ov_ref, oc_ref, sel_ref)

    @pl.when(sh_ref[3] > 0.5)
    def _():
        _sel_fast(sh_ref, *data, tq=tq, n_sel=n_sel)

    @pl.when(sh_ref[3] <= 0.5)
    def _():
        _sel_slow(*data, tq=tq, n_sel=n_sel)


def _att_fast(sh_ref, q_ref, ks_ref, vs_ref, kw_ref, vw_ref, gl_ref, gp_ref, oc_ref, sel_ref, o_ref,
              lhs_ref, acc_ref, pre_ref, gm_ref, lay_ref, *, tq, tk):
    rep, hd = NSA_REP, NSA_HEAD_DIM
    rows = rep * tq
    t0 = pl.program_id(2) * tq
    c_s, c_w = sh_ref[1], sh_ref[2]
    q4 = q_ref[0].reshape(rows, hd)
    t_q = t0 + lax.broadcasted_iota(jnp.int32, (tq, 1), 0)

    j_lane = lax.broadcasted_iota(jnp.int32, (1, hd), 1)
    shift = jnp.where((sel_ref[0, 0].astype(F32) > 0.5) & (j_lane * SEL_BLOCK < t0), c_s, NEG_BIG).astype(BF16)
    lhs_ref[:, 0:hd] = q4
    for r in range(rep):
        lhs_ref[r * tq:(r + 1) * tq, hd:2 * hd] = shift

    span = WINDOW + tq
    w0 = pl.multiple_of(jnp.maximum(t0 - WINDOW, 0), tq)
    kp = w0 + lax.broadcasted_iota(jnp.int32, (1, span), 1)
    bias_w = jnp.where((kp <= t_q) & (kp > t_q - WINDOW), c_w, NEG_BIG)
    s_w = lax.dot_general(q4, kw_ref[0, 0, pl.ds(w0, span), :], _NT, preferred_element_type=F32)
    p_w = jnp.exp(s_w.reshape(rep, tq, span) + bias_w[None]).reshape(rows, span)
    pre_ref[...] = jnp.dot(p_w.astype(BF16), vw_ref[0, 0, pl.ds(w0, span), :], preferred_element_type=F32)

    d0 = pl.multiple_of(t0, tq)
    kcol = t0 + lax.broadcasted_iota(jnp.int32, (1, tq), 1)
    bias_d = jnp.where(kcol <= t_q, c_s, NEG_BIG)
    s_d = lax.dot_general(q4, ks_ref[0, 0, pl.ds(d0, tq), :][:, 0:hd], _NT, preferred_element_type=F32)
    p_d = jnp.exp(s_d.reshape(rep, tq, tq) + bias_d[None]).reshape(rows, tq)
    acc_ref[...] = jnp.dot(p_d.astype(BF16), vs_ref[0, 0, pl.ds(d0, tq), :], preferred_element_type=F32)

    def key_tile(k0, width):
        s = lax.dot_general(lhs_ref[...], ks_ref[0, 0, pl.ds(k0, width), :], _NT, preferred_element_type=F32)
        acc_ref[...] += jnp.dot(jnp.exp(s).astype(BF16), vs_ref[0, 0, pl.ds(k0, width), :],
                                preferred_element_type=F32)

    wide = 2 * tk
    n_wide = t0 // wide

    def wide_step(c, carry):
        key_tile(pl.multiple_of(c * wide, wide), wide)
        return carry

    def tail_step(c, carry):
        key_tile(pl.multiple_of(n_wide * wide + c * tk, tk), tk)
        return carry

    lax.fori_loop(0, n_wide, wide_step, 0)
    lax.fori_loop(0, (t0 - n_wide * wide + tk - 1) // tk, tail_step, 0)

    _to_lane_layout(lay_ref, 0, _normalize_aug(acc_ref[...]), tq)
    _to_lane_layout(lay_ref, 1, _normalize_aug(pre_ref[...]), tq)
    _gate_maps(gl_ref, gp_ref, gm_ref)
    _gated_store(gm_ref, oc_ref, lay_ref, o_ref)


def _att_slow(q_ref, ks_ref, vs_ref, kw_ref, vw_ref, gl_ref, gp_ref, oc_ref, sel_ref, o_ref,
              acc_ref, gm_ref, lay_ref, m_ref, l_ref, *, tq, tk):
    rep, hd = NSA_REP, NSA_HEAD_DIM
    rows = rep * tq
    t0 = pl.program_id(2) * tq
    q4 = q_ref[0].reshape(rows, hd)
    t_row = t0 + (lax.broadcasted_iota(jnp.int32, (rows, 1), 0) & (tq - 1))
    sel = sel_ref[0, 0]
    n_blk = sel.shape[1]

    m_ref[...] = jnp.full(m_ref.shape, NEG_BIG, F32)
    l_ref[...] = jnp.zeros(l_ref.shape, F32)
    acc_ref[...] = jnp.zeros(acc_ref.shape, F32)
    t_q = t0 + lax.broadcasted_iota(jnp.int32, (tq, 1), 0)

    def kv_step(c, carry):
        k0 = pl.multiple_of(c * tk, tk)
        kt = ks_ref[0, 0, pl.ds(k0, tk), :][:, 0:hd]
        s = lax.dot_general(q4, kt, _NT, preferred_element_type=F32).reshape(rep, tq, tk)
        kk = lax.broadcasted_iota(jnp.int32, (1, tk), 1)
        blk = (k0 >> 6) + (kk >> 6)
        expand = jnp.where(lax.broadcasted_iota(jnp.int32, (n_blk, 1), 0) == blk, 1.0, 0.0).astype(BF16)
        chosen = jnp.dot(sel, expand, preferred_element_type=F32)
        mask = ((chosen > 0.5) & (k0 + kk <= t_q))[None]
        s = jnp.where(mask, s, NEG_BIG)
        m_old = m_ref[...]
        m_new = jnp.maximum(m_old, jnp.max(s, axis=-1, keepdims=True))
        p = jnp.where(mask, jnp.exp(s - m_new), 0.0)
        alpha = jnp.exp(m_old - m_new)
        l_ref[...] = alpha * l_ref[...] + jnp.sum(p, axis=-1, keepdims=True)
        pv = jnp.dot(p.reshape(rows, tk).astype(BF16), vs_ref[0, 0, pl.ds(k0, tk), :], preferred_element_type=F32)
        acc_ref[...] = alpha.reshape(rows, 1) * acc_ref[...] + pv
        m_ref[...] = m_new
        return carry

    lax.fori_loop(0, (t0 + tq + tk - 1) // tk, kv_step, 0)
    o_s = acc_ref[:, 0:hd] / l_ref[...].reshape(rows, 1)

    span = WINDOW + tq
    w0 = pl.multiple_of(jnp.maximum(t0 - WINDOW, 0), tq)
    s_w = lax.dot_general(q4, kw_ref[0, 0, pl.ds(w0, span), :], _NT, preferred_element_type=F32)
    kp = w0 + lax.broadcasted_iota(jnp.int32, (1, span), 1)
    p_w = _softmax_rows(s_w, (kp <= t_row) & (kp > t_row - WINDOW))
    o_w = jnp.dot(p_w.astype(BF16), vw_ref[0, 0, pl.ds(w0, span), :], preferred_element_type=F32)[:, 0:hd]

    _to_lane_layout(lay_ref, 0, o_s, tq)
    _to_lane_layout(lay_ref, 1, o_w, tq)
    _gate_maps(gl_ref, gp_ref, gm_ref)
    _gated_store(gm_ref, oc_ref, lay_ref, o_ref)


def _nsa_attend_kernel(sh_ref, q_ref, ks_ref, vs_ref, kw_ref, vw_ref, gl_ref, gp_ref, oc_ref, sel_ref, o_ref,
                       lhs_ref, acc_ref, pre_ref, gm_ref, lay_ref, m_ref, l_ref, *, tq, tk):
    data = (q_ref, ks_ref, vs_ref, kw_ref, vw_ref, gl_ref, gp_ref, oc_ref, sel_ref, o_ref)

    @pl.when(sh_ref[3] > 0.5)
    def _():
        _att_fast(sh_ref, *data, lhs_ref, acc_ref, pre_ref, gm_ref, lay_ref, tq=tq, tk=tk)

    @pl.when(sh_ref[3] <= 0.5)
    def _():
        _att_slow(*data, acc_ref, gm_ref, lay_ref, m_ref, l_ref, tq=tq, tk=tk)


def _overlap_t(n_blk, ncp):
    c_start = np.arange(ncp) * CMP_STRIDE
    b_start = np.arange(n_blk) * SEL_BLOCK
    ov = ((c_start[None, :] < b_start[:, None] + SEL_BLOCK) & (b_start[:, None] < c_start[None, :] + CMP_BLOCK))
    ov[:, ncp - 1] = False
    return jnp.asarray(ov.astype(np.float32)).astype(BF16)


def _nsa_shifts(q_norm, k_norm):
    bound = (NSA_HEAD_DIM ** 0.5) * jnp.max(jnp.abs(q_norm)) * jnp.max(jnp.abs(k_norm), axis=-1)
    bound = bound.astype(BF16).astype(F32)
    fast = jnp.all(bound <= MAX_CONST_SHIFT).astype(F32)
    return jnp.concatenate([-bound, fast[None]])


def _nsa_select(shifts, q, cmp_kv, *, tq, parts):
    b, h, s, hd = q.shape
    g, rep = NSA_GROUPS, NSA_REP
    ncp = cmp_kv.shape[3]
    n_blk = s // SEL_BLOCK
    tb = tq * parts
    assert n_blk % 8 == 0 and n_blk <= hd and s % tb == 0
    kernel = functools.partial(_nsa_select_kernel, tq=tq, n_sel=min(N_SEL, n_blk))
    return pl.pallas_call(
        kernel,
        grid=(b, g, s // tb),
        in_specs=[
            pl.BlockSpec(memory_space=pltpu.SMEM),
            pl.BlockSpec((1, rep, tb, hd), lambda bi, gi, i: (bi, gi, i, 0)),
            pl.BlockSpec((1, 1, 1, ncp, 2 * hd), lambda bi, gi, i: (0, bi, gi, 0, 0)),
            pl.BlockSpec((1, 1, 1, ncp, 2 * hd), lambda bi, gi, i: (1, bi, gi, 0, 0)),
            pl.BlockSpec((n_blk, ncp), lambda bi, gi, i: (0, 0)),
        ],
        out_specs=[pl.BlockSpec((1, tb, rep * hd), lambda bi, gi, i: (bi, i, gi)),
                   pl.BlockSpec((1, 1, tb, hd), lambda bi, gi, i: (bi, gi, i, 0))],
        out_shape=[jax.ShapeDtypeStruct((b, s, h * hd), BF16), jax.ShapeDtypeStruct((b, g, s, hd), BF16)],
        compiler_params=_cparams("parallel", "parallel", "parallel"),
        name="nsa_select",
    )(shifts, q, cmp_kv, cmp_kv, _overlap_t(n_blk, ncp))


def _nsa_attend(shifts, q, ks, vs, kw, vw, z3, o_cmp, sel, *, tq, tk):
    b, h, s, hd = q.shape
    g, rep = NSA_GROUPS, NSA_REP
    rows = rep * tq
    assert WINDOW % tq == 0
    full = lambda w: pl.BlockSpec((1, 1, s, w), lambda bi, gi, i: (bi, gi, 0, 0))
    return pl.pallas_call(
        functools.partial(_nsa_attend_kernel, tq=tq, tk=tk),
        grid=(b, g, s // tq),
        in_specs=[
            pl.BlockSpec(memory_space=pltpu.SMEM),
            pl.BlockSpec((1, rep, tq, hd), lambda bi, gi, i: (bi, gi, i, 0)),
            full(2 * hd), full(2 * hd), full(hd), full(2 * hd),
            pl.BlockSpec((1, tq, LANES), lambda bi, gi, i: (bi, i, COL_SMALL // LANES + gi)),
            pl.BlockSpec((LANES, 3 * rep * hd), lambda bi, gi, i: (0, 0)),
            pl.BlockSpec((1, tq, rep * hd), lambda bi, gi, i: (bi, i, gi)),
            pl.BlockSpec((1, 1, tq, hd), lambda bi, gi, i: (bi, gi, i, 0)),
        ],
        out_specs=pl.BlockSpec((1, tq, rep * hd), lambda bi, gi, i: (bi, i, gi)),
        out_shape=jax.ShapeDtypeStruct((b, s, h * hd), BF16),
        scratch_shapes=[pltpu.VMEM((rows, 2 * hd), BF16), pltpu.VMEM((rows, 2 * hd), F32),
                        pltpu.VMEM((rows, 2 * hd), F32), pltpu.VMEM((tq, 3 * rep * hd), F32),
                        pltpu.VMEM((2, tq, rep * hd), F32),
                        pltpu.VMEM((rep, tq, 1), F32), pltpu.VMEM((rep, tq, 1), F32)],
        compiler_params=_cparams("parallel", "parallel", "arbitrary"),
        name="nsa_attend",
    )(shifts, q, ks, vs, kw, vw, z3, _gate_placement(), o_cmp, sel)


def _head_block_diag(x, n_rows, head_of_lane, dtype):
    return jnp.concatenate([jnp.where(head_of_lane == h, x, 0.0) for h in range(GLA_HEADS)], axis=0).astype(dtype)


def _gla_kernel(q_ref, k_ref, v_ref, r_ref, sm_ref, wg_ref, bg_ref, ng_ref, tril_ref, o_ref,
                st_ref, upd_ref, oin_ref, qin_ref, *, n_chunks):
    c, sub = GLA_CHUNK, GLA_SUB
    dk, dv, nh = GLA_HEAD_DK, GLA_HEAD_DV, GLA_HEADS
    nk, nv = nh * dk, nh * dv

    @pl.when(pl.program_id(1) == 0)
    def _():
        st_ref[...] = jnp.zeros(st_ref.shape, F32)

    sm = sm_ref[0]
    x = (jnp.dot(sm, wg_ref[0], preferred_element_type=F32) + jnp.dot(sm, wg_ref[1], preferred_element_type=F32)
         + bg_ref[...])
    log_a = (jnp.minimum(x, 0.0) - jnp.log1p(jnp.exp(-jnp.abs(x)))) / GLA_TAU
    la_hi = log_a.astype(BF16)
    la_lo = (log_a - la_hi.astype(F32)).astype(BF16)
    bcum_all = (jnp.dot(tril_ref[...], la_hi, preferred_element_type=F32)
                + jnp.dot(tril_ref[...], la_lo, preferred_element_type=F32))
    head_k = lax.broadcasted_iota(jnp.int32, (1, nk), 1) >> 6
    head_v = lax.broadcasted_iota(jnp.int32, (1, nv), 1) >> 7
    state_mask = (lax.broadcasted_iota(jnp.int32, (nv, 1), 0) >> 7) == head_k
    causal = ((lax.broadcasted_iota(jnp.int32, (c, nh * c), 1) & (c - 1))
              <= lax.broadcasted_iota(jnp.int32, (c, nh * c), 0))

    decays = []
    for cc in range(n_chunks):
        rs = slice(cc * c, (cc + 1) * c)
        bcum = bcum_all[rs]
        q = q_ref[0, rs].astype(F32) * (dk ** -0.5)
        k = k_ref[0, rs].astype(F32)
        v = v_ref[0, rs].astype(F32)
        b_last = bcum[c - 1:c, :]

        score_rows = []
        for i in range(c // sub):
            lo, hi = i * sub, (i + 1) * sub
            ref = bcum[lo - 1:lo, :] if i > 0 else jnp.zeros((1, nk), F32)
            q_i = (q[lo:hi] * jnp.exp(bcum[lo:hi] - ref)).astype(BF16)
            k_i = k[0:hi] * jnp.exp(ref - bcum[0:hi])
            if hi < c:
                k_i = jnp.concatenate([k_i, jnp.zeros((c - hi, nk), F32)], axis=0)
            score_rows.append(lax.dot_general(q_i, _head_block_diag(k_i, c, head_k, BF16), _NT,
                                              preferred_element_type=F32))
        a = jnp.where(causal, jnp.concatenate(score_rows, axis=0), 0.0).astype(BF16)
        oin_ref[rs] = jnp.dot(a, _head_block_diag(v, c, head_v, BF16), preferred_element_type=F32)
        qin_ref[rs] = (q * jnp.exp(bcum)).astype(BF16)
        k_out = (k * jnp.exp(b_last - bcum)).astype(BF16)
        upd = lax.dot_general(v.astype(BF16), k_out, (((0,), (0,)), ((), ())), preferred_element_type=F32)
        upd_ref[cc] = jnp.where(state_mask, upd, 0.0)
        decays.append(jnp.exp(b_last))

    for cc in range(n_chunks):
        rs = slice(cc * c, (cc + 1) * c)
        st = st_ref[...]
        o = oin_ref[rs] + lax.dot_general(qin_ref[rs], st.astype(BF16), _NT, preferred_element_type=F32)
        st_ref[...] = st * decays[cc] + upd_ref[cc]
        r_gate = r_ref[0, rs].astype(F32)
        for h in range(nh):
            vs_ = slice(h * dv, (h + 1) * dv)
            o_h = o[:, vs_]
            ms = jnp.mean(o_h * o_h, axis=-1, keepdims=True)
            rg = r_gate[:, vs_]
            o_ref[0, rs, vs_] = (o_h * lax.rsqrt(ms + NORM_EPS) * ng_ref[...] * (rg * jax.nn.sigmoid(rg))).astype(o_ref.dtype)


def _gla(z3, w_gate, b_gate, norm_g, *, n_chunks):
    b, s, _ = z3.shape
    c = GLA_CHUNK * n_chunks
    nk = GLA_HEADS * GLA_HEAD_DK
    nv = GLA_HEADS * GLA_HEAD_DV
    wg = jnp.zeros((LANES, nk), F32).at[SMALL_GLOW_LANE:SMALL_GLOW_LANE + GLA_RANK].set(w_gate.astype(F32))
    wg_hi = wg.astype(BF16)
    wg = jnp.stack([wg_hi, (wg - wg_hi.astype(F32)).astype(BF16)])
    idx = np.arange(c)
    tril = jnp.asarray(((idx[:, None] >= idx[None, :])
                        & (idx[:, None] // GLA_CHUNK == idx[None, :] // GLA_CHUNK)).astype(np.float32)).astype(BF16)
    const = lambda shape: pl.BlockSpec(shape, lambda bi, i: (0,) * len(shape))
    return pl.pallas_call(
        functools.partial(_gla_kernel, n_chunks=n_chunks),
        grid=(b, s // c),
        in_specs=[
            pl.BlockSpec((1, c, nk), lambda bi, i: (bi, i, COL_GQ // nk)),
            pl.BlockSpec((1, c, nk), lambda bi, i: (bi, i, COL_GK // nk)),
            pl.BlockSpec((1, c, nv), lambda bi, i: (bi, i, COL_GV // nv)),
            pl.BlockSpec((1, c, nv), lambda bi, i: (bi, i, COL_GR // nv)),
            pl.BlockSpec((1, c, LANES), lambda bi, i: (bi, i, COL_SMALL // LANES)),
            const((2, LANES, nk)), const((1, nk)), const((1, GLA_HEAD_DV)), const((c, c)),
        ],
        out_specs=pl.BlockSpec((1, c, nv), lambda bi, i: (bi, i, 0)),
        out_shape=jax.ShapeDtypeStruct((b, s, nv), BF16),
        scratch_shapes=[pltpu.VMEM((nv, nk), F32), pltpu.VMEM((n_chunks, nv, nk), F32),
                        pltpu.VMEM((c, nv), F32), pltpu.VMEM((c, nk), BF16)],
        compiler_params=_cparams("parallel", "arbitrary"),
        name="gla",
    )(z3, z3, z3, z3, z3, wg, b_gate.reshape(1, nk).astype(F32), norm_g.reshape(1, GLA_HEAD_DV).astype(F32), tril)


def _mem_attn_kernel(q_ref, k_ref, v_ref, qg_ref, kg_ref, o_ref):
    dh = MEM_HEAD_DIM
    nt = (((1,), (1,)), ((), ()))
    for h in range(MEM_HEADS):
        sl = slice(h * dh, (h + 1) * dh)
        q = q_ref[0, :, sl].astype(F32)
        q = q * lax.rsqrt(jnp.mean(q * q, axis=-1, keepdims=True) + NORM_EPS) * qg_ref[...] * (dh ** -0.5)
        k = k_ref[0, :, sl].astype(F32)
        k = k * lax.rsqrt(jnp.mean(k * k, axis=-1, keepdims=True) + NORM_EPS) * kg_ref[...]
        s = lax.dot_general(q.astype(BF16), k.astype(BF16), nt, preferred_element_type=F32)
        m = jnp.max(s, axis=-1, keepdims=True)
        p = jnp.exp(s - m)
        p = p / jnp.sum(p, axis=-1, keepdims=True)
        o = jnp.dot(p.astype(BF16), v_ref[0, :, sl].astype(BF16), preferred_element_type=F32)
        o_ref[0, :, sl] = o.astype(o_ref.dtype)


def _mem_attention(z3, kv, q_norm, k_norm, *, tq):
    b, s, _ = z3.shape
    m = kv.shape[1]
    w = MEM_HEADS * MEM_HEAD_DIM
    const = lambda shape: pl.BlockSpec(shape, lambda bi, i: (0,) * len(shape))
    return pl.pallas_call(
        _mem_attn_kernel,
        grid=(b, s // tq),
        in_specs=[
            pl.BlockSpec((1, tq, w), lambda bi, i: (bi, i, COL_MQ // w)),
            pl.BlockSpec((1, m, w), lambda bi, i: (bi, 0, 0)),
            pl.BlockSpec((1, m, w), lambda bi, i: (bi, 0, 1)),
            const((1, MEM_HEAD_DIM)), const((1, MEM_HEAD_DIM)),
        ],
        out_specs=pl.BlockSpec((1, tq, w), lambda bi, i: (bi, i, 0)),
        out_shape=jax.ShapeDtypeStruct((b, s, w), BF16),
        compiler_params=_cparams("parallel", "parallel"),
        name="mem_attention",
    )(z3, kv, kv, q_norm.reshape(1, MEM_HEAD_DIM).astype(F32), k_norm.reshape(1, MEM_HEAD_DIM).astype(F32))


def _merge_kernel(x_ref, on_ref, og_ref, om_ref, m0_ref, m1_ref, m2_ref, bm_ref, wb_ref, wo_ref, o_ref):
    merged = None
    for br, (ref, mg_ref) in enumerate(((on_ref, m0_ref), (og_ref, m1_ref), (om_ref, m2_ref))):
        y = jnp.dot(ref[...], wb_ref[br], preferred_element_type=F32)
        gate = jax.nn.sigmoid(mg_ref[...].astype(F32) + bm_ref[br:br + 1, :])
        merged = gate * y if merged is None else merged + gate * y
    o_ref[...] = x_ref[...] + jnp.dot(merged.astype(BF16), wo_ref[...], preferred_element_type=F32)


def _merge_out(x2, o_nsa, o_gla, o_mem, z2, b_merge, w_branch, w_out, layer, *, tm):
    t, d = x2.shape
    bw = BRANCH_WIDTH
    row = lambda w: pl.BlockSpec((tm, w), lambda i: (i, 0))
    gate_cols = lambda br: pl.BlockSpec((tm, d), lambda i: (i, COL_MERGE // d + br))
    return pl.pallas_call(
        _merge_kernel,
        grid=(t // tm,),
        in_specs=[
            row(d), row(bw), row(bw), row(bw),
            gate_cols(0), gate_cols(1), gate_cols(2),
            pl.BlockSpec((N_BRANCH, d), lambda i: (0, 0)),
            pl.BlockSpec((None, N_BRANCH, bw, d), lambda i: (layer, 0, 0, 0)),
            pl.BlockSpec((None, d, d), lambda i: (layer, 0, 0)),
        ],
        out_specs=row(d),
        out_shape=jax.ShapeDtypeStruct((t, d), F32),
        compiler_params=_cparams("parallel"),
        name="merge_out",
    )(x2, o_nsa, o_gla, o_mem, z2, z2, z2, b_merge.astype(F32), w_branch, w_out)


def _mlp_kernel(x_ref, g_ref, wu_ref, wd_ref, o_ref, h_ref, acc_ref):
    j = pl.program_id(1)

    @pl.when(j == 0)
    def _():
        x = x_ref[...]
        ms = jnp.mean(x * x, axis=-1, keepdims=True)
        h_ref[...] = (x * lax.rsqrt(ms + NORM_EPS) * g_ref[...]).astype(h_ref.dtype)
        acc_ref[...] = jnp.zeros(acc_ref.shape, F32)

    u = jnp.dot(h_ref[...], wu_ref[...], preferred_element_type=F32)
    u = jnp.square(jnp.maximum(u, 0.0)).astype(BF16)
    acc_ref[...] += jnp.dot(u, wd_ref[...], preferred_element_type=F32)

    @pl.when(j == pl.num_programs(1) - 1)
    def _():
        o_ref[...] = x_ref[...] + acc_ref[...]


def _mlp(x2, g, w_up, w_down, layer, *, tm, th):
    t, d = x2.shape
    hid = w_up.shape[2]
    return pl.pallas_call(
        _mlp_kernel,
        grid=(t // tm, hid // th),
        in_specs=[
            pl.BlockSpec((tm, d), lambda i, j: (i, 0)),
            pl.BlockSpec((1, d), lambda i, j: (0, 0)),
            pl.BlockSpec((None, d, th), lambda i, j: (layer, 0, j)),
            pl.BlockSpec((None, th, d), lambda i, j: (layer, j, 0)),
        ],
        out_specs=pl.BlockSpec((tm, d), lambda i, j: (i, 0)),
        out_shape=jax.ShapeDtypeStruct((t, d), F32),
        scratch_shapes=[pltpu.VMEM((tm, d), BF16), pltpu.VMEM((tm, d), F32)],
        compiler_params=_cparams("parallel", "arbitrary"),
        name="mlp",
    )(x2, g.reshape(1, d).astype(F32), w_up, w_down)


def _tile(n, pref):
    t = min(n, pref)
    assert n % t == 0, (n, pref)
    return t


def _layer(x, mem2, pos_rows, pos_cmp, p, big, layer):
    b, s, d = x.shape
    t = b * s
    hd, g = NSA_HEAD_DIM, NSA_GROUPS
    x2 = x.reshape(t, d)

    z2 = _norm_matmul(x2, p["ln_mix"].astype(F32), big["w_in"], layer, tm=_tile(t, 1024), tn=D_IN_PAD // 2, out_dtype=BF16, name="in_proj")
    z3 = z2.reshape(b, s, D_IN_PAD)

    q, ks, vs, kw, vw = _nsa_prep(z3, pos_rows, p["nsa_q_norm"].astype(F32), p["nsa_k_norm"][1].astype(F32),
                                  p["nsa_k_norm"][2].astype(F32), tp=_tile(s, 256))
    pe2 = jnp.pad(p["cmp_pe"].reshape(2, 2, CMP_STRIDE * hd), ((0, 0), (0, 6), (0, 0))).astype(F32)
    cmp_kv = _nsa_compress(z3, p["cmp_w1"], p["cmp_w2"], pe2, pos_cmp, p["nsa_k_norm"][0].astype(F32))
    shifts = _nsa_shifts(p["nsa_q_norm"].astype(F32), p["nsa_k_norm"].astype(F32))
    o_cmp, sel = _nsa_select(shifts, q, cmp_kv, tq=_tile(s, 256), parts=2)
    o_nsa = _nsa_attend(shifts, q, ks, vs, kw, vw, z3, o_cmp, sel, tq=_tile(s, 256), tk=_tile(s, 512))

    o_gla = _gla(z3, p["gla_w_gate"], p["gla_b_gate"], p["gla_norm"], n_chunks=8)

    kv = _norm_matmul(mem2, p["mem_norm"].astype(F32), big["mem_w_kv"], layer,
                      tm=_tile(mem2.shape[0], 512), tn=512, out_dtype=BF16, name="mem_kv")
    kv = kv.reshape(b, mem2.shape[0] // b, 2 * MEM_HEADS * MEM_HEAD_DIM)
    o_mem = _mem_attention(z3, kv, p["mem_q_norm"], p["mem_k_norm"], tq=_tile(s, 512))

    x2 = _merge_out(x2, o_nsa.reshape(t, -1), o_gla.reshape(t, -1), o_mem.reshape(t, -1), z2,
                    p["b_merge"], big["w_branch"], big["w_out"], layer, tm=_tile(t, 512))
    x2 = _mlp(x2, p["ln_mlp"], big["w_up"], big["w_down"], layer, tm=_tile(t, 1024), th=1024)
    return x2.reshape(b, s, d)


def kernel(x, mem, positions, ln_mix, w_in, b_merge, nsa_q_norm, nsa_k_norm, cmp_pe, cmp_w1, cmp_w2,
           gla_w_gate, gla_b_gate, gla_norm, mem_norm, mem_w_kv, mem_q_norm, mem_k_norm, w_branch, w_out,
           ln_mlp, w_up, w_down):
    b, s, d = x.shape
    assert d == 1024 and s % WINDOW == 0 and s >= 2 * WINDOW
    depth = w_in.shape[0]
    perm_np, d_in = _in_proj_permutation(d)
    assert w_in.shape[2] == d_in
    w_in = _relayout_in_proj(w_in.astype(BF16), [int(c) for c in perm_np], d_in)
    half = CMP_STRIDE * NSA_HEAD_DIM
    cmp_w1 = jnp.concatenate([cmp_w1[:, :, :half], cmp_w1[:, :, half:]], axis=-1).astype(BF16)
    cmp_w2, mem_w_kv, w_branch, w_out, w_up, w_down = (
        a.astype(BF16) for a in (cmp_w2, mem_w_kv, w_branch, w_out, w_up, w_down))
    pos3 = positions.astype(jnp.int32).reshape(b, s, 1)
    nc = s // CMP_STRIDE
    cmp_end = np.minimum(np.arange(nc) * CMP_STRIDE + CMP_BLOCK - 1, s - 1)
    pos_cmp = pos3[:, cmp_end, :]
    pos_rows = positions.astype(jnp.int32).reshape(b, 1, s)
    mem2 = mem.reshape(b * mem.shape[1], d)
    names = ("ln_mix", "b_merge", "nsa_q_norm", "nsa_k_norm", "cmp_pe", "cmp_w1", "cmp_w2",
             "gla_w_gate", "gla_b_gate", "gla_norm", "mem_norm", "mem_q_norm", "mem_k_norm", "ln_mlp")
    stacked = (ln_mix, b_merge, nsa_q_norm, nsa_k_norm, cmp_pe, cmp_w1, cmp_w2, gla_w_gate, gla_b_gate,
               gla_norm, mem_norm, mem_q_norm, mem_k_norm, ln_mlp)
    big = dict(w_in=w_in, mem_w_kv=mem_w_kv, w_branch=w_branch, w_out=w_out, w_up=w_up, w_down=w_down)
    for l in range(depth):
        x = _layer(x, mem2, pos_rows, pos_cmp, {n: a[l] for n, a in zip(names, stacked)}, big, l)
    return x
```

```python
import functools

import numpy as np
import jax
import jax.numpy as jnp
from jax import lax
from jax.experimental import pallas as pl
from jax.experimental.pallas import tpu as pltpu

NSA_HEADS = 8
NSA_GROUPS = 2
NSA_REP = NSA_HEADS // NSA_GROUPS
NSA_HEAD_DIM = 64
CMP_BLOCK = 32
CMP_STRIDE = 16
CMP_HIDDEN = 4 * NSA_HEAD_DIM
SEL_BLOCK = 64
N_SEL = 16
WINDOW = 512
FORCE_SCORE = 1e4
GLA_HEADS = 4
GLA_HEAD_DK = 64
GLA_HEAD_DV = 128
GLA_RANK = 16
GLA_TAU = 16.0
GLA_CHUNK = 64
GLA_SUB = 16
MEM_HEADS = 4
MEM_HEAD_DIM = 128
N_BRANCH = 3
BRANCH_WIDTH = 512
ROPE_THETA = 500000.0
ROPE_ROT = NSA_HEAD_DIM // 4
ROPE_HALF = ROPE_ROT // 2
NORM_EPS = 1e-6

LANES = 128
VMEM_LIMIT_BYTES = 48 * 1024 * 1024

F32 = jnp.float32
BF16 = jnp.bfloat16
HIGHEST = lax.Precision.HIGHEST
NEG_BIG = -1e30
MAX_CONST_SHIFT = 40.0

COL_NQ = 0
COL_GV = 512
COL_GR = 1024
COL_MQ = 1536
COL_MERGE = 2048
COL_GQ = 5120
COL_GK = 5376
COL_KV = 5632
COL_SMALL = 6400
D_IN_PAD = 6656
SMALL_GLOW_LANE = 12


def _in_proj_permutation(d_model):
    sizes = (512, 128, 128, 128, 128, 128, 128, 24, 256, 256, 512, 512, 16, 512, 3 * d_model)
    off = np.concatenate([[0], np.cumsum(sizes)])
    (o_nq, o_kc, o_vc, o_ks, o_vs, o_kw, o_vw, o_ng, o_gq, o_gk, o_gv, o_gr, o_gl, o_mq, o_mg) = off[:-1]
    d_in = int(off[-1])
    perm = np.full((D_IN_PAD,), d_in, np.int32)

    def put(new, old, n):
        perm[new:new + n] = np.arange(old, old + n)

    put(COL_NQ, o_nq, 512)
    put(COL_GV, o_gv, 512)
    put(COL_GR, o_gr, 512)
    put(COL_MQ, o_mq, 512)
    put(COL_MERGE, o_mg, 3 * d_model)
    put(COL_GQ, o_gq, 256)
    put(COL_GK, o_gk, 256)
    put(COL_KV, o_kc, 768)
    for g in range(NSA_GROUPS):
        for br in range(3):
            for r in range(NSA_REP):
                perm[COL_SMALL + g * LANES + br * NSA_REP + r] = o_ng + (g * NSA_REP + r) * 3 + br
    put(COL_SMALL + SMALL_GLOW_LANE, o_gl, GLA_RANK)
    return perm, d_in


def _relayout_in_proj(w, perm, d_in):
    pieces, start = [], 0
    for i in range(1, len(perm) + 1):
        prev = perm[i - 1]
        if i < len(perm) and (perm[i] == prev + 1 if prev != d_in else perm[i] == d_in):
            continue
        n = i - start
        if prev == d_in:
            pieces.append(jnp.zeros(w.shape[:-1] + (n,), w.dtype))
        else:
            pieces.append(w[..., perm[start]:perm[start] + n])
        start = i
    return jnp.concatenate(pieces, axis=-1)


def _cparams(*sem):
    return pltpu.CompilerParams(dimension_semantics=sem, vmem_limit_bytes=VMEM_LIMIT_BYTES)


def _rms_rows(x, g):
    ms = jnp.mean(x * x, axis=-1, keepdims=True)
    return (x * lax.rsqrt(ms + NORM_EPS) * g).astype(BF16)


def _norm_matmul_kernel(x_ref, g_ref, w_ref, o_ref):
    h = _rms_rows(x_ref[...].astype(F32), g_ref[...])
    o_ref[...] = jnp.dot(h, w_ref[...], preferred_element_type=F32).astype(o_ref.dtype)


def _norm_matmul(x, g, w, layer, *, tm, tn, out_dtype, name):
    m, k = x.shape
    n = w.shape[2]
    return pl.pallas_call(
        _norm_matmul_kernel,
        grid=(n // tn, m // tm),
        in_specs=[
            pl.BlockSpec((tm, k), lambda j, i: (i, 0)),
            pl.BlockSpec((1, k), lambda j, i: (0, 0)),
            pl.BlockSpec((None, k, tn), lambda j, i: (layer, 0, j)),
        ],
        out_specs=pl.BlockSpec((tm, tn), lambda j, i: (i, j)),
        out_shape=jax.ShapeDtypeStruct((m, n), out_dtype),
        compiler_params=_cparams("parallel", "parallel"),
        name=name,
    )(x, g.reshape(1, k), w)


def _rope_tables(pos_row, freq_col, place):
    ang = freq_col * pos_row
    tn = (((0,), (0,)), ((), ()))
    lane = lax.broadcasted_iota(jnp.int32, (1, LANES), 1)
    c = lax.dot_general(jnp.cos(ang), place, tn, preferred_element_type=F32, precision=HIGHEST)
    s = lax.dot_general(jnp.sin(ang), place, tn, preferred_element_type=F32, precision=HIGHEST)
    return c + jnp.where((lane & (NSA_HEAD_DIM - 1)) >= ROPE_ROT, 1.0, 0.0), s


def _rope_lanes(x, c, s):
    n = x.shape[-1]
    if n > LANES:
        c = jnp.concatenate([c] * (n // LANES), axis=1)
        s = jnp.concatenate([s] * (n // LANES), axis=1)
    lane = lax.broadcasted_iota(jnp.int32, (1, n), 1) & (NSA_HEAD_DIM - 1)
    up = pltpu.roll(x, n - ROPE_HALF, axis=1)
    dn = pltpu.roll(x, ROPE_HALF, axis=1)
    y = jnp.where(lane < ROPE_HALF, -up, jnp.where(lane < ROPE_ROT, dn, 0.0))
    return x * c + y * s


def _head_rms(x, bd, g):
    x2 = x * x
    hi = x2.astype(BF16)
    lo = (x2 - hi.astype(F32)).astype(BF16)
    ms = (jnp.dot(hi, bd, preferred_element_type=F32) + jnp.dot(lo, bd, preferred_element_type=F32))
    return x * lax.rsqrt(ms * (1.0 / NSA_HEAD_DIM) + NORM_EPS) * g


def _nsa_prep_kernel(q_ref, ks_ref, vs_ref, kw_ref, vw_ref, pos_ref, qg_ref, ksg_ref, kwg_ref,
                     bdq_ref, bdk_ref, f_ref, place_ref,
                     qo_ref, kso_ref, vso_ref, kwo_ref, vwo_ref):
    hd = NSA_HEAD_DIM
    tp = q_ref.shape[1]
    c, s = _rope_tables(pos_ref[0].astype(F32), f_ref[...], place_ref[...])
    q = _head_rms(q_ref[0].astype(F32), bdq_ref[...], qg_ref[...])
    q = (_rope_lanes(q, c, s) * (hd ** -0.5)).astype(qo_ref.dtype)
    for h in range(NSA_HEADS):
        qo_ref[0, h] = q[:, h * hd:(h + 1) * hd]
    ks = _rope_lanes(_head_rms(ks_ref[0].astype(F32), bdk_ref[...], ksg_ref[...]), c, s)
    kw = _rope_lanes(_head_rms(kw_ref[0].astype(F32), bdk_ref[...], kwg_ref[...]), c, s)
    vs = vs_ref[0]
    vw = vw_ref[0]
    tok = pl.program_id(1) * tp + lax.broadcasted_iota(jnp.int32, (tp, hd), 0)
    lane = lax.broadcasted_iota(jnp.int32, (tp, hd), 1)
    blk_onehot = jnp.where((tok >> 6) == lane, 1.0, 0.0).astype(kso_ref.dtype)
    ones_col = jnp.ones((tp, hd), vso_ref.dtype)
    for g in range(NSA_GROUPS):
        sl = slice(g * hd, (g + 1) * hd)
        kso_ref[0, g, :, 0:hd] = ks[:, sl].astype(kso_ref.dtype)
        kso_ref[0, g, :, hd:2 * hd] = blk_onehot
        kwo_ref[0, g] = kw[:, sl].astype(kwo_ref.dtype)
        vso_ref[0, g, :, 0:hd] = vs[:, sl].astype(vso_ref.dtype)
        vso_ref[0, g, :, hd:2 * hd] = ones_col
        vwo_ref[0, g, :, 0:hd] = vw[:, sl].astype(vwo_ref.dtype)
        vwo_ref[0, g, :, hd:2 * hd] = ones_col


def _block_diag_ones(n, width):
    i = np.arange(n)
    return jnp.asarray((i[:, None] // width == i[None, :] // width).astype(np.float32)).astype(BF16)


def _rope_freq_col():
    inv = ROPE_THETA ** (-(np.arange(ROPE_HALF, dtype=np.float64)) / ROPE_HALF)
    return jnp.asarray(inv.astype(np.float32)).reshape(ROPE_HALF, 1)


def _rope_placement():
    m = np.zeros((ROPE_HALF, LANES), np.float32)
    for f in range(ROPE_HALF):
        for base in range(0, LANES, NSA_HEAD_DIM):
            m[f, base + f] = 1.0
            m[f, base + f + ROPE_HALF] = 1.0
    return jnp.asarray(m)


def _nsa_prep(z3, pos_rows, q_norm, ks_norm, kw_norm, *, tp):
    b, s, _ = z3.shape
    hd, g = NSA_HEAD_DIM, NSA_GROUPS
    assert SEL_BLOCK == 64 and s // SEL_BLOCK <= hd
    kvb = COL_KV // LANES
    qg = jnp.tile(q_norm, NSA_HEADS).reshape(1, NSA_HEADS * hd)
    ksg = jnp.tile(ks_norm, g).reshape(1, g * hd)
    kwg = jnp.tile(kw_norm, g).reshape(1, g * hd)
    const = lambda shape: pl.BlockSpec(shape, lambda bi, i: (0,) * len(shape))
    plain_out = jax.ShapeDtypeStruct((b, g, s, hd), BF16)
    plain_spec = pl.BlockSpec((1, g, tp, hd), lambda bi, i: (bi, 0, i, 0))
    aug_out = jax.ShapeDtypeStruct((b, g, s, 2 * hd), BF16)
    aug_spec = pl.BlockSpec((1, g, tp, 2 * hd), lambda bi, i: (bi, 0, i, 0))
    return pl.pallas_call(
        _nsa_prep_kernel,
        grid=(b, s // tp),
        in_specs=[
            pl.BlockSpec((1, tp, 512), lambda bi, i: (bi, i, COL_NQ // 512)),
            pl.BlockSpec((1, tp, LANES), lambda bi, i: (bi, i, kvb + 2)),
            pl.BlockSpec((1, tp, LANES), lambda bi, i: (bi, i, kvb + 3)),
            pl.BlockSpec((1, tp, LANES), lambda bi, i: (bi, i, kvb + 4)),
            pl.BlockSpec((1, tp, LANES), lambda bi, i: (bi, i, kvb + 5)),
            pl.BlockSpec((1, 1, tp), lambda bi, i: (bi, 0, i)),
            const((1, 512)), const((1, LANES)), const((1, LANES)),
            const((512, 512)), const((LANES, LANES)),
            const((ROPE_HALF, 1)), const((ROPE_HALF, LANES)),
        ],
        out_specs=[
            pl.BlockSpec((1, NSA_HEADS, tp, hd), lambda bi, i: (bi, 0, i, 0)),
            aug_spec, aug_spec, plain_spec, aug_spec,
        ],
        out_shape=[jax.ShapeDtypeStruct((b, NSA_HEADS, s, hd), BF16), aug_out, aug_out, plain_out, aug_out],
        compiler_params=_cparams("parallel", "parallel"),
        name="nsa_prep",
    )(z3, z3, z3, z3, z3, pos_rows, qg, ksg, kwg,
      _block_diag_ones(512, hd), _block_diag_ones(LANES, hd), _rope_freq_col(), _rope_placement())


def _gelu_tanh(x):
    return 0.5 * x * (1.0 + jnp.tanh(np.sqrt(2.0 / np.pi) * (x + 0.044715 * x * x * x)))


def _nsa_cmp_kernel(x_ref, w1_ref, w2_ref, pe_ref, pos_ref, g_ref, rot_ref, f_ref, o_ref, xs_ref):
    kind = pl.program_id(0)
    hd = NSA_HEAD_DIM
    nc = x_ref.shape[1] // CMP_STRIDE
    xs_ref[...] = x_ref[0].astype(F32)
    r = jnp.dot(pe_ref[0].astype(BF16), w1_ref[0], preferred_element_type=F32)
    ab = [jnp.zeros((nc, 2 * CMP_HIDDEN), F32) for _ in range(NSA_GROUPS)]
    for p in range(CMP_STRIDE):
        xp = xs_ref[pl.ds(p, nc, stride=CMP_STRIDE), :].astype(BF16)
        w1p = w1_ref[0, p * hd:(p + 1) * hd, :]
        for g in range(NSA_GROUPS):
            ab[g] = ab[g] + jnp.dot(xp[:, g * hd:(g + 1) * hd], w1p, preferred_element_type=F32)

    for g in range(NSA_GROUPS):
        a = ab[g][:, :CMP_HIDDEN] + r[0:1, :CMP_HIDDEN]
        bm = ab[g][:, CMP_HIDDEN:] + r[1:2, CMP_HIDDEN:]
        hid = _gelu_tanh(a + pltpu.roll(bm, nc - 1, axis=0))
        comp = jnp.dot(hid.astype(BF16), w2_ref[0], preferred_element_type=F32)

        @pl.when(kind == 0)
        def _():
            ms = jnp.mean(comp * comp, axis=-1, keepdims=True)
            kn = comp * lax.rsqrt(ms + NORM_EPS) * g_ref[...]
            ang = pos_ref[0].astype(F32) * f_ref[...]
            y = jnp.dot(kn, rot_ref[...], preferred_element_type=F32, precision=HIGHEST)
            o_ref[0, 0, g, :, 0:hd] = (kn * jnp.cos(ang) + y * jnp.sin(ang)).astype(o_ref.dtype)
            o_ref[0, 0, g, :, hd:2 * hd] = jnp.zeros((nc, hd), o_ref.dtype)

        @pl.when(kind != 0)
        def _():
            o_ref[0, 0, g, :, 0:hd] = comp.astype(o_ref.dtype)
            o_ref[0, 0, g, :, hd:2 * hd] = jnp.ones((nc, hd), o_ref.dtype)


def _rope_freq_head():
    p = np.arange(NSA_HEAD_DIM)
    inv = ROPE_THETA ** (-(np.arange(ROPE_HALF, dtype=np.float64)) / ROPE_HALF)
    f = np.where(p < ROPE_ROT, inv[p % ROPE_HALF], 0.0)
    return jnp.asarray(f.astype(np.float32)).reshape(1, NSA_HEAD_DIM)


def _rope_rot_matrix(n):
    m = np.zeros((n, n), np.float32)
    for base in range(0, n, NSA_HEAD_DIM):
        for l in range(ROPE_HALF):
            m[base + l + ROPE_HALF, base + l] = -1.0
            m[base + l, base + l + ROPE_HALF] = 1.0
    return jnp.asarray(m)


def _nsa_compress(z3, w1cat, w2, pe2, pos_cmp, k_norm0):
    b, s, _ = z3.shape
    hd, g = NSA_HEAD_DIM, NSA_GROUPS
    nc = s // CMP_STRIDE
    assert g * hd == LANES
    const = lambda shape: pl.BlockSpec(shape, lambda k, bi: (0,) * len(shape))
    return pl.pallas_call(
        _nsa_cmp_kernel,
        grid=(2, b),
        in_specs=[
            pl.BlockSpec((1, s, LANES), lambda k, bi: (bi, 0, COL_KV // LANES + k)),
            pl.BlockSpec((1, CMP_STRIDE * hd, 2 * CMP_HIDDEN), lambda k, bi: (k, 0, 0)),
            pl.BlockSpec((1, CMP_HIDDEN, hd), lambda k, bi: (k, 0, 0)),
            pl.BlockSpec((1, 8, CMP_STRIDE * hd), lambda k, bi: (k, 0, 0)),
            pl.BlockSpec((1, nc, 1), lambda k, bi: (bi, 0, 0)),
            const((1, hd)), const((hd, hd)), const((1, hd)),
        ],
        out_specs=pl.BlockSpec((1, 1, g, nc, 2 * hd), lambda k, bi: (k, bi, 0, 0, 0)),
        out_shape=jax.ShapeDtypeStruct((2, b, g, nc, 2 * hd), BF16),
        scratch_shapes=[pltpu.VMEM((s, LANES), F32)],
        compiler_params=_cparams("parallel", "parallel"),
        name="nsa_compress",
    )(z3, w1cat, w2, pe2, pos_cmp, k_norm0.reshape(1, hd), _rope_rot_matrix(hd), _rope_freq_head())


def _softmax_rows(s, mask):
    s = jnp.where(mask, s, NEG_BIG)
    m = jnp.max(s, axis=-1, keepdims=True)
    p = jnp.where(mask, jnp.exp(s - m), 0.0)
    d = jnp.sum(p, axis=-1, keepdims=True)
    return p / jnp.where(d > 0, d, 1.0)


_NT = (((1,), (1,)), ((), ()))


def _select_blocks(p_sum, ov_ref, t0, tq, n_sel):
    n_blk = ov_ref.shape[0]
    p_hi = p_sum.astype(BF16)
    p_lo = (p_sum - p_hi.astype(F32)).astype(BF16)
    imp_t = (lax.dot_general(ov_ref[...], p_hi, _NT, preferred_element_type=F32)
             + lax.dot_general(ov_ref[...], p_lo, _NT, preferred_element_type=F32))
    j_col = lax.broadcasted_iota(jnp.int32, (n_blk, 1), 0)
    t_lane = t0 + lax.broadcasted_iota(jnp.int32, (1, tq), 1)
    causal = j_col * SEL_BLOCK <= t_lane
    cur = t_lane >> 6
    forced = causal & ((j_col == 0) | (j_col == cur) | (j_col == cur - 1))
    score = jnp.where(forced, FORCE_SCORE, jnp.where(causal, imp_t, -FORCE_SCORE))
    ng = n_blk // 8
    groups = [score[8 * v:8 * v + 8] for v in range(ng)]
    ranks = [jnp.zeros((8, tq), F32) for _ in range(ng)]
    sub = lax.broadcasted_iota(jnp.int32, (8, tq), 0)
    for jp in range(n_blk):
        row = jnp.broadcast_to(score[jp:jp + 1, :], (8, tq))
        vj = jp // 8
        for v in range(ng):
            if v < vj:
                ahead = row > groups[v]
            elif v > vj:
                ahead = row >= groups[v]
            else:
                ahead = (row > groups[v]) | ((row == groups[v]) & (sub > jp % 8))
            ranks[v] = ranks[v] + jnp.where(ahead, 1.0, 0.0)
    sel_t = jnp.where(jnp.concatenate(ranks, axis=0) < n_sel, 1.0, 0.0).astype(BF16)
    eye = (lax.broadcasted_iota(jnp.int32, (tq, tq), 0) == lax.broadcasted_iota(jnp.int32, (tq, tq), 1)).astype(BF16)
    return lax.dot_general(eye, sel_t, _NT, preferred_element_type=F32)


def _sum_heads(p, tq):
    out = p[0:tq]
    for r in range(1, NSA_REP):
        out = out + p[r * tq:(r + 1) * tq]
    return out


def _store_heads(o_ref, o, tq, row0):
    hd = NSA_HEAD_DIM
    for r in range(NSA_REP):
        o_ref[0, row0:row0 + tq, r * hd:(r + 1) * hd] = o[r * tq:(r + 1) * tq].astype(o_ref.dtype)


def _store_selection(sel_ref, sel, row0):
    tq, n_blk = sel.shape
    sel = sel.astype(sel_ref.dtype)
    if n_blk < NSA_HEAD_DIM:
        sel = jnp.concatenate([sel, jnp.zeros((tq, NSA_HEAD_DIM - n_blk), sel_ref.dtype)], axis=1)
    sel_ref[0, 0, row0:row0 + tq] = sel


def _normalize_aug(ol):
    return ol * pltpu.roll(1.0 / ol, NSA_HEAD_DIM, axis=1)


def _gate_maps(gl_ref, gp_ref, gm_ref):
    gates = jax.nn.sigmoid(gl_ref[0].astype(F32))
    g_hi = gates.astype(BF16)
    g_lo = (gates - g_hi.astype(F32)).astype(BF16)
    gm_ref[...] = (jnp.dot(g_hi, gp_ref[...], preferred_element_type=F32)
                   + jnp.dot(g_lo, gp_ref[...], preferred_element_type=F32))


def _to_lane_layout(lay_ref, idx, o, tq):
    hd = NSA_HEAD_DIM
    for r in range(NSA_REP):
        lay_ref[idx, :, r * hd:(r + 1) * hd] = o[r * tq:(r + 1) * tq, 0:hd]


def _gated_store(gm_ref, oc_ref, lay_ref, o_ref):
    w = NSA_REP * NSA_HEAD_DIM
    out = (gm_ref[:, 0:w] * oc_ref[0].astype(F32) + gm_ref[:, w:2 * w] * lay_ref[0]
           + gm_ref[:, 2 * w:3 * w] * lay_ref[1])
    o_ref[0] = out.astype(o_ref.dtype)


def _gate_placement():
    m = np.zeros((LANES, 3 * NSA_REP * NSA_HEAD_DIM), np.float32)
    for br in range(3):
        for r in range(NSA_REP):
            c0 = br * NSA_REP * NSA_HEAD_DIM + r * NSA_HEAD_DIM
            m[br * NSA_REP + r, c0:c0 + NSA_HEAD_DIM] = 1.0
    return jnp.asarray(m).astype(BF16)


def _sel_fast(sh_ref, q_ref, kc_ref, vc_ref, ov_ref, oc_ref, sel_ref, *, tq, n_sel):
    rep, hd = NSA_REP, NSA_HEAD_DIM
    rows = rep * tq
    kc = kc_ref[0, 0, 0][:, 0:hd]
    ncp = kc.shape[0]
    n_idx = lax.broadcasted_iota(jnp.int32, (1, ncp), 1)
    for part in range(q_ref.shape[2] // tq):
        row0 = part * tq
        t0 = pl.program_id(2) * q_ref.shape[2] + row0
        q4 = q_ref[0, :, row0:row0 + tq, :].reshape(rows, hd)
        t_q = t0 + lax.broadcasted_iota(jnp.int32, (tq, 1), 0)
        valid_c = (n_idx * CMP_STRIDE + (CMP_BLOCK - 1) <= t_q) & (n_idx < ncp - 1)
        bias_c = jnp.where(valid_c, sh_ref[0], NEG_BIG)
        s_c = lax.dot_general(q4, kc, _NT, preferred_element_type=F32).reshape(rep, tq, ncp)
        p_c = jnp.exp(s_c + bias_c[None]).reshape(rows, ncp)
        ol_c = jnp.dot(p_c.astype(BF16), vc_ref[0, 0, 0], preferred_element_type=F32)
        inv_c = 1.0 / jnp.where(ol_c[:, hd:hd + 1] > 0, ol_c[:, hd:hd + 1], 1.0)
        _store_heads(oc_ref, ol_c[:, 0:hd] * inv_c, tq, row0)
        _store_selection(sel_ref, _select_blocks(_sum_heads(p_c * inv_c, tq), ov_ref, t0, tq, n_sel), row0)


def _sel_slow(q_ref, kc_ref, vc_ref, ov_ref, oc_ref, sel_ref, *, tq, n_sel):
    rep, hd = NSA_REP, NSA_HEAD_DIM
    rows = rep * tq
    kc = kc_ref[0, 0, 0][:, 0:hd]
    ncp = kc.shape[0]
    n_idx = lax.broadcasted_iota(jnp.int32, (1, ncp), 1)
    for part in range(q_ref.shape[2] // tq):
        row0 = part * tq
        t0 = pl.program_id(2) * q_ref.shape[2] + row0
        q4 = q_ref[0, :, row0:row0 + tq, :].reshape(rows, hd)
        t_row = t0 + (lax.broadcasted_iota(jnp.int32, (rows, 1), 0) & (tq - 1))
        s_c = lax.dot_general(q4, kc, _NT, preferred_element_type=F32)
        p_c = _softmax_rows(s_c, (n_idx * CMP_STRIDE + (CMP_BLOCK - 1) <= t_row) & (n_idx < ncp - 1))
        o_c = jnp.dot(p_c.astype(BF16), vc_ref[0, 0, 0], preferred_element_type=F32)[:, 0:hd]
        _store_heads(oc_ref, o_c, tq, row0)
        _store_selection(sel_ref, _select_blocks(_sum_heads(p_c, tq), ov_ref, t0, tq, n_sel), row0)


def _nsa_select_kernel(sh_ref, q_ref, kc_ref, vc_ref, ov_ref, oc_ref, sel_ref, *, tq, n_sel):
    data = (q_ref, kc_ref, vc_ref, ov_ref, oc_ref, sel_ref)

    @pl.when(sh_ref[3] > 0.5)
    def _():
        _sel_fast(sh_ref, *data, tq=tq, n_sel=n_sel)

    @pl.when(sh_ref[3] <= 0.5)
    def _():
        _sel_slow(*data, tq=tq, n_sel=n_sel)


def _att_fast(sh_ref, q_ref, ks_ref, vs_ref, kw_ref, vw_ref, gl_ref, gp_ref, oc_ref, sel_ref, o_ref,
              lhs_ref, acc_ref, pre_ref, gm_ref, lay_ref, *, tq, tk):
    rep, hd = NSA_REP, NSA_HEAD_DIM
    rows = rep * tq
    t0 = pl.program_id(2) * tq
    c_s, c_w = sh_ref[1], sh_ref[2]
    q4 = q_ref[0].reshape(rows, hd)
    t_q = t0 + lax.broadcasted_iota(jnp.int32, (tq, 1), 0)

    j_lane = lax.broadcasted_iota(jnp.int32, (1, hd), 1)
    shift = jnp.where((sel_ref[0, 0].astype(F32) > 0.5) & (j_lane * SEL_BLOCK < t0), c_s, NEG_BIG).astype(BF16)
    lhs_ref[:, 0:hd] = q4
    for r in range(rep):
        lhs_ref[r * tq:(r + 1) * tq, hd:2 * hd] = shift

    span = WINDOW + tq
    w0 = pl.multiple_of(jnp.maximum(t0 - WINDOW, 0), tq)
    kp = w0 + lax.broadcasted_iota(jnp.int32, (1, span), 1)
    bias_w = jnp.where((kp <= t_q) & (kp > t_q - WINDOW), c_w, NEG_BIG)
    s_w = lax.dot_general(q4, kw_ref[0, 0, pl.ds(w0, span), :], _NT, preferred_element_type=F32)
    p_w = jnp.exp(s_w.reshape(rep, tq, span) + bias_w[None]).reshape(rows, span)
    pre_ref[...] = jnp.dot(p_w.astype(BF16), vw_ref[0, 0, pl.ds(w0, span), :], preferred_element_type=F32)

    d0 = pl.multiple_of(t0, tq)
    kcol = t0 + lax.broadcasted_iota(jnp.int32, (1, tq), 1)
    bias_d = jnp.where(kcol <= t_q, c_s, NEG_BIG)
    s_d = lax.dot_general(q4, ks_ref[0, 0, pl.ds(d0, tq), :][:, 0:hd], _NT, preferred_element_type=F32)
    p_d = jnp.exp(s_d.reshape(rep, tq, tq) + bias_d[None]).reshape(rows, tq)
    acc_ref[...] = jnp.dot(p_d.astype(BF16), vs_ref[0, 0, pl.ds(d0, tq), :], preferred_element_type=F32)

    def key_tile(k0, width):
        s = lax.dot_general(lhs_ref[...], ks_ref[0, 0, pl.ds(k0, width), :], _NT, preferred_element_type=F32)
        acc_ref[...] += jnp.dot(jnp.exp(s).astype(BF16), vs_ref[0, 0, pl.ds(k0, width), :],
                                preferred_element_type=F32)

    wide = 2 * tk
    n_wide = t0 // wide

    def wide_step(c, carry):
        key_tile(pl.multiple_of(c * wide, wide), wide)
        return carry

    def tail_step(c, carry):
        key_tile(pl.multiple_of(n_wide * wide + c * tk, tk), tk)
        return carry

    lax.fori_loop(0, n_wide, wide_step, 0)
    lax.fori_loop(0, (t0 - n_wide * wide + tk - 1) // tk, tail_step, 0)

    _to_lane_layout(lay_ref, 0, _normalize_aug(acc_ref[...]), tq)
    _to_lane_layout(lay_ref, 1, _normalize_aug(pre_ref[...]), tq)
    _gate_maps(gl_ref, gp_ref, gm_ref)
    _gated_store(gm_ref, oc_ref, lay_ref, o_ref)


def _att_slow(q_ref, ks_ref, vs_ref, kw_ref, vw_ref, gl_ref, gp_ref, oc_ref, sel_ref, o_ref,
              acc_ref, gm_ref, lay_ref, m_ref, l_ref, *, tq, tk):
    rep, hd = NSA_REP, NSA_HEAD_DIM
    rows = rep * tq
    t0 = pl.program_id(2) * tq
    q4 = q_ref[0].reshape(rows, hd)
    t_row = t0 + (lax.broadcasted_iota(jnp.int32, (rows, 1), 0) & (tq - 1))
    sel = sel_ref[0, 0]
    n_blk = sel.shape[1]

    m_ref[...] = jnp.full(m_ref.shape, NEG_BIG, F32)
    l_ref[...] = jnp.zeros(l_ref.shape, F32)
    acc_ref[...] = jnp.zeros(acc_ref.shape, F32)
    t_q = t0 + lax.broadcasted_iota(jnp.int32, (tq, 1), 0)

    def kv_step(c, carry):
        k0 = pl.multiple_of(c * tk, tk)
        kt = ks_ref[0, 0, pl.ds(k0, tk), :][:, 0:hd]
        s = lax.dot_general(q4, kt, _NT, preferred_element_type=F32).reshape(rep, tq, tk)
        kk = lax.broadcasted_iota(jnp.int32, (1, tk), 1)
        blk = (k0 >> 6) + (kk >> 6)
        expand = jnp.where(lax.broadcasted_iota(jnp.int32, (n_blk, 1), 0) == blk, 1.0, 0.0).astype(BF16)
        chosen = jnp.dot(sel, expand, preferred_element_type=F32)
        mask = ((chosen > 0.5) & (k0 + kk <= t_q))[None]
        s = jnp.where(mask, s, NEG_BIG)
        m_old = m_ref[...]
        m_new = jnp.maximum(m_old, jnp.max(s, axis=-1, keepdims=True))
        p = jnp.where(mask, jnp.exp(s - m_new), 0.0)
        alpha = jnp.exp(m_old - m_new)
        l_ref[...] = alpha * l_ref[...] + jnp.sum(p, axis=-1, keepdims=True)
        pv = jnp.dot(p.reshape(rows, tk).astype(BF16), vs_ref[0, 0, pl.ds(k0, tk), :], preferred_element_type=F32)
        acc_ref[...] = alpha.reshape(rows, 1) * acc_ref[...] + pv
        m_ref[...] = m_new
        return carry

    lax.fori_loop(0, (t0 + tq + tk - 1) // tk, kv_step, 0)
    o_s = acc_ref[:, 0:hd] / l_ref[...].reshape(rows, 1)

    span = WINDOW + tq
    w0 = pl.multiple_of(jnp.maximum(t0 - WINDOW, 0), tq)
    s_w = lax.dot_general(q4, kw_ref[0, 0, pl.ds(w0, span), :], _NT, preferred_element_type=F32)
    kp = w0 + lax.broadcasted_iota(jnp.int32, (1, span), 1)
    p_w = _softmax_rows(s_w, (kp <= t_row) & (kp > t_row - WINDOW))
    o_w = jnp.dot(p_w.astype(BF16), vw_ref[0, 0, pl.ds(w0, span), :], preferred_element_type=F32)[:, 0:hd]

    _to_lane_layout(lay_ref, 0, o_s, tq)
    _to_lane_layout(lay_ref, 1, o_w, tq)
    _gate_maps(gl_ref, gp_ref, gm_ref)
    _gated_store(gm_ref, oc_ref, lay_ref, o_ref)


def _nsa_attend_kernel(sh_ref, q_ref, ks_ref, vs_ref, kw_ref, vw_ref, gl_ref, gp_ref, oc_ref, sel_ref, o_ref,
                       lhs_ref, acc_ref, pre_ref, gm_ref, lay_ref, m_ref, l_ref, *, tq, tk):
    data = (q_ref, ks_ref, vs_ref, kw_ref, vw_ref, gl_ref, gp_ref, oc_ref, sel_ref, o_ref)

    @pl.when(sh_ref[3] > 0.5)
    def _():
        _att_fast(sh_ref, *data, lhs_ref, acc_ref, pre_ref, gm_ref, lay_ref, tq=tq, tk=tk)

    @pl.when(sh_ref[3] <= 0.5)
    def _():
        _att_slow(*data, acc_ref, gm_ref, lay_ref, m_ref, l_ref, tq=tq, tk=tk)


def _overlap_t(n_blk, ncp):
    c_start = np.arange(ncp) * CMP_STRIDE
    b_start = np.arange(n_blk) * SEL_BLOCK
    ov = ((c_start[None, :] < b_start[:, None] + SEL_BLOCK) & (b_start[:, None] < c_start[None, :] + CMP_BLOCK))
    ov[:, ncp - 1] = False
    return jnp.asarray(ov.astype(np.float32)).astype(BF16)


def _nsa_shifts(q_norm, k_norm):
    bound = (NSA_HEAD_DIM ** 0.5) * jnp.max(jnp.abs(q_norm)) * jnp.max(jnp.abs(k_norm), axis=-1)
    bound = bound.astype(BF16).astype(F32)
    fast = jnp.all(bound <= MAX_CONST_SHIFT).astype(F32)
    return jnp.concatenate([-bound, fast[None]])


def _nsa_select(shifts, q, cmp_kv, *, tq, parts):
    b, h, s, hd = q.shape
    g, rep = NSA_GROUPS, NSA_REP
    ncp = cmp_kv.shape[3]
    n_blk = s // SEL_BLOCK
    tb = tq * parts
    assert n_blk % 8 == 0 and n_blk <= hd and s % tb == 0
    kernel = functools.partial(_nsa_select_kernel, tq=tq, n_sel=min(N_SEL, n_blk))
    return pl.pallas_call(
        kernel,
        grid=(b, g, s // tb),
        in_specs=[
            pl.BlockSpec(memory_space=pltpu.SMEM),
            pl.BlockSpec((1, rep, tb, hd), lambda bi, gi, i: (bi, gi, i, 0)),
            pl.BlockSpec((1, 1, 1, ncp, 2 * hd), lambda bi, gi, i: (0, bi, gi, 0, 0)),
            pl.BlockSpec((1, 1, 1, ncp, 2 * hd), lambda bi, gi, i: (1, bi, gi, 0, 0)),
            pl.BlockSpec((n_blk, ncp), lambda bi, gi, i: (0, 0)),
        ],
        out_specs=[pl.BlockSpec((1, tb, rep * hd), lambda bi, gi, i: (bi, i, gi)),
                   pl.BlockSpec((1, 1, tb, hd), lambda bi, gi, i: (bi, gi, i, 0))],
        out_shape=[jax.ShapeDtypeStruct((b, s, h * hd), BF16), jax.ShapeDtypeStruct((b, g, s, hd), BF16)],
        compiler_params=_cparams("parallel", "parallel", "parallel"),
        name="nsa_select",
    )(shifts, q, cmp_kv, cmp_kv, _overlap_t(n_blk, ncp))


def _nsa_attend(shifts, q, ks, vs, kw, vw, z3, o_cmp, sel, *, tq, tk):
    b, h, s, hd = q.shape
    g, rep = NSA_GROUPS, NSA_REP
    rows = rep * tq
    assert WINDOW % tq == 0
    full = lambda w: pl.BlockSpec((1, 1, s, w), lambda bi, gi, i: (bi, gi, 0, 0))
    return pl.pallas_call(
        functools.partial(_nsa_attend_kernel, tq=tq, tk=tk),
        grid=(b, g, s // tq),
        in_specs=[
            pl.BlockSpec(memory_space=pltpu.SMEM),
            pl.BlockSpec((1, rep, tq, hd), lambda bi, gi, i: (bi, gi, i, 0)),
            full(2 * hd), full(2 * hd), full(hd), full(2 * hd),
            pl.BlockSpec((1, tq, LANES), lambda bi, gi, i: (bi, i, COL_SMALL // LANES + gi)),
            pl.BlockSpec((LANES, 3 * rep * hd), lambda bi, gi, i: (0, 0)),
            pl.BlockSpec((1, tq, rep * hd), lambda bi, gi, i: (bi, i, gi)),
            pl.BlockSpec((1, 1, tq, hd), lambda bi, gi, i: (bi, gi, i, 0)),
        ],
        out_specs=pl.BlockSpec((1, tq, rep * hd), lambda bi, gi, i: (bi, i, gi)),
        out_shape=jax.ShapeDtypeStruct((b, s, h * hd), BF16),
        scratch_shapes=[pltpu.VMEM((rows, 2 * hd), BF16), pltpu.VMEM((rows, 2 * hd), F32),
                        pltpu.VMEM((rows, 2 * hd), F32), pltpu.VMEM((tq, 3 * rep * hd), F32),
                        pltpu.VMEM((2, tq, rep * hd), F32),
                        pltpu.VMEM((rep, tq, 1), F32), pltpu.VMEM((rep, tq, 1), F32)],
        compiler_params=_cparams("parallel", "parallel", "arbitrary"),
        name="nsa_attend",
    )(shifts, q, ks, vs, kw, vw, z3, _gate_placement(), o_cmp, sel)


def _head_block_diag(x, n_rows, head_of_lane, dtype):
    return jnp.concatenate([jnp.where(head_of_lane == h, x, 0.0) for h in range(GLA_HEADS)], axis=0).astype(dtype)


def _gla_kernel(q_ref, k_ref, v_ref, r_ref, sm_ref, wg_ref, bg_ref, ng_ref, tril_ref, o_ref,
                st_ref, upd_ref, oin_ref, qin_ref, *, n_chunks):
    c, sub = GLA_CHUNK, GLA_SUB
    dk, dv, nh = GLA_HEAD_DK, GLA_HEAD_DV, GLA_HEADS
    nk, nv = nh * dk, nh * dv

    @pl.when(pl.program_id(1) == 0)
    def _():
        st_ref[...] = jnp.zeros(st_ref.shape, F32)

    sm = sm_ref[0]
    x = (jnp.dot(sm, wg_ref[0], preferred_element_type=F32) + jnp.dot(sm, wg_ref[1], preferred_element_type=F32)
         + bg_ref[...])
    log_a = (jnp.minimum(x, 0.0) - jnp.log1p(jnp.exp(-jnp.abs(x)))) / GLA_TAU
    la_hi = log_a.astype(BF16)
    la_lo = (log_a - la_hi.astype(F32)).astype(BF16)
    bcum_all = (jnp.dot(tril_ref[...], la_hi, preferred_element_type=F32)
                + jnp.dot(tril_ref[...], la_lo, preferred_element_type=F32))
    head_k = lax.broadcasted_iota(jnp.int32, (1, nk), 1) >> 6
    head_v = lax.broadcasted_iota(jnp.int32, (1, nv), 1) >> 7
    state_mask = (lax.broadcasted_iota(jnp.int32, (nv, 1), 0) >> 7) == head_k
    causal = ((lax.broadcasted_iota(jnp.int32, (c, nh * c), 1) & (c - 1))
              <= lax.broadcasted_iota(jnp.int32, (c, nh * c), 0))

    decays = []
    for cc in range(n_chunks):
        rs = slice(cc * c, (cc + 1) * c)
        bcum = bcum_all[rs]
        q = q_ref[0, rs].astype(F32) * (dk ** -0.5)
        k = k_ref[0, rs].astype(F32)
        v = v_ref[0, rs].astype(F32)
        b_last = bcum[c - 1:c, :]

        score_rows = []
        for i in range(c // sub):
            lo, hi = i * sub, (i + 1) * sub
            ref = bcum[lo - 1:lo, :] if i > 0 else jnp.zeros((1, nk), F32)
            q_i = (q[lo:hi] * jnp.exp(bcum[lo:hi] - ref)).astype(BF16)
            k_i = k[0:hi] * jnp.exp(ref - bcum[0:hi])
            if hi < c:
                k_i = jnp.concatenate([k_i, jnp.zeros((c - hi, nk), F32)], axis=0)
            score_rows.append(lax.dot_general(q_i, _head_block_diag(k_i, c, head_k, BF16), _NT,
                                              preferred_element_type=F32))
        a = jnp.where(causal, jnp.concatenate(score_rows, axis=0), 0.0).astype(BF16)
        oin_ref[rs] = jnp.dot(a, _head_block_diag(v, c, head_v, BF16), preferred_element_type=F32)
        qin_ref[rs] = (q * jnp.exp(bcum)).astype(BF16)
        k_out = (k * jnp.exp(b_last - bcum)).astype(BF16)
        upd = lax.dot_general(v.astype(BF16), k_out, (((0,), (0,)), ((), ())), preferred_element_type=F32)
        upd_ref[cc] = jnp.where(state_mask, upd, 0.0)
        decays.append(jnp.exp(b_last))

    for cc in range(n_chunks):
        rs = slice(cc * c, (cc + 1) * c)
        st = st_ref[...]
        o = oin_ref[rs] + lax.dot_general(qin_ref[rs], st.astype(BF16), _NT, preferred_element_type=F32)
        st_ref[...] = st * decays[cc] + upd_ref[cc]
        r_gate = r_ref[0, rs].astype(F32)
        for h in range(nh):
            vs_ = slice(h * dv, (h + 1) * dv)
            o_h = o[:, vs_]
            ms = jnp.mean(o_h * o_h, axis=-1, keepdims=True)
            rg = r_gate[:, vs_]
            o_ref[0, rs, vs_] = (o_h * lax.rsqrt(ms + NORM_EPS) * ng_ref[...] * (rg * jax.nn.sigmoid(rg))).astype(o_ref.dtype)


def _gla(z3, w_gate, b_gate, norm_g, *, n_chunks):
    b, s, _ = z3.shape
    c = GLA_CHUNK * n_chunks
    nk = GLA_HEADS * GLA_HEAD_DK
    nv = GLA_HEADS * GLA_HEAD_DV
    wg = jnp.zeros((LANES, nk), F32).at[SMALL_GLOW_LANE:SMALL_GLOW_LANE + GLA_RANK].set(w_gate.astype(F32))
    wg_hi = wg.astype(BF16)
    wg = jnp.stack([wg_hi, (wg - wg_hi.astype(F32)).astype(BF16)])
    idx = np.arange(c)
    tril = jnp.asarray(((idx[:, None] >= idx[None, :])
                        & (idx[:, None] // GLA_CHUNK == idx[None, :] // GLA_CHUNK)).astype(np.float32)).astype(BF16)
    const = lambda shape: pl.BlockSpec(shape, lambda bi, i: (0,) * len(shape))
    return pl.pallas_call(
        functools.partial(_gla_kernel, n_chunks=n_chunks),
        grid=(b, s // c),
        in_specs=[
            pl.BlockSpec((1, c, nk), lambda bi, i: (bi, i, COL_GQ // nk)),
            pl.BlockSpec((1, c, nk), lambda bi, i: (bi, i, COL_GK // nk)),
            pl.BlockSpec((1, c, nv), lambda bi, i: (bi, i, COL_GV // nv)),
            pl.BlockSpec((1, c, nv), lambda bi, i: (bi, i, COL_GR // nv)),
            pl.BlockSpec((1, c, LANES), lambda bi, i: (bi, i, COL_SMALL // LANES)),
            const((2, LANES, nk)), const((1, nk)), const((1, GLA_HEAD_DV)), const((c, c)),
        ],
        out_specs=pl.BlockSpec((1, c, nv), lambda bi, i: (bi, i, 0)),
        out_shape=jax.ShapeDtypeStruct((b, s, nv), BF16),
        scratch_shapes=[pltpu.VMEM((nv, nk), F32), pltpu.VMEM((n_chunks, nv, nk), F32),
                        pltpu.VMEM((c, nv), F32), pltpu.VMEM((c, nk), BF16)],
        compiler_params=_cparams("parallel", "arbitrary"),
        name="gla",
    )(z3, z3, z3, z3, z3, wg, b_gate.reshape(1, nk).astype(F32), norm_g.reshape(1, GLA_HEAD_DV).astype(F32), tril)


def _mem_attn_kernel(q_ref, k_ref, v_ref, qg_ref, kg_ref, o_ref):
    dh = MEM_HEAD_DIM
    nt = (((1,), (1,)), ((), ()))
    for h in range(MEM_HEADS):
        sl = slice(h * dh, (h + 1) * dh)
        q = q_ref[0, :, sl].astype(F32)
        q = q * lax.rsqrt(jnp.mean(q * q, axis=-1, keepdims=True) + NORM_EPS) * qg_ref[...] * (dh ** -0.5)
        k = k_ref[0, :, sl].astype(F32)
        k = k * lax.rsqrt(jnp.mean(k * k, axis=-1, keepdims=True) + NORM_EPS) * kg_ref[...]
        s = lax.dot_general(q.astype(BF16), k.astype(BF16), nt, preferred_element_type=F32)
        m = jnp.max(s, axis=-1, keepdims=True)
        p = jnp.exp(s - m)
        p = p / jnp.sum(p, axis=-1, keepdims=True)
        o = jnp.dot(p.astype(BF16), v_ref[0, :, sl].astype(BF16), preferred_element_type=F32)
        o_ref[0, :, sl] = o.astype(o_ref.dtype)


def _mem_attention(z3, kv, q_norm, k_norm, *, tq):
    b, s, _ = z3.shape
    m = kv.shape[1]
    w = MEM_HEADS * MEM_HEAD_DIM
    const = lambda shape: pl.BlockSpec(shape, lambda bi, i: (0,) * len(shape))
    return pl.pallas_call(
        _mem_attn_kernel,
        grid=(b, s // tq),
        in_specs=[
            pl.BlockSpec((1, tq, w), lambda bi, i: (bi, i, COL_MQ // w)),
            pl.BlockSpec((1, m, w), lambda bi, i: (bi, 0, 0)),
            pl.BlockSpec((1, m, w), lambda bi, i: (bi, 0, 1)),
            const((1, MEM_HEAD_DIM)), const((1, MEM_HEAD_DIM)),
        ],
        out_specs=pl.BlockSpec((1, tq, w), lambda bi, i: (bi, i, 0)),
        out_shape=jax.ShapeDtypeStruct((b, s, w), BF16),
        compiler_params=_cparams("parallel", "parallel"),
        name="mem_attention",
    )(z3, kv, kv, q_norm.reshape(1, MEM_HEAD_DIM).astype(F32), k_norm.reshape(1, MEM_HEAD_DIM).astype(F32))


def _merge_kernel(x_ref, on_ref, og_ref, om_ref, m0_ref, m1_ref, m2_ref, bm_ref, wb_ref, wo_ref, o_ref):
    merged = None
    for br, (ref, mg_ref) in enumerate(((on_ref, m0_ref), (og_ref, m1_ref), (om_ref, m2_ref))):
        y = jnp.dot(ref[...], wb_ref[br], preferred_element_type=F32)
        gate = jax.nn.sigmoid(mg_ref[...].astype(F32) + bm_ref[br:br + 1, :])
        merged = gate * y if merged is None else merged + gate * y
    o_ref[...] = x_ref[...] + jnp.dot(merged.astype(BF16), wo_ref[...], preferred_element_type=F32)


def _merge_out(x2, o_nsa, o_gla, o_mem, z2, b_merge, w_branch, w_out, layer, *, tm):
    t, d = x2.shape
    bw = BRANCH_WIDTH
    row = lambda w: pl.BlockSpec((tm, w), lambda i: (i, 0))
    gate_cols = lambda br: pl.BlockSpec((tm, d), lambda i: (i, COL_MERGE // d + br))
    return pl.pallas_call(
        _merge_kernel,
        grid=(t // tm,),
        in_specs=[
            row(d), row(bw), row(bw), row(bw),
            gate_cols(0), gate_cols(1), gate_cols(2),
            pl.BlockSpec((N_BRANCH, d), lambda i: (0, 0)),
            pl.BlockSpec((None, N_BRANCH, bw, d), lambda i: (layer, 0, 0, 0)),
            pl.BlockSpec((None, d, d), lambda i: (layer, 0, 0)),
        ],
        out_specs=row(d),
        out_shape=jax.ShapeDtypeStruct((t, d), F32),
        compiler_params=_cparams("parallel"),
        name="merge_out",
    )(x2, o_nsa, o_gla, o_mem, z2, z2, z2, b_merge.astype(F32), w_branch, w_out)


def _mlp_kernel(x_ref, g_ref, wu_ref, wd_ref, o_ref, h_ref, acc_ref):
    j = pl.program_id(1)

    @pl.when(j == 0)
    def _():
        h_ref[...] = _rms_rows(x_ref[...], g_ref[...])
        acc_ref[...] = jnp.zeros(acc_ref.shape, F32)

    u = jnp.dot(h_ref[...], wu_ref[...], preferred_element_type=F32)
    u = jnp.square(jnp.maximum(u, 0.0)).astype(BF16)
    acc_ref[...] += jnp.dot(u, wd_ref[...], preferred_element_type=F32)

    @pl.when(j == pl.num_programs(1) - 1)
    def _():
        o_ref[...] = x_ref[...] + acc_ref[...]


def _mlp(x2, g, w_up, w_down, layer, *, tm, th):
    t, d = x2.shape
    hid = w_up.shape[2]
    return pl.pallas_call(
        _mlp_kernel,
        grid=(t // tm, hid // th),
        in_specs=[
            pl.BlockSpec((tm, d), lambda i, j: (i, 0)),
            pl.BlockSpec((1, d), lambda i, j: (0, 0)),
            pl.BlockSpec((None, d, th), lambda i, j: (layer, 0, j)),
            pl.BlockSpec((None, th, d), lambda i, j: (layer, j, 0)),
        ],
        out_specs=pl.BlockSpec((tm, d), lambda i, j: (i, 0)),
        out_shape=jax.ShapeDtypeStruct((t, d), F32),
        scratch_shapes=[pltpu.VMEM((tm, d), BF16), pltpu.VMEM((tm, d), F32)],
        compiler_params=_cparams("parallel", "arbitrary"),
        name="mlp",
    )(x2, g.reshape(1, d).astype(F32), w_up, w_down)


IN_PROJ_ROWS, IN_PROJ_COL_TILES = 1024, 2
NSA_PREP_ROWS = 256
NSA_QUERY_TILE, NSA_SELECT_PARTS, NSA_KEY_TILE = 256, 2, 512
GLA_CHUNKS_PER_STEP = 8
MEM_KV_ROWS, MEM_KV_COLS, MEM_QUERY_TILE = 512, 512, 512
MERGE_ROWS = 512
MLP_ROWS, MLP_HIDDEN_TILE = 1024, 1024


def _tile(n, pref):
    t = min(n, pref)
    assert n % t == 0, (n, pref)
    return t


def _layer(x, mem2, pos_rows, pos_cmp, p, big, layer):
    b, s, d = x.shape
    t = b * s
    hd = NSA_HEAD_DIM
    x2 = x.reshape(t, d)

    z2 = _norm_matmul(x2, p["ln_mix"].astype(F32), big["w_in"], layer, tm=_tile(t, IN_PROJ_ROWS),
                      tn=D_IN_PAD // IN_PROJ_COL_TILES, out_dtype=BF16, name="in_proj")
    z3 = z2.reshape(b, s, D_IN_PAD)

    q, ks, vs, kw, vw = _nsa_prep(z3, pos_rows, p["nsa_q_norm"].astype(F32), p["nsa_k_norm"][1].astype(F32),
                                  p["nsa_k_norm"][2].astype(F32), tp=_tile(s, NSA_PREP_ROWS))
    pe2 = jnp.pad(p["cmp_pe"].reshape(2, 2, CMP_STRIDE * hd), ((0, 0), (0, 6), (0, 0))).astype(F32)
    cmp_kv = _nsa_compress(z3, p["cmp_w1"], p["cmp_w2"], pe2, pos_cmp, p["nsa_k_norm"][0].astype(F32))
    shifts = _nsa_shifts(p["nsa_q_norm"].astype(F32), p["nsa_k_norm"].astype(F32))
    tq = _tile(s, NSA_QUERY_TILE)
    o_cmp, sel = _nsa_select(shifts, q, cmp_kv, tq=tq, parts=NSA_SELECT_PARTS)
    o_nsa = _nsa_attend(shifts, q, ks, vs, kw, vw, z3, o_cmp, sel, tq=tq, tk=_tile(s, NSA_KEY_TILE))

    o_gla = _gla(z3, p["gla_w_gate"], p["gla_b_gate"], p["gla_norm"], n_chunks=GLA_CHUNKS_PER_STEP)

    kv = _norm_matmul(mem2, p["mem_norm"].astype(F32), big["mem_w_kv"], layer,
                      tm=_tile(mem2.shape[0], MEM_KV_ROWS), tn=MEM_KV_COLS, out_dtype=BF16, name="mem_kv")
    kv = kv.reshape(b, mem2.shape[0] // b, 2 * MEM_HEADS * MEM_HEAD_DIM)
    o_mem = _mem_attention(z3, kv, p["mem_q_norm"], p["mem_k_norm"], tq=_tile(s, MEM_QUERY_TILE))

    x2 = _merge_out(x2, o_nsa.reshape(t, -1), o_gla.reshape(t, -1), o_mem.reshape(t, -1), z2,
                    p["b_merge"], big["w_branch"], big["w_out"], layer, tm=_tile(t, MERGE_ROWS))
    x2 = _mlp(x2, p["ln_mlp"], big["w_up"], big["w_down"], layer, tm=_tile(t, MLP_ROWS), th=MLP_HIDDEN_TILE)
    return x2.reshape(b, s, d)


def kernel(x, mem, positions, ln_mix, w_in, b_merge, nsa_q_norm, nsa_k_norm, cmp_pe, cmp_w1, cmp_w2,
           gla_w_gate, gla_b_gate, gla_norm, mem_norm, mem_w_kv, mem_q_norm, mem_k_norm, w_branch, w_out,
           ln_mlp, w_up, w_down):
    b, s, d = x.shape
    assert d == 1024 and s % WINDOW == 0 and s >= 2 * WINDOW
    depth = w_in.shape[0]
    perm_np, d_in = _in_proj_permutation(d)
    assert w_in.shape[2] == d_in
    w_in = _relayout_in_proj(w_in.astype(BF16), [int(c) for c in perm_np], d_in)
    half = CMP_STRIDE * NSA_HEAD_DIM
    cmp_w1 = jnp.concatenate([cmp_w1[:, :, :half], cmp_w1[:, :, half:]], axis=-1).astype(BF16)
    cmp_w2, mem_w_kv, w_branch, w_out, w_up, w_down = (
        a.astype(BF16) for a in (cmp_w2, mem_w_kv, w_branch, w_out, w_up, w_down))
    pos3 = positions.astype(jnp.int32).reshape(b, s, 1)
    nc = s // CMP_STRIDE
    cmp_end = np.minimum(np.arange(nc) * CMP_STRIDE + CMP_BLOCK - 1, s - 1)
    pos_cmp = pos3[:, cmp_end, :]
    pos_rows = positions.astype(jnp.int32).reshape(b, 1, s)
    mem2 = mem.reshape(b * mem.shape[1], d)
    names = ("ln_mix", "b_merge", "nsa_q_norm", "nsa_k_norm", "cmp_pe", "cmp_w1", "cmp_w2",
             "gla_w_gate", "gla_b_gate", "gla_norm", "mem_norm", "mem_q_norm", "mem_k_norm", "ln_mlp")
    stacked = (ln_mix, b_merge, nsa_q_norm, nsa_k_norm, cmp_pe, cmp_w1, cmp_w2, gla_w_gate, gla_b_gate,
               gla_norm, mem_norm, mem_q_norm, mem_k_norm, ln_mlp)
    big = dict(w_in=w_in, mem_w_kv=mem_w_kv, w_branch=w_branch, w_out=w_out, w_up=w_up, w_down=w_down)
    for l in range(depth):
        x = _layer(x, mem2, pos_rows, pos_cmp, {n: a[l] for n, a in zip(names, stacked)}, big, l)
    return x
```

```python
import functools

import numpy as np
import jax
import jax.numpy as jnp
from jax import lax
from jax.experimental import pallas as pl
from jax.experimental.pallas import tpu as pltpu

NSA_HEADS = 8
NSA_GROUPS = 2
NSA_REP = NSA_HEADS // NSA_GROUPS
NSA_HEAD_DIM = 64
CMP_BLOCK = 32
CMP_STRIDE = 16
CMP_HIDDEN = 4 * NSA_HEAD_DIM
SEL_BLOCK = 64
N_SEL = 16
WINDOW = 512
FORCE_SCORE = 1e4
GLA_HEADS = 4
GLA_HEAD_DK = 64
GLA_HEAD_DV = 128
GLA_RANK = 16
GLA_TAU = 16.0
GLA_CHUNK = 64
GLA_SUB = 16
MEM_HEADS = 4
MEM_HEAD_DIM = 128
N_BRANCH = 3
BRANCH_WIDTH = 512
ROPE_THETA = 500000.0
ROPE_ROT = NSA_HEAD_DIM // 4
ROPE_HALF = ROPE_ROT // 2
NORM_EPS = 1e-6

LANES = 128
VMEM_LIMIT_BYTES = 48 * 1024 * 1024

F32 = jnp.float32
BF16 = jnp.bfloat16
HIGHEST = lax.Precision.HIGHEST
NEG_BIG = -1e30
MAX_CONST_SHIFT = 40.0

COL_NQ = 0
COL_GV = 512
COL_GR = 1024
COL_MQ = 1536
COL_MERGE = 2048
COL_GQ = 5120
COL_GK = 5376
COL_KV = 5632
COL_SMALL = 6400
D_IN_PAD = 6656
SMALL_GLOW_LANE = 12


def _in_proj_permutation(d_model):
    sizes = (512, 128, 128, 128, 128, 128, 128, 24, 256, 256, 512, 512, 16, 512, 3 * d_model)
    off = np.concatenate([[0], np.cumsum(sizes)])
    (o_nq, o_kc, o_vc, o_ks, o_vs, o_kw, o_vw, o_ng, o_gq, o_gk, o_gv, o_gr, o_gl, o_mq, o_mg) = off[:-1]
    d_in = int(off[-1])
    perm = np.full((D_IN_PAD,), d_in, np.int32)

    def put(new, old, n):
        perm[new:new + n] = np.arange(old, old + n)

    put(COL_NQ, o_nq, 512)
    put(COL_GV, o_gv, 512)
    put(COL_GR, o_gr, 512)
    put(COL_MQ, o_mq, 512)
    put(COL_MERGE, o_mg, 3 * d_model)
    put(COL_GQ, o_gq, 256)
    put(COL_GK, o_gk, 256)
    put(COL_KV, o_kc, 768)
    for g in range(NSA_GROUPS):
        for br in range(3):
            for r in range(NSA_REP):
                perm[COL_SMALL + g * LANES + br * NSA_REP + r] = o_ng + (g * NSA_REP + r) * 3 + br
    put(COL_SMALL + SMALL_GLOW_LANE, o_gl, GLA_RANK)
    return perm, d_in


def _relayout_in_proj(w, perm, d_in):
    pieces, start = [], 0
    for i in range(1, len(perm) + 1):
        prev = perm[i - 1]
        if i < len(perm) and (perm[i] == prev + 1 if prev != d_in else perm[i] == d_in):
            continue
        n = i - start
        if prev == d_in:
            pieces.append(jnp.zeros(w.shape[:-1] + (n,), w.dtype))
        else:
            pieces.append(w[..., perm[start]:perm[start] + n])
        start = i
    return jnp.concatenate(pieces, axis=-1)


def _cparams(*sem):
    return pltpu.CompilerParams(dimension_semantics=sem, vmem_limit_bytes=VMEM_LIMIT_BYTES)


def _rms_rows(x, g):
    ms = jnp.mean(x * x, axis=-1, keepdims=True)
    return (x * lax.rsqrt(ms + NORM_EPS) * g).astype(BF16)


def _norm_matmul_kernel(x_ref, g_ref, w_ref, o_ref):
    h = _rms_rows(x_ref[...].astype(F32), g_ref[...])
    o_ref[...] = jnp.dot(h, w_ref[...], preferred_element_type=F32).astype(o_ref.dtype)


def _norm_matmul(x, g, w, layer, *, tm, tn, out_dtype, name):
    m, k = x.shape
    n = w.shape[2]
    return pl.pallas_call(
        _norm_matmul_kernel,
        grid=(n // tn, m // tm),
        in_specs=[
            pl.BlockSpec((tm, k), lambda j, i: (i, 0)),
            pl.BlockSpec((1, k), lambda j, i: (0, 0)),
            pl.BlockSpec((None, k, tn), lambda j, i: (layer, 0, j)),
        ],
        out_specs=pl.BlockSpec((tm, tn), lambda j, i: (i, j)),
        out_shape=jax.ShapeDtypeStruct((m, n), out_dtype),
        compiler_params=_cparams("parallel", "parallel"),
        name=name,
    )(x, g.reshape(1, k), w)


def _rope_tables(pos_row, freq_col, place):
    ang = freq_col * pos_row
    tn = (((0,), (0,)), ((), ()))
    lane = lax.broadcasted_iota(jnp.int32, (1, LANES), 1)
    c = lax.dot_general(jnp.cos(ang), place, tn, preferred_element_type=F32, precision=HIGHEST)
    s = lax.dot_general(jnp.sin(ang), place, tn, preferred_element_type=F32, precision=HIGHEST)
    return c + jnp.where((lane & (NSA_HEAD_DIM - 1)) >= ROPE_ROT, 1.0, 0.0), s


def _rope_lanes(x, c, s):
    n = x.shape[-1]
    if n > LANES:
        c = jnp.concatenate([c] * (n // LANES), axis=1)
        s = jnp.concatenate([s] * (n // LANES), axis=1)
    lane = lax.broadcasted_iota(jnp.int32, (1, n), 1) & (NSA_HEAD_DIM - 1)
    up = pltpu.roll(x, n - ROPE_HALF, axis=1)
    dn = pltpu.roll(x, ROPE_HALF, axis=1)
    y = jnp.where(lane < ROPE_HALF, -up, jnp.where(lane < ROPE_ROT, dn, 0.0))
    return x * c + y * s


def _head_rms(x, bd, g):
    x2 = x * x
    hi = x2.astype(BF16)
    lo = (x2 - hi.astype(F32)).astype(BF16)
    ms = (jnp.dot(hi, bd, preferred_element_type=F32) + jnp.dot(lo, bd, preferred_element_type=F32))
    return x * lax.rsqrt(ms * (1.0 / NSA_HEAD_DIM) + NORM_EPS) * g


def _nsa_prep_kernel(q_ref, ks_ref, vs_ref, kw_ref, vw_ref, pos_ref, qg_ref, ksg_ref, kwg_ref,
                     bdq_ref, bdk_ref, f_ref, place_ref,
                     qo_ref, kso_ref, vso_ref, kwo_ref, vwo_ref):
    hd = NSA_HEAD_DIM
    tp = q_ref.shape[1]
    c, s = _rope_tables(pos_ref[0].astype(F32), f_ref[...], place_ref[...])
    q = _head_rms(q_ref[0].astype(F32), bdq_ref[...], qg_ref[...])
    q = (_rope_lanes(q, c, s) * (hd ** -0.5)).astype(qo_ref.dtype)
    for h in range(NSA_HEADS):
        qo_ref[0, h] = q[:, h * hd:(h + 1) * hd]
    ks = _rope_lanes(_head_rms(ks_ref[0].astype(F32), bdk_ref[...], ksg_ref[...]), c, s)
    kw = _rope_lanes(_head_rms(kw_ref[0].astype(F32), bdk_ref[...], kwg_ref[...]), c, s)
    vs = vs_ref[0]
    vw = vw_ref[0]
    tok = pl.program_id(1) * tp + lax.broadcasted_iota(jnp.int32, (tp, hd), 0)
    lane = lax.broadcasted_iota(jnp.int32, (tp, hd), 1)
    blk_onehot = jnp.where((tok >> 6) == lane, 1.0, 0.0).astype(kso_ref.dtype)
    ones_col = jnp.ones((tp, hd), vso_ref.dtype)
    for g in range(NSA_GROUPS):
        sl = slice(g * hd, (g + 1) * hd)
        kso_ref[0, g, :, 0:hd] = ks[:, sl].astype(kso_ref.dtype)
        kso_ref[0, g, :, hd:2 * hd] = blk_onehot
        kwo_ref[0, g] = kw[:, sl].astype(kwo_ref.dtype)
        vso_ref[0, g, :, 0:hd] = vs[:, sl].astype(vso_ref.dtype)
        vso_ref[0, g, :, hd:2 * hd] = ones_col
        vwo_ref[0, g, :, 0:hd] = vw[:, sl].astype(vwo_ref.dtype)
        vwo_ref[0, g, :, hd:2 * hd] = ones_col


def _block_diag_ones(n, width):
    i = np.arange(n)
    return jnp.asarray((i[:, None] // width == i[None, :] // width).astype(np.float32)).astype(BF16)


def _rope_freq_col():
    inv = ROPE_THETA ** (-(np.arange(ROPE_HALF, dtype=np.float64)) / ROPE_HALF)
    return jnp.asarray(inv.astype(np.float32)).reshape(ROPE_HALF, 1)


def _rope_placement():
    m = np.zeros((ROPE_HALF, LANES), np.float32)
    for f in range(ROPE_HALF):
        for base in range(0, LANES, NSA_HEAD_DIM):
            m[f, base + f] = 1.0
            m[f, base + f + ROPE_HALF] = 1.0
    return jnp.asarray(m)


def _nsa_prep(z3, pos_rows, q_norm, ks_norm, kw_norm, *, tp):
    b, s, _ = z3.shape
    hd, g = NSA_HEAD_DIM, NSA_GROUPS
    assert SEL_BLOCK == 64 and s // SEL_BLOCK <= hd
    kvb = COL_KV // LANES
    qg = jnp.tile(q_norm, NSA_HEADS).reshape(1, NSA_HEADS * hd)
    ksg = jnp.tile(ks_norm, g).reshape(1, g * hd)
    kwg = jnp.tile(kw_norm, g).reshape(1, g * hd)
    const = lambda shape: pl.BlockSpec(shape, lambda bi, i: (0,) * len(shape))
    plain_out = jax.ShapeDtypeStruct((b, g, s, hd), BF16)
    plain_spec = pl.BlockSpec((1, g, tp, hd), lambda bi, i: (bi, 0, i, 0))
    aug_out = jax.ShapeDtypeStruct((b, g, s, 2 * hd), BF16)
    aug_spec = pl.BlockSpec((1, g, tp, 2 * hd), lambda bi, i: (bi, 0, i, 0))
    return pl.pallas_call(
        _nsa_prep_kernel,
        grid=(b, s // tp),
        in_specs=[
            pl.BlockSpec((1, tp, 512), lambda bi, i: (bi, i, COL_NQ // 512)),
            pl.BlockSpec((1, tp, LANES), lambda bi, i: (bi, i, kvb + 2)),
            pl.BlockSpec((1, tp, LANES), lambda bi, i: (bi, i, kvb + 3)),
            pl.BlockSpec((1, tp, LANES), lambda bi, i: (bi, i, kvb + 4)),
            pl.BlockSpec((1, tp, LANES), lambda bi, i: (bi, i, kvb + 5)),
            pl.BlockSpec((1, 1, tp), lambda bi, i: (bi, 0, i)),
            const((1, 512)), const((1, LANES)), const((1, LANES)),
            const((512, 512)), const((LANES, LANES)),
            const((ROPE_HALF, 1)), const((ROPE_HALF, LANES)),
        ],
        out_specs=[
            pl.BlockSpec((1, NSA_HEADS, tp, hd), lambda bi, i: (bi, 0, i, 0)),
            aug_spec, aug_spec, plain_spec, aug_spec,
        ],
        out_shape=[jax.ShapeDtypeStruct((b, NSA_HEADS, s, hd), BF16), aug_out, aug_out, plain_out, aug_out],
        compiler_params=_cparams("parallel", "parallel"),
        name="nsa_prep",
    )(z3, z3, z3, z3, z3, pos_rows, qg, ksg, kwg,
      _block_diag_ones(512, hd), _block_diag_ones(LANES, hd), _rope_freq_col(), _rope_placement())


def _gelu_tanh(x):
    return 0.5 * x * (1.0 + jnp.tanh(np.sqrt(2.0 / np.pi) * (x + 0.044715 * x * x * x)))


def _nsa_cmp_kernel(x_ref, w1_ref, w2_ref, pe_ref, pos_ref, g_ref, rot_ref, f_ref, o_ref, xs_ref):
    kind = pl.program_id(0)
    hd = NSA_HEAD_DIM
    nc = x_ref.shape[1] // CMP_STRIDE
    xs_ref[...] = x_ref[0].astype(F32)
    r = jnp.dot(pe_ref[0].astype(BF16), w1_ref[0], preferred_element_type=F32)
    ab = [jnp.zeros((nc, 2 * CMP_HIDDEN), F32) for _ in range(NSA_GROUPS)]
    for p in range(CMP_STRIDE):
        xp = xs_ref[pl.ds(p, nc, stride=CMP_STRIDE), :].astype(BF16)
        w1p = w1_ref[0, p * hd:(p + 1) * hd, :]
        for g in range(NSA_GROUPS):
            ab[g] = ab[g] + jnp.dot(xp[:, g * hd:(g + 1) * hd], w1p, preferred_element_type=F32)

    for g in range(NSA_GROUPS):
        a = ab[g][:, :CMP_HIDDEN] + r[0:1, :CMP_HIDDEN]
        bm = ab[g][:, CMP_HIDDEN:] + r[1:2, CMP_HIDDEN:]
        hid = _gelu_tanh(a + pltpu.roll(bm, nc - 1, axis=0))
        comp = jnp.dot(hid.astype(BF16), w2_ref[0], preferred_element_type=F32)

        @pl.when(kind == 0)
        def _():
            ms = jnp.mean(comp * comp, axis=-1, keepdims=True)
            kn = comp * lax.rsqrt(ms + NORM_EPS) * g_ref[...]
            ang = pos_ref[0].astype(F32) * f_ref[...]
            y = jnp.dot(kn, rot_ref[...], preferred_element_type=F32, precision=HIGHEST)
            o_ref[0, 0, g, :, 0:hd] = (kn * jnp.cos(ang) + y * jnp.sin(ang)).astype(o_ref.dtype)
            o_ref[0, 0, g, :, hd:2 * hd] = jnp.zeros((nc, hd), o_ref.dtype)

        @pl.when(kind != 0)
        def _():
            o_ref[0, 0, g, :, 0:hd] = comp.astype(o_ref.dtype)
            o_ref[0, 0, g, :, hd:2 * hd] = jnp.ones((nc, hd), o_ref.dtype)


def _rope_freq_head():
    p = np.arange(NSA_HEAD_DIM)
    inv = ROPE_THETA ** (-(np.arange(ROPE_HALF, dtype=np.float64)) / ROPE_HALF)
    f = np.where(p < ROPE_ROT, inv[p % ROPE_HALF], 0.0)
    return jnp.asarray(f.astype(np.float32)).reshape(1, NSA_HEAD_DIM)


def _rope_rot_matrix(n):
    m = np.zeros((n, n), np.float32)
    for base in range(0, n, NSA_HEAD_DIM):
        for l in range(ROPE_HALF):
            m[base + l + ROPE_HALF, base + l] = -1.0
            m[base + l, base + l + ROPE_HALF] = 1.0
    return jnp.asarray(m)


def _nsa_compress(z3, w1cat, w2, pe2, pos_cmp, k_norm0):
    b, s, _ = z3.shape
    hd, g = NSA_HEAD_DIM, NSA_GROUPS
    nc = s // CMP_STRIDE
    assert g * hd == LANES
    const = lambda shape: pl.BlockSpec(shape, lambda k, bi: (0,) * len(shape))
    return pl.pallas_call(
        _nsa_cmp_kernel,
        grid=(2, b),
        in_specs=[
            pl.BlockSpec((1, s, LANES), lambda k, bi: (bi, 0, COL_KV // LANES + k)),
            pl.BlockSpec((1, CMP_STRIDE * hd, 2 * CMP_HIDDEN), lambda k, bi: (k, 0, 0)),
            pl.BlockSpec((1, CMP_HIDDEN, hd), lambda k, bi: (k, 0, 0)),
            pl.BlockSpec((1, 8, CMP_STRIDE * hd), lambda k, bi: (k, 0, 0)),
            pl.BlockSpec((1, nc, 1), lambda k, bi: (bi, 0, 0)),
            const((1, hd)), const((hd, hd)), const((1, hd)),
        ],
        out_specs=pl.BlockSpec((1, 1, g, nc, 2 * hd), lambda k, bi: (k, bi, 0, 0, 0)),
        out_shape=jax.ShapeDtypeStruct((2, b, g, nc, 2 * hd), BF16),
        scratch_shapes=[pltpu.VMEM((s, LANES), F32)],
        compiler_params=_cparams("parallel", "parallel"),
        name="nsa_compress",
    )(z3, w1cat, w2, pe2, pos_cmp, k_norm0.reshape(1, hd), _rope_rot_matrix(hd), _rope_freq_head())


def _softmax_rows(s, mask):
    s = jnp.where(mask, s, NEG_BIG)
    m = jnp.max(s, axis=-1, keepdims=True)
    p = jnp.where(mask, jnp.exp(s - m), 0.0)
    d = jnp.sum(p, axis=-1, keepdims=True)
    return p / jnp.where(d > 0, d, 1.0)


_NT = (((1,), (1,)), ((), ()))


def _select_blocks(p_sum, ov_ref, t0, tq, n_sel):
    n_blk = ov_ref.shape[0]
    p_hi = p_sum.astype(BF16)
    p_lo = (p_sum - p_hi.astype(F32)).astype(BF16)
    imp_t = (lax.dot_general(ov_ref[...], p_hi, _NT, preferred_element_type=F32)
             + lax.dot_general(ov_ref[...], p_lo, _NT, preferred_element_type=F32))
    j_col = lax.broadcasted_iota(jnp.int32, (n_blk, 1), 0)
    t_lane = t0 + lax.broadcasted_iota(jnp.int32, (1, tq), 1)
    causal = j_col * SEL_BLOCK <= t_lane
    cur = t_lane >> 6
    forced = causal & ((j_col == 0) | (j_col == cur) | (j_col == cur - 1))
    score = jnp.where(forced, FORCE_SCORE, jnp.where(causal, imp_t, -FORCE_SCORE))
    ng = n_blk // 8
    groups = [score[8 * v:8 * v + 8] for v in range(ng)]
    ranks = [jnp.zeros((8, tq), F32) for _ in range(ng)]
    sub = lax.broadcasted_iota(jnp.int32, (8, tq), 0)
    for jp in range(n_blk):
        row = jnp.broadcast_to(score[jp:jp + 1, :], (8, tq))
        vj = jp // 8
        for v in range(ng):
            if v < vj:
                ahead = row > groups[v]
            elif v > vj:
                ahead = row >= groups[v]
            else:
                ahead = (row > groups[v]) | ((row == groups[v]) & (sub > jp % 8))
            ranks[v] = ranks[v] + jnp.where(ahead, 1.0, 0.0)
    sel_t = jnp.where(jnp.concatenate(ranks, axis=0) < n_sel, 1.0, 0.0).astype(BF16)
    eye = (lax.broadcasted_iota(jnp.int32, (tq, tq), 0) == lax.broadcasted_iota(jnp.int32, (tq, tq), 1)).astype(BF16)
    return lax.dot_general(eye, sel_t, _NT, preferred_element_type=F32)


def _sum_heads(p, tq):
    out = p[0:tq]
    for r in range(1, NSA_REP):
        out = out + p[r * tq:(r + 1) * tq]
    return out


def _store_heads(o_ref, o, tq, row0):
    hd = NSA_HEAD_DIM
    for r in range(NSA_REP):
        o_ref[0, row0:row0 + tq, r * hd:(r + 1) * hd] = o[r * tq:(r + 1) * tq].astype(o_ref.dtype)


def _store_selection(sel_ref, sel, row0):
    tq, n_blk = sel.shape
    sel = sel.astype(sel_ref.dtype)
    if n_blk < NSA_HEAD_DIM:
        sel = jnp.concatenate([sel, jnp.zeros((tq, NSA_HEAD_DIM - n_blk), sel_ref.dtype)], axis=1)
    sel_ref[0, 0, row0:row0 + tq] = sel


def _gate_maps(gl_ref, gp_ref, gm_ref):
    gates = jax.nn.sigmoid(gl_ref[0].astype(F32))
    g_hi = gates.astype(BF16)
    g_lo = (gates - g_hi.astype(F32)).astype(BF16)
    gm_ref[...] = (jnp.dot(g_hi, gp_ref[...], preferred_element_type=F32)
                   + jnp.dot(g_lo, gp_ref[...], preferred_element_type=F32))


def _to_lane_layout(lay_ref, idx, o, tq, lane0=0):
    hd = NSA_HEAD_DIM
    for r in range(NSA_REP):
        lay_ref[idx, :, r * hd:(r + 1) * hd] = o[r * tq:(r + 1) * tq, lane0:lane0 + hd]


def _gated_store(gm_ref, oc_ref, lay_ref, o_ref, with_sums):
    w = NSA_REP * NSA_HEAD_DIM
    o_sel, o_win = lay_ref[0], lay_ref[1]
    if with_sums:
        o_sel, o_win = o_sel / lay_ref[2], o_win / lay_ref[3]
    out = gm_ref[:, 0:w] * oc_ref[0].astype(F32) + gm_ref[:, w:2 * w] * o_sel + gm_ref[:, 2 * w:3 * w] * o_win
    o_ref[0] = out.astype(o_ref.dtype)


def _gate_placement():
    m = np.zeros((LANES, 3 * NSA_REP * NSA_HEAD_DIM), np.float32)
    for br in range(3):
        for r in range(NSA_REP):
            c0 = br * NSA_REP * NSA_HEAD_DIM + r * NSA_HEAD_DIM
            m[br * NSA_REP + r, c0:c0 + NSA_HEAD_DIM] = 1.0
    return jnp.asarray(m).astype(BF16)


def _sel_fast(sh_ref, q_ref, kc_ref, vc_ref, ov_ref, oc_ref, sel_ref, *, tq, n_sel):
    rep, hd = NSA_REP, NSA_HEAD_DIM
    rows = rep * tq
    kc = kc_ref[0, 0, 0][:, 0:hd]
    ncp = kc.shape[0]
    n_idx = lax.broadcasted_iota(jnp.int32, (1, ncp), 1)
    for part in range(q_ref.shape[2] // tq):
        row0 = part * tq
        t0 = pl.program_id(2) * q_ref.shape[2] + row0
        q4 = q_ref[0, :, row0:row0 + tq, :].reshape(rows, hd)
        t_q = t0 + lax.broadcasted_iota(jnp.int32, (tq, 1), 0)
        valid_c = (n_idx * CMP_STRIDE + (CMP_BLOCK - 1) <= t_q) & (n_idx < ncp - 1)
        bias_c = jnp.where(valid_c, sh_ref[0], NEG_BIG)
        s_c = lax.dot_general(q4, kc, _NT, preferred_element_type=F32).reshape(rep, tq, ncp)
        p_c = jnp.exp(s_c + bias_c[None]).reshape(rows, ncp)
        ol_c = jnp.dot(p_c.astype(BF16), vc_ref[0, 0, 0], preferred_element_type=F32)
        inv_c = 1.0 / jnp.where(ol_c[:, hd:hd + 1] > 0, ol_c[:, hd:hd + 1], 1.0)
        _store_heads(oc_ref, ol_c[:, 0:hd] * inv_c, tq, row0)
        _store_selection(sel_ref, _select_blocks(_sum_heads(p_c * inv_c, tq), ov_ref, t0, tq, n_sel), row0)


def _sel_slow(q_ref, kc_ref, vc_ref, ov_ref, oc_ref, sel_ref, *, tq, n_sel):
    rep, hd = NSA_REP, NSA_HEAD_DIM
    rows = rep * tq
    kc = kc_ref[0, 0, 0][:, 0:hd]
    ncp = kc.shape[0]
    n_idx = lax.broadcasted_iota(jnp.int32, (1, ncp), 1)
    for part in range(q_ref.shape[2] // tq):
        row0 = part * tq
        t0 = pl.program_id(2) * q_ref.shape[2] + row0
        q4 = q_ref[0, :, row0:row0 + tq, :].reshape(rows, hd)
        t_row = t0 + (lax.broadcasted_iota(jnp.int32, (rows, 1), 0) & (tq - 1))
        s_c = lax.dot_general(q4, kc, _NT, preferred_element_type=F32)
        p_c = _softmax_rows(s_c, (n_idx * CMP_STRIDE + (CMP_BLOCK - 1) <= t_row) & (n_idx < ncp - 1))
        o_c = jnp.dot(p_c.astype(BF16), vc_ref[0, 0, 0], preferred_element_type=F32)[:, 0:hd]
        _store_heads(oc_ref, o_c, tq, row0)
        _store_selection(sel_ref, _select_blocks(_sum_heads(p_c, tq), ov_ref, t0, tq, n_sel), row0)


def _nsa_select_kernel(sh_ref, q_ref, kc_ref, vc_ref, ov_ref, oc_ref, sel_ref, *, tq, n_sel):
    data = (q_ref, kc_ref, vc_ref, ov_ref, oc_ref, sel_ref)

    @pl.when(sh_ref[3] > 0.5)
    def _():
        _sel_fast(sh_ref, *data, tq=tq, n_sel=n_sel)

    @pl.when(sh_ref[3] <= 0.5)
    def _():
        _sel_slow(*data, tq=tq, n_sel=n_sel)


def _att_fast(sh_ref, q_ref, ks_ref, vs_ref, kw_ref, vw_ref, gl_ref, gp_ref, oc_ref, sel_ref, o_ref,
              lhs_ref, acc_ref, pre_ref, gm_ref, lay_ref, *, tq, tk):
    rep, hd = NSA_REP, NSA_HEAD_DIM
    rows = rep * tq
    t0 = pl.program_id(2) * tq
    c_s, c_w = sh_ref[1], sh_ref[2]
    q4 = q_ref[0].reshape(rows, hd)
    t_q = t0 + lax.broadcasted_iota(jnp.int32, (tq, 1), 0)

    j_lane = lax.broadcasted_iota(jnp.int32, (1, hd), 1)
    shift = jnp.where((sel_ref[0, 0].astype(F32) > 0.5) & (j_lane * SEL_BLOCK < t0), c_s, NEG_BIG).astype(BF16)
    lhs_ref[:, 0:hd] = q4
    for r in range(rep):
        lhs_ref[r * tq:(r + 1) * tq, hd:2 * hd] = shift

    span = WINDOW + tq
    w0 = pl.multiple_of(jnp.maximum(t0 - WINDOW, 0), tq)
    kp = w0 + lax.broadcasted_iota(jnp.int32, (1, span), 1)
    bias_w = jnp.where((kp <= t_q) & (kp > t_q - WINDOW), c_w, NEG_BIG)
    s_w = lax.dot_general(q4, kw_ref[0, 0, pl.ds(w0, span), :], _NT, preferred_element_type=F32)
    p_w = jnp.exp(s_w.reshape(rep, tq, span) + bias_w[None]).reshape(rows, span)
    pre_ref[...] = jnp.dot(p_w.astype(BF16), vw_ref[0, 0, pl.ds(w0, span), :], preferred_element_type=F32)

    d0 = pl.multiple_of(t0, tq)
    kcol = t0 + lax.broadcasted_iota(jnp.int32, (1, tq), 1)
    bias_d = jnp.where(kcol <= t_q, c_s, NEG_BIG)
    s_d = lax.dot_general(q4, ks_ref[0, 0, pl.ds(d0, tq), :][:, 0:hd], _NT, preferred_element_type=F32)
    p_d = jnp.exp(s_d.reshape(rep, tq, tq) + bias_d[None]).reshape(rows, tq)
    acc_ref[...] = jnp.dot(p_d.astype(BF16), vs_ref[0, 0, pl.ds(d0, tq), :], preferred_element_type=F32)

    def key_tile(k0, width):
        s = lax.dot_general(lhs_ref[...], ks_ref[0, 0, pl.ds(k0, width), :], _NT, preferred_element_type=F32)
        acc_ref[...] += jnp.dot(jnp.exp(s).astype(BF16), vs_ref[0, 0, pl.ds(k0, width), :],
                                preferred_element_type=F32)

    wide = 2 * tk
    n_wide = t0 // wide

    def wide_step(c, carry):
        key_tile(pl.multiple_of(c * wide, wide), wide)
        return carry

    def tail_step(c, carry):
        key_tile(pl.multiple_of(n_wide * wide + c * tk, tk), tk)
        return carry

    lax.fori_loop(0, n_wide, wide_step, 0)
    lax.fori_loop(0, (t0 - n_wide * wide + tk - 1) // tk, tail_step, 0)

    for idx, ol in ((0, acc_ref[...]), (1, pre_ref[...])):
        _to_lane_layout(lay_ref, idx, ol, tq)
        _to_lane_layout(lay_ref, 2 + idx, ol, tq, lane0=hd)
    _gate_maps(gl_ref, gp_ref, gm_ref)
    _gated_store(gm_ref, oc_ref, lay_ref, o_ref, with_sums=True)


def _att_slow(q_ref, ks_ref, vs_ref, kw_ref, vw_ref, gl_ref, gp_ref, oc_ref, sel_ref, o_ref,
              acc_ref, gm_ref, lay_ref, m_ref, l_ref, *, tq, tk):
    rep, hd = NSA_REP, NSA_HEAD_DIM
    rows = rep * tq
    t0 = pl.program_id(2) * tq
    q4 = q_ref[0].reshape(rows, hd)
    t_row = t0 + (lax.broadcasted_iota(jnp.int32, (rows, 1), 0) & (tq - 1))
    sel = sel_ref[0, 0]
    n_blk = sel.shape[1]

    m_ref[...] = jnp.full(m_ref.shape, NEG_BIG, F32)
    l_ref[...] = jnp.zeros(l_ref.shape, F32)
    acc_ref[...] = jnp.zeros(acc_ref.shape, F32)
    t_q = t0 + lax.broadcasted_iota(jnp.int32, (tq, 1), 0)

    def kv_step(c, carry):
        k0 = pl.multiple_of(c * tk, tk)
        kt = ks_ref[0, 0, pl.ds(k0, tk), :][:, 0:hd]
        s = lax.dot_general(q4, kt, _NT, preferred_element_type=F32).reshape(rep, tq, tk)
        kk = lax.broadcasted_iota(jnp.int32, (1, tk), 1)
        blk = (k0 >> 6) + (kk >> 6)
        expand = jnp.where(lax.broadcasted_iota(jnp.int32, (n_blk, 1), 0) == blk, 1.0, 0.0).astype(BF16)
        chosen = jnp.dot(sel, expand, preferred_element_type=F32)
        mask = ((chosen > 0.5) & (k0 + kk <= t_q))[None]
        s = jnp.where(mask, s, NEG_BIG)
        m_old = m_ref[...]
        m_new = jnp.maximum(m_old, jnp.max(s, axis=-1, keepdims=True))
        p = jnp.where(mask, jnp.exp(s - m_new), 0.0)
        alpha = jnp.exp(m_old - m_new)
        l_ref[...] = alpha * l_ref[...] + jnp.sum(p, axis=-1, keepdims=True)
        pv = jnp.dot(p.reshape(rows, tk).astype(BF16), vs_ref[0, 0, pl.ds(k0, tk), :], preferred_element_type=F32)
        acc_ref[...] = alpha.reshape(rows, 1) * acc_ref[...] + pv
        m_ref[...] = m_new
        return carry

    lax.fori_loop(0, (t0 + tq + tk - 1) // tk, kv_step, 0)
    o_s = acc_ref[:, 0:hd] / l_ref[...].reshape(rows, 1)

    span = WINDOW + tq
    w0 = pl.multiple_of(jnp.maximum(t0 - WINDOW, 0), tq)
    s_w = lax.dot_general(q4, kw_ref[0, 0, pl.ds(w0, span), :], _NT, preferred_element_type=F32)
    kp = w0 + lax.broadcasted_iota(jnp.int32, (1, span), 1)
    p_w = _softmax_rows(s_w, (kp <= t_row) & (kp > t_row - WINDOW))
    o_w = jnp.dot(p_w.astype(BF16), vw_ref[0, 0, pl.ds(w0, span), :], preferred_element_type=F32)[:, 0:hd]

    _to_lane_layout(lay_ref, 0, o_s, tq)
    _to_lane_layout(lay_ref, 1, o_w, tq)
    _gate_maps(gl_ref, gp_ref, gm_ref)
    _gated_store(gm_ref, oc_ref, lay_ref, o_ref, with_sums=False)


def _nsa_attend_kernel(sh_ref, q_ref, ks_ref, vs_ref, kw_ref, vw_ref, gl_ref, gp_ref, oc_ref, sel_ref, o_ref,
                       lhs_ref, acc_ref, pre_ref, gm_ref, lay_ref, m_ref, l_ref, *, tq, tk):
    data = (q_ref, ks_ref, vs_ref, kw_ref, vw_ref, gl_ref, gp_ref, oc_ref, sel_ref, o_ref)

    @pl.when(sh_ref[3] > 0.5)
    def _():
        _att_fast(sh_ref, *data, lhs_ref, acc_ref, pre_ref, gm_ref, lay_ref, tq=tq, tk=tk)

    @pl.when(sh_ref[3] <= 0.5)
    def _():
        _att_slow(*data, acc_ref, gm_ref, lay_ref, m_ref, l_ref, tq=tq, tk=tk)


def _overlap_t(n_blk, ncp):
    c_start = np.arange(ncp) * CMP_STRIDE
    b_start = np.arange(n_blk) * SEL_BLOCK
    ov = ((c_start[None, :] < b_start[:, None] + SEL_BLOCK) & (b_start[:, None] < c_start[None, :] + CMP_BLOCK))
    ov[:, ncp - 1] = False
    return jnp.asarray(ov.astype(np.float32)).astype(BF16)


def _nsa_shifts(q_norm, k_norm):
    bound = (NSA_HEAD_DIM ** 0.5) * jnp.max(jnp.abs(q_norm)) * jnp.max(jnp.abs(k_norm), axis=-1)
    bound = bound.astype(BF16).astype(F32)
    fast = jnp.all(bound <= MAX_CONST_SHIFT).astype(F32)
    return jnp.concatenate([-bound, fast[None]])


def _nsa_select(shifts, q, cmp_kv, *, tq, parts):
    b, h, s, hd = q.shape
    g, rep = NSA_GROUPS, NSA_REP
    ncp = cmp_kv.shape[3]
    n_blk = s // SEL_BLOCK
    tb = tq * parts
    assert n_blk % 8 == 0 and n_blk <= hd and s % tb == 0
    kernel = functools.partial(_nsa_select_kernel, tq=tq, n_sel=min(N_SEL, n_blk))
    return pl.pallas_call(
        kernel,
        grid=(b, g, s // tb),
        in_specs=[
            pl.BlockSpec(memory_space=pltpu.SMEM),
            pl.BlockSpec((1, rep, tb, hd), lambda bi, gi, i: (bi, gi, i, 0)),
            pl.BlockSpec((1, 1, 1, ncp, 2 * hd), lambda bi, gi, i: (0, bi, gi, 0, 0)),
            pl.BlockSpec((1, 1, 1, ncp, 2 * hd), lambda bi, gi, i: (1, bi, gi, 0, 0)),
            pl.BlockSpec((n_blk, ncp), lambda bi, gi, i: (0, 0)),
        ],
        out_specs=[pl.BlockSpec((1, tb, rep * hd), lambda bi, gi, i: (bi, i, gi)),
                   pl.BlockSpec((1, 1, tb, hd), lambda bi, gi, i: (bi, gi, i, 0))],
        out_shape=[jax.ShapeDtypeStruct((b, s, h * hd), BF16), jax.ShapeDtypeStruct((b, g, s, hd), BF16)],
        compiler_params=_cparams("parallel", "parallel", "parallel"),
        name="nsa_select",
    )(shifts, q, cmp_kv, cmp_kv, _overlap_t(n_blk, ncp))


def _nsa_attend(shifts, q, ks, vs, kw, vw, z3, o_cmp, sel, *, tq, tk):
    b, h, s, hd = q.shape
    g, rep = NSA_GROUPS, NSA_REP
    rows = rep * tq
    assert WINDOW % tq == 0
    full = lambda w: pl.BlockSpec((1, 1, s, w), lambda bi, gi, i: (bi, gi, 0, 0))
    return pl.pallas_call(
        functools.partial(_nsa_attend_kernel, tq=tq, tk=tk),
        grid=(b, g, s // tq),
        in_specs=[
            pl.BlockSpec(memory_space=pltpu.SMEM),
            pl.BlockSpec((1, rep, tq, hd), lambda bi, gi, i: (bi, gi, i, 0)),
            full(2 * hd), full(2 * hd), full(hd), full(2 * hd),
            pl.BlockSpec((1, tq, LANES), lambda bi, gi, i: (bi, i, COL_SMALL // LANES + gi)),
            pl.BlockSpec((LANES, 3 * rep * hd), lambda bi, gi, i: (0, 0)),
            pl.BlockSpec((1, tq, rep * hd), lambda bi, gi, i: (bi, i, gi)),
            pl.BlockSpec((1, 1, tq, hd), lambda bi, gi, i: (bi, gi, i, 0)),
        ],
        out_specs=pl.BlockSpec((1, tq, rep * hd), lambda bi, gi, i: (bi, i, gi)),
        out_shape=jax.ShapeDtypeStruct((b, s, h * hd), BF16),
        scratch_shapes=[pltpu.VMEM((rows, 2 * hd), BF16), pltpu.VMEM((rows, 2 * hd), F32),
                        pltpu.VMEM((rows, 2 * hd), F32), pltpu.VMEM((tq, 3 * rep * hd), F32),
                        pltpu.VMEM((4, tq, rep * hd), F32),
                        pltpu.VMEM((rep, tq, 1), F32), pltpu.VMEM((rep, tq, 1), F32)],
        compiler_params=_cparams("parallel", "parallel", "arbitrary"),
        name="nsa_attend",
    )(shifts, q, ks, vs, kw, vw, z3, _gate_placement(), o_cmp, sel)


def _head_block_diag(x, n_rows, head_of_lane, dtype):
    return jnp.concatenate([jnp.where(head_of_lane == h, x, 0.0) for h in range(GLA_HEADS)], axis=0).astype(dtype)


def _gla_kernel(q_ref, k_ref, v_ref, r_ref, sm_ref, wg_ref, bg_ref, ng_ref, tril_ref, o_ref,
                st_ref, upd_ref, oin_ref, qin_ref, *, n_chunks):
    c, sub = GLA_CHUNK, GLA_SUB
    dk, dv, nh = GLA_HEAD_DK, GLA_HEAD_DV, GLA_HEADS
    nk, nv = nh * dk, nh * dv

    @pl.when(pl.program_id(1) == 0)
    def _():
        st_ref[...] = jnp.zeros(st_ref.shape, F32)

    sm = sm_ref[0]
    x = (jnp.dot(sm, wg_ref[0], preferred_element_type=F32) + jnp.dot(sm, wg_ref[1], preferred_element_type=F32)
         + bg_ref[...])
    log_a = (jnp.minimum(x, 0.0) - jnp.log1p(jnp.exp(-jnp.abs(x)))) / GLA_TAU
    la_hi = log_a.astype(BF16)
    la_lo = (log_a - la_hi.astype(F32)).astype(BF16)
    bcum_all = (jnp.dot(tril_ref[...], la_hi, preferred_element_type=F32)
                + jnp.dot(tril_ref[...], la_lo, preferred_element_type=F32))
    head_k = lax.broadcasted_iota(jnp.int32, (1, nk), 1) >> 6
    head_v = lax.broadcasted_iota(jnp.int32, (1, nv), 1) >> 7
    state_mask = (lax.broadcasted_iota(jnp.int32, (nv, 1), 0) >> 7) == head_k
    causal = ((lax.broadcasted_iota(jnp.int32, (c, nh * c), 1) & (c - 1))
              <= lax.broadcasted_iota(jnp.int32, (c, nh * c), 0))

    decays = []
    for cc in range(n_chunks):
        rs = slice(cc * c, (cc + 1) * c)
        bcum = bcum_all[rs]
        q = q_ref[0, rs].astype(F32) * (dk ** -0.5)
        k = k_ref[0, rs].astype(F32)
        v = v_ref[0, rs].astype(F32)
        b_last = bcum[c - 1:c, :]

        score_rows = []
        for i in range(c // sub):
            lo, hi = i * sub, (i + 1) * sub
            ref = bcum[lo - 1:lo, :] if i > 0 else jnp.zeros((1, nk), F32)
            q_i = (q[lo:hi] * jnp.exp(bcum[lo:hi] - ref)).astype(BF16)
            k_i = k[0:hi] * jnp.exp(ref - bcum[0:hi])
            if hi < c:
                k_i = jnp.concatenate([k_i, jnp.zeros((c - hi, nk), F32)], axis=0)
            score_rows.append(lax.dot_general(q_i, _head_block_diag(k_i, c, head_k, BF16), _NT,
                                              preferred_element_type=F32))
        a = jnp.where(causal, jnp.concatenate(score_rows, axis=0), 0.0).astype(BF16)
        oin_ref[rs] = jnp.dot(a, _head_block_diag(v, c, head_v, BF16), preferred_element_type=F32)
        qin_ref[rs] = (q * jnp.exp(bcum)).astype(BF16)
        k_out = (k * jnp.exp(b_last - bcum)).astype(BF16)
        upd = lax.dot_general(v.astype(BF16), k_out, (((0,), (0,)), ((), ())), preferred_element_type=F32)
        upd_ref[cc] = jnp.where(state_mask, upd, 0.0)
        decays.append(jnp.exp(b_last))

    for cc in range(n_chunks):
        rs = slice(cc * c, (cc + 1) * c)
        st = st_ref[...]
        o = oin_ref[rs] + lax.dot_general(qin_ref[rs], st.astype(BF16), _NT, preferred_element_type=F32)
        st_ref[...] = st * decays[cc] + upd_ref[cc]
        r_gate = r_ref[0, rs].astype(F32)
        for h in range(nh):
            vs_ = slice(h * dv, (h + 1) * dv)
            o_h = o[:, vs_]
            ms = jnp.mean(o_h * o_h, axis=-1, keepdims=True)
            rg = r_gate[:, vs_]
            o_ref[0, rs, vs_] = (o_h * lax.rsqrt(ms + NORM_EPS) * ng_ref[...] * (rg * jax.nn.sigmoid(rg))).astype(o_ref.dtype)


def _gla(z3, w_gate, b_gate, norm_g, *, n_chunks):
    b, s, _ = z3.shape
    c = GLA_CHUNK * n_chunks
    nk = GLA_HEADS * GLA_HEAD_DK
    nv = GLA_HEADS * GLA_HEAD_DV
    wg = jnp.zeros((LANES, nk), F32).at[SMALL_GLOW_LANE:SMALL_GLOW_LANE + GLA_RANK].set(w_gate.astype(F32))
    wg_hi = wg.astype(BF16)
    wg = jnp.stack([wg_hi, (wg - wg_hi.astype(F32)).astype(BF16)])
    idx = np.arange(c)
    tril = jnp.asarray(((idx[:, None] >= idx[None, :])
                        & (idx[:, None] // GLA_CHUNK == idx[None, :] // GLA_CHUNK)).astype(np.float32)).astype(BF16)
    const = lambda shape: pl.BlockSpec(shape, lambda bi, i: (0,) * len(shape))
    return pl.pallas_call(
        functools.partial(_gla_kernel, n_chunks=n_chunks),
        grid=(b, s // c),
        in_specs=[
            pl.BlockSpec((1, c, nk), lambda bi, i: (bi, i, COL_GQ // nk)),
            pl.BlockSpec((1, c, nk), lambda bi, i: (bi, i, COL_GK // nk)),
            pl.BlockSpec((1, c, nv), lambda bi, i: (bi, i, COL_GV // nv)),
            pl.BlockSpec((1, c, nv), lambda bi, i: (bi, i, COL_GR // nv)),
            pl.BlockSpec((1, c, LANES), lambda bi, i: (bi, i, COL_SMALL // LANES)),
            const((2, LANES, nk)), const((1, nk)), const((1, GLA_HEAD_DV)), const((c, c)),
        ],
        out_specs=pl.BlockSpec((1, c, nv), lambda bi, i: (bi, i, 0)),
        out_shape=jax.ShapeDtypeStruct((b, s, nv), BF16),
        scratch_shapes=[pltpu.VMEM((nv, nk), F32), pltpu.VMEM((n_chunks, nv, nk), F32),
                        pltpu.VMEM((c, nv), F32), pltpu.VMEM((c, nk), BF16)],
        compiler_params=_cparams("parallel", "arbitrary"),
        name="gla",
    )(z3, z3, z3, z3, z3, wg, b_gate.reshape(1, nk).astype(F32), norm_g.reshape(1, GLA_HEAD_DV).astype(F32), tril)


def _mem_attn_kernel(q_ref, k_ref, v_ref, qg_ref, kg_ref, o_ref):
    dh = MEM_HEAD_DIM
    nt = (((1,), (1,)), ((), ()))
    for h in range(MEM_HEADS):
        sl = slice(h * dh, (h + 1) * dh)
        q = q_ref[0, :, sl].astype(F32)
        q = q * lax.rsqrt(jnp.mean(q * q, axis=-1, keepdims=True) + NORM_EPS) * qg_ref[...] * (dh ** -0.5)
        k = k_ref[0, :, sl].astype(F32)
        k = k * lax.rsqrt(jnp.mean(k * k, axis=-1, keepdims=True) + NORM_EPS) * kg_ref[...]
        s = lax.dot_general(q.astype(BF16), k.astype(BF16), nt, preferred_element_type=F32)
        m = jnp.max(s, axis=-1, keepdims=True)
        p = jnp.exp(s - m)
        p = p / jnp.sum(p, axis=-1, keepdims=True)
        o = jnp.dot(p.astype(BF16), v_ref[0, :, sl].astype(BF16), preferred_element_type=F32)
        o_ref[0, :, sl] = o.astype(o_ref.dtype)


def _mem_attention(z3, kv, q_norm, k_norm, *, tq):
    b, s, _ = z3.shape
    m = kv.shape[1]
    w = MEM_HEADS * MEM_HEAD_DIM
    const = lambda shape: pl.BlockSpec(shape, lambda bi, i: (0,) * len(shape))
    return pl.pallas_call(
        _mem_attn_kernel,
        grid=(b, s // tq),
        in_specs=[
            pl.BlockSpec((1, tq, w), lambda bi, i: (bi, i, COL_MQ // w)),
            pl.BlockSpec((1, m, w), lambda bi, i: (bi, 0, 0)),
            pl.BlockSpec((1, m, w), lambda bi, i: (bi, 0, 1)),
            const((1, MEM_HEAD_DIM)), const((1, MEM_HEAD_DIM)),
        ],
        out_specs=pl.BlockSpec((1, tq, w), lambda bi, i: (bi, i, 0)),
        out_shape=jax.ShapeDtypeStruct((b, s, w), BF16),
        compiler_params=_cparams("parallel", "parallel"),
        name="mem_attention",
    )(z3, kv, kv, q_norm.reshape(1, MEM_HEAD_DIM).astype(F32), k_norm.reshape(1, MEM_HEAD_DIM).astype(F32))


def _merge_kernel(x_ref, on_ref, og_ref, om_ref, m0_ref, m1_ref, m2_ref, bm_ref, wb_ref, wo_ref, o_ref):
    merged = None
    for br, (ref, mg_ref) in enumerate(((on_ref, m0_ref), (og_ref, m1_ref), (om_ref, m2_ref))):
        y = jnp.dot(ref[...], wb_ref[br], preferred_element_type=F32)
        gate = jax.nn.sigmoid(mg_ref[...].astype(F32) + bm_ref[br:br + 1, :])
        merged = gate * y if merged is None else merged + gate * y
    o_ref[...] = x_ref[...] + jnp.dot(merged.astype(BF16), wo_ref[...], preferred_element_type=F32)


def _merge_out(x2, o_nsa, o_gla, o_mem, z2, b_merge, w_branch, w_out, layer, *, tm):
    t, d = x2.shape
    bw = BRANCH_WIDTH
    row = lambda w: pl.BlockSpec((tm, w), lambda i: (i, 0))
    gate_cols = lambda br: pl.BlockSpec((tm, d), lambda i: (i, COL_MERGE // d + br))
    return pl.pallas_call(
        _merge_kernel,
        grid=(t // tm,),
        in_specs=[
            row(d), row(bw), row(bw), row(bw),
            gate_cols(0), gate_cols(1), gate_cols(2),
            pl.BlockSpec((N_BRANCH, d), lambda i: (0, 0)),
            pl.BlockSpec((None, N_BRANCH, bw, d), lambda i: (layer, 0, 0, 0)),
            pl.BlockSpec((None, d, d), lambda i: (layer, 0, 0)),
        ],
        out_specs=row(d),
        out_shape=jax.ShapeDtypeStruct((t, d), F32),
        compiler_params=_cparams("parallel"),
        name="merge_out",
    )(x2, o_nsa, o_gla, o_mem, z2, z2, z2, b_merge.astype(F32), w_branch, w_out)


def _mlp_kernel(x_ref, g_ref, wu_ref, wd_ref, o_ref, h_ref, acc_ref):
    j = pl.program_id(1)

    @pl.when(j == 0)
    def _():
        h_ref[...] = _rms_rows(x_ref[...], g_ref[...])
        acc_ref[...] = jnp.zeros(acc_ref.shape, F32)

    u = jnp.dot(h_ref[...], wu_ref[...], preferred_element_type=F32)
    u = jnp.square(jnp.maximum(u, 0.0)).astype(BF16)
    acc_ref[...] += jnp.dot(u, wd_ref[...], preferred_element_type=F32)

    @pl.when(j == pl.num_programs(1) - 1)
    def _():
        o_ref[...] = x_ref[...] + acc_ref[...]


def _mlp(x2, g, w_up, w_down, layer, *, tm, th):
    t, d = x2.shape
    hid = w_up.shape[2]
    return pl.pallas_call(
        _mlp_kernel,
        grid=(t // tm, hid // th),
        in_specs=[
            pl.BlockSpec((tm, d), lambda i, j: (i, 0)),
            pl.BlockSpec((1, d), lambda i, j: (0, 0)),
            pl.BlockSpec((None, d, th), lambda i, j: (layer, 0, j)),
            pl.BlockSpec((None, th, d), lambda i, j: (layer, j, 0)),
        ],
        out_specs=pl.BlockSpec((tm, d), lambda i, j: (i, 0)),
        out_shape=jax.ShapeDtypeStruct((t, d), F32),
        scratch_shapes=[pltpu.VMEM((tm, d), BF16), pltpu.VMEM((tm, d), F32)],
        compiler_params=_cparams("parallel", "arbitrary"),
        name="mlp",
    )(x2, g.reshape(1, d).astype(F32), w_up, w_down)


IN_PROJ_ROWS, IN_PROJ_COL_TILES = 1024, 2
NSA_PREP_ROWS = 256
NSA_QUERY_TILE, NSA_SELECT_PARTS, NSA_KEY_TILE = 256, 2, 512
GLA_CHUNKS_PER_STEP = 8
MEM_KV_ROWS, MEM_KV_COLS, MEM_QUERY_TILE = 512, 512, 512
MERGE_ROWS = 512
MLP_ROWS, MLP_HIDDEN_TILE = 1024, 1024


def _tile(n, pref):
    t = min(n, pref)
    assert n % t == 0, (n, pref)
    return t


def _layer(x, mem2, pos_rows, pos_cmp, p, big, layer):
    b, s, d = x.shape
    t = b * s
    hd = NSA_HEAD_DIM
    x2 = x.reshape(t, d)

    z2 = _norm_matmul(x2, p["ln_mix"].astype(F32), big["w_in"], layer, tm=_tile(t, IN_PROJ_ROWS),
                      tn=D_IN_PAD // IN_PROJ_COL_TILES, out_dtype=BF16, name="in_proj")
    z3 = z2.reshape(b, s, D_IN_PAD)

    q, ks, vs, kw, vw = _nsa_prep(z3, pos_rows, p["nsa_q_norm"].astype(F32), p["nsa_k_norm"][1].astype(F32),
                                  p["nsa_k_norm"][2].astype(F32), tp=_tile(s, NSA_PREP_ROWS))
    pe2 = jnp.pad(p["cmp_pe"].reshape(2, 2, CMP_STRIDE * hd), ((0, 0), (0, 6), (0, 0))).astype(F32)
    cmp_kv = _nsa_compress(z3, p["cmp_w1"], p["cmp_w2"], pe2, pos_cmp, p["nsa_k_norm"][0].astype(F32))
    shifts = _nsa_shifts(p["nsa_q_norm"].astype(F32), p["nsa_k_norm"].astype(F32))
    tq = _tile(s, NSA_QUERY_TILE)
    o_cmp, sel = _nsa_select(shifts, q, cmp_kv, tq=tq, parts=NSA_SELECT_PARTS)
    o_nsa = _nsa_attend(shifts, q, ks, vs, kw, vw, z3, o_cmp, sel, tq=tq, tk=_tile(s, NSA_KEY_TILE))

    o_gla = _gla(z3, p["gla_w_gate"], p["gla_b_gate"], p["gla_norm"], n_chunks=GLA_CHUNKS_PER_STEP)

    kv = _norm_matmul(mem2, p["mem_norm"].astype(F32), big["mem_w_kv"], layer,
                      tm=_tile(mem2.shape[0], MEM_KV_ROWS), tn=MEM_KV_COLS, out_dtype=BF16, name="mem_kv")
    kv = kv.reshape(b, mem2.shape[0] // b, 2 * MEM_HEADS * MEM_HEAD_DIM)
    o_mem = _mem_attention(z3, kv, p["mem_q_norm"], p["mem_k_norm"], tq=_tile(s, MEM_QUERY_TILE))

    x2 = _merge_out(x2, o_nsa.reshape(t, -1), o_gla.reshape(t, -1), o_mem.reshape(t, -1), z2,
                    p["b_merge"], big["w_branch"], big["w_out"], layer, tm=_tile(t, MERGE_ROWS))
    x2 = _mlp(x2, p["ln_mlp"], big["w_up"], big["w_down"], layer, tm=_tile(t, MLP_ROWS), th=MLP_HIDDEN_TILE)
    return x2.reshape(b, s, d)


def kernel(x, mem, positions, ln_mix, w_in, b_merge, nsa_q_norm, nsa_k_norm, cmp_pe, cmp_w1, cmp_w2,
           gla_w_gate, gla_b_gate, gla_norm, mem_norm, mem_w_kv, mem_q_norm, mem_k_norm, w_branch, w_out,
           ln_mlp, w_up, w_down):
    b, s, d = x.shape
    assert d == 1024 and s % WINDOW == 0 and s >= 2 * WINDOW
    depth = w_in.shape[0]
    perm_np, d_in = _in_proj_permutation(d)
    assert w_in.shape[2] == d_in
    w_in = _relayout_in_proj(w_in.astype(BF16), [int(c) for c in perm_np], d_in)
    half = CMP_STRIDE * NSA_HEAD_DIM
    cmp_w1 = jnp.concatenate([cmp_w1[:, :, :half], cmp_w1[:, :, half:]], axis=-1).astype(BF16)
    cmp_w2, mem_w_kv, w_branch, w_out, w_up, w_down = (
        a.astype(BF16) for a in (cmp_w2, mem_w_kv, w_branch, w_out, w_up, w_down))
    pos3 = positions.astype(jnp.int32).reshape(b, s, 1)
    nc = s // CMP_STRIDE
    cmp_end = np.minimum(np.arange(nc) * CMP_STRIDE + CMP_BLOCK - 1, s - 1)
    pos_cmp = pos3[:, cmp_end, :]
    pos_rows = positions.astype(jnp.int32).reshape(b, 1, s)
    mem2 = mem.reshape(b * mem.shape[1], d)
    names = ("ln_mix", "b_merge", "nsa_q_norm", "nsa_k_norm", "cmp_pe", "cmp_w1", "cmp_w2",
             "gla_w_gate", "gla_b_gate", "gla_norm", "mem_norm", "mem_q_norm", "mem_k_norm", "ln_mlp")
    stacked = (ln_mix, b_merge, nsa_q_norm, nsa_k_norm, cmp_pe, cmp_w1, cmp_w2, gla_w_gate, gla_b_gate,
               gla_norm, mem_norm, mem_q_norm, mem_k_norm, ln_mlp)
    big = dict(w_in=w_in, mem_w_kv=mem_w_kv, w_branch=w_branch, w_out=w_out, w_up=w_up, w_down=w_down)
    for l in range(depth):
        x = _layer(x, mem2, pos_rows, pos_cmp, {n: a[l] for n, a in zip(names, stacked)}, big, l)
    return x
```

```python
import functools

import numpy as np
import jax
import jax.numpy as jnp
from jax import lax
from jax.experimental import pallas as pl
from jax.experimental.pallas import tpu as pltpu

NSA_HEADS = 8
NSA_GROUPS = 2
NSA_REP = NSA_HEADS // NSA_GROUPS
NSA_HEAD_DIM = 64
CMP_BLOCK = 32
CMP_STRIDE = 16
CMP_HIDDEN = 4 * NSA_HEAD_DIM
SEL_BLOCK = 64
N_SEL = 16
WINDOW = 512
FORCE_SCORE = 1e4
GLA_HEADS = 4
GLA_HEAD_DK = 64
GLA_HEAD_DV = 128
GLA_RANK = 16
GLA_TAU = 16.0
GLA_CHUNK = 64
GLA_SUB = 16
MEM_HEADS = 4
MEM_HEAD_DIM = 128
N_BRANCH = 3
BRANCH_WIDTH = 512
ROPE_THETA = 500000.0
ROPE_ROT = NSA_HEAD_DIM // 4
ROPE_HALF = ROPE_ROT // 2
NORM_EPS = 1e-6

LANES = 128
VMEM_LIMIT_BYTES = 48 * 1024 * 1024

F32 = jnp.float32
BF16 = jnp.bfloat16
HIGHEST = lax.Precision.HIGHEST
NEG_BIG = -1e30
MAX_CONST_SHIFT = 40.0

COL_NQ = 0
COL_GV = 512
COL_GR = 1024
COL_MQ = 1536
COL_MERGE = 2048
COL_GQ = 5120
COL_GK = 5376
COL_KV = 5632
COL_SMALL = 6400
D_IN_PAD = 6656
SMALL_GLOW_LANE = 12


def _in_proj_permutation(d_model):
    sizes = (512, 128, 128, 128, 128, 128, 128, 24, 256, 256, 512, 512, 16, 512, 3 * d_model)
    off = np.concatenate([[0], np.cumsum(sizes)])
    (o_nq, o_kc, o_vc, o_ks, o_vs, o_kw, o_vw, o_ng, o_gq, o_gk, o_gv, o_gr, o_gl, o_mq, o_mg) = off[:-1]
    d_in = int(off[-1])
    perm = np.full((D_IN_PAD,), d_in, np.int32)

    def put(new, old, n):
        perm[new:new + n] = np.arange(old, old + n)

    put(COL_NQ, o_nq, 512)
    put(COL_GV, o_gv, 512)
    put(COL_GR, o_gr, 512)
    put(COL_MQ, o_mq, 512)
    put(COL_MERGE, o_mg, 3 * d_model)
    put(COL_GQ, o_gq, 256)
    put(COL_GK, o_gk, 256)
    put(COL_KV, o_kc, 768)
    for g in range(NSA_GROUPS):
        for br in range(3):
            for r in range(NSA_REP):
                perm[COL_SMALL + g * LANES + br * NSA_REP + r] = o_ng + (g * NSA_REP + r) * 3 + br
    put(COL_SMALL + SMALL_GLOW_LANE, o_gl, GLA_RANK)
    return perm, d_in


def _relayout_in_proj(w, perm, d_in):
    pieces, start = [], 0
    for i in range(1, len(perm) + 1):
        prev = perm[i - 1]
        if i < len(perm) and (perm[i] == prev + 1 if prev != d_in else perm[i] == d_in):
            continue
        n = i - start
        if prev == d_in:
            pieces.append(jnp.zeros(w.shape[:-1] + (n,), w.dtype))
        else:
            pieces.append(w[..., perm[start]:perm[start] + n])
        start = i
    return jnp.concatenate(pieces, axis=-1)


def _cparams(*sem):
    return pltpu.CompilerParams(dimension_semantics=sem, vmem_limit_bytes=VMEM_LIMIT_BYTES)


def _rms_rows(x, g):
    ms = jnp.mean(x * x, axis=-1, keepdims=True)
    return (x * lax.rsqrt(ms + NORM_EPS) * g).astype(BF16)


def _norm_matmul_kernel(x_ref, g_ref, w_ref, o_ref):
    h = _rms_rows(x_ref[...].astype(F32), g_ref[...])
    o_ref[...] = jnp.dot(h, w_ref[...], preferred_element_type=F32).astype(o_ref.dtype)


def _norm_matmul(x, g, w, layer, *, tm, tn, out_dtype, name):
    m, k = x.shape
    n = w.shape[2]
    return pl.pallas_call(
        _norm_matmul_kernel,
        grid=(n // tn, m // tm),
        in_specs=[
            pl.BlockSpec((tm, k), lambda j, i: (i, 0)),
            pl.BlockSpec((1, k), lambda j, i: (0, 0)),
            pl.BlockSpec((None, k, tn), lambda j, i: (layer, 0, j)),
        ],
        out_specs=pl.BlockSpec((tm, tn), lambda j, i: (i, j)),
        out_shape=jax.ShapeDtypeStruct((m, n), out_dtype),
        compiler_params=_cparams("parallel", "parallel"),
        name=name,
    )(x, g.reshape(1, k), w)


def _rope_tables(pos_row, freq_col, place):
    ang = freq_col * pos_row
    tn = (((0,), (0,)), ((), ()))
    lane = lax.broadcasted_iota(jnp.int32, (1, LANES), 1)
    c = lax.dot_general(jnp.cos(ang), place, tn, preferred_element_type=F32, precision=HIGHEST)
    s = lax.dot_general(jnp.sin(ang), place, tn, preferred_element_type=F32, precision=HIGHEST)
    return c + jnp.where((lane & (NSA_HEAD_DIM - 1)) >= ROPE_ROT, 1.0, 0.0), s


def _rope_lanes(x, c, s):
    n = x.shape[-1]
    if n > LANES:
        c = jnp.concatenate([c] * (n // LANES), axis=1)
        s = jnp.concatenate([s] * (n // LANES), axis=1)
    lane = lax.broadcasted_iota(jnp.int32, (1, n), 1) & (NSA_HEAD_DIM - 1)
    up = pltpu.roll(x, n - ROPE_HALF, axis=1)
    dn = pltpu.roll(x, ROPE_HALF, axis=1)
    y = jnp.where(lane < ROPE_HALF, -up, jnp.where(lane < ROPE_ROT, dn, 0.0))
    return x * c + y * s


def _head_rms(x, bd, g):
    x2 = x * x
    hi = x2.astype(BF16)
    lo = (x2 - hi.astype(F32)).astype(BF16)
    ms = (jnp.dot(hi, bd, preferred_element_type=F32) + jnp.dot(lo, bd, preferred_element_type=F32))
    return x * lax.rsqrt(ms * (1.0 / NSA_HEAD_DIM) + NORM_EPS) * g


def _nsa_prep_kernel(q_ref, ks_ref, vs_ref, kw_ref, vw_ref, pos_ref, qg_ref, ksg_ref, kwg_ref,
                     bdq_ref, bdk_ref, f_ref, place_ref,
                     qo_ref, kso_ref, vso_ref, kwo_ref, vwo_ref):
    hd = NSA_HEAD_DIM
    tp = q_ref.shape[1]
    c, s = _rope_tables(pos_ref[0].astype(F32), f_ref[...], place_ref[...])
    q = _head_rms(q_ref[0].astype(F32), bdq_ref[...], qg_ref[...])
    q = (_rope_lanes(q, c, s) * (hd ** -0.5)).astype(qo_ref.dtype)
    for h in range(NSA_HEADS):
        qo_ref[0, h] = q[:, h * hd:(h + 1) * hd]
    ks = _rope_lanes(_head_rms(ks_ref[0].astype(F32), bdk_ref[...], ksg_ref[...]), c, s)
    kw = _rope_lanes(_head_rms(kw_ref[0].astype(F32), bdk_ref[...], kwg_ref[...]), c, s)
    vs = vs_ref[0]
    vw = vw_ref[0]
    tok = pl.program_id(1) * tp + lax.broadcasted_iota(jnp.int32, (tp, hd), 0)
    lane = lax.broadcasted_iota(jnp.int32, (tp, hd), 1)
    blk_onehot = jnp.where((tok >> 6) == lane, 1.0, 0.0).astype(kso_ref.dtype)
    ones_col = jnp.ones((tp, hd), vso_ref.dtype)
    for g in range(NSA_GROUPS):
        sl = slice(g * hd, (g + 1) * hd)
        kso_ref[0, g, :, 0:hd] = ks[:, sl].astype(kso_ref.dtype)
        kso_ref[0, g, :, hd:2 * hd] = blk_onehot
        kwo_ref[0, g] = kw[:, sl].astype(kwo_ref.dtype)
        vso_ref[0, g, :, 0:hd] = vs[:, sl].astype(vso_ref.dtype)
        vso_ref[0, g, :, hd:2 * hd] = ones_col
        vwo_ref[0, g, :, 0:hd] = vw[:, sl].astype(vwo_ref.dtype)
        vwo_ref[0, g, :, hd:2 * hd] = ones_col


def _block_diag_ones(n, width):
    i = np.arange(n)
    return jnp.asarray((i[:, None] // width == i[None, :] // width).astype(np.float32)).astype(BF16)


def _rope_freq_col():
    inv = ROPE_THETA ** (-(np.arange(ROPE_HALF, dtype=np.float64)) / ROPE_HALF)
    return jnp.asarray(inv.astype(np.float32)).reshape(ROPE_HALF, 1)


def _rope_placement():
    m = np.zeros((ROPE_HALF, LANES), np.float32)
    for f in range(ROPE_HALF):
        for base in range(0, LANES, NSA_HEAD_DIM):
            m[f, base + f] = 1.0
            m[f, base + f + ROPE_HALF] = 1.0
    return jnp.asarray(m)


def _nsa_prep(z3, pos_rows, q_norm, ks_norm, kw_norm, *, tp):
    b, s, _ = z3.shape
    hd, g = NSA_HEAD_DIM, NSA_GROUPS
    assert SEL_BLOCK == 64 and s // SEL_BLOCK <= hd
    kvb = COL_KV // LANES
    qg = jnp.tile(q_norm, NSA_HEADS).reshape(1, NSA_HEADS * hd)
    ksg = jnp.tile(ks_norm, g).reshape(1, g * hd)
    kwg = jnp.tile(kw_norm, g).reshape(1, g * hd)
    const = lambda shape: pl.BlockSpec(shape, lambda bi, i: (0,) * len(shape))
    plain_out = jax.ShapeDtypeStruct((b, g, s, hd), BF16)
    plain_spec = pl.BlockSpec((1, g, tp, hd), lambda bi, i: (bi, 0, i, 0))
    aug_out = jax.ShapeDtypeStruct((b, g, s, 2 * hd), BF16)
    aug_spec = pl.BlockSpec((1, g, tp, 2 * hd), lambda bi, i: (bi, 0, i, 0))
    return pl.pallas_call(
        _nsa_prep_kernel,
        grid=(b, s // tp),
        in_specs=[
            pl.BlockSpec((1, tp, 512), lambda bi, i: (bi, i, COL_NQ // 512)),
            pl.BlockSpec((1, tp, LANES), lambda bi, i: (bi, i, kvb + 2)),
            pl.BlockSpec((1, tp, LANES), lambda bi, i: (bi, i, kvb + 3)),
            pl.BlockSpec((1, tp, LANES), lambda bi, i: (bi, i, kvb + 4)),
            pl.BlockSpec((1, tp, LANES), lambda bi, i: (bi, i, kvb + 5)),
            pl.BlockSpec((1, 1, tp), lambda bi, i: (bi, 0, i)),
            const((1, 512)), const((1, LANES)), const((1, LANES)),
            const((512, 512)), const((LANES, LANES)),
            const((ROPE_HALF, 1)), const((ROPE_HALF, LANES)),
        ],
        out_specs=[
            pl.BlockSpec((1, NSA_HEADS, tp, hd), lambda bi, i: (bi, 0, i, 0)),
            aug_spec, aug_spec, plain_spec, aug_spec,
        ],
        out_shape=[jax.ShapeDtypeStruct((b, NSA_HEADS, s, hd), BF16), aug_out, aug_out, plain_out, aug_out],
        compiler_params=_cparams("parallel", "parallel"),
        name="nsa_prep",
    )(z3, z3, z3, z3, z3, pos_rows, qg, ksg, kwg,
      _block_diag_ones(512, hd), _block_diag_ones(LANES, hd), _rope_freq_col(), _rope_placement())


def _gelu_tanh(x):
    return 0.5 * x * (1.0 + jnp.tanh(np.sqrt(2.0 / np.pi) * (x + 0.044715 * x * x * x)))


def _nsa_cmp_kernel(x_ref, w1_ref, w2_ref, pe_ref, pos_ref, g_ref, rot_ref, f_ref, o_ref, xs_ref):
    kind = pl.program_id(0)
    hd = NSA_HEAD_DIM
    nc = x_ref.shape[1] // CMP_STRIDE
    xs_ref[...] = x_ref[0].astype(F32)
    r = jnp.dot(pe_ref[0].astype(BF16), w1_ref[0], preferred_element_type=F32)
    ab = [jnp.zeros((nc, 2 * CMP_HIDDEN), F32) for _ in range(NSA_GROUPS)]
    for p in range(CMP_STRIDE):
        xp = xs_ref[pl.ds(p, nc, stride=CMP_STRIDE), :].astype(BF16)
        w1p = w1_ref[0, p * hd:(p + 1) * hd, :]
        for g in range(NSA_GROUPS):
            ab[g] = ab[g] + jnp.dot(xp[:, g * hd:(g + 1) * hd], w1p, preferred_element_type=F32)

    for g in range(NSA_GROUPS):
        a = ab[g][:, :CMP_HIDDEN] + r[0:1, :CMP_HIDDEN]
        bm = ab[g][:, CMP_HIDDEN:] + r[1:2, CMP_HIDDEN:]
        hid = _gelu_tanh(a + pltpu.roll(bm, nc - 1, axis=0))
        comp = jnp.dot(hid.astype(BF16), w2_ref[0], preferred_element_type=F32)

        @pl.when(kind == 0)
        def _():
            ms = jnp.mean(comp * comp, axis=-1, keepdims=True)
            kn = comp * lax.rsqrt(ms + NORM_EPS) * g_ref[...]
            ang = pos_ref[0].astype(F32) * f_ref[...]
            y = jnp.dot(kn, rot_ref[...], preferred_element_type=F32, precision=HIGHEST)
            o_ref[0, 0, g, :, 0:hd] = (kn * jnp.cos(ang) + y * jnp.sin(ang)).astype(o_ref.dtype)
            o_ref[0, 0, g, :, hd:2 * hd] = jnp.zeros((nc, hd), o_ref.dtype)

        @pl.when(kind != 0)
        def _():
            o_ref[0, 0, g, :, 0:hd] = comp.astype(o_ref.dtype)
            o_ref[0, 0, g, :, hd:2 * hd] = jnp.ones((nc, hd), o_ref.dtype)


def _rope_freq_head():
    p = np.arange(NSA_HEAD_DIM)
    inv = ROPE_THETA ** (-(np.arange(ROPE_HALF, dtype=np.float64)) / ROPE_HALF)
    f = np.where(p < ROPE_ROT, inv[p % ROPE_HALF], 0.0)
    return jnp.asarray(f.astype(np.float32)).reshape(1, NSA_HEAD_DIM)


def _rope_rot_matrix(n):
    m = np.zeros((n, n), np.float32)
    for base in range(0, n, NSA_HEAD_DIM):
        for l in range(ROPE_HALF):
            m[base + l + ROPE_HALF, base + l] = -1.0
            m[base + l, base + l + ROPE_HALF] = 1.0
    return jnp.asarray(m)


def _nsa_compress(z3, w1cat, w2, pe2, pos_cmp, k_norm0):
    b, s, _ = z3.shape
    hd, g = NSA_HEAD_DIM, NSA_GROUPS
    nc = s // CMP_STRIDE
    assert g * hd == LANES
    const = lambda shape: pl.BlockSpec(shape, lambda k, bi: (0,) * len(shape))
    return pl.pallas_call(
        _nsa_cmp_kernel,
        grid=(2, b),
        in_specs=[
            pl.BlockSpec((1, s, LANES), lambda k, bi: (bi, 0, COL_KV // LANES + k)),
            pl.BlockSpec((1, CMP_STRIDE * hd, 2 * CMP_HIDDEN), lambda k, bi: (k, 0, 0)),
            pl.BlockSpec((1, CMP_HIDDEN, hd), lambda k, bi: (k, 0, 0)),
            pl.BlockSpec((1, 8, CMP_STRIDE * hd), lambda k, bi: (k, 0, 0)),
            pl.BlockSpec((1, nc, 1), lambda k, bi: (bi, 0, 0)),
            const((1, hd)), const((hd, hd)), const((1, hd)),
        ],
        out_specs=pl.BlockSpec((1, 1, g, nc, 2 * hd), lambda k, bi: (k, bi, 0, 0, 0)),
        out_shape=jax.ShapeDtypeStruct((2, b, g, nc, 2 * hd), BF16),
        scratch_shapes=[pltpu.VMEM((s, LANES), F32)],
        compiler_params=_cparams("parallel", "parallel"),
        name="nsa_compress",
    )(z3, w1cat, w2, pe2, pos_cmp, k_norm0.reshape(1, hd), _rope_rot_matrix(hd), _rope_freq_head())


def _softmax_rows(s, mask):
    s = jnp.where(mask, s, NEG_BIG)
    m = jnp.max(s, axis=-1, keepdims=True)
    p = jnp.where(mask, jnp.exp(s - m), 0.0)
    d = jnp.sum(p, axis=-1, keepdims=True)
    return p / jnp.where(d > 0, d, 1.0)


_NT = (((1,), (1,)), ((), ()))


def _select_blocks(p_sum, ov_ref, t0, tq, n_sel):
    n_blk = ov_ref.shape[0]
    p_hi = p_sum.astype(BF16)
    p_lo = (p_sum - p_hi.astype(F32)).astype(BF16)
    imp_t = (lax.dot_general(ov_ref[...], p_hi, _NT, preferred_element_type=F32)
             + lax.dot_general(ov_ref[...], p_lo, _NT, preferred_element_type=F32))
    j_col = lax.broadcasted_iota(jnp.int32, (n_blk, 1), 0)
    t_lane = t0 + lax.broadcasted_iota(jnp.int32, (1, tq), 1)
    causal = j_col * SEL_BLOCK <= t_lane
    cur = t_lane >> 6
    forced = causal & ((j_col == 0) | (j_col == cur) | (j_col == cur - 1))
    score = jnp.where(forced, FORCE_SCORE, jnp.where(causal, imp_t, -FORCE_SCORE))
    ng = n_blk // 8
    groups = [score[8 * v:8 * v + 8] for v in range(ng)]
    ranks = [jnp.zeros((8, tq), F32) for _ in range(ng)]
    sub = lax.broadcasted_iota(jnp.int32, (8, tq), 0)
    for jp in range(n_blk):
        row = jnp.broadcast_to(score[jp:jp + 1, :], (8, tq))
        vj = jp // 8
        for v in range(ng):
            if v < vj:
                ahead = row > groups[v]
            elif v > vj:
                ahead = row >= groups[v]
            else:
                ahead = (row > groups[v]) | ((row == groups[v]) & (sub > jp % 8))
            ranks[v] = ranks[v] + jnp.where(ahead, 1.0, 0.0)
    sel_t = jnp.where(jnp.concatenate(ranks, axis=0) < n_sel, 1.0, 0.0).astype(BF16)
    eye = (lax.broadcasted_iota(jnp.int32, (tq, tq), 0) == lax.broadcasted_iota(jnp.int32, (tq, tq), 1)).astype(BF16)
    return lax.dot_general(eye, sel_t, _NT, preferred_element_type=F32)


def _sum_heads(p, tq):
    out = p[0:tq]
    for r in range(1, NSA_REP):
        out = out + p[r * tq:(r + 1) * tq]
    return out


def _store_heads(o_ref, o, tq, row0):
    hd = NSA_HEAD_DIM
    for r in range(NSA_REP):
        o_ref[0, row0:row0 + tq, r * hd:(r + 1) * hd] = o[r * tq:(r + 1) * tq].astype(o_ref.dtype)


def _store_selection(sel_ref, sel, row0):
    tq, n_blk = sel.shape
    sel = sel.astype(sel_ref.dtype)
    if n_blk < NSA_HEAD_DIM:
        sel = jnp.concatenate([sel, jnp.zeros((tq, NSA_HEAD_DIM - n_blk), sel_ref.dtype)], axis=1)
    sel_ref[0, 0, row0:row0 + tq] = sel


def _gate_maps(gl_ref, gp_ref, gm_ref):
    gates = jax.nn.sigmoid(gl_ref[0].astype(F32))
    g_hi = gates.astype(BF16)
    g_lo = (gates - g_hi.astype(F32)).astype(BF16)
    gm_ref[...] = (jnp.dot(g_hi, gp_ref[...], preferred_element_type=F32)
                   + jnp.dot(g_lo, gp_ref[...], preferred_element_type=F32))


def _to_lane_layout(lay_ref, idx, o, tq, lane0=0):
    hd = NSA_HEAD_DIM
    for r in range(NSA_REP):
        lay_ref[idx, :, r * hd:(r + 1) * hd] = o[r * tq:(r + 1) * tq, lane0:lane0 + hd]


def _gated_store(gm_ref, oc_ref, lay_ref, o_ref, with_sums):
    w = NSA_REP * NSA_HEAD_DIM
    o_sel, o_win = lay_ref[0], lay_ref[1]
    if with_sums:
        o_sel, o_win = o_sel / lay_ref[2], o_win / lay_ref[3]
    out = gm_ref[:, 0:w] * oc_ref[0].astype(F32) + gm_ref[:, w:2 * w] * o_sel + gm_ref[:, 2 * w:3 * w] * o_win
    o_ref[0] = out.astype(o_ref.dtype)


def _gate_placement():
    m = np.zeros((LANES, 3 * NSA_REP * NSA_HEAD_DIM), np.float32)
    for br in range(3):
        for r in range(NSA_REP):
            c0 = br * NSA_REP * NSA_HEAD_DIM + r * NSA_HEAD_DIM
            m[br * NSA_REP + r, c0:c0 + NSA_HEAD_DIM] = 1.0
    return jnp.asarray(m).astype(BF16)


def _sel_fast(sh_ref, q_ref, kc_ref, vc_ref, ov_ref, oc_ref, sel_ref, *, tq, n_sel):
    rep, hd = NSA_REP, NSA_HEAD_DIM
    rows = rep * tq
    kc = kc_ref[0, 0, 0][:, 0:hd]
    ncp = kc.shape[0]
    n_idx = lax.broadcasted_iota(jnp.int32, (1, ncp), 1)
    for part in range(q_ref.shape[2] // tq):
        row0 = part * tq
        t0 = pl.program_id(2) * q_ref.shape[2] + row0
        q4 = q_ref[0, :, row0:row0 + tq, :].reshape(rows, hd)
        t_q = t0 + lax.broadcasted_iota(jnp.int32, (tq, 1), 0)
        valid_c = (n_idx * CMP_STRIDE + (CMP_BLOCK - 1) <= t_q) & (n_idx < ncp - 1)
        bias_c = jnp.where(valid_c, sh_ref[0], NEG_BIG)
        s_c = lax.dot_general(q4, kc, _NT, preferred_element_type=F32).reshape(rep, tq, ncp)
        p_c = jnp.exp(s_c + bias_c[None]).reshape(rows, ncp)
        ol_c = jnp.dot(p_c.astype(BF16), vc_ref[0, 0, 0], preferred_element_type=F32)
        inv_c = 1.0 / jnp.where(ol_c[:, hd:hd + 1] > 0, ol_c[:, hd:hd + 1], 1.0)
        _store_heads(oc_ref, ol_c[:, 0:hd] * inv_c, tq, row0)
        _store_selection(sel_ref, _select_blocks(_sum_heads(p_c * inv_c, tq), ov_ref, t0, tq, n_sel), row0)


def _sel_slow(q_ref, kc_ref, vc_ref, ov_ref, oc_ref, sel_ref, *, tq, n_sel):
    rep, hd = NSA_REP, NSA_HEAD_DIM
    rows = rep * tq
    kc = kc_ref[0, 0, 0][:, 0:hd]
    ncp = kc.shape[0]
    n_idx = lax.broadcasted_iota(jnp.int32, (1, ncp), 1)
    for part in range(q_ref.shape[2] // tq):
        row0 = part * tq
        t0 = pl.program_id(2) * q_ref.shape[2] + row0
        q4 = q_ref[0, :, row0:row0 + tq, :].reshape(rows, hd)
        t_row = t0 + (lax.broadcasted_iota(jnp.int32, (rows, 1), 0) & (tq - 1))
        s_c = lax.dot_general(q4, kc, _NT, preferred_element_type=F32)
        p_c = _softmax_rows(s_c, (n_idx * CMP_STRIDE + (CMP_BLOCK - 1) <= t_row) & (n_idx < ncp - 1))
        o_c = jnp.dot(p_c.astype(BF16), vc_ref[0, 0, 0], preferred_element_type=F32)[:, 0:hd]
        _store_heads(oc_ref, o_c, tq, row0)
        _store_selection(sel_ref, _select_blocks(_sum_heads(p_c, tq), ov_ref, t0, tq, n_sel), row0)


def _nsa_select_kernel(sh_ref, q_ref, kc_ref, vc_ref, ov_ref, oc_ref, sel_ref, *, tq, n_sel):
    data = (q_ref, kc_ref, vc_ref, ov_ref, oc_ref, sel_ref)

    @pl.when(sh_ref[3] > 0.5)
    def _():
        _sel_fast(sh_ref, *data, tq=tq, n_sel=n_sel)

    @pl.when(sh_ref[3] <= 0.5)
    def _():
        _sel_slow(*data, tq=tq, n_sel=n_sel)


def _att_fast(sh_ref, q_ref, ks_ref, vs_ref, kw_ref, vw_ref, gl_ref, gp_ref, oc_ref, sel_ref, o_ref,
              lhs_ref, acc_ref, pre_ref, gm_ref, lay_ref, *, tq, tk):
    rep, hd = NSA_REP, NSA_HEAD_DIM
    rows = rep * tq
    t0 = pl.program_id(2) * tq
    c_s, c_w = sh_ref[1], sh_ref[2]
    q4 = q_ref[0].reshape(rows, hd)
    t_q = t0 + lax.broadcasted_iota(jnp.int32, (tq, 1), 0)

    j_lane = lax.broadcasted_iota(jnp.int32, (1, hd), 1)
    shift = jnp.where((sel_ref[0, 0].astype(F32) > 0.5) & (j_lane * SEL_BLOCK < t0), c_s, NEG_BIG).astype(BF16)
    lhs_ref[:, 0:hd] = q4
    for r in range(rep):
        lhs_ref[r * tq:(r + 1) * tq, hd:2 * hd] = shift

    span = WINDOW + tq
    w0 = pl.multiple_of(jnp.maximum(t0 - WINDOW, 0), tq)
    kp = w0 + lax.broadcasted_iota(jnp.int32, (1, span), 1)
    bias_w = jnp.where((kp <= t_q) & (kp > t_q - WINDOW), c_w, NEG_BIG)
    s_w = lax.dot_general(q4, kw_ref[0, 0, pl.ds(w0, span), :], _NT, preferred_element_type=F32)
    p_w = jnp.exp(s_w.reshape(rep, tq, span) + bias_w[None]).reshape(rows, span)
    pre_ref[...] = jnp.dot(p_w.astype(BF16), vw_ref[0, 0, pl.ds(w0, span), :], preferred_element_type=F32)

    d0 = pl.multiple_of(t0, tq)
    kcol = t0 + lax.broadcasted_iota(jnp.int32, (1, tq), 1)
    bias_d = jnp.where(kcol <= t_q, c_s, NEG_BIG)
    s_d = lax.dot_general(q4, ks_ref[0, 0, pl.ds(d0, tq), :][:, 0:hd], _NT, preferred_element_type=F32)
    p_d = jnp.exp(s_d.reshape(rep, tq, tq) + bias_d[None]).reshape(rows, tq)
    acc_ref[...] = jnp.dot(p_d.astype(BF16), vs_ref[0, 0, pl.ds(d0, tq), :], preferred_element_type=F32)

    def key_tile(k0, width):
        s = lax.dot_general(lhs_ref[...], ks_ref[0, 0, pl.ds(k0, width), :], _NT, preferred_element_type=F32)
        acc_ref[...] += jnp.dot(jnp.exp(s).astype(BF16), vs_ref[0, 0, pl.ds(k0, width), :],
                                preferred_element_type=F32)

    done = 0
    for mult in KEY_TILE_MULTIPLES:
        width = mult * tk
        if width > ks_ref.shape[2]:
            continue
        left = t0 - done
        n = left // width if mult > 1 else (left + width - 1) // width

        def step(c, carry, width=width, base=done):
            key_tile(pl.multiple_of(base + c * width, tk), width)
            return carry

        lax.fori_loop(0, n, step, 0)
        done = done + n * width

    for idx, ol in ((0, acc_ref[...]), (1, pre_ref[...])):
        _to_lane_layout(lay_ref, idx, ol, tq)
        _to_lane_layout(lay_ref, 2 + idx, ol, tq, lane0=hd)
    _gate_maps(gl_ref, gp_ref, gm_ref)
    _gated_store(gm_ref, oc_ref, lay_ref, o_ref, with_sums=True)


def _att_slow(q_ref, ks_ref, vs_ref, kw_ref, vw_ref, gl_ref, gp_ref, oc_ref, sel_ref, o_ref,
              acc_ref, gm_ref, lay_ref, m_ref, l_ref, *, tq, tk):
    rep, hd = NSA_REP, NSA_HEAD_DIM
    rows = rep * tq
    t0 = pl.program_id(2) * tq
    q4 = q_ref[0].reshape(rows, hd)
    t_row = t0 + (lax.broadcasted_iota(jnp.int32, (rows, 1), 0) & (tq - 1))
    sel = sel_ref[0, 0]
    n_blk = sel.shape[1]

    m_ref[...] = jnp.full(m_ref.shape, NEG_BIG, F32)
    l_ref[...] = jnp.zeros(l_ref.shape, F32)
    acc_ref[...] = jnp.zeros(acc_ref.shape, F32)
    t_q = t0 + lax.broadcasted_iota(jnp.int32, (tq, 1), 0)

    def kv_step(c, carry):
        k0 = pl.multiple_of(c * tk, tk)
        kt = ks_ref[0, 0, pl.ds(k0, tk), :][:, 0:hd]
        s = lax.dot_general(q4, kt, _NT, preferred_element_type=F32).reshape(rep, tq, tk)
        kk = lax.broadcasted_iota(jnp.int32, (1, tk), 1)
        blk = (k0 >> 6) + (kk >> 6)
        expand = jnp.where(lax.broadcasted_iota(jnp.int32, (n_blk, 1), 0) == blk, 1.0, 0.0).astype(BF16)
        chosen = jnp.dot(sel, expand, preferred_element_type=F32)
        mask = ((chosen > 0.5) & (k0 + kk <= t_q))[None]
        s = jnp.where(mask, s, NEG_BIG)
        m_old = m_ref[...]
        m_new = jnp.maximum(m_old, jnp.max(s, axis=-1, keepdims=True))
        p = jnp.where(mask, jnp.exp(s - m_new), 0.0)
        alpha = jnp.exp(m_old - m_new)
        l_ref[...] = alpha * l_ref[...] + jnp.sum(p, axis=-1, keepdims=True)
        pv = jnp.dot(p.reshape(rows, tk).astype(BF16), vs_ref[0, 0, pl.ds(k0, tk), :], preferred_element_type=F32)
        acc_ref[...] = alpha.reshape(rows, 1) * acc_ref[...] + pv
        m_ref[...] = m_new
        return carry

    lax.fori_loop(0, (t0 + tq + tk - 1) // tk, kv_step, 0)
    o_s = acc_ref[:, 0:hd] / l_ref[...].reshape(rows, 1)

    span = WINDOW + tq
    w0 = pl.multiple_of(jnp.maximum(t0 - WINDOW, 0), tq)
    s_w = lax.dot_general(q4, kw_ref[0, 0, pl.ds(w0, span), :], _NT, preferred_element_type=F32)
    kp = w0 + lax.broadcasted_iota(jnp.int32, (1, span), 1)
    p_w = _softmax_rows(s_w, (kp <= t_row) & (kp > t_row - WINDOW))
    o_w = jnp.dot(p_w.astype(BF16), vw_ref[0, 0, pl.ds(w0, span), :], preferred_element_type=F32)[:, 0:hd]

    _to_lane_layout(lay_ref, 0, o_s, tq)
    _to_lane_layout(lay_ref, 1, o_w, tq)
    _gate_maps(gl_ref, gp_ref, gm_ref)
    _gated_store(gm_ref, oc_ref, lay_ref, o_ref, with_sums=False)


def _nsa_attend_kernel(sh_ref, q_ref, ks_ref, vs_ref, kw_ref, vw_ref, gl_ref, gp_ref, oc_ref, sel_ref, o_ref,
                       lhs_ref, acc_ref, pre_ref, gm_ref, lay_ref, m_ref, l_ref, *, tq, tk):
    data = (q_ref, ks_ref, vs_ref, kw_ref, vw_ref, gl_ref, gp_ref, oc_ref, sel_ref, o_ref)

    @pl.when(sh_ref[3] > 0.5)
    def _():
        _att_fast(sh_ref, *data, lhs_ref, acc_ref, pre_ref, gm_ref, lay_ref, tq=tq, tk=tk)

    @pl.when(sh_ref[3] <= 0.5)
    def _():
        _att_slow(*data, acc_ref, gm_ref, lay_ref, m_ref, l_ref, tq=tq, tk=tk)


def _overlap_t(n_blk, ncp):
    c_start = np.arange(ncp) * CMP_STRIDE
    b_start = np.arange(n_blk) * SEL_BLOCK
    ov = ((c_start[None, :] < b_start[:, None] + SEL_BLOCK) & (b_start[:, None] < c_start[None, :] + CMP_BLOCK))
    ov[:, ncp - 1] = False
    return jnp.asarray(ov.astype(np.float32)).astype(BF16)


def _nsa_shifts(q_norm, k_norm):
    bound = (NSA_HEAD_DIM ** 0.5) * jnp.max(jnp.abs(q_norm)) * jnp.max(jnp.abs(k_norm), axis=-1)
    bound = bound.astype(BF16).astype(F32)
    fast = jnp.all(bound <= MAX_CONST_SHIFT).astype(F32)
    return jnp.concatenate([-bound, fast[None]])


def _nsa_select(shifts, q, cmp_kv, *, tq, parts):
    b, h, s, hd = q.shape
    g, rep = NSA_GROUPS, NSA_REP
    ncp = cmp_kv.shape[3]
    n_blk = s // SEL_BLOCK
    tb = tq * parts
    assert n_blk % 8 == 0 and n_blk <= hd and s % tb == 0
    kernel = functools.partial(_nsa_select_kernel, tq=tq, n_sel=min(N_SEL, n_blk))
    return pl.pallas_call(
        kernel,
        grid=(b, g, s // tb),
        in_specs=[
            pl.BlockSpec(memory_space=pltpu.SMEM),
            pl.BlockSpec((1, rep, tb, hd), lambda bi, gi, i: (bi, gi, i, 0)),
            pl.BlockSpec((1, 1, 1, ncp, 2 * hd), lambda bi, gi, i: (0, bi, gi, 0, 0)),
            pl.BlockSpec((1, 1, 1, ncp, 2 * hd), lambda bi, gi, i: (1, bi, gi, 0, 0)),
            pl.BlockSpec((n_blk, ncp), lambda bi, gi, i: (0, 0)),
        ],
        out_specs=[pl.BlockSpec((1, tb, rep * hd), lambda bi, gi, i: (bi, i, gi)),
                   pl.BlockSpec((1, 1, tb, hd), lambda bi, gi, i: (bi, gi, i, 0))],
        out_shape=[jax.ShapeDtypeStruct((b, s, h * hd), BF16), jax.ShapeDtypeStruct((b, g, s, hd), BF16)],
        compiler_params=_cparams("parallel", "parallel", "parallel"),
        name="nsa_select",
    )(shifts, q, cmp_kv, cmp_kv, _overlap_t(n_blk, ncp))


def _nsa_attend(shifts, q, ks, vs, kw, vw, z3, o_cmp, sel, *, tq, tk):
    b, h, s, hd = q.shape
    g, rep = NSA_GROUPS, NSA_REP
    rows = rep * tq
    assert WINDOW % tq == 0
    full = lambda w: pl.BlockSpec((1, 1, s, w), lambda bi, gi, i: (bi, gi, 0, 0))
    return pl.pallas_call(
        functools.partial(_nsa_attend_kernel, tq=tq, tk=tk),
        grid=(b, g, s // tq),
        in_specs=[
            pl.BlockSpec(memory_space=pltpu.SMEM),
            pl.BlockSpec((1, rep, tq, hd), lambda bi, gi, i: (bi, gi, i, 0)),
            full(2 * hd), full(2 * hd), full(hd), full(2 * hd),
            pl.BlockSpec((1, tq, LANES), lambda bi, gi, i: (bi, i, COL_SMALL // LANES + gi)),
            pl.BlockSpec((LANES, 3 * rep * hd), lambda bi, gi, i: (0, 0)),
            pl.BlockSpec((1, tq, rep * hd), lambda bi, gi, i: (bi, i, gi)),
            pl.BlockSpec((1, 1, tq, hd), lambda bi, gi, i: (bi, gi, i, 0)),
        ],
        out_specs=pl.BlockSpec((1, tq, rep * hd), lambda bi, gi, i: (bi, i, gi)),
        out_shape=jax.ShapeDtypeStruct((b, s, h * hd), BF16),
        scratch_shapes=[pltpu.VMEM((rows, 2 * hd), BF16), pltpu.VMEM((rows, 2 * hd), F32),
                        pltpu.VMEM((rows, 2 * hd), F32), pltpu.VMEM((tq, 3 * rep * hd), F32),
                        pltpu.VMEM((4, tq, rep * hd), F32),
                        pltpu.VMEM((rep, tq, 1), F32), pltpu.VMEM((rep, tq, 1), F32)],
        compiler_params=_cparams("parallel", "parallel", "arbitrary"),
        name="nsa_attend",
    )(shifts, q, ks, vs, kw, vw, z3, _gate_placement(), o_cmp, sel)


def _head_block_diag(x, n_rows, head_of_lane, dtype):
    return jnp.concatenate([jnp.where(head_of_lane == h, x, 0.0) for h in range(GLA_HEADS)], axis=0).astype(dtype)


def _gla_kernel(q_ref, k_ref, v_ref, r_ref, sm_ref, wg_ref, bg_ref, ng_ref, tril_ref, o_ref,
                st_ref, upd_ref, oin_ref, qin_ref, *, n_chunks):
    c, sub = GLA_CHUNK, GLA_SUB
    dk, dv, nh = GLA_HEAD_DK, GLA_HEAD_DV, GLA_HEADS
    nk, nv = nh * dk, nh * dv

    @pl.when(pl.program_id(1) == 0)
    def _():
        st_ref[...] = jnp.zeros(st_ref.shape, F32)

    sm = sm_ref[0]
    x = (jnp.dot(sm, wg_ref[0], preferred_element_type=F32) + jnp.dot(sm, wg_ref[1], preferred_element_type=F32)
         + bg_ref[...])
    log_a = (jnp.minimum(x, 0.0) - jnp.log1p(jnp.exp(-jnp.abs(x)))) / GLA_TAU
    la_hi = log_a.astype(BF16)
    la_lo = (log_a - la_hi.astype(F32)).astype(BF16)
    bcum_all = (jnp.dot(tril_ref[...], la_hi, preferred_element_type=F32)
                + jnp.dot(tril_ref[...], la_lo, preferred_element_type=F32))
    head_k = lax.broadcasted_iota(jnp.int32, (1, nk), 1) >> 6
    head_v = lax.broadcasted_iota(jnp.int32, (1, nv), 1) >> 7
    state_mask = (lax.broadcasted_iota(jnp.int32, (nv, 1), 0) >> 7) == head_k
    causal = ((lax.broadcasted_iota(jnp.int32, (c, nh * c), 1) & (c - 1))
              <= lax.broadcasted_iota(jnp.int32, (c, nh * c), 0))

    decays = []
    for cc in range(n_chunks):
        rs = slice(cc * c, (cc + 1) * c)
        bcum = bcum_all[rs]
        q = q_ref[0, rs].astype(F32) * (dk ** -0.5)
        k = k_ref[0, rs].astype(F32)
        v = v_ref[0, rs].astype(F32)
        b_last = bcum[c - 1:c, :]

        score_rows = []
        for i in range(c // sub):
            lo, hi = i * sub, (i + 1) * sub
            ref = bcum[lo - 1:lo, :] if i > 0 else jnp.zeros((1, nk), F32)
            q_i = (q[lo:hi] * jnp.exp(bcum[lo:hi] - ref)).astype(BF16)
            k_i = k[0:hi] * jnp.exp(ref - bcum[0:hi])
            if hi < c:
                k_i = jnp.concatenate([k_i, jnp.zeros((c - hi, nk), F32)], axis=0)
            score_rows.append(lax.dot_general(q_i, _head_block_diag(k_i, c, head_k, BF16), _NT,
                                              preferred_element_type=F32))
        a = jnp.where(causal, jnp.concatenate(score_rows, axis=0), 0.0).astype(BF16)
        oin_ref[rs] = jnp.dot(a, _head_block_diag(v, c, head_v, BF16), preferred_element_type=F32)
        qin_ref[rs] = (q * jnp.exp(bcum)).astype(BF16)
        k_out = (k * jnp.exp(b_last - bcum)).astype(BF16)
        upd = lax.dot_general(v.astype(BF16), k_out, (((0,), (0,)), ((), ())), preferred_element_type=F32)
        upd_ref[cc] = jnp.where(state_mask, upd, 0.0)
        decays.append(jnp.exp(b_last))

    for cc in range(n_chunks):
        rs = slice(cc * c, (cc + 1) * c)
        st = st_ref[...]
        o = oin_ref[rs] + lax.dot_general(qin_ref[rs], st.astype(BF16), _NT, preferred_element_type=F32)
        st_ref[...] = st * decays[cc] + upd_ref[cc]
        r_gate = r_ref[0, rs].astype(F32)
        for h in range(nh):
            vs_ = slice(h * dv, (h + 1) * dv)
            o_h = o[:, vs_]
            ms = jnp.mean(o_h * o_h, axis=-1, keepdims=True)
            rg = r_gate[:, vs_]
            o_ref[0, rs, vs_] = (o_h * lax.rsqrt(ms + NORM_EPS) * ng_ref[...] * (rg * jax.nn.sigmoid(rg))).astype(o_ref.dtype)


def _gla(z3, w_gate, b_gate, norm_g, *, n_chunks):
    b, s, _ = z3.shape
    c = GLA_CHUNK * n_chunks
    nk = GLA_HEADS * GLA_HEAD_DK
    nv = GLA_HEADS * GLA_HEAD_DV
    wg = jnp.zeros((LANES, nk), F32).at[SMALL_GLOW_LANE:SMALL_GLOW_LANE + GLA_RANK].set(w_gate.astype(F32))
    wg_hi = wg.astype(BF16)
    wg = jnp.stack([wg_hi, (wg - wg_hi.astype(F32)).astype(BF16)])
    idx = np.arange(c)
    tril = jnp.asarray(((idx[:, None] >= idx[None, :])
                        & (idx[:, None] // GLA_CHUNK == idx[None, :] // GLA_CHUNK)).astype(np.float32)).astype(BF16)
    const = lambda shape: pl.BlockSpec(shape, lambda bi, i: (0,) * len(shape))
    return pl.pallas_call(
        functools.partial(_gla_kernel, n_chunks=n_chunks),
        grid=(b, s // c),
        in_specs=[
            pl.BlockSpec((1, c, nk), lambda bi, i: (bi, i, COL_GQ // nk)),
            pl.BlockSpec((1, c, nk), lambda bi, i: (bi, i, COL_GK // nk)),
            pl.BlockSpec((1, c, nv), lambda bi, i: (bi, i, COL_GV // nv)),
            pl.BlockSpec((1, c, nv), lambda bi, i: (bi, i, COL_GR // nv)),
            pl.BlockSpec((1, c, LANES), lambda bi, i: (bi, i, COL_SMALL // LANES)),
            const((2, LANES, nk)), const((1, nk)), const((1, GLA_HEAD_DV)), const((c, c)),
        ],
        out_specs=pl.BlockSpec((1, c, nv), lambda bi, i: (bi, i, 0)),
        out_shape=jax.ShapeDtypeStruct((b, s, nv), BF16),
        scratch_shapes=[pltpu.VMEM((nv, nk), F32), pltpu.VMEM((n_chunks, nv, nk), F32),
                        pltpu.VMEM((c, nv), F32), pltpu.VMEM((c, nk), BF16)],
        compiler_params=_cparams("parallel", "arbitrary"),
        name="gla",
    )(z3, z3, z3, z3, z3, wg, b_gate.reshape(1, nk).astype(F32), norm_g.reshape(1, GLA_HEAD_DV).astype(F32), tril)


def _mem_attn_kernel(q_ref, k_ref, v_ref, qg_ref, kg_ref, o_ref):
    dh = MEM_HEAD_DIM
    nt = (((1,), (1,)), ((), ()))
    for h in range(MEM_HEADS):
        sl = slice(h * dh, (h + 1) * dh)
        q = q_ref[0, :, sl].astype(F32)
        q = q * lax.rsqrt(jnp.mean(q * q, axis=-1, keepdims=True) + NORM_EPS) * qg_ref[...] * (dh ** -0.5)
        k = k_ref[0, :, sl].astype(F32)
        k = k * lax.rsqrt(jnp.mean(k * k, axis=-1, keepdims=True) + NORM_EPS) * kg_ref[...]
        s = lax.dot_general(q.astype(BF16), k.astype(BF16), nt, preferred_element_type=F32)
        m = jnp.max(s, axis=-1, keepdims=True)
        p = jnp.exp(s - m)
        p = p / jnp.sum(p, axis=-1, keepdims=True)
        o = jnp.dot(p.astype(BF16), v_ref[0, :, sl].astype(BF16), preferred_element_type=F32)
        o_ref[0, :, sl] = o.astype(o_ref.dtype)


def _mem_attention(z3, kv, q_norm, k_norm, *, tq):
    b, s, _ = z3.shape
    m = kv.shape[1]
    w = MEM_HEADS * MEM_HEAD_DIM
    const = lambda shape: pl.BlockSpec(shape, lambda bi, i: (0,) * len(shape))
    return pl.pallas_call(
        _mem_attn_kernel,
        grid=(b, s // tq),
        in_specs=[
            pl.BlockSpec((1, tq, w), lambda bi, i: (bi, i, COL_MQ // w)),
            pl.BlockSpec((1, m, w), lambda bi, i: (bi, 0, 0)),
            pl.BlockSpec((1, m, w), lambda bi, i: (bi, 0, 1)),
            const((1, MEM_HEAD_DIM)), const((1, MEM_HEAD_DIM)),
        ],
        out_specs=pl.BlockSpec((1, tq, w), lambda bi, i: (bi, i, 0)),
        out_shape=jax.ShapeDtypeStruct((b, s, w), BF16),
        compiler_params=_cparams("parallel", "parallel"),
        name="mem_attention",
    )(z3, kv, kv, q_norm.reshape(1, MEM_HEAD_DIM).astype(F32), k_norm.reshape(1, MEM_HEAD_DIM).astype(F32))


def _merge_kernel(x_ref, on_ref, og_ref, om_ref, m0_ref, m1_ref, m2_ref, bm_ref, wb_ref, wo_ref, o_ref):
    merged = None
    for br, (ref, mg_ref) in enumerate(((on_ref, m0_ref), (og_ref, m1_ref), (om_ref, m2_ref))):
        y = jnp.dot(ref[...], wb_ref[br], preferred_element_type=F32)
        gate = jax.nn.sigmoid(mg_ref[...].astype(F32) + bm_ref[br:br + 1, :])
        merged = gate * y if merged is None else merged + gate * y
    o_ref[...] = x_ref[...] + jnp.dot(merged.astype(BF16), wo_ref[...], preferred_element_type=F32)


def _merge_out(x2, o_nsa, o_gla, o_mem, z2, b_merge, w_branch, w_out, layer, *, tm):
    t, d = x2.shape
    bw = BRANCH_WIDTH
    row = lambda w: pl.BlockSpec((tm, w), lambda i: (i, 0))
    gate_cols = lambda br: pl.BlockSpec((tm, d), lambda i: (i, COL_MERGE // d + br))
    return pl.pallas_call(
        _merge_kernel,
        grid=(t // tm,),
        in_specs=[
            row(d), row(bw), row(bw), row(bw),
            gate_cols(0), gate_cols(1), gate_cols(2),
            pl.BlockSpec((N_BRANCH, d), lambda i: (0, 0)),
            pl.BlockSpec((None, N_BRANCH, bw, d), lambda i: (layer, 0, 0, 0)),
            pl.BlockSpec((None, d, d), lambda i: (layer, 0, 0)),
        ],
        out_specs=row(d),
        out_shape=jax.ShapeDtypeStruct((t, d), F32),
        compiler_params=_cparams("parallel"),
        name="merge_out",
    )(x2, o_nsa, o_gla, o_mem, z2, z2, z2, b_merge.astype(F32), w_branch, w_out)


def _mlp_kernel(x_ref, g_ref, wu_ref, wd_ref, o_ref, h_ref):
    @pl.when(pl.program_id(1) == 0)
    def _():
        h_ref[...] = _rms_rows(x_ref[...], g_ref[...])
        o_ref[...] = x_ref[...]

    u = jnp.dot(h_ref[...], wu_ref[...], preferred_element_type=F32)
    u = jnp.square(jnp.maximum(u, 0.0)).astype(BF16)
    o_ref[...] += jnp.dot(u, wd_ref[...], preferred_element_type=F32)


def _mlp(x2, g, w_up, w_down, layer, *, tm, th):
    t, d = x2.shape
    hid = w_up.shape[2]
    return pl.pallas_call(
        _mlp_kernel,
        grid=(t // tm, hid // th),
        in_specs=[
            pl.BlockSpec((tm, d), lambda i, j: (i, 0)),
            pl.BlockSpec((1, d), lambda i, j: (0, 0)),
            pl.BlockSpec((None, d, th), lambda i, j: (layer, 0, j)),
            pl.BlockSpec((None, th, d), lambda i, j: (layer, j, 0)),
        ],
        out_specs=pl.BlockSpec((tm, d), lambda i, j: (i, 0)),
        out_shape=jax.ShapeDtypeStruct((t, d), F32),
        scratch_shapes=[pltpu.VMEM((tm, d), BF16)],
        compiler_params=_cparams("parallel", "arbitrary"),
        name="mlp",
    )(x2, g.reshape(1, d).astype(F32), w_up, w_down)


IN_PROJ_ROWS, IN_PROJ_COL_TILES = 1024, 2
NSA_PREP_ROWS = 256
NSA_QUERY_TILE, NSA_SELECT_PARTS, NSA_KEY_TILE = 256, 2, 512
KEY_TILE_MULTIPLES = (4, 2, 1)
GLA_CHUNKS_PER_STEP = 8
MEM_KV_ROWS, MEM_KV_COLS, MEM_QUERY_TILE = 512, 512, 512
MERGE_ROWS = 512
MLP_ROWS, MLP_HIDDEN_TILE = 1024, 1024


def _tile(n, pref):
    t = min(n, pref)
    assert n % t == 0, (n, pref)
    return t


def _layer(x, mem2, pos_rows, pos_cmp, p, big, layer):
    b, s, d = x.shape
    t = b * s
    hd = NSA_HEAD_DIM
    x2 = x.reshape(t, d)

    z2 = _norm_matmul(x2, p["ln_mix"].astype(F32), big["w_in"], layer, tm=_tile(t, IN_PROJ_ROWS),
                      tn=D_IN_PAD // IN_PROJ_COL_TILES, out_dtype=BF16, name="in_proj")
    z3 = z2.reshape(b, s, D_IN_PAD)

    q, ks, vs, kw, vw = _nsa_prep(z3, pos_rows, p["nsa_q_norm"].astype(F32), p["nsa_k_norm"][1].astype(F32),
                                  p["nsa_k_norm"][2].astype(F32), tp=_tile(s, NSA_PREP_ROWS))
    pe2 = jnp.pad(p["cmp_pe"].reshape(2, 2, CMP_STRIDE * hd), ((0, 0), (0, 6), (0, 0))).astype(F32)
    cmp_kv = _nsa_compress(z3, p["cmp_w1"], p["cmp_w2"], pe2, pos_cmp, p["nsa_k_norm"][0].astype(F32))
    shifts = _nsa_shifts(p["nsa_q_norm"].astype(F32), p["nsa_k_norm"].astype(F32))
    tq = _tile(s, NSA_QUERY_TILE)
    o_cmp, sel = _nsa_select(shifts, q, cmp_kv, tq=tq, parts=NSA_SELECT_PARTS)
    o_nsa = _nsa_attend(shifts, q, ks, vs, kw, vw, z3, o_cmp, sel, tq=tq, tk=_tile(s, NSA_KEY_TILE))

    o_gla = _gla(z3, p["gla_w_gate"], p["gla_b_gate"], p["gla_norm"], n_chunks=GLA_CHUNKS_PER_STEP)

    kv = _norm_matmul(mem2, p["mem_norm"].astype(F32), big["mem_w_kv"], layer,
                      tm=_tile(mem2.shape[0], MEM_KV_ROWS), tn=MEM_KV_COLS, out_dtype=BF16, name="mem_kv")
    kv = kv.reshape(b, mem2.shape[0] // b, 2 * MEM_HEADS * MEM_HEAD_DIM)
    o_mem = _mem_attention(z3, kv, p["mem_q_norm"], p["mem_k_norm"], tq=_tile(s, MEM_QUERY_TILE))

    x2 = _merge_out(x2, o_nsa.reshape(t, -1), o_gla.reshape(t, -1), o_mem.reshape(t, -1), z2,
                    p["b_merge"], big["w_branch"], big["w_out"], layer, tm=_tile(t, MERGE_ROWS))
    x2 = _mlp(x2, p["ln_mlp"], big["w_up"], big["w_down"], layer, tm=_tile(t, MLP_ROWS), th=MLP_HIDDEN_TILE)
    return x2.reshape(b, s, d)


def kernel(x, mem, positions, ln_mix, w_in, b_merge, nsa_q_norm, nsa_k_norm, cmp_pe, cmp_w1, cmp_w2,
           gla_w_gate, gla_b_gate, gla_norm, mem_norm, mem_w_kv, mem_q_norm, mem_k_norm, w_branch, w_out,
           ln_mlp, w_up, w_down):
    b, s, d = x.shape
    assert d == 1024 and s % WINDOW == 0 and s >= 2 * WINDOW
    depth = w_in.shape[0]
    perm_np, d_in = _in_proj_permutation(d)
    assert w_in.shape[2] == d_in
    w_in = _relayout_in_proj(w_in.astype(BF16), [int(c) for c in perm_np], d_in)
    half = CMP_STRIDE * NSA_HEAD_DIM
    cmp_w1 = jnp.concatenate([cmp_w1[:, :, :half], cmp_w1[:, :, half:]], axis=-1).astype(BF16)
    cmp_w2, mem_w_kv, w_branch, w_out, w_up, w_down = (
        a.astype(BF16) for a in (cmp_w2, mem_w_kv, w_branch, w_out, w_up, w_down))
    pos3 = positions.astype(jnp.int32).reshape(b, s, 1)
    nc = s // CMP_STRIDE
    cmp_end = np.minimum(np.arange(nc) * CMP_STRIDE + CMP_BLOCK - 1, s - 1)
    pos_cmp = pos3[:, cmp_end, :]
    pos_rows = positions.astype(jnp.int32).reshape(b, 1, s)
    mem2 = mem.reshape(b * mem.shape[1], d)
    names = ("ln_mix", "b_merge", "nsa_q_norm", "nsa_k_norm", "cmp_pe", "cmp_w1", "cmp_w2",
             "gla_w_gate", "gla_b_gate", "gla_norm", "mem_norm", "mem_q_norm", "mem_k_norm", "ln_mlp")
    stacked = (ln_mix, b_merge, nsa_q_norm, nsa_k_norm, cmp_pe, cmp_w1, cmp_w2, gla_w_gate, gla_b_gate,
               gla_norm, mem_norm, mem_q_norm, mem_k_norm, ln_mlp)
    big = dict(w_in=w_in, mem_w_kv=mem_w_kv, w_branch=w_branch, w_out=w_out, w_up=w_up, w_down=w_down)
    for l in range(depth):
        x = _layer(x, mem2, pos_rows, pos_cmp, {n: a[l] for n, a in zip(names, stacked)}, big, l)
    return x
```

```python
import functools

import numpy as np
import jax
import jax.numpy as jnp
from jax import lax
from jax.experimental import pallas as pl
from jax.experimental.pallas import tpu as pltpu

NSA_HEADS = 8
NSA_GROUPS = 2
NSA_REP = NSA_HEADS // NSA_GROUPS
NSA_HEAD_DIM = 64
CMP_BLOCK = 32
CMP_STRIDE = 16
CMP_HIDDEN = 4 * NSA_HEAD_DIM
SEL_BLOCK = 64
N_SEL = 16
WINDOW = 512
FORCE_SCORE = 1e4
GLA_HEADS = 4
GLA_HEAD_DK = 64
GLA_HEAD_DV = 128
GLA_RANK = 16
GLA_TAU = 16.0
GLA_CHUNK = 64
GLA_SUB = 16
MEM_HEADS = 4
MEM_HEAD_DIM = 128
N_BRANCH = 3
BRANCH_WIDTH = 512
ROPE_THETA = 500000.0
ROPE_ROT = NSA_HEAD_DIM // 4
ROPE_HALF = ROPE_ROT // 2
NORM_EPS = 1e-6

LANES = 128
VMEM_LIMIT_BYTES = 48 * 1024 * 1024

F32 = jnp.float32
BF16 = jnp.bfloat16
HIGHEST = lax.Precision.HIGHEST
NEG_BIG = -1e30
MAX_CONST_SHIFT = 40.0

COL_NQ = 0
COL_GV = 512
COL_GR = 1024
COL_MQ = 1536
COL_MERGE = 2048
COL_GQ = 5120
COL_GK = 5376
COL_KV = 5632
COL_SMALL = 6400
D_IN_PAD = 6656
SMALL_GLOW_LANE = 12


def _in_proj_permutation(d_model):
    sizes = (512, 128, 128, 128, 128, 128, 128, 24, 256, 256, 512, 512, 16, 512, 3 * d_model)
    off = np.concatenate([[0], np.cumsum(sizes)])
    (o_nq, o_kc, o_vc, o_ks, o_vs, o_kw, o_vw, o_ng, o_gq, o_gk, o_gv, o_gr, o_gl, o_mq, o_mg) = off[:-1]
    d_in = int(off[-1])
    perm = np.full((D_IN_PAD,), d_in, np.int32)

    def put(new, old, n):
        perm[new:new + n] = np.arange(old, old + n)

    put(COL_NQ, o_nq, 512)
    put(COL_GV, o_gv, 512)
    put(COL_GR, o_gr, 512)
    put(COL_MQ, o_mq, 512)
    put(COL_MERGE, o_mg, 3 * d_model)
    put(COL_GQ, o_gq, 256)
    put(COL_GK, o_gk, 256)
    put(COL_KV, o_kc, 768)
    for g in range(NSA_GROUPS):
        for br in range(3):
            for r in range(NSA_REP):
                perm[COL_SMALL + g * LANES + br * NSA_REP + r] = o_ng + (g * NSA_REP + r) * 3 + br
    put(COL_SMALL + SMALL_GLOW_LANE, o_gl, GLA_RANK)
    return perm, d_in


def _relayout_in_proj(w, perm, d_in):
    pieces, start = [], 0
    for i in range(1, len(perm) + 1):
        prev = perm[i - 1]
        if i < len(perm) and (perm[i] == prev + 1 if prev != d_in else perm[i] == d_in):
            continue
        n = i - start
        if prev == d_in:
            pieces.append(jnp.zeros(w.shape[:-1] + (n,), w.dtype))
        else:
            pieces.append(w[..., perm[start]:perm[start] + n])
        start = i
    return jnp.concatenate(pieces, axis=-1)


def _cparams(*sem):
    return pltpu.CompilerParams(dimension_semantics=sem, vmem_limit_bytes=VMEM_LIMIT_BYTES)


def _rms_rows(x, g):
    ms = jnp.mean(x * x, axis=-1, keepdims=True)
    return (x * lax.rsqrt(ms + NORM_EPS) * g).astype(BF16)


def _norm_matmul_kernel(x_ref, g_ref, w_ref, o_ref):
    h = _rms_rows(x_ref[...].astype(F32), g_ref[...])
    o_ref[...] = jnp.dot(h, w_ref[...], preferred_element_type=F32).astype(o_ref.dtype)


def _norm_matmul(x, g, w, layer, *, tm, tn, out_dtype, name):
    m, k = x.shape
    n = w.shape[2]
    return pl.pallas_call(
        _norm_matmul_kernel,
        grid=(n // tn, m // tm),
        in_specs=[
            pl.BlockSpec((tm, k), lambda j, i: (i, 0)),
            pl.BlockSpec((1, k), lambda j, i: (0, 0)),
            pl.BlockSpec((None, k, tn), lambda j, i: (layer, 0, j)),
        ],
        out_specs=pl.BlockSpec((tm, tn), lambda j, i: (i, j)),
        out_shape=jax.ShapeDtypeStruct((m, n), out_dtype),
        compiler_params=_cparams("parallel", "parallel"),
        name=name,
    )(x, g.reshape(1, k), w)


def _rope_tables(pos_row, freq_col, place):
    ang = freq_col * pos_row
    tn = (((0,), (0,)), ((), ()))
    lane = lax.broadcasted_iota(jnp.int32, (1, LANES), 1)
    c = lax.dot_general(jnp.cos(ang), place, tn, preferred_element_type=F32, precision=HIGHEST)
    s = lax.dot_general(jnp.sin(ang), place, tn, preferred_element_type=F32, precision=HIGHEST)
    return c + jnp.where((lane & (NSA_HEAD_DIM - 1)) >= ROPE_ROT, 1.0, 0.0), s


def _rope_lanes(x, c, s):
    n = x.shape[-1]
    if n > LANES:
        c = jnp.concatenate([c] * (n // LANES), axis=1)
        s = jnp.concatenate([s] * (n // LANES), axis=1)
    lane = lax.broadcasted_iota(jnp.int32, (1, n), 1) & (NSA_HEAD_DIM - 1)
    up = pltpu.roll(x, n - ROPE_HALF, axis=1)
    dn = pltpu.roll(x, ROPE_HALF, axis=1)
    y = jnp.where(lane < ROPE_HALF, -up, jnp.where(lane < ROPE_ROT, dn, 0.0))
    return x * c + y * s


def _head_rms(x, bd, g):
    x2 = x * x
    hi = x2.astype(BF16)
    lo = (x2 - hi.astype(F32)).astype(BF16)
    ms = (jnp.dot(hi, bd, preferred_element_type=F32) + jnp.dot(lo, bd, preferred_element_type=F32))
    return x * lax.rsqrt(ms * (1.0 / NSA_HEAD_DIM) + NORM_EPS) * g


def _nsa_prep_kernel(q_ref, ks_ref, vs_ref, kw_ref, vw_ref, pos_ref, qg_ref, ksg_ref, kwg_ref,
                     bdq_ref, bdk_ref, f_ref, place_ref,
                     qo_ref, kso_ref, vso_ref, kwo_ref, vwo_ref):
    hd = NSA_HEAD_DIM
    tp = q_ref.shape[1]
    c, s = _rope_tables(pos_ref[0].astype(F32), f_ref[...], place_ref[...])
    q = _head_rms(q_ref[0].astype(F32), bdq_ref[...], qg_ref[...])
    q = (_rope_lanes(q, c, s) * (hd ** -0.5)).astype(qo_ref.dtype)
    for h in range(NSA_HEADS):
        qo_ref[0, h] = q[:, h * hd:(h + 1) * hd]
    ks = _rope_lanes(_head_rms(ks_ref[0].astype(F32), bdk_ref[...], ksg_ref[...]), c, s)
    kw = _rope_lanes(_head_rms(kw_ref[0].astype(F32), bdk_ref[...], kwg_ref[...]), c, s)
    vs = vs_ref[0]
    vw = vw_ref[0]
    tok = pl.program_id(1) * tp + lax.broadcasted_iota(jnp.int32, (tp, hd), 0)
    lane = lax.broadcasted_iota(jnp.int32, (tp, hd), 1)
    blk_onehot = jnp.where((tok >> 6) == lane, 1.0, 0.0).astype(kso_ref.dtype)
    ones_col = jnp.ones((tp, hd), vso_ref.dtype)
    for g in range(NSA_GROUPS):
        sl = slice(g * hd, (g + 1) * hd)
        kso_ref[0, g, :, 0:hd] = ks[:, sl].astype(kso_ref.dtype)
        kso_ref[0, g, :, hd:2 * hd] = blk_onehot
        kwo_ref[0, g] = kw[:, sl].astype(kwo_ref.dtype)
        vso_ref[0, g, :, 0:hd] = vs[:, sl].astype(vso_ref.dtype)
        vso_ref[0, g, :, hd:2 * hd] = ones_col
        vwo_ref[0, g, :, 0:hd] = vw[:, sl].astype(vwo_ref.dtype)
        vwo_ref[0, g, :, hd:2 * hd] = ones_col


def _block_diag_ones(n, width):
    i = np.arange(n)
    return jnp.asarray((i[:, None] // width == i[None, :] // width).astype(np.float32)).astype(BF16)


def _rope_freq_col():
    inv = ROPE_THETA ** (-(np.arange(ROPE_HALF, dtype=np.float64)) / ROPE_HALF)
    return jnp.asarray(inv.astype(np.float32)).reshape(ROPE_HALF, 1)


def _rope_placement():
    m = np.zeros((ROPE_HALF, LANES), np.float32)
    for f in range(ROPE_HALF):
        for base in range(0, LANES, NSA_HEAD_DIM):
            m[f, base + f] = 1.0
            m[f, base + f + ROPE_HALF] = 1.0
    return jnp.asarray(m)


def _nsa_prep(z3, pos_rows, q_norm, ks_norm, kw_norm, *, tp):
    b, s, _ = z3.shape
    hd, g = NSA_HEAD_DIM, NSA_GROUPS
    assert SEL_BLOCK == 64 and s // SEL_BLOCK <= hd
    kvb = COL_KV // LANES
    qg = jnp.tile(q_norm, NSA_HEADS).reshape(1, NSA_HEADS * hd)
    ksg = jnp.tile(ks_norm, g).reshape(1, g * hd)
    kwg = jnp.tile(kw_norm, g).reshape(1, g * hd)
    const = lambda shape: pl.BlockSpec(shape, lambda bi, i: (0,) * len(shape))
    plain_out = jax.ShapeDtypeStruct((b, g, s, hd), BF16)
    plain_spec = pl.BlockSpec((1, g, tp, hd), lambda bi, i: (bi, 0, i, 0))
    aug_out = jax.ShapeDtypeStruct((b, g, s, 2 * hd), BF16)
    aug_spec = pl.BlockSpec((1, g, tp, 2 * hd), lambda bi, i: (bi, 0, i, 0))
    return pl.pallas_call(
        _nsa_prep_kernel,
        grid=(b, s // tp),
        in_specs=[
            pl.BlockSpec((1, tp, 512), lambda bi, i: (bi, i, COL_NQ // 512)),
            pl.BlockSpec((1, tp, LANES), lambda bi, i: (bi, i, kvb + 2)),
            pl.BlockSpec((1, tp, LANES), lambda bi, i: (bi, i, kvb + 3)),
            pl.BlockSpec((1, tp, LANES), lambda bi, i: (bi, i, kvb + 4)),
            pl.BlockSpec((1, tp, LANES), lambda bi, i: (bi, i, kvb + 5)),
            pl.BlockSpec((1, 1, tp), lambda bi, i: (bi, 0, i)),
            const((1, 512)), const((1, LANES)), const((1, LANES)),
            const((512, 512)), const((LANES, LANES)),
            const((ROPE_HALF, 1)), const((ROPE_HALF, LANES)),
        ],
        out_specs=[
            pl.BlockSpec((1, NSA_HEADS, tp, hd), lambda bi, i: (bi, 0, i, 0)),
            aug_spec, aug_spec, plain_spec, aug_spec,
        ],
        out_shape=[jax.ShapeDtypeStruct((b, NSA_HEADS, s, hd), BF16), aug_out, aug_out, plain_out, aug_out],
        compiler_params=_cparams("parallel", "parallel"),
        name="nsa_prep",
    )(z3, z3, z3, z3, z3, pos_rows, qg, ksg, kwg,
      _block_diag_ones(512, hd), _block_diag_ones(LANES, hd), _rope_freq_col(), _rope_placement())


def _gelu_tanh(x):
    return 0.5 * x * (1.0 + jnp.tanh(np.sqrt(2.0 / np.pi) * (x + 0.044715 * x * x * x)))


def _nsa_cmp_kernel(x_ref, w1_ref, w2_ref, pe_ref, pos_ref, g_ref, rot_ref, f_ref, o_ref, xs_ref):
    kind = pl.program_id(0)
    hd = NSA_HEAD_DIM
    nc = x_ref.shape[1] // CMP_STRIDE
    xs_ref[...] = x_ref[0].astype(F32)
    r = jnp.dot(pe_ref[0].astype(BF16), w1_ref[0], preferred_element_type=F32)
    ab = [jnp.zeros((nc, 2 * CMP_HIDDEN), F32) for _ in range(NSA_GROUPS)]
    for p in range(CMP_STRIDE):
        xp = xs_ref[pl.ds(p, nc, stride=CMP_STRIDE), :].astype(BF16)
        w1p = w1_ref[0, p * hd:(p + 1) * hd, :]
        for g in range(NSA_GROUPS):
            ab[g] = ab[g] + jnp.dot(xp[:, g * hd:(g + 1) * hd], w1p, preferred_element_type=F32)

    for g in range(NSA_GROUPS):
        a = ab[g][:, :CMP_HIDDEN] + r[0:1, :CMP_HIDDEN]
        bm = ab[g][:, CMP_HIDDEN:] + r[1:2, CMP_HIDDEN:]
        hid = _gelu_tanh(a + pltpu.roll(bm, nc - 1, axis=0))
        comp = jnp.dot(hid.astype(BF16), w2_ref[0], preferred_element_type=F32)

        @pl.when(kind == 0)
        def _():
            ms = jnp.mean(comp * comp, axis=-1, keepdims=True)
            kn = comp * lax.rsqrt(ms + NORM_EPS) * g_ref[...]
            ang = pos_ref[0].astype(F32) * f_ref[...]
            y = jnp.dot(kn, rot_ref[...], preferred_element_type=F32, precision=HIGHEST)
            o_ref[0, 0, g, :, 0:hd] = (kn * jnp.cos(ang) + y * jnp.sin(ang)).astype(o_ref.dtype)
            o_ref[0, 0, g, :, hd:2 * hd] = jnp.zeros((nc, hd), o_ref.dtype)

        @pl.when(kind != 0)
        def _():
            o_ref[0, 0, g, :, 0:hd] = comp.astype(o_ref.dtype)
            o_ref[0, 0, g, :, hd:2 * hd] = jnp.ones((nc, hd), o_ref.dtype)


def _rope_freq_head():
    p = np.arange(NSA_HEAD_DIM)
    inv = ROPE_THETA ** (-(np.arange(ROPE_HALF, dtype=np.float64)) / ROPE_HALF)
    f = np.where(p < ROPE_ROT, inv[p % ROPE_HALF], 0.0)
    return jnp.asarray(f.astype(np.float32)).reshape(1, NSA_HEAD_DIM)


def _rope_rot_matrix(n):
    m = np.zeros((n, n), np.float32)
    for base in range(0, n, NSA_HEAD_DIM):
        for l in range(ROPE_HALF):
            m[base + l + ROPE_HALF, base + l] = -1.0
            m[base + l, base + l + ROPE_HALF] = 1.0
    return jnp.asarray(m)


def _nsa_compress(z3, w1cat, w2, pe2, pos_cmp, k_norm0):
    b, s, _ = z3.shape
    hd, g = NSA_HEAD_DIM, NSA_GROUPS
    nc = s // CMP_STRIDE
    assert g * hd == LANES
    const = lambda shape: pl.BlockSpec(shape, lambda k, bi: (0,) * len(shape))
    return pl.pallas_call(
        _nsa_cmp_kernel,
        grid=(2, b),
        in_specs=[
            pl.BlockSpec((1, s, LANES), lambda k, bi: (bi, 0, COL_KV // LANES + k)),
            pl.BlockSpec((1, CMP_STRIDE * hd, 2 * CMP_HIDDEN), lambda k, bi: (k, 0, 0)),
            pl.BlockSpec((1, CMP_HIDDEN, hd), lambda k, bi: (k, 0, 0)),
            pl.BlockSpec((1, 8, CMP_STRIDE * hd), lambda k, bi: (k, 0, 0)),
            pl.BlockSpec((1, nc, 1), lambda k, bi: (bi, 0, 0)),
            const((1, hd)), const((hd, hd)), const((1, hd)),
        ],
        out_specs=pl.BlockSpec((1, 1, g, nc, 2 * hd), lambda k, bi: (k, bi, 0, 0, 0)),
        out_shape=jax.ShapeDtypeStruct((2, b, g, nc, 2 * hd), BF16),
        scratch_shapes=[pltpu.VMEM((s, LANES), F32)],
        compiler_params=_cparams("parallel", "parallel"),
        name="nsa_compress",
    )(z3, w1cat, w2, pe2, pos_cmp, k_norm0.reshape(1, hd), _rope_rot_matrix(hd), _rope_freq_head())


def _softmax_rows(s, mask):
    s = jnp.where(mask, s, NEG_BIG)
    m = jnp.max(s, axis=-1, keepdims=True)
    p = jnp.where(mask, jnp.exp(s - m), 0.0)
    d = jnp.sum(p, axis=-1, keepdims=True)
    return p / jnp.where(d > 0, d, 1.0)


_NT = (((1,), (1,)), ((), ()))


def _select_blocks(p_sum, ov_ref, t0, tq, n_sel):
    n_blk = ov_ref.shape[0]
    p_hi = p_sum.astype(BF16)
    p_lo = (p_sum - p_hi.astype(F32)).astype(BF16)
    imp_t = (lax.dot_general(ov_ref[...], p_hi, _NT, preferred_element_type=F32)
             + lax.dot_general(ov_ref[...], p_lo, _NT, preferred_element_type=F32))
    j_col = lax.broadcasted_iota(jnp.int32, (n_blk, 1), 0)
    t_lane = t0 + lax.broadcasted_iota(jnp.int32, (1, tq), 1)
    causal = j_col * SEL_BLOCK <= t_lane
    cur = t_lane >> 6
    forced = causal & ((j_col == 0) | (j_col == cur) | (j_col == cur - 1))
    score = jnp.where(forced, FORCE_SCORE, jnp.where(causal, imp_t, -FORCE_SCORE))
    ng = n_blk // 8
    groups = [score[8 * v:8 * v + 8] for v in range(ng)]
    ranks = [jnp.zeros((8, tq), F32) for _ in range(ng)]
    sub = lax.broadcasted_iota(jnp.int32, (8, tq), 0)
    for jp in range(n_blk):
        row = jnp.broadcast_to(score[jp:jp + 1, :], (8, tq))
        vj = jp // 8
        for v in range(ng):
            if v < vj:
                ahead = row > groups[v]
            elif v > vj:
                ahead = row >= groups[v]
            else:
                ahead = (row > groups[v]) | ((row == groups[v]) & (sub > jp % 8))
            ranks[v] = ranks[v] + jnp.where(ahead, 1.0, 0.0)
    sel_t = jnp.where(jnp.concatenate(ranks, axis=0) < n_sel, 1.0, 0.0).astype(BF16)
    eye = (lax.broadcasted_iota(jnp.int32, (tq, tq), 0) == lax.broadcasted_iota(jnp.int32, (tq, tq), 1)).astype(BF16)
    return lax.dot_general(eye, sel_t, _NT, preferred_element_type=F32)


def _sum_heads(p, tq):
    out = p[0:tq]
    for r in range(1, NSA_REP):
        out = out + p[r * tq:(r + 1) * tq]
    return out


def _store_heads(o_ref, o, tq, row0):
    hd = NSA_HEAD_DIM
    for r in range(NSA_REP):
        o_ref[0, row0:row0 + tq, r * hd:(r + 1) * hd] = o[r * tq:(r + 1) * tq].astype(o_ref.dtype)


def _store_selection(sel_ref, sel, row0):
    tq, n_blk = sel.shape
    sel = sel.astype(sel_ref.dtype)
    if n_blk < NSA_HEAD_DIM:
        sel = jnp.concatenate([sel, jnp.zeros((tq, NSA_HEAD_DIM - n_blk), sel_ref.dtype)], axis=1)
    sel_ref[0, 0, row0:row0 + tq] = sel


def _gate_maps(gl_ref, gp_ref, gm_ref):
    gates = jax.nn.sigmoid(gl_ref[0].astype(F32))
    g_hi = gates.astype(BF16)
    g_lo = (gates - g_hi.astype(F32)).astype(BF16)
    gm_ref[...] = (jnp.dot(g_hi, gp_ref[...], preferred_element_type=F32)
                   + jnp.dot(g_lo, gp_ref[...], preferred_element_type=F32))


def _to_lane_layout(lay_ref, idx, o, tq, lane0=0):
    hd = NSA_HEAD_DIM
    for r in range(NSA_REP):
        lay_ref[idx, :, r * hd:(r + 1) * hd] = o[r * tq:(r + 1) * tq, lane0:lane0 + hd]


def _gated_store(gm_ref, oc_ref, lay_ref, o_ref, with_sums):
    w = NSA_REP * NSA_HEAD_DIM
    o_sel, o_win = lay_ref[0], lay_ref[1]
    if with_sums:
        o_sel, o_win = o_sel / lay_ref[2], o_win / lay_ref[3]
    out = gm_ref[:, 0:w] * oc_ref[0].astype(F32) + gm_ref[:, w:2 * w] * o_sel + gm_ref[:, 2 * w:3 * w] * o_win
    o_ref[0] = out.astype(o_ref.dtype)


def _gate_placement():
    m = np.zeros((LANES, 3 * NSA_REP * NSA_HEAD_DIM), np.float32)
    for br in range(3):
        for r in range(NSA_REP):
            c0 = br * NSA_REP * NSA_HEAD_DIM + r * NSA_HEAD_DIM
            m[br * NSA_REP + r, c0:c0 + NSA_HEAD_DIM] = 1.0
    return jnp.asarray(m).astype(BF16)


def _sel_fast(sh_ref, q_ref, kc_ref, vc_ref, ov_ref, oc_ref, sel_ref, *, tq, n_sel):
    rep, hd = NSA_REP, NSA_HEAD_DIM
    rows = rep * tq
    kc = kc_ref[0, 0, 0][:, 0:hd]
    ncp = kc.shape[0]
    n_idx = lax.broadcasted_iota(jnp.int32, (1, ncp), 1)
    for part in range(q_ref.shape[2] // tq):
        row0 = part * tq
        t0 = pl.program_id(2) * q_ref.shape[2] + row0
        q4 = q_ref[0, :, row0:row0 + tq, :].reshape(rows, hd)
        t_q = t0 + lax.broadcasted_iota(jnp.int32, (tq, 1), 0)
        valid_c = (n_idx * CMP_STRIDE + (CMP_BLOCK - 1) <= t_q) & (n_idx < ncp - 1)
        bias_c = jnp.where(valid_c, sh_ref[0], NEG_BIG)
        s_c = lax.dot_general(q4, kc, _NT, preferred_element_type=F32).reshape(rep, tq, ncp)
        p_c = jnp.exp(s_c + bias_c[None]).reshape(rows, ncp)
        ol_c = jnp.dot(p_c.astype(BF16), vc_ref[0, 0, 0], preferred_element_type=F32)
        inv_c = 1.0 / jnp.where(ol_c[:, hd:hd + 1] > 0, ol_c[:, hd:hd + 1], 1.0)
        _store_heads(oc_ref, ol_c[:, 0:hd] * inv_c, tq, row0)
        _store_selection(sel_ref, _select_blocks(_sum_heads(p_c * inv_c, tq), ov_ref, t0, tq, n_sel), row0)


def _sel_slow(q_ref, kc_ref, vc_ref, ov_ref, oc_ref, sel_ref, *, tq, n_sel):
    rep, hd = NSA_REP, NSA_HEAD_DIM
    rows = rep * tq
    kc = kc_ref[0, 0, 0][:, 0:hd]
    ncp = kc.shape[0]
    n_idx = lax.broadcasted_iota(jnp.int32, (1, ncp), 1)
    for part in range(q_ref.shape[2] // tq):
        row0 = part * tq
        t0 = pl.program_id(2) * q_ref.shape[2] + row0
        q4 = q_ref[0, :, row0:row0 + tq, :].reshape(rows, hd)
        t_row = t0 + (lax.broadcasted_iota(jnp.int32, (rows, 1), 0) & (tq - 1))
        s_c = lax.dot_general(q4, kc, _NT, preferred_element_type=F32)
        p_c = _softmax_rows(s_c, (n_idx * CMP_STRIDE + (CMP_BLOCK - 1) <= t_row) & (n_idx < ncp - 1))
        o_c = jnp.dot(p_c.astype(BF16), vc_ref[0, 0, 0], preferred_element_type=F32)[:, 0:hd]
        _store_heads(oc_ref, o_c, tq, row0)
        _store_selection(sel_ref, _select_blocks(_sum_heads(p_c, tq), ov_ref, t0, tq, n_sel), row0)


def _nsa_select_kernel(sh_ref, q_ref, kc_ref, vc_ref, ov_ref, oc_ref, sel_ref, *, tq, n_sel):
    data = (q_ref, kc_ref, vc_ref, ov_ref, oc_ref, sel_ref)

    @pl.when(sh_ref[3] > 0.5)
    def _():
        _sel_fast(sh_ref, *data, tq=tq, n_sel=n_sel)

    @pl.when(sh_ref[3] <= 0.5)
    def _():
        _sel_slow(*data, tq=tq, n_sel=n_sel)


def _att_fast(sh_ref, q_ref, ks_ref, vs_ref, kw_ref, vw_ref, gl_ref, gp_ref, oc_ref, sel_ref, o_ref,
              lhs_ref, acc_ref, pre_ref, gm_ref, lay_ref, *, tq, tk):
    rep, hd = NSA_REP, NSA_HEAD_DIM
    rows = rep * tq
    t0 = pl.program_id(2) * tq
    c_s, c_w = sh_ref[1], sh_ref[2]
    q4 = q_ref[0].reshape(rows, hd)
    t_q = t0 + lax.broadcasted_iota(jnp.int32, (tq, 1), 0)

    j_lane = lax.broadcasted_iota(jnp.int32, (1, hd), 1)
    shift = jnp.where((sel_ref[0, 0].astype(F32) > 0.5) & (j_lane * SEL_BLOCK < t0), c_s, NEG_BIG).astype(BF16)
    lhs_ref[:, 0:hd] = q4
    for r in range(rep):
        lhs_ref[r * tq:(r + 1) * tq, hd:2 * hd] = shift

    span = WINDOW + tq
    w0 = pl.multiple_of(jnp.maximum(t0 - WINDOW, 0), tq)
    kp = w0 + lax.broadcasted_iota(jnp.int32, (1, span), 1)
    bias_w = jnp.where((kp <= t_q) & (kp > t_q - WINDOW), c_w, NEG_BIG)
    s_w = lax.dot_general(q4, kw_ref[0, 0, pl.ds(w0, span), :], _NT, preferred_element_type=F32)
    p_w = jnp.exp(s_w.reshape(rep, tq, span) + bias_w[None]).reshape(rows, span)
    pre_ref[...] = jnp.dot(p_w.astype(BF16), vw_ref[0, 0, pl.ds(w0, span), :], preferred_element_type=F32)

    d0 = pl.multiple_of(t0, tq)
    kcol = t0 + lax.broadcasted_iota(jnp.int32, (1, tq), 1)
    bias_d = jnp.where(kcol <= t_q, c_s, NEG_BIG)
    s_d = lax.dot_general(q4, ks_ref[0, 0, pl.ds(d0, tq), :][:, 0:hd], _NT, preferred_element_type=F32)
    p_d = jnp.exp(s_d.reshape(rep, tq, tq) + bias_d[None]).reshape(rows, tq)
    acc_ref[...] = jnp.dot(p_d.astype(BF16), vs_ref[0, 0, pl.ds(d0, tq), :], preferred_element_type=F32)

    def key_tile(k0, width):
        s = lax.dot_general(lhs_ref[...], ks_ref[0, 0, pl.ds(k0, width), :], _NT, preferred_element_type=F32)
        acc_ref[...] += jnp.dot(jnp.exp(s).astype(BF16), vs_ref[0, 0, pl.ds(k0, width), :],
                                preferred_element_type=F32)

    done = 0
    for mult in KEY_TILE_MULTIPLES:
        width = mult * tk
        if width > ks_ref.shape[2]:
            continue
        left = t0 - done
        n = left // width if mult > 1 else (left + width - 1) // width

        def step(c, carry, width=width, base=done):
            key_tile(pl.multiple_of(base + c * width, tk), width)
            return carry

        lax.fori_loop(0, n, step, 0)
        done = done + n * width

    for idx, ol in ((0, acc_ref[...]), (1, pre_ref[...])):
        _to_lane_layout(lay_ref, idx, ol, tq)
        _to_lane_layout(lay_ref, 2 + idx, ol, tq, lane0=hd)
    _gate_maps(gl_ref, gp_ref, gm_ref)
    _gated_store(gm_ref, oc_ref, lay_ref, o_ref, with_sums=True)


def _att_slow(q_ref, ks_ref, vs_ref, kw_ref, vw_ref, gl_ref, gp_ref, oc_ref, sel_ref, o_ref,
              acc_ref, gm_ref, lay_ref, m_ref, l_ref, *, tq, tk):
    rep, hd = NSA_REP, NSA_HEAD_DIM
    rows = rep * tq
    t0 = pl.program_id(2) * tq
    q4 = q_ref[0].reshape(rows, hd)
    t_row = t0 + (lax.broadcasted_iota(jnp.int32, (rows, 1), 0) & (tq - 1))
    sel = sel_ref[0, 0]
    n_blk = sel.shape[1]

    m_ref[...] = jnp.full(m_ref.shape, NEG_BIG, F32)
    l_ref[...] = jnp.zeros(l_ref.shape, F32)
    acc_ref[...] = jnp.zeros(acc_ref.shape, F32)
    t_q = t0 + lax.broadcasted_iota(jnp.int32, (tq, 1), 0)

    def kv_step(c, carry):
        k0 = pl.multiple_of(c * tk, tk)
        kt = ks_ref[0, 0, pl.ds(k0, tk), :][:, 0:hd]
        s = lax.dot_general(q4, kt, _NT, preferred_element_type=F32).reshape(rep, tq, tk)
        kk = lax.broadcasted_iota(jnp.int32, (1, tk), 1)
        blk = (k0 >> 6) + (kk >> 6)
        expand = jnp.where(lax.broadcasted_iota(jnp.int32, (n_blk, 1), 0) == blk, 1.0, 0.0).astype(BF16)
        chosen = jnp.dot(sel, expand, preferred_element_type=F32)
        mask = ((chosen > 0.5) & (k0 + kk <= t_q))[None]
        s = jnp.where(mask, s, NEG_BIG)
        m_old = m_ref[...]
        m_new = jnp.maximum(m_old, jnp.max(s, axis=-1, keepdims=True))
        p = jnp.where(mask, jnp.exp(s - m_new), 0.0)
        alpha = jnp.exp(m_old - m_new)
        l_ref[...] = alpha * l_ref[...] + jnp.sum(p, axis=-1, keepdims=True)
        pv = jnp.dot(p.reshape(rows, tk).astype(BF16), vs_ref[0, 0, pl.ds(k0, tk), :], preferred_element_type=F32)
        acc_ref[...] = alpha.reshape(rows, 1) * acc_ref[...] + pv
        m_ref[...] = m_new
        return carry

    lax.fori_loop(0, (t0 + tq + tk - 1) // tk, kv_step, 0)
    o_s = acc_ref[:, 0:hd] / l_ref[...].reshape(rows, 1)

    span = WINDOW + tq
    w0 = pl.multiple_of(jnp.maximum(t0 - WINDOW, 0), tq)
    s_w = lax.dot_general(q4, kw_ref[0, 0, pl.ds(w0, span), :], _NT, preferred_element_type=F32)
    kp = w0 + lax.broadcasted_iota(jnp.int32, (1, span), 1)
    p_w = _softmax_rows(s_w, (kp <= t_row) & (kp > t_row - WINDOW))
    o_w = jnp.dot(p_w.astype(BF16), vw_ref[0, 0, pl.ds(w0, span), :], preferred_element_type=F32)[:, 0:hd]

    _to_lane_layout(lay_ref, 0, o_s, tq)
    _to_lane_layout(lay_ref, 1, o_w, tq)
    _gate_maps(gl_ref, gp_ref, gm_ref)
    _gated_store(gm_ref, oc_ref, lay_ref, o_ref, with_sums=False)


def _nsa_attend_kernel(sh_ref, q_ref, ks_ref, vs_ref, kw_ref, vw_ref, gl_ref, gp_ref, oc_ref, sel_ref, o_ref,
                       lhs_ref, acc_ref, pre_ref, gm_ref, lay_ref, m_ref, l_ref, *, tq, tk):
    data = (q_ref, ks_ref, vs_ref, kw_ref, vw_ref, gl_ref, gp_ref, oc_ref, sel_ref, o_ref)

    @pl.when(sh_ref[3] > 0.5)
    def _():
        _att_fast(sh_ref, *data, lhs_ref, acc_ref, pre_ref, gm_ref, lay_ref, tq=tq, tk=tk)

    @pl.when(sh_ref[3] <= 0.5)
    def _():
        _att_slow(*data, acc_ref, gm_ref, lay_ref, m_ref, l_ref, tq=tq, tk=tk)


def _overlap_t(n_blk, ncp):
    c_start = np.arange(ncp) * CMP_STRIDE
    b_start = np.arange(n_blk) * SEL_BLOCK
    ov = ((c_start[None, :] < b_start[:, None] + SEL_BLOCK) & (b_start[:, None] < c_start[None, :] + CMP_BLOCK))
    ov[:, ncp - 1] = False
    return jnp.asarray(ov.astype(np.float32)).astype(BF16)


def _nsa_shifts(q_norm, k_norm):
    bound = (NSA_HEAD_DIM ** 0.5) * jnp.max(jnp.abs(q_norm)) * jnp.max(jnp.abs(k_norm), axis=-1)
    bound = bound.astype(BF16).astype(F32)
    fast = jnp.all(bound <= MAX_CONST_SHIFT).astype(F32)
    return jnp.concatenate([-bound, fast[None]])


def _nsa_select(shifts, q, cmp_kv, *, tq, parts):
    b, h, s, hd = q.shape
    g, rep = NSA_GROUPS, NSA_REP
    ncp = cmp_kv.shape[3]
    n_blk = s // SEL_BLOCK
    tb = tq * parts
    assert n_blk % 8 == 0 and n_blk <= hd and s % tb == 0
    kernel = functools.partial(_nsa_select_kernel, tq=tq, n_sel=min(N_SEL, n_blk))
    return pl.pallas_call(
        kernel,
        grid=(b, g, s // tb),
        in_specs=[
            pl.BlockSpec(memory_space=pltpu.SMEM),
            pl.BlockSpec((1, rep, tb, hd), lambda bi, gi, i: (bi, gi, i, 0)),
            pl.BlockSpec((1, 1, 1, ncp, 2 * hd), lambda bi, gi, i: (0, bi, gi, 0, 0)),
            pl.BlockSpec((1, 1, 1, ncp, 2 * hd), lambda bi, gi, i: (1, bi, gi, 0, 0)),
            pl.BlockSpec((n_blk, ncp), lambda bi, gi, i: (0, 0)),
        ],
        out_specs=[pl.BlockSpec((1, tb, rep * hd), lambda bi, gi, i: (bi, i, gi)),
                   pl.BlockSpec((1, 1, tb, hd), lambda bi, gi, i: (bi, gi, i, 0))],
        out_shape=[jax.ShapeDtypeStruct((b, s, h * hd), BF16), jax.ShapeDtypeStruct((b, g, s, hd), BF16)],
        compiler_params=_cparams("parallel", "parallel", "parallel"),
        name="nsa_select",
    )(shifts, q, cmp_kv, cmp_kv, _overlap_t(n_blk, ncp))


def _nsa_attend(shifts, q, ks, vs, kw, vw, z3, o_cmp, sel, *, tq, tk):
    b, h, s, hd = q.shape
    g, rep = NSA_GROUPS, NSA_REP
    rows = rep * tq
    assert WINDOW % tq == 0
    full = lambda w: pl.BlockSpec((1, 1, s, w), lambda bi, gi, i: (bi, gi, 0, 0))
    return pl.pallas_call(
        functools.partial(_nsa_attend_kernel, tq=tq, tk=tk),
        grid=(b, g, s // tq),
        in_specs=[
            pl.BlockSpec(memory_space=pltpu.SMEM),
            pl.BlockSpec((1, rep, tq, hd), lambda bi, gi, i: (bi, gi, i, 0)),
            full(2 * hd), full(2 * hd), full(hd), full(2 * hd),
            pl.BlockSpec((1, tq, LANES), lambda bi, gi, i: (bi, i, COL_SMALL // LANES + gi)),
            pl.BlockSpec((LANES, 3 * rep * hd), lambda bi, gi, i: (0, 0)),
            pl.BlockSpec((1, tq, rep * hd), lambda bi, gi, i: (bi, i, gi)),
            pl.BlockSpec((1, 1, tq, hd), lambda bi, gi, i: (bi, gi, i, 0)),
        ],
        out_specs=pl.BlockSpec((1, tq, rep * hd), lambda bi, gi, i: (bi, i, gi)),
        out_shape=jax.ShapeDtypeStruct((b, s, h * hd), BF16),
        scratch_shapes=[pltpu.VMEM((rows, 2 * hd), BF16), pltpu.VMEM((rows, 2 * hd), F32),
                        pltpu.VMEM((rows, 2 * hd), F32), pltpu.VMEM((tq, 3 * rep * hd), F32),
                        pltpu.VMEM((4, tq, rep * hd), F32),
                        pltpu.VMEM((rep, tq, 1), F32), pltpu.VMEM((rep, tq, 1), F32)],
        compiler_params=_cparams("parallel", "parallel", "arbitrary"),
        name="nsa_attend",
    )(shifts, q, ks, vs, kw, vw, z3, _gate_placement(), o_cmp, sel)


def _head_block_diag(x, n_rows, head_of_lane, dtype):
    return jnp.concatenate([jnp.where(head_of_lane == h, x, 0.0) for h in range(GLA_HEADS)], axis=0).astype(dtype)


def _gla_kernel(q_ref, k_ref, v_ref, r_ref, sm_ref, wg_ref, bg_ref, ng_ref, tril_ref, o_ref,
                st_ref, upd_ref, oin_ref, qin_ref, *, n_chunks):
    c, sub = GLA_CHUNK, GLA_SUB
    dk, dv, nh = GLA_HEAD_DK, GLA_HEAD_DV, GLA_HEADS
    nk, nv = nh * dk, nh * dv

    @pl.when(pl.program_id(1) == 0)
    def _():
        st_ref[...] = jnp.zeros(st_ref.shape, F32)

    sm = sm_ref[0]
    x = (jnp.dot(sm, wg_ref[0], preferred_element_type=F32) + jnp.dot(sm, wg_ref[1], preferred_element_type=F32)
         + bg_ref[...])
    log_a = (jnp.minimum(x, 0.0) - jnp.log1p(jnp.exp(-jnp.abs(x)))) / GLA_TAU
    la_hi = log_a.astype(BF16)
    la_lo = (log_a - la_hi.astype(F32)).astype(BF16)
    bcum_all = (jnp.dot(tril_ref[...], la_hi, preferred_element_type=F32)
                + jnp.dot(tril_ref[...], la_lo, preferred_element_type=F32))
    head_k = lax.broadcasted_iota(jnp.int32, (1, nk), 1) >> 6
    head_v = lax.broadcasted_iota(jnp.int32, (1, nv), 1) >> 7
    state_mask = (lax.broadcasted_iota(jnp.int32, (nv, 1), 0) >> 7) == head_k
    causal = ((lax.broadcasted_iota(jnp.int32, (c, nh * c), 1) & (c - 1))
              <= lax.broadcasted_iota(jnp.int32, (c, nh * c), 0))

    decays = []
    for cc in range(n_chunks):
        rs = slice(cc * c, (cc + 1) * c)
        bcum = bcum_all[rs]
        q = q_ref[0, rs].astype(F32) * (dk ** -0.5)
        k = k_ref[0, rs].astype(F32)
        v = v_ref[0, rs].astype(F32)
        b_last = bcum[c - 1:c, :]

        score_rows = []
        for i in range(c // sub):
            lo, hi = i * sub, (i + 1) * sub
            ref = bcum[lo - 1:lo, :] if i > 0 else jnp.zeros((1, nk), F32)
            q_i = (q[lo:hi] * jnp.exp(bcum[lo:hi] - ref)).astype(BF16)
            k_i = k[0:hi] * jnp.exp(ref - bcum[0:hi])
            if hi < c:
                k_i = jnp.concatenate([k_i, jnp.zeros((c - hi, nk), F32)], axis=0)
            score_rows.append(lax.dot_general(q_i, _head_block_diag(k_i, c, head_k, BF16), _NT,
                                              preferred_element_type=F32))
        a = jnp.where(causal, jnp.concatenate(score_rows, axis=0), 0.0).astype(BF16)
        oin_ref[rs] = jnp.dot(a, _head_block_diag(v, c, head_v, BF16), preferred_element_type=F32)
        qin_ref[rs] = (q * jnp.exp(bcum)).astype(BF16)
        k_out = (k * jnp.exp(b_last - bcum)).astype(BF16)
        upd = lax.dot_general(v.astype(BF16), k_out, (((0,), (0,)), ((), ())), preferred_element_type=F32)
        upd_ref[cc] = jnp.where(state_mask, upd, 0.0)
        decays.append(jnp.exp(b_last))

    for cc in range(n_chunks):
        rs = slice(cc * c, (cc + 1) * c)
        st = st_ref[...]
        o = oin_ref[rs] + lax.dot_general(qin_ref[rs], st.astype(BF16), _NT, preferred_element_type=F32)
        st_ref[...] = st * decays[cc] + upd_ref[cc]
        r_gate = r_ref[0, rs].astype(F32)
        for h in range(nh):
            vs_ = slice(h * dv, (h + 1) * dv)
            o_h = o[:, vs_]
            ms = jnp.mean(o_h * o_h, axis=-1, keepdims=True)
            rg = r_gate[:, vs_]
            o_ref[0, rs, vs_] = (o_h * lax.rsqrt(ms + NORM_EPS) * ng_ref[...] * (rg * jax.nn.sigmoid(rg))).astype(o_ref.dtype)


def _gla(z3, w_gate, b_gate, norm_g, *, n_chunks):
    b, s, _ = z3.shape
    c = GLA_CHUNK * n_chunks
    nk = GLA_HEADS * GLA_HEAD_DK
    nv = GLA_HEADS * GLA_HEAD_DV
    wg = jnp.zeros((LANES, nk), F32).at[SMALL_GLOW_LANE:SMALL_GLOW_LANE + GLA_RANK].set(w_gate.astype(F32))
    wg_hi = wg.astype(BF16)
    wg = jnp.stack([wg_hi, (wg - wg_hi.astype(F32)).astype(BF16)])
    idx = np.arange(c)
    tril = jnp.asarray(((idx[:, None] >= idx[None, :])
                        & (idx[:, None] // GLA_CHUNK == idx[None, :] // GLA_CHUNK)).astype(np.float32)).astype(BF16)
    const = lambda shape: pl.BlockSpec(shape, lambda bi, i: (0,) * len(shape))
    return pl.pallas_call(
        functools.partial(_gla_kernel, n_chunks=n_chunks),
        grid=(b, s // c),
        in_specs=[
            pl.BlockSpec((1, c, nk), lambda bi, i: (bi, i, COL_GQ // nk)),
            pl.BlockSpec((1, c, nk), lambda bi, i: (bi, i, COL_GK // nk)),
            pl.BlockSpec((1, c, nv), lambda bi, i: (bi, i, COL_GV // nv)),
            pl.BlockSpec((1, c, nv), lambda bi, i: (bi, i, COL_GR // nv)),
            pl.BlockSpec((1, c, LANES), lambda bi, i: (bi, i, COL_SMALL // LANES)),
            const((2, LANES, nk)), const((1, nk)), const((1, GLA_HEAD_DV)), const((c, c)),
        ],
        out_specs=pl.BlockSpec((1, c, nv), lambda bi, i: (bi, i, 0)),
        out_shape=jax.ShapeDtypeStruct((b, s, nv), BF16),
        scratch_shapes=[pltpu.VMEM((nv, nk), F32), pltpu.VMEM((n_chunks, nv, nk), F32),
                        pltpu.VMEM((c, nv), F32), pltpu.VMEM((c, nk), BF16)],
        compiler_params=_cparams("parallel", "arbitrary"),
        name="gla",
    )(z3, z3, z3, z3, z3, wg, b_gate.reshape(1, nk).astype(F32), norm_g.reshape(1, GLA_HEAD_DV).astype(F32), tril)


def _mem_attn_kernel(q_ref, k_ref, v_ref, qg_ref, kg_ref, o_ref):
    dh = MEM_HEAD_DIM
    nt = (((1,), (1,)), ((), ()))
    for h in range(MEM_HEADS):
        sl = slice(h * dh, (h + 1) * dh)
        q = q_ref[0, :, sl].astype(F32)
        q = q * lax.rsqrt(jnp.mean(q * q, axis=-1, keepdims=True) + NORM_EPS) * qg_ref[...] * (dh ** -0.5)
        k = k_ref[0, :, sl].astype(F32)
        k = k * lax.rsqrt(jnp.mean(k * k, axis=-1, keepdims=True) + NORM_EPS) * kg_ref[...]
        s = lax.dot_general(q.astype(BF16), k.astype(BF16), nt, preferred_element_type=F32)
        m = jnp.max(s, axis=-1, keepdims=True)
        p = jnp.exp(s - m)
        p = p / jnp.sum(p, axis=-1, keepdims=True)
        o = jnp.dot(p.astype(BF16), v_ref[0, :, sl].astype(BF16), preferred_element_type=F32)
        o_ref[0, :, sl] = o.astype(o_ref.dtype)


def _mem_attention(z3, kv, q_norm, k_norm, *, tq):
    b, s, _ = z3.shape
    m = kv.shape[1]
    w = MEM_HEADS * MEM_HEAD_DIM
    const = lambda shape: pl.BlockSpec(shape, lambda bi, i: (0,) * len(shape))
    return pl.pallas_call(
        _mem_attn_kernel,
        grid=(b, s // tq),
        in_specs=[
            pl.BlockSpec((1, tq, w), lambda bi, i: (bi, i, COL_MQ // w)),
            pl.BlockSpec((1, m, w), lambda bi, i: (bi, 0, 0)),
            pl.BlockSpec((1, m, w), lambda bi, i: (bi, 0, 1)),
            const((1, MEM_HEAD_DIM)), const((1, MEM_HEAD_DIM)),
        ],
        out_specs=pl.BlockSpec((1, tq, w), lambda bi, i: (bi, i, 0)),
        out_shape=jax.ShapeDtypeStruct((b, s, w), BF16),
        compiler_params=_cparams("parallel", "parallel"),
        name="mem_attention",
    )(z3, kv, kv, q_norm.reshape(1, MEM_HEAD_DIM).astype(F32), k_norm.reshape(1, MEM_HEAD_DIM).astype(F32))


def _merge_kernel(x_ref, on_ref, og_ref, om_ref, m0_ref, m1_ref, m2_ref, bm_ref, wb_ref, wo_ref, o_ref):
    merged = None
    for br, (ref, mg_ref) in enumerate(((on_ref, m0_ref), (og_ref, m1_ref), (om_ref, m2_ref))):
        y = jnp.dot(ref[...], wb_ref[br], preferred_element_type=F32)
        gate = jax.nn.sigmoid(mg_ref[...].astype(F32) + bm_ref[br:br + 1, :])
        merged = gate * y if merged is None else merged + gate * y
    o_ref[...] = x_ref[...] + jnp.dot(merged.astype(BF16), wo_ref[...], preferred_element_type=F32)


def _merge_out(x2, o_nsa, o_gla, o_mem, z2, b_merge, w_branch, w_out, layer, *, tm):
    t, d = x2.shape
    bw = BRANCH_WIDTH
    row = lambda w: pl.BlockSpec((tm, w), lambda i: (i, 0))
    gate_cols = lambda br: pl.BlockSpec((tm, d), lambda i: (i, COL_MERGE // d + br))
    return pl.pallas_call(
        _merge_kernel,
        grid=(t // tm,),
        in_specs=[
            row(d), row(bw), row(bw), row(bw),
            gate_cols(0), gate_cols(1), gate_cols(2),
            pl.BlockSpec((N_BRANCH, d), lambda i: (0, 0)),
            pl.BlockSpec((None, N_BRANCH, bw, d), lambda i: (layer, 0, 0, 0)),
            pl.BlockSpec((None, d, d), lambda i: (layer, 0, 0)),
        ],
        out_specs=row(d),
        out_shape=jax.ShapeDtypeStruct((t, d), F32),
        compiler_params=_cparams("parallel"),
        name="merge_out",
    )(x2, o_nsa, o_gla, o_mem, z2, z2, z2, b_merge.astype(F32), w_branch, w_out)


def _mlp_kernel(x_ref, g_ref, wu_ref, wd_ref, o_ref, h_ref):
    @pl.when(pl.program_id(1) == 0)
    def _():
        h_ref[...] = _rms_rows(x_ref[...], g_ref[...])
        o_ref[...] = x_ref[...]

    u = jnp.dot(h_ref[...], wu_ref[...], preferred_element_type=F32)
    u = jnp.square(jnp.maximum(u, 0.0)).astype(BF16)
    o_ref[...] += jnp.dot(u, wd_ref[...], preferred_element_type=F32)


def _mlp(x2, g, w_up, w_down, layer, *, tm, th):
    t, d = x2.shape
    hid = w_up.shape[2]
    return pl.pallas_call(
        _mlp_kernel,
        grid=(t // tm, hid // th),
        in_specs=[
            pl.BlockSpec((tm, d), lambda i, j: (i, 0)),
            pl.BlockSpec((1, d), lambda i, j: (0, 0)),
            pl.BlockSpec((None, d, th), lambda i, j: (layer, 0, j)),
            pl.BlockSpec((None, th, d), lambda i, j: (layer, j, 0)),
        ],
        out_specs=pl.BlockSpec((tm, d), lambda i, j: (i, 0)),
        out_shape=jax.ShapeDtypeStruct((t, d), F32),
        scratch_shapes=[pltpu.VMEM((tm, d), BF16)],
        compiler_params=_cparams("parallel", "arbitrary"),
        name="mlp",
    )(x2, g.reshape(1, d).astype(F32), w_up, w_down)


IN_PROJ_ROWS, IN_PROJ_COL_TILES = 1024, 2
NSA_PREP_ROWS = 512
NSA_QUERY_TILE, NSA_SELECT_PARTS, NSA_KEY_TILE = 256, 2, 512
KEY_TILE_MULTIPLES = (4, 2, 1)
GLA_CHUNKS_PER_STEP = 8
MEM_KV_ROWS, MEM_KV_COLS, MEM_QUERY_TILE = 512, 512, 1024
MERGE_ROWS = 512
MLP_ROWS, MLP_HIDDEN_TILE = 1024, 1024


def _tile(n, pref):
    t = min(n, pref)
    assert n % t == 0, (n, pref)
    return t


def _layer(x, mem2, pos_rows, pos_cmp, p, big, layer):
    b, s, d = x.shape
    t = b * s
    hd = NSA_HEAD_DIM
    x2 = x.reshape(t, d)

    z2 = _norm_matmul(x2, p["ln_mix"].astype(F32), big["w_in"], layer, tm=_tile(t, IN_PROJ_ROWS),
                      tn=D_IN_PAD // IN_PROJ_COL_TILES, out_dtype=BF16, name="in_proj")
    z3 = z2.reshape(b, s, D_IN_PAD)

    q, ks, vs, kw, vw = _nsa_prep(z3, pos_rows, p["nsa_q_norm"].astype(F32), p["nsa_k_norm"][1].astype(F32),
                                  p["nsa_k_norm"][2].astype(F32), tp=_tile(s, NSA_PREP_ROWS))
    pe2 = jnp.pad(p["cmp_pe"].reshape(2, 2, CMP_STRIDE * hd), ((0, 0), (0, 6), (0, 0))).astype(F32)
    cmp_kv = _nsa_compress(z3, p["cmp_w1"], p["cmp_w2"], pe2, pos_cmp, p["nsa_k_norm"][0].astype(F32))
    shifts = _nsa_shifts(p["nsa_q_norm"].astype(F32), p["nsa_k_norm"].astype(F32))
    tq = _tile(s, NSA_QUERY_TILE)
    o_cmp, sel = _nsa_select(shifts, q, cmp_kv, tq=tq, parts=NSA_SELECT_PARTS)
    o_nsa = _nsa_attend(shifts, q, ks, vs, kw, vw, z3, o_cmp, sel, tq=tq, tk=_tile(s, NSA_KEY_TILE))

    o_gla = _gla(z3, p["gla_w_gate"], p["gla_b_gate"], p["gla_norm"], n_chunks=GLA_CHUNKS_PER_STEP)

    kv = _norm_matmul(mem2, p["mem_norm"].astype(F32), big["mem_w_kv"], layer,
                      tm=_tile(mem2.shape[0], MEM_KV_ROWS), tn=MEM_KV_COLS, out_dtype=BF16, name="mem_kv")
    kv = kv.reshape(b, mem2.shape[0] // b, 2 * MEM_HEADS * MEM_HEAD_DIM)
    o_mem = _mem_attention(z3, kv, p["mem_q_norm"], p["mem_k_norm"], tq=_tile(s, MEM_QUERY_TILE))

    x2 = _merge_out(x2, o_nsa.reshape(t, -1), o_gla.reshape(t, -1), o_mem.reshape(t, -1), z2,
                    p["b_merge"], big["w_branch"], big["w_out"], layer, tm=_tile(t, MERGE_ROWS))
    x2 = _mlp(x2, p["ln_mlp"], big["w_up"], big["w_down"], layer, tm=_tile(t, MLP_ROWS), th=MLP_HIDDEN_TILE)
    return x2.reshape(b, s, d)


def kernel(x, mem, positions, ln_mix, w_in, b_merge, nsa_q_norm, nsa_k_norm, cmp_pe, cmp_w1, cmp_w2,
           gla_w_gate, gla_b_gate, gla_norm, mem_norm, mem_w_kv, mem_q_norm, mem_k_norm, w_branch, w_out,
           ln_mlp, w_up, w_down):
    b, s, d = x.shape
    assert d == 1024 and s % WINDOW == 0 and s >= 2 * WINDOW
    depth = w_in.shape[0]
    perm_np, d_in = _in_proj_permutation(d)
    assert w_in.shape[2] == d_in
    w_in = _relayout_in_proj(w_in.astype(BF16), [int(c) for c in perm_np], d_in)
    half = CMP_STRIDE * NSA_HEAD_DIM
    cmp_w1 = jnp.concatenate([cmp_w1[:, :, :half], cmp_w1[:, :, half:]], axis=-1).astype(BF16)
    cmp_w2, mem_w_kv, w_branch, w_out, w_up, w_down = (
        a.astype(BF16) for a in (cmp_w2, mem_w_kv, w_branch, w_out, w_up, w_down))
    pos3 = positions.astype(jnp.int32).reshape(b, s, 1)
    nc = s // CMP_STRIDE
    cmp_end = np.minimum(np.arange(nc) * CMP_STRIDE + CMP_BLOCK - 1, s - 1)
    pos_cmp = pos3[:, cmp_end, :]
    pos_rows = positions.astype(jnp.int32).reshape(b, 1, s)
    mem2 = mem.reshape(b * mem.shape[1], d)
    names = ("ln_mix", "b_merge", "nsa_q_norm", "nsa_k_norm", "cmp_pe", "cmp_w1", "cmp_w2",
             "gla_w_gate", "gla_b_gate", "gla_norm", "mem_norm", "mem_q_norm", "mem_k_norm", "ln_mlp")
    stacked = (ln_mix, b_merge, nsa_q_norm, nsa_k_norm, cmp_pe, cmp_w1, cmp_w2, gla_w_gate, gla_b_gate,
               gla_norm, mem_norm, mem_q_norm, mem_k_norm, ln_mlp)
    big = dict(w_in=w_in, mem_w_kv=mem_w_kv, w_branch=w_branch, w_out=w_out, w_up=w_up, w_down=w_down)
    for l in range(depth):
        x = _layer(x, mem2, pos_rows, pos_cmp, {n: a[l] for n, a in zip(names, stacked)}, big, l)
    return x
```

```python
import functools

import numpy as np
import jax
import jax.numpy as jnp
from jax import lax
from jax.experimental import pallas as pl
from jax.experimental.pallas import tpu as pltpu

NSA_HEADS = 8
NSA_GROUPS = 2
NSA_REP = NSA_HEADS // NSA_GROUPS
NSA_HEAD_DIM = 64
CMP_BLOCK = 32
CMP_STRIDE = 16
CMP_HIDDEN = 4 * NSA_HEAD_DIM
SEL_BLOCK = 64
N_SEL = 16
WINDOW = 512
FORCE_SCORE = 1e4
GLA_HEADS = 4
GLA_HEAD_DK = 64
GLA_HEAD_DV = 128
GLA_RANK = 16
GLA_TAU = 16.0
GLA_CHUNK = 64
GLA_SUB = 16
MEM_HEADS = 4
MEM_HEAD_DIM = 128
N_BRANCH = 3
BRANCH_WIDTH = 512
ROPE_THETA = 500000.0
ROPE_ROT = NSA_HEAD_DIM // 4
ROPE_HALF = ROPE_ROT // 2
NORM_EPS = 1e-6

LANES = 128
VMEM_LIMIT_BYTES = 48 * 1024 * 1024

F32 = jnp.float32
BF16 = jnp.bfloat16
HIGHEST = lax.Precision.HIGHEST
NEG_BIG = -1e30
MAX_CONST_SHIFT = 40.0

COL_NQ = 0
COL_GV = 512
COL_GR = 1024
COL_MQ = 1536
COL_MERGE = 2048
COL_GQ = 5120
COL_GK = 5376
COL_KV = 5632
COL_SMALL = 6400
D_IN_PAD = 6656
SMALL_GLOW_LANE = 12


def _in_proj_permutation(d_model):
    sizes = (512, 128, 128, 128, 128, 128, 128, 24, 256, 256, 512, 512, 16, 512, 3 * d_model)
    off = np.concatenate([[0], np.cumsum(sizes)])
    (o_nq, o_kc, o_vc, o_ks, o_vs, o_kw, o_vw, o_ng, o_gq, o_gk, o_gv, o_gr, o_gl, o_mq, o_mg) = off[:-1]
    d_in = int(off[-1])
    perm = np.full((D_IN_PAD,), d_in, np.int32)

    def put(new, old, n):
        perm[new:new + n] = np.arange(old, old + n)

    put(COL_NQ, o_nq, 512)
    put(COL_GV, o_gv, 512)
    put(COL_GR, o_gr, 512)
    put(COL_MQ, o_mq, 512)
    put(COL_MERGE, o_mg, 3 * d_model)
    put(COL_GQ, o_gq, 256)
    put(COL_GK, o_gk, 256)
    put(COL_KV, o_kc, 768)
    for g in range(NSA_GROUPS):
        for br in range(3):
            for r in range(NSA_REP):
                perm[COL_SMALL + g * LANES + br * NSA_REP + r] = o_ng + (g * NSA_REP + r) * 3 + br
    put(COL_SMALL + SMALL_GLOW_LANE, o_gl, GLA_RANK)
    return perm, d_in


def _relayout_in_proj(w, perm, d_in):
    pieces, start = [], 0
    for i in range(1, len(perm) + 1):
        prev = perm[i - 1]
        if i < len(perm) and (perm[i] == prev + 1 if prev != d_in else perm[i] == d_in):
            continue
        n = i - start
        if prev == d_in:
            pieces.append(jnp.zeros(w.shape[:-1] + (n,), w.dtype))
        else:
            pieces.append(w[..., perm[start]:perm[start] + n])
        start = i
    return jnp.concatenate(pieces, axis=-1)


def _cparams(*sem):
    return pltpu.CompilerParams(dimension_semantics=sem, vmem_limit_bytes=VMEM_LIMIT_BYTES)


def _rms_rows(x, g):
    ms = jnp.mean(x * x, axis=-1, keepdims=True)
    return (x * lax.rsqrt(ms + NORM_EPS) * g).astype(BF16)


def _norm_matmul_kernel(x_ref, g_ref, w_ref, o_ref):
    h = _rms_rows(x_ref[...].astype(F32), g_ref[...])
    o_ref[...] = jnp.dot(h, w_ref[...], preferred_element_type=F32).astype(o_ref.dtype)


def _norm_matmul(x, g, w, layer, *, tm, tn, out_dtype, name):
    m, k = x.shape
    n = w.shape[2]
    return pl.pallas_call(
        _norm_matmul_kernel,
        grid=(n // tn, m // tm),
        in_specs=[
            pl.BlockSpec((tm, k), lambda j, i: (i, 0)),
            pl.BlockSpec((1, k), lambda j, i: (0, 0)),
            pl.BlockSpec((None, k, tn), lambda j, i: (layer, 0, j)),
        ],
        out_specs=pl.BlockSpec((tm, tn), lambda j, i: (i, j)),
        out_shape=jax.ShapeDtypeStruct((m, n), out_dtype),
        compiler_params=_cparams("parallel", "parallel"),
        name=name,
    )(x, g.reshape(1, k), w)


def _rope_tables(pos_row, freq_col, place):
    ang = freq_col * pos_row
    tn = (((0,), (0,)), ((), ()))
    lane = lax.broadcasted_iota(jnp.int32, (1, LANES), 1)
    c = lax.dot_general(jnp.cos(ang), place, tn, preferred_element_type=F32, precision=HIGHEST)
    s = lax.dot_general(jnp.sin(ang), place, tn, preferred_element_type=F32, precision=HIGHEST)
    return c + jnp.where((lane & (NSA_HEAD_DIM - 1)) >= ROPE_ROT, 1.0, 0.0), s


def _rope_lanes(x, c, s):
    n = x.shape[-1]
    if n > LANES:
        c = jnp.concatenate([c] * (n // LANES), axis=1)
        s = jnp.concatenate([s] * (n // LANES), axis=1)
    lane = lax.broadcasted_iota(jnp.int32, (1, n), 1) & (NSA_HEAD_DIM - 1)
    up = pltpu.roll(x, n - ROPE_HALF, axis=1)
    dn = pltpu.roll(x, ROPE_HALF, axis=1)
    y = jnp.where(lane < ROPE_HALF, -up, jnp.where(lane < ROPE_ROT, dn, 0.0))
    return x * c + y * s


def _head_rms(x, bd, g):
    x2 = x * x
    hi = x2.astype(BF16)
    lo = (x2 - hi.astype(F32)).astype(BF16)
    ms = (jnp.dot(hi, bd, preferred_element_type=F32) + jnp.dot(lo, bd, preferred_element_type=F32))
    return x * lax.rsqrt(ms * (1.0 / NSA_HEAD_DIM) + NORM_EPS) * g


def _nsa_prep_kernel(q_ref, ks_ref, vs_ref, kw_ref, vw_ref, pos_ref, qg_ref, ksg_ref, kwg_ref,
                     bdq_ref, bdk_ref, f_ref, place_ref,
                     qo_ref, kso_ref, vso_ref, kwo_ref, vwo_ref):
    hd = NSA_HEAD_DIM
    tp = q_ref.shape[1]
    c, s = _rope_tables(pos_ref[0].astype(F32), f_ref[...], place_ref[...])
    q = _head_rms(q_ref[0].astype(F32), bdq_ref[...], qg_ref[...])
    q = (_rope_lanes(q, c, s) * (hd ** -0.5)).astype(qo_ref.dtype)
    for h in range(NSA_HEADS):
        qo_ref[0, h] = q[:, h * hd:(h + 1) * hd]
    ks = _rope_lanes(_head_rms(ks_ref[0].astype(F32), bdk_ref[...], ksg_ref[...]), c, s)
    kw = _rope_lanes(_head_rms(kw_ref[0].astype(F32), bdk_ref[...], kwg_ref[...]), c, s)
    vs = vs_ref[0]
    vw = vw_ref[0]
    tok = pl.program_id(1) * tp + lax.broadcasted_iota(jnp.int32, (tp, hd), 0)
    lane = lax.broadcasted_iota(jnp.int32, (tp, hd), 1)
    blk_onehot = jnp.where((tok >> 6) == lane, 1.0, 0.0).astype(kso_ref.dtype)
    ones_col = jnp.ones((tp, hd), vso_ref.dtype)
    for g in range(NSA_GROUPS):
        sl = slice(g * hd, (g + 1) * hd)
        kso_ref[0, g, :, 0:hd] = ks[:, sl].astype(kso_ref.dtype)
        kso_ref[0, g, :, hd:2 * hd] = blk_onehot
        kwo_ref[0, g] = kw[:, sl].astype(kwo_ref.dtype)
        vso_ref[0, g, :, 0:hd] = vs[:, sl].astype(vso_ref.dtype)
        vso_ref[0, g, :, hd:2 * hd] = ones_col
        vwo_ref[0, g, :, 0:hd] = vw[:, sl].astype(vwo_ref.dtype)
        vwo_ref[0, g, :, hd:2 * hd] = ones_col


def _block_diag_ones(n, width):
    i = np.arange(n)
    return jnp.asarray((i[:, None] // width == i[None, :] // width).astype(np.float32)).astype(BF16)


def _rope_freq_col():
    inv = ROPE_THETA ** (-(np.arange(ROPE_HALF, dtype=np.float64)) / ROPE_HALF)
    return jnp.asarray(inv.astype(np.float32)).reshape(ROPE_HALF, 1)


def _rope_placement():
    m = np.zeros((ROPE_HALF, LANES), np.float32)
    for f in range(ROPE_HALF):
        for base in range(0, LANES, NSA_HEAD_DIM):
            m[f, base + f] = 1.0
            m[f, base + f + ROPE_HALF] = 1.0
    return jnp.asarray(m)


def _nsa_prep(z3, pos_rows, q_norm, ks_norm, kw_norm, *, tp):
    b, s, _ = z3.shape
    hd, g = NSA_HEAD_DIM, NSA_GROUPS
    assert SEL_BLOCK == 64 and s // SEL_BLOCK <= hd
    kvb = COL_KV // LANES
    qg = jnp.tile(q_norm, NSA_HEADS).reshape(1, NSA_HEADS * hd)
    ksg = jnp.tile(ks_norm, g).reshape(1, g * hd)
    kwg = jnp.tile(kw_norm, g).reshape(1, g * hd)
    const = lambda shape: pl.BlockSpec(shape, lambda bi, i: (0,) * len(shape))
    plain_out = jax.ShapeDtypeStruct((b, g, s, hd), BF16)
    plain_spec = pl.BlockSpec((1, g, tp, hd), lambda bi, i: (bi, 0, i, 0))
    aug_out = jax.ShapeDtypeStruct((b, g, s, 2 * hd), BF16)
    aug_spec = pl.BlockSpec((1, g, tp, 2 * hd), lambda bi, i: (bi, 0, i, 0))
    return pl.pallas_call(
        _nsa_prep_kernel,
        grid=(b, s // tp),
        in_specs=[
            pl.BlockSpec((1, tp, 512), lambda bi, i: (bi, i, COL_NQ // 512)),
            pl.BlockSpec((1, tp, LANES), lambda bi, i: (bi, i, kvb + 2)),
            pl.BlockSpec((1, tp, LANES), lambda bi, i: (bi, i, kvb + 3)),
            pl.BlockSpec((1, tp, LANES), lambda bi, i: (bi, i, kvb + 4)),
            pl.BlockSpec((1, tp, LANES), lambda bi, i: (bi, i, kvb + 5)),
            pl.BlockSpec((1, 1, tp), lambda bi, i: (bi, 0, i)),
            const((1, 512)), const((1, LANES)), const((1, LANES)),
            const((512, 512)), const((LANES, LANES)),
            const((ROPE_HALF, 1)), const((ROPE_HALF, LANES)),
        ],
        out_specs=[
            pl.BlockSpec((1, NSA_HEADS, tp, hd), lambda bi, i: (bi, 0, i, 0)),
            aug_spec, aug_spec, plain_spec, aug_spec,
        ],
        out_shape=[jax.ShapeDtypeStruct((b, NSA_HEADS, s, hd), BF16), aug_out, aug_out, plain_out, aug_out],
        compiler_params=_cparams("parallel", "parallel"),
        name="nsa_prep",
    )(z3, z3, z3, z3, z3, pos_rows, qg, ksg, kwg,
      _block_diag_ones(512, hd), _block_diag_ones(LANES, hd), _rope_freq_col(), _rope_placement())


def _gelu_tanh(x):
    return 0.5 * x * (1.0 + jnp.tanh(np.sqrt(2.0 / np.pi) * (x + 0.044715 * x * x * x)))


def _nsa_cmp_kernel(x_ref, w1_ref, w2_ref, pe_ref, pos_ref, g_ref, rot_ref, f_ref, o_ref, xs_ref):
    kind = pl.program_id(0)
    hd = NSA_HEAD_DIM
    nc = x_ref.shape[1] // CMP_STRIDE
    xs_ref[...] = x_ref[0].astype(F32)
    r = jnp.dot(pe_ref[0].astype(BF16), w1_ref[0], preferred_element_type=F32)
    ab = [jnp.zeros((nc, 2 * CMP_HIDDEN), F32) for _ in range(NSA_GROUPS)]
    for p in range(CMP_STRIDE):
        xp = xs_ref[pl.ds(p, nc, stride=CMP_STRIDE), :].astype(BF16)
        w1p = w1_ref[0, p * hd:(p + 1) * hd, :]
        for g in range(NSA_GROUPS):
            ab[g] = ab[g] + jnp.dot(xp[:, g * hd:(g + 1) * hd], w1p, preferred_element_type=F32)

    for g in range(NSA_GROUPS):
        a = ab[g][:, :CMP_HIDDEN] + r[0:1, :CMP_HIDDEN]
        bm = ab[g][:, CMP_HIDDEN:] + r[1:2, CMP_HIDDEN:]
        hid = _gelu_tanh(a + pltpu.roll(bm, nc - 1, axis=0))
        comp = jnp.dot(hid.astype(BF16), w2_ref[0], preferred_element_type=F32)

        @pl.when(kind == 0)
        def _():
            ms = jnp.mean(comp * comp, axis=-1, keepdims=True)
            kn = comp * lax.rsqrt(ms + NORM_EPS) * g_ref[...]
            ang = pos_ref[0].astype(F32) * f_ref[...]
            y = jnp.dot(kn, rot_ref[...], preferred_element_type=F32, precision=HIGHEST)
            o_ref[0, 0, g, :, 0:hd] = (kn * jnp.cos(ang) + y * jnp.sin(ang)).astype(o_ref.dtype)
            o_ref[0, 0, g, :, hd:2 * hd] = jnp.zeros((nc, hd), o_ref.dtype)

        @pl.when(kind != 0)
        def _():
            o_ref[0, 0, g, :, 0:hd] = comp.astype(o_ref.dtype)
            o_ref[0, 0, g, :, hd:2 * hd] = jnp.ones((nc, hd), o_ref.dtype)


def _rope_freq_head():
    p = np.arange(NSA_HEAD_DIM)
    inv = ROPE_THETA ** (-(np.arange(ROPE_HALF, dtype=np.float64)) / ROPE_HALF)
    f = np.where(p < ROPE_ROT, inv[p % ROPE_HALF], 0.0)
    return jnp.asarray(f.astype(np.float32)).reshape(1, NSA_HEAD_DIM)


def _rope_rot_matrix(n):
    m = np.zeros((n, n), np.float32)
    for base in range(0, n, NSA_HEAD_DIM):
        for l in range(ROPE_HALF):
            m[base + l + ROPE_HALF, base + l] = -1.0
            m[base + l, base + l + ROPE_HALF] = 1.0
    return jnp.asarray(m)


def _nsa_compress(z3, w1cat, w2, pe2, pos_cmp, k_norm0):
    b, s, _ = z3.shape
    hd, g = NSA_HEAD_DIM, NSA_GROUPS
    nc = s // CMP_STRIDE
    assert g * hd == LANES
    const = lambda shape: pl.BlockSpec(shape, lambda k, bi: (0,) * len(shape))
    return pl.pallas_call(
        _nsa_cmp_kernel,
        grid=(2, b),
        in_specs=[
            pl.BlockSpec((1, s, LANES), lambda k, bi: (bi, 0, COL_KV // LANES + k)),
            pl.BlockSpec((1, CMP_STRIDE * hd, 2 * CMP_HIDDEN), lambda k, bi: (k, 0, 0)),
            pl.BlockSpec((1, CMP_HIDDEN, hd), lambda k, bi: (k, 0, 0)),
            pl.BlockSpec((1, 8, CMP_STRIDE * hd), lambda k, bi: (k, 0, 0)),
            pl.BlockSpec((1, nc, 1), lambda k, bi: (bi, 0, 0)),
            const((1, hd)), const((hd, hd)), const((1, hd)),
        ],
        out_specs=pl.BlockSpec((1, 1, g, nc, 2 * hd), lambda k, bi: (k, bi, 0, 0, 0)),
        out_shape=jax.ShapeDtypeStruct((2, b, g, nc, 2 * hd), BF16),
        scratch_shapes=[pltpu.VMEM((s, LANES), F32)],
        compiler_params=_cparams("parallel", "parallel"),
        name="nsa_compress",
    )(z3, w1cat, w2, pe2, pos_cmp, k_norm0.reshape(1, hd), _rope_rot_matrix(hd), _rope_freq_head())


def _softmax_rows(s, mask):
    s = jnp.where(mask, s, NEG_BIG)
    m = jnp.max(s, axis=-1, keepdims=True)
    p = jnp.where(mask, jnp.exp(s - m), 0.0)
    d = jnp.sum(p, axis=-1, keepdims=True)
    return p / jnp.where(d > 0, d, 1.0)


_NT = (((1,), (1,)), ((), ()))


def _select_blocks(p_sum, ov_ref, t0, tq, n_sel, n_live=None):
    n_blk = ov_ref.shape[0]
    n_live = n_blk if n_live is None else n_live
    p_hi = p_sum.astype(BF16)
    p_lo = (p_sum - p_hi.astype(F32)).astype(BF16)
    imp_t = (lax.dot_general(ov_ref[...], p_hi, _NT, preferred_element_type=F32)
             + lax.dot_general(ov_ref[...], p_lo, _NT, preferred_element_type=F32))
    j_col = lax.broadcasted_iota(jnp.int32, (n_blk, 1), 0)
    t_lane = t0 + lax.broadcasted_iota(jnp.int32, (1, tq), 1)
    causal = j_col * SEL_BLOCK <= t_lane
    cur = t_lane >> 6
    forced = causal & ((j_col == 0) | (j_col == cur) | (j_col == cur - 1))
    score = jnp.where(forced, FORCE_SCORE, jnp.where(causal, imp_t, -FORCE_SCORE))
    ng = n_live // 8
    groups = [score[8 * v:8 * v + 8] for v in range(ng)]
    ranks = [jnp.zeros((8, tq), F32) for _ in range(ng)]
    sub = lax.broadcasted_iota(jnp.int32, (8, tq), 0)
    for jp in range(n_live):
        row = jnp.broadcast_to(score[jp:jp + 1, :], (8, tq))
        vj = jp // 8
        for v in range(ng):
            if v < vj:
                ahead = row > groups[v]
            elif v > vj:
                ahead = row >= groups[v]
            else:
                ahead = (row > groups[v]) | ((row == groups[v]) & (sub > jp % 8))
            ranks[v] = ranks[v] + jnp.where(ahead, 1.0, 0.0)
    if n_live < n_blk:
        ranks.append(jnp.full((n_blk - n_live, tq), float(n_blk), F32))
    sel_t = jnp.where(jnp.concatenate(ranks, axis=0) < n_sel, 1.0, 0.0).astype(BF16)
    eye = (lax.broadcasted_iota(jnp.int32, (tq, tq), 0) == lax.broadcasted_iota(jnp.int32, (tq, tq), 1)).astype(BF16)
    return lax.dot_general(eye, sel_t, _NT, preferred_element_type=F32)


def _sum_heads(p, tq):
    out = p[0:tq]
    for r in range(1, NSA_REP):
        out = out + p[r * tq:(r + 1) * tq]
    return out


def _store_heads(o_ref, o, tq, row0):
    hd = NSA_HEAD_DIM
    for r in range(NSA_REP):
        o_ref[0, row0:row0 + tq, r * hd:(r + 1) * hd] = o[r * tq:(r + 1) * tq].astype(o_ref.dtype)


def _store_selection(sel_ref, sel, row0):
    tq, n_blk = sel.shape
    sel = sel.astype(sel_ref.dtype)
    if n_blk < NSA_HEAD_DIM:
        sel = jnp.concatenate([sel, jnp.zeros((tq, NSA_HEAD_DIM - n_blk), sel_ref.dtype)], axis=1)
    sel_ref[0, 0, row0:row0 + tq] = sel


def _gate_maps(gl_ref, gp_ref, gm_ref):
    gates = jax.nn.sigmoid(gl_ref[0].astype(F32))
    g_hi = gates.astype(BF16)
    g_lo = (gates - g_hi.astype(F32)).astype(BF16)
    gm_ref[...] = (jnp.dot(g_hi, gp_ref[...], preferred_element_type=F32)
                   + jnp.dot(g_lo, gp_ref[...], preferred_element_type=F32))


def _to_lane_layout(lay_ref, idx, o, tq, lane0=0):
    hd = NSA_HEAD_DIM
    for r in range(NSA_REP):
        lay_ref[idx, :, r * hd:(r + 1) * hd] = o[r * tq:(r + 1) * tq, lane0:lane0 + hd]


def _gated_store(gm_ref, oc_ref, lay_ref, o_ref, with_sums):
    w = NSA_REP * NSA_HEAD_DIM
    o_sel, o_win = lay_ref[0], lay_ref[1]
    if with_sums:
        o_sel, o_win = o_sel / lay_ref[2], o_win / lay_ref[3]
    out = gm_ref[:, 0:w] * oc_ref[0].astype(F32) + gm_ref[:, w:2 * w] * o_sel + gm_ref[:, 2 * w:3 * w] * o_win
    o_ref[0] = out.astype(o_ref.dtype)


def _gate_placement():
    m = np.zeros((LANES, 3 * NSA_REP * NSA_HEAD_DIM), np.float32)
    for br in range(3):
        for r in range(NSA_REP):
            c0 = br * NSA_REP * NSA_HEAD_DIM + r * NSA_HEAD_DIM
            m[br * NSA_REP + r, c0:c0 + NSA_HEAD_DIM] = 1.0
    return jnp.asarray(m).astype(BF16)


def _sel_fast(sh_ref, q_ref, kc_ref, vc_ref, ov_ref, oc_ref, sel_ref, *, tq, n_sel):
    rep, hd = NSA_REP, NSA_HEAD_DIM
    rows = rep * tq
    tb = q_ref.shape[2]
    n_blk = ov_ref.shape[0]

    def body(n_live):
        kc = kc_ref[0, 0, 0][:, 0:hd]
        ncp = kc.shape[0]
        n_idx = lax.broadcasted_iota(jnp.int32, (1, ncp), 1)
        for part in range(tb // tq):
            row0 = part * tq
            t0 = pl.program_id(2) * tb + row0
            q4 = q_ref[0, :, row0:row0 + tq, :].reshape(rows, hd)
            t_q = t0 + lax.broadcasted_iota(jnp.int32, (tq, 1), 0)
            valid_c = (n_idx * CMP_STRIDE + (CMP_BLOCK - 1) <= t_q) & (n_idx < ncp - 1)
            bias_c = jnp.where(valid_c, sh_ref[0], NEG_BIG)
            s_c = lax.dot_general(q4, kc, _NT, preferred_element_type=F32).reshape(rep, tq, ncp)
            p_c = jnp.exp(s_c + bias_c[None]).reshape(rows, ncp)
            ol_c = jnp.dot(p_c.astype(BF16), vc_ref[0, 0, 0], preferred_element_type=F32)
            inv_c = 1.0 / jnp.where(ol_c[:, hd:hd + 1] > 0, ol_c[:, hd:hd + 1], 1.0)
            _store_heads(oc_ref, ol_c[:, 0:hd] * inv_c, tq, row0)
            sel = _select_blocks(_sum_heads(p_c * inv_c, tq), ov_ref, t0, tq, n_sel, n_live)
            _store_selection(sel_ref, sel, row0)

    regions = SELECT_REGIONS if (n_blk % (8 * SELECT_REGIONS) == 0
                                 and (n_blk // SELECT_REGIONS * SEL_BLOCK) % tb == 0) else 1
    steps = n_blk // regions * SEL_BLOCK // tb
    step = pl.program_id(2)
    for r in range(regions):
        pl.when((step >= r * steps) & (step < (r + 1) * steps))(functools.partial(body, (r + 1) * (n_blk // regions)))


def _sel_slow(q_ref, kc_ref, vc_ref, ov_ref, oc_ref, sel_ref, *, tq, n_sel):
    rep, hd = NSA_REP, NSA_HEAD_DIM
    rows = rep * tq
    kc = kc_ref[0, 0, 0][:, 0:hd]
    ncp = kc.shape[0]
    n_idx = lax.broadcasted_iota(jnp.int32, (1, ncp), 1)
    for part in range(q_ref.shape[2] // tq):
        row0 = part * tq
        t0 = pl.program_id(2) * q_ref.shape[2] + row0
        q4 = q_ref[0, :, row0:row0 + tq, :].reshape(rows, hd)
        t_row = t0 + (lax.broadcasted_iota(jnp.int32, (rows, 1), 0) & (tq - 1))
        s_c = lax.dot_general(q4, kc, _NT, preferred_element_type=F32)
        p_c = _softmax_rows(s_c, (n_idx * CMP_STRIDE + (CMP_BLOCK - 1) <= t_row) & (n_idx < ncp - 1))
        o_c = jnp.dot(p_c.astype(BF16), vc_ref[0, 0, 0], preferred_element_type=F32)[:, 0:hd]
        _store_heads(oc_ref, o_c, tq, row0)
        _store_selection(sel_ref, _select_blocks(_sum_heads(p_c, tq), ov_ref, t0, tq, n_sel), row0)


def _nsa_select_kernel(sh_ref, q_ref, kc_ref, vc_ref, ov_ref, oc_ref, sel_ref, *, tq, n_sel):
    data = (q_ref, kc_ref, vc_ref, ov_ref, oc_ref, sel_ref)

    @pl.when(sh_ref[3] > 0.5)
    def _():
        _sel_fast(sh_ref, *data, tq=tq, n_sel=n_sel)

    @pl.when(sh_ref[3] <= 0.5)
    def _():
        _sel_slow(*data, tq=tq, n_sel=n_sel)


def _att_fast(sh_ref, q_ref, ks_ref, vs_ref, kw_ref, vw_ref, gl_ref, gp_ref, oc_ref, sel_ref, o_ref,
              lhs_ref, acc_ref, pre_ref, gm_ref, lay_ref, *, tq, tk):
    rep, hd = NSA_REP, NSA_HEAD_DIM
    rows = rep * tq
    t0 = pl.program_id(2) * tq
    c_s, c_w = sh_ref[1], sh_ref[2]
    q4 = q_ref[0].reshape(rows, hd)
    t_q = t0 + lax.broadcasted_iota(jnp.int32, (tq, 1), 0)

    j_lane = lax.broadcasted_iota(jnp.int32, (1, hd), 1)
    shift = jnp.where((sel_ref[0, 0].astype(F32) > 0.5) & (j_lane * SEL_BLOCK < t0), c_s, NEG_BIG).astype(BF16)
    lhs_ref[:, 0:hd] = q4
    for r in range(rep):
        lhs_ref[r * tq:(r + 1) * tq, hd:2 * hd] = shift

    span = WINDOW + tq
    w0 = pl.multiple_of(jnp.maximum(t0 - WINDOW, 0), tq)
    kp = w0 + lax.broadcasted_iota(jnp.int32, (1, span), 1)
    bias_w = jnp.where((kp <= t_q) & (kp > t_q - WINDOW), c_w, NEG_BIG)
    s_w = lax.dot_general(q4, kw_ref[0, 0, pl.ds(w0, span), :], _NT, preferred_element_type=F32)
    p_w = jnp.exp(s_w.reshape(rep, tq, span) + bias_w[None]).reshape(rows, span)
    pre_ref[...] = jnp.dot(p_w.astype(BF16), vw_ref[0, 0, pl.ds(w0, span), :], preferred_element_type=F32)

    d0 = pl.multiple_of(t0, tq)
    kcol = t0 + lax.broadcasted_iota(jnp.int32, (1, tq), 1)
    bias_d = jnp.where(kcol <= t_q, c_s, NEG_BIG)
    s_d = lax.dot_general(q4, ks_ref[0, 0, pl.ds(d0, tq), :][:, 0:hd], _NT, preferred_element_type=F32)
    p_d = jnp.exp(s_d.reshape(rep, tq, tq) + bias_d[None]).reshape(rows, tq)
    acc_ref[...] = jnp.dot(p_d.astype(BF16), vs_ref[0, 0, pl.ds(d0, tq), :], preferred_element_type=F32)

    def key_tile(k0, width):
        s = lax.dot_general(lhs_ref[...], ks_ref[0, 0, pl.ds(k0, width), :], _NT, preferred_element_type=F32)
        acc_ref[...] += jnp.dot(jnp.exp(s).astype(BF16), vs_ref[0, 0, pl.ds(k0, width), :],
                                preferred_element_type=F32)

    done = 0
    for mult in KEY_TILE_MULTIPLES:
        width = mult * tk
        if width > ks_ref.shape[2]:
            continue
        left = t0 - done
        n = left // width if mult > 1 else (left + width - 1) // width

        def step(c, carry, width=width, base=done):
            key_tile(pl.multiple_of(base + c * width, tk), width)
            return carry

        lax.fori_loop(0, n, step, 0)
        done = done + n * width

    for idx, ol in ((0, acc_ref[...]), (1, pre_ref[...])):
        _to_lane_layout(lay_ref, idx, ol, tq)
        _to_lane_layout(lay_ref, 2 + idx, ol, tq, lane0=hd)
    _gate_maps(gl_ref, gp_ref, gm_ref)
    _gated_store(gm_ref, oc_ref, lay_ref, o_ref, with_sums=True)


def _att_slow(q_ref, ks_ref, vs_ref, kw_ref, vw_ref, gl_ref, gp_ref, oc_ref, sel_ref, o_ref,
              acc_ref, gm_ref, lay_ref, m_ref, l_ref, *, tq, tk):
    rep, hd = NSA_REP, NSA_HEAD_DIM
    rows = rep * tq
    t0 = pl.program_id(2) * tq
    q4 = q_ref[0].reshape(rows, hd)
    t_row = t0 + (lax.broadcasted_iota(jnp.int32, (rows, 1), 0) & (tq - 1))
    sel = sel_ref[0, 0]
    n_blk = sel.shape[1]

    m_ref[...] = jnp.full(m_ref.shape, NEG_BIG, F32)
    l_ref[...] = jnp.zeros(l_ref.shape, F32)
    acc_ref[...] = jnp.zeros(acc_ref.shape, F32)
    t_q = t0 + lax.broadcasted_iota(jnp.int32, (tq, 1), 0)

    def kv_step(c, carry):
        k0 = pl.multiple_of(c * tk, tk)
        kt = ks_ref[0, 0, pl.ds(k0, tk), :][:, 0:hd]
        s = lax.dot_general(q4, kt, _NT, preferred_element_type=F32).reshape(rep, tq, tk)
        kk = lax.broadcasted_iota(jnp.int32, (1, tk), 1)
        blk = (k0 >> 6) + (kk >> 6)
        expand = jnp.where(lax.broadcasted_iota(jnp.int32, (n_blk, 1), 0) == blk, 1.0, 0.0).astype(BF16)
        chosen = jnp.dot(sel, expand, preferred_element_type=F32)
        mask = ((chosen > 0.5) & (k0 + kk <= t_q))[None]
        s = jnp.where(mask, s, NEG_BIG)
        m_old = m_ref[...]
        m_new = jnp.maximum(m_old, jnp.max(s, axis=-1, keepdims=True))
        p = jnp.where(mask, jnp.exp(s - m_new), 0.0)
        alpha = jnp.exp(m_old - m_new)
        l_ref[...] = alpha * l_ref[...] + jnp.sum(p, axis=-1, keepdims=True)
        pv = jnp.dot(p.reshape(rows, tk).astype(BF16), vs_ref[0, 0, pl.ds(k0, tk), :], preferred_element_type=F32)
        acc_ref[...] = alpha.reshape(rows, 1) * acc_ref[...] + pv
        m_ref[...] = m_new
        return carry

    lax.fori_loop(0, (t0 + tq + tk - 1) // tk, kv_step, 0)
    o_s = acc_ref[:, 0:hd] / l_ref[...].reshape(rows, 1)

    span = WINDOW + tq
    w0 = pl.multiple_of(jnp.maximum(t0 - WINDOW, 0), tq)
    s_w = lax.dot_general(q4, kw_ref[0, 0, pl.ds(w0, span), :], _NT, preferred_element_type=F32)
    kp = w0 + lax.broadcasted_iota(jnp.int32, (1, span), 1)
    p_w = _softmax_rows(s_w, (kp <= t_row) & (kp > t_row - WINDOW))
    o_w = jnp.dot(p_w.astype(BF16), vw_ref[0, 0, pl.ds(w0, span), :], preferred_element_type=F32)[:, 0:hd]

    _to_lane_layout(lay_ref, 0, o_s, tq)
    _to_lane_layout(lay_ref, 1, o_w, tq)
    _gate_maps(gl_ref, gp_ref, gm_ref)
    _gated_store(gm_ref, oc_ref, lay_ref, o_ref, with_sums=False)


def _nsa_attend_kernel(sh_ref, q_ref, ks_ref, vs_ref, kw_ref, vw_ref, gl_ref, gp_ref, oc_ref, sel_ref, o_ref,
                       lhs_ref, acc_ref, pre_ref, gm_ref, lay_ref, m_ref, l_ref, *, tq, tk):
    data = (q_ref, ks_ref, vs_ref, kw_ref, vw_ref, gl_ref, gp_ref, oc_ref, sel_ref, o_ref)

    @pl.when(sh_ref[3] > 0.5)
    def _():
        _att_fast(sh_ref, *data, lhs_ref, acc_ref, pre_ref, gm_ref, lay_ref, tq=tq, tk=tk)

    @pl.when(sh_ref[3] <= 0.5)
    def _():
        _att_slow(*data, acc_ref, gm_ref, lay_ref, m_ref, l_ref, tq=tq, tk=tk)


def _overlap_t(n_blk, ncp):
    c_start = np.arange(ncp) * CMP_STRIDE
    b_start = np.arange(n_blk) * SEL_BLOCK
    ov = ((c_start[None, :] < b_start[:, None] + SEL_BLOCK) & (b_start[:, None] < c_start[None, :] + CMP_BLOCK))
    ov[:, ncp - 1] = False
    return jnp.asarray(ov.astype(np.float32)).astype(BF16)


def _nsa_shifts(q_norm, k_norm):
    bound = (NSA_HEAD_DIM ** 0.5) * jnp.max(jnp.abs(q_norm)) * jnp.max(jnp.abs(k_norm), axis=-1)
    bound = bound.astype(BF16).astype(F32)
    fast = jnp.all(bound <= MAX_CONST_SHIFT).astype(F32)
    return jnp.concatenate([-bound, fast[None]])


def _nsa_select(shifts, q, cmp_kv, *, tq, parts):
    b, h, s, hd = q.shape
    g, rep = NSA_GROUPS, NSA_REP
    ncp = cmp_kv.shape[3]
    n_blk = s // SEL_BLOCK
    tb = tq * parts
    assert n_blk % 8 == 0 and n_blk <= hd and s % tb == 0
    kernel = functools.partial(_nsa_select_kernel, tq=tq, n_sel=min(N_SEL, n_blk))
    return pl.pallas_call(
        kernel,
        grid=(b, g, s // tb),
        in_specs=[
            pl.BlockSpec(memory_space=pltpu.SMEM),
            pl.BlockSpec((1, rep, tb, hd), lambda bi, gi, i: (bi, gi, i, 0)),
            pl.BlockSpec((1, 1, 1, ncp, 2 * hd), lambda bi, gi, i: (0, bi, gi, 0, 0)),
            pl.BlockSpec((1, 1, 1, ncp, 2 * hd), lambda bi, gi, i: (1, bi, gi, 0, 0)),
            pl.BlockSpec((n_blk, ncp), lambda bi, gi, i: (0, 0)),
        ],
        out_specs=[pl.BlockSpec((1, tb, rep * hd), lambda bi, gi, i: (bi, i, gi)),
                   pl.BlockSpec((1, 1, tb, hd), lambda bi, gi, i: (bi, gi, i, 0))],
        out_shape=[jax.ShapeDtypeStruct((b, s, h * hd), BF16), jax.ShapeDtypeStruct((b, g, s, hd), BF16)],
        compiler_params=_cparams("parallel", "parallel", "parallel"),
        name="nsa_select",
    )(shifts, q, cmp_kv, cmp_kv, _overlap_t(n_blk, ncp))


def _nsa_attend(shifts, q, ks, vs, kw, vw, z3, o_cmp, sel, *, tq, tk):
    b, h, s, hd = q.shape
    g, rep = NSA_GROUPS, NSA_REP
    rows = rep * tq
    assert WINDOW % tq == 0
    full = lambda w: pl.BlockSpec((1, 1, s, w), lambda bi, gi, i: (bi, gi, 0, 0))
    return pl.pallas_call(
        functools.partial(_nsa_attend_kernel, tq=tq, tk=tk),
        grid=(b, g, s // tq),
        in_specs=[
            pl.BlockSpec(memory_space=pltpu.SMEM),
            pl.BlockSpec((1, rep, tq, hd), lambda bi, gi, i: (bi, gi, i, 0)),
            full(2 * hd), full(2 * hd), full(hd), full(2 * hd),
            pl.BlockSpec((1, tq, LANES), lambda bi, gi, i: (bi, i, COL_SMALL // LANES + gi)),
            pl.BlockSpec((LANES, 3 * rep * hd), lambda bi, gi, i: (0, 0)),
            pl.BlockSpec((1, tq, rep * hd), lambda bi, gi, i: (bi, i, gi)),
            pl.BlockSpec((1, 1, tq, hd), lambda bi, gi, i: (bi, gi, i, 0)),
        ],
        out_specs=pl.BlockSpec((1, tq, rep * hd), lambda bi, gi, i: (bi, i, gi)),
        out_shape=jax.ShapeDtypeStruct((b, s, h * hd), BF16),
        scratch_shapes=[pltpu.VMEM((rows, 2 * hd), BF16), pltpu.VMEM((rows, 2 * hd), F32),
                        pltpu.VMEM((rows, 2 * hd), F32), pltpu.VMEM((tq, 3 * rep * hd), F32),
                        pltpu.VMEM((4, tq, rep * hd), F32),
                        pltpu.VMEM((rep, tq, 1), F32), pltpu.VMEM((rep, tq, 1), F32)],
        compiler_params=_cparams("parallel", "parallel", "arbitrary"),
        name="nsa_attend",
    )(shifts, q, ks, vs, kw, vw, z3, _gate_placement(), o_cmp, sel)


def _head_block_diag(x, n_rows, head_of_lane, dtype):
    return jnp.concatenate([jnp.where(head_of_lane == h, x, 0.0) for h in range(GLA_HEADS)], axis=0).astype(dtype)


def _gla_kernel(q_ref, k_ref, v_ref, r_ref, sm_ref, wg_ref, bg_ref, ng_ref, tril_ref, o_ref,
                st_ref, upd_ref, oin_ref, qin_ref, *, n_chunks):
    c, sub = GLA_CHUNK, GLA_SUB
    dk, dv, nh = GLA_HEAD_DK, GLA_HEAD_DV, GLA_HEADS
    nk, nv = nh * dk, nh * dv

    @pl.when(pl.program_id(1) == 0)
    def _():
        st_ref[...] = jnp.zeros(st_ref.shape, F32)

    sm = sm_ref[0]
    x = (jnp.dot(sm, wg_ref[0], preferred_element_type=F32) + jnp.dot(sm, wg_ref[1], preferred_element_type=F32)
         + bg_ref[...])
    log_a = (jnp.minimum(x, 0.0) - jnp.log1p(jnp.exp(-jnp.abs(x)))) / GLA_TAU
    la_hi = log_a.astype(BF16)
    la_lo = (log_a - la_hi.astype(F32)).astype(BF16)
    bcum_all = (jnp.dot(tril_ref[...], la_hi, preferred_element_type=F32)
                + jnp.dot(tril_ref[...], la_lo, preferred_element_type=F32))
    head_k = lax.broadcasted_iota(jnp.int32, (1, nk), 1) >> 6
    head_v = lax.broadcasted_iota(jnp.int32, (1, nv), 1) >> 7
    state_mask = (lax.broadcasted_iota(jnp.int32, (nv, 1), 0) >> 7) == head_k
    causal = ((lax.broadcasted_iota(jnp.int32, (c, nh * c), 1) & (c - 1))
              <= lax.broadcasted_iota(jnp.int32, (c, nh * c), 0))

    decays = []
    for cc in range(n_chunks):
        rs = slice(cc * c, (cc + 1) * c)
        bcum = bcum_all[rs]
        q = q_ref[0, rs].astype(F32) * (dk ** -0.5)
        k = k_ref[0, rs].astype(F32)
        v = v_ref[0, rs].astype(F32)
        b_last = bcum[c - 1:c, :]

        score_rows = []
        for i in range(c // sub):
            lo, hi = i * sub, (i + 1) * sub
            ref = bcum[lo - 1:lo, :] if i > 0 else jnp.zeros((1, nk), F32)
            q_i = (q[lo:hi] * jnp.exp(bcum[lo:hi] - ref)).astype(BF16)
            k_i = k[0:hi] * jnp.exp(ref - bcum[0:hi])
            if hi < c:
                k_i = jnp.concatenate([k_i, jnp.zeros((c - hi, nk), F32)], axis=0)
            score_rows.append(lax.dot_general(q_i, _head_block_diag(k_i, c, head_k, BF16), _NT,
                                              preferred_element_type=F32))
        a = jnp.where(causal, jnp.concatenate(score_rows, axis=0), 0.0).astype(BF16)
        oin_ref[rs] = jnp.dot(a, _head_block_diag(v, c, head_v, BF16), preferred_element_type=F32)
        qin_ref[rs] = (q * jnp.exp(bcum)).astype(BF16)
        k_out = (k * jnp.exp(b_last - bcum)).astype(BF16)
        upd = lax.dot_general(v.astype(BF16), k_out, (((0,), (0,)), ((), ())), preferred_element_type=F32)
        upd_ref[cc] = jnp.where(state_mask, upd, 0.0)
        decays.append(jnp.exp(b_last))

    for cc in range(n_chunks):
        rs = slice(cc * c, (cc + 1) * c)
        st = st_ref[...]
        o = oin_ref[rs] + lax.dot_general(qin_ref[rs], st.astype(BF16), _NT, preferred_element_type=F32)
        st_ref[...] = st * decays[cc] + upd_ref[cc]
        r_gate = r_ref[0, rs].astype(F32)
        for h in range(nh):
            vs_ = slice(h * dv, (h + 1) * dv)
            o_h = o[:, vs_]
            ms = jnp.mean(o_h * o_h, axis=-1, keepdims=True)
            rg = r_gate[:, vs_]
            o_ref[0, rs, vs_] = (o_h * lax.rsqrt(ms + NORM_EPS) * ng_ref[...] * (rg * jax.nn.sigmoid(rg))).astype(o_ref.dtype)


def _gla(z3, w_gate, b_gate, norm_g, *, n_chunks):
    b, s, _ = z3.shape
    c = GLA_CHUNK * n_chunks
    nk = GLA_HEADS * GLA_HEAD_DK
    nv = GLA_HEADS * GLA_HEAD_DV
    wg = jnp.zeros((LANES, nk), F32).at[SMALL_GLOW_LANE:SMALL_GLOW_LANE + GLA_RANK].set(w_gate.astype(F32))
    wg_hi = wg.astype(BF16)
    wg = jnp.stack([wg_hi, (wg - wg_hi.astype(F32)).astype(BF16)])
    idx = np.arange(c)
    tril = jnp.asarray(((idx[:, None] >= idx[None, :])
                        & (idx[:, None] // GLA_CHUNK == idx[None, :] // GLA_CHUNK)).astype(np.float32)).astype(BF16)
    const = lambda shape: pl.BlockSpec(shape, lambda bi, i: (0,) * len(shape))
    return pl.pallas_call(
        functools.partial(_gla_kernel, n_chunks=n_chunks),
        grid=(b, s // c),
        in_specs=[
            pl.BlockSpec((1, c, nk), lambda bi, i: (bi, i, COL_GQ // nk)),
            pl.BlockSpec((1, c, nk), lambda bi, i: (bi, i, COL_GK // nk)),
            pl.BlockSpec((1, c, nv), lambda bi, i: (bi, i, COL_GV // nv)),
            pl.BlockSpec((1, c, nv), lambda bi, i: (bi, i, COL_GR // nv)),
            pl.BlockSpec((1, c, LANES), lambda bi, i: (bi, i, COL_SMALL // LANES)),
            const((2, LANES, nk)), const((1, nk)), const((1, GLA_HEAD_DV)), const((c, c)),
        ],
        out_specs=pl.BlockSpec((1, c, nv), lambda bi, i: (bi, i, 0)),
        out_shape=jax.ShapeDtypeStruct((b, s, nv), BF16),
        scratch_shapes=[pltpu.VMEM((nv, nk), F32), pltpu.VMEM((n_chunks, nv, nk), F32),
                        pltpu.VMEM((c, nv), F32), pltpu.VMEM((c, nk), BF16)],
        compiler_params=_cparams("parallel", "arbitrary"),
        name="gla",
    )(z3, z3, z3, z3, z3, wg, b_gate.reshape(1, nk).astype(F32), norm_g.reshape(1, GLA_HEAD_DV).astype(F32), tril)


def _mem_attn_kernel(q_ref, k_ref, v_ref, qg_ref, kg_ref, o_ref):
    dh = MEM_HEAD_DIM
    nt = (((1,), (1,)), ((), ()))
    for h in range(MEM_HEADS):
        sl = slice(h * dh, (h + 1) * dh)
        q = q_ref[0, :, sl].astype(F32)
        q = q * lax.rsqrt(jnp.mean(q * q, axis=-1, keepdims=True) + NORM_EPS) * qg_ref[...] * (dh ** -0.5)
        k = k_ref[0, :, sl].astype(F32)
        k = k * lax.rsqrt(jnp.mean(k * k, axis=-1, keepdims=True) + NORM_EPS) * kg_ref[...]
        s = lax.dot_general(q.astype(BF16), k.astype(BF16), nt, preferred_element_type=F32)
        m = jnp.max(s, axis=-1, keepdims=True)
        p = jnp.exp(s - m)
        p = p / jnp.sum(p, axis=-1, keepdims=True)
        o = jnp.dot(p.astype(BF16), v_ref[0, :, sl].astype(BF16), preferred_element_type=F32)
        o_ref[0, :, sl] = o.astype(o_ref.dtype)


def _mem_attention(z3, kv, q_norm, k_norm, *, tq):
    b, s, _ = z3.shape
    m = kv.shape[1]
    w = MEM_HEADS * MEM_HEAD_DIM
    const = lambda shape: pl.BlockSpec(shape, lambda bi, i: (0,) * len(shape))
    return pl.pallas_call(
        _mem_attn_kernel,
        grid=(b, s // tq),
        in_specs=[
            pl.BlockSpec((1, tq, w), lambda bi, i: (bi, i, COL_MQ // w)),
            pl.BlockSpec((1, m, w), lambda bi, i: (bi, 0, 0)),
            pl.BlockSpec((1, m, w), lambda bi, i: (bi, 0, 1)),
            const((1, MEM_HEAD_DIM)), const((1, MEM_HEAD_DIM)),
        ],
        out_specs=pl.BlockSpec((1, tq, w), lambda bi, i: (bi, i, 0)),
        out_shape=jax.ShapeDtypeStruct((b, s, w), BF16),
        compiler_params=_cparams("parallel", "parallel"),
        name="mem_attention",
    )(z3, kv, kv, q_norm.reshape(1, MEM_HEAD_DIM).astype(F32), k_norm.reshape(1, MEM_HEAD_DIM).astype(F32))


def _merge_kernel(x_ref, on_ref, og_ref, om_ref, m0_ref, m1_ref, m2_ref, bm_ref, wb_ref, wo_ref, o_ref):
    merged = None
    for br, (ref, mg_ref) in enumerate(((on_ref, m0_ref), (og_ref, m1_ref), (om_ref, m2_ref))):
        y = jnp.dot(ref[...], wb_ref[br], preferred_element_type=F32)
        gate = jax.nn.sigmoid(mg_ref[...].astype(F32) + bm_ref[br:br + 1, :])
        merged = gate * y if merged is None else merged + gate * y
    o_ref[...] = x_ref[...] + jnp.dot(merged.astype(BF16), wo_ref[...], preferred_element_type=F32)


def _merge_out(x2, o_nsa, o_gla, o_mem, z2, b_merge, w_branch, w_out, layer, *, tm):
    t, d = x2.shape
    bw = BRANCH_WIDTH
    row = lambda w: pl.BlockSpec((tm, w), lambda i: (i, 0))
    gate_cols = lambda br: pl.BlockSpec((tm, d), lambda i: (i, COL_MERGE // d + br))
    return pl.pallas_call(
        _merge_kernel,
        grid=(t // tm,),
        in_specs=[
            row(d), row(bw), row(bw), row(bw),
            gate_cols(0), gate_cols(1), gate_cols(2),
            pl.BlockSpec((N_BRANCH, d), lambda i: (0, 0)),
            pl.BlockSpec((None, N_BRANCH, bw, d), lambda i: (layer, 0, 0, 0)),
            pl.BlockSpec((None, d, d), lambda i: (layer, 0, 0)),
        ],
        out_specs=row(d),
        out_shape=jax.ShapeDtypeStruct((t, d), F32),
        compiler_params=_cparams("parallel"),
        name="merge_out",
    )(x2, o_nsa, o_gla, o_mem, z2, z2, z2, b_merge.astype(F32), w_branch, w_out)


def _mlp_kernel(x_ref, g_ref, wu_ref, wd_ref, o_ref, h_ref):
    @pl.when(pl.program_id(1) == 0)
    def _():
        h_ref[...] = _rms_rows(x_ref[...], g_ref[...])
        o_ref[...] = x_ref[...]

    u = jnp.dot(h_ref[...], wu_ref[...], preferred_element_type=F32)
    u = jnp.square(jnp.maximum(u, 0.0)).astype(BF16)
    o_ref[...] += jnp.dot(u, wd_ref[...], preferred_element_type=F32)


def _mlp(x2, g, w_up, w_down, layer, *, tm, th):
    t, d = x2.shape
    hid = w_up.shape[2]
    return pl.pallas_call(
        _mlp_kernel,
        grid=(t // tm, hid // th),
        in_specs=[
            pl.BlockSpec((tm, d), lambda i, j: (i, 0)),
            pl.BlockSpec((1, d), lambda i, j: (0, 0)),
            pl.BlockSpec((None, d, th), lambda i, j: (layer, 0, j)),
            pl.BlockSpec((None, th, d), lambda i, j: (layer, j, 0)),
        ],
        out_specs=pl.BlockSpec((tm, d), lambda i, j: (i, 0)),
        out_shape=jax.ShapeDtypeStruct((t, d), F32),
        scratch_shapes=[pltpu.VMEM((tm, d), BF16)],
        compiler_params=_cparams("parallel", "arbitrary"),
        name="mlp",
    )(x2, g.reshape(1, d).astype(F32), w_up, w_down)


IN_PROJ_ROWS, IN_PROJ_COL_TILES = 1024, 2
NSA_PREP_ROWS = 512
NSA_QUERY_TILE, NSA_SELECT_PARTS, NSA_KEY_TILE = 256, 2, 512
SELECT_REGIONS = 4
KEY_TILE_MULTIPLES = (4, 2, 1)
GLA_CHUNKS_PER_STEP = 8
MEM_KV_ROWS, MEM_KV_COLS, MEM_QUERY_TILE = 512, 512, 1024
MERGE_ROWS = 512
MLP_ROWS, MLP_HIDDEN_TILE = 1024, 1024


def _tile(n, pref):
    t = min(n, pref)
    assert n % t == 0, (n, pref)
    return t


def _layer(x, mem2, pos_rows, pos_cmp, p, big, layer):
    b, s, d = x.shape
    t = b * s
    hd = NSA_HEAD_DIM
    x2 = x.reshape(t, d)

    z2 = _norm_matmul(x2, p["ln_mix"].astype(F32), big["w_in"], layer, tm=_tile(t, IN_PROJ_ROWS),
                      tn=D_IN_PAD // IN_PROJ_COL_TILES, out_dtype=BF16, name="in_proj")
    z3 = z2.reshape(b, s, D_IN_PAD)

    q, ks, vs, kw, vw = _nsa_prep(z3, pos_rows, p["nsa_q_norm"].astype(F32), p["nsa_k_norm"][1].astype(F32),
                                  p["nsa_k_norm"][2].astype(F32), tp=_tile(s, NSA_PREP_ROWS))
    pe2 = jnp.pad(p["cmp_pe"].reshape(2, 2, CMP_STRIDE * hd), ((0, 0), (0, 6), (0, 0))).astype(F32)
    cmp_kv = _nsa_compress(z3, p["cmp_w1"], p["cmp_w2"], pe2, pos_cmp, p["nsa_k_norm"][0].astype(F32))
    shifts = _nsa_shifts(p["nsa_q_norm"].astype(F32), p["nsa_k_norm"].astype(F32))
    tq = _tile(s, NSA_QUERY_TILE)
    o_cmp, sel = _nsa_select(shifts, q, cmp_kv, tq=tq, parts=NSA_SELECT_PARTS)
    o_nsa = _nsa_attend(shifts, q, ks, vs, kw, vw, z3, o_cmp, sel, tq=tq, tk=_tile(s, NSA_KEY_TILE))

    o_gla = _gla(z3, p["gla_w_gate"], p["gla_b_gate"], p["gla_norm"], n_chunks=GLA_CHUNKS_PER_STEP)

    kv = _norm_matmul(mem2, p["mem_norm"].astype(F32), big["mem_w_kv"], layer,
                      tm=_tile(mem2.shape[0], MEM_KV_ROWS), tn=MEM_KV_COLS, out_dtype=BF16, name="mem_kv")
    kv = kv.reshape(b, mem2.shape[0] // b, 2 * MEM_HEADS * MEM_HEAD_DIM)
    o_mem = _mem_attention(z3, kv, p["mem_q_norm"], p["mem_k_norm"], tq=_tile(s, MEM_QUERY_TILE))

    x2 = _merge_out(x2, o_nsa.reshape(t, -1), o_gla.reshape(t, -1), o_mem.reshape(t, -1), z2,
                    p["b_merge"], big["w_branch"], big["w_out"], layer, tm=_tile(t, MERGE_ROWS))
    x2 = _mlp(x2, p["ln_mlp"], big["w_up"], big["w_down"], layer, tm=_tile(t, MLP_ROWS), th=MLP_HIDDEN_TILE)
    return x2.reshape(b, s, d)


def kernel(x, mem, positions, ln_mix, w_in, b_merge, nsa_q_norm, nsa_k_norm, cmp_pe, cmp_w1, cmp_w2,
           gla_w_gate, gla_b_gate, gla_norm, mem_norm, mem_w_kv, mem_q_norm, mem_k_norm, w_branch, w_out,
           ln_mlp, w_up, w_down):
    b, s, d = x.shape
    assert d == 1024 and s % WINDOW == 0 and s >= 2 * WINDOW
    depth = w_in.shape[0]
    perm_np, d_in = _in_proj_permutation(d)
    assert w_in.shape[2] == d_in
    w_in = _relayout_in_proj(w_in.astype(BF16), [int(c) for c in perm_np], d_in)
    half = CMP_STRIDE * NSA_HEAD_DIM
    cmp_w1 = jnp.concatenate([cmp_w1[:, :, :half], cmp_w1[:, :, half:]], axis=-1).astype(BF16)
    cmp_w2, mem_w_kv, w_branch, w_out, w_up, w_down = (
        a.astype(BF16) for a in (cmp_w2, mem_w_kv, w_branch, w_out, w_up, w_down))
    pos3 = positions.astype(jnp.int32).reshape(b, s, 1)
    nc = s // CMP_STRIDE
    cmp_end = np.minimum(np.arange(nc) * CMP_STRIDE + CMP_BLOCK - 1, s - 1)
    pos_cmp = pos3[:, cmp_end, :]
    pos_rows = positions.astype(jnp.int32).reshape(b, 1, s)
    mem2 = mem.reshape(b * mem.shape[1], d)
    names = ("ln_mix", "b_merge", "nsa_q_norm", "nsa_k_norm", "cmp_pe", "cmp_w1", "cmp_w2",
             "gla_w_gate", "gla_b_gate", "gla_norm", "mem_norm", "mem_q_norm", "mem_k_norm", "ln_mlp")
    stacked = (ln_mix, b_merge, nsa_q_norm, nsa_k_norm, cmp_pe, cmp_w1, cmp_w2, gla_w_gate, gla_b_gate,
               gla_norm, mem_norm, mem_q_norm, mem_k_norm, ln_mlp)
    big = dict(w_in=w_in, mem_w_kv=mem_w_kv, w_branch=w_branch, w_out=w_out, w_up=w_up, w_down=w_down)
    for l in range(depth):
        x = _layer(x, mem2, pos_rows, pos_cmp, {n: a[l] for n, a in zip(names, stacked)}, big, l)
    return x
```

```python
import functools

import numpy as np
import jax
import jax.numpy as jnp
from jax import lax
from jax.experimental import pallas as pl
from jax.experimental.pallas import tpu as pltpu

NSA_HEADS = 8
NSA_GROUPS = 2
NSA_REP = NSA_HEADS // NSA_GROUPS
NSA_HEAD_DIM = 64
CMP_BLOCK = 32
CMP_STRIDE = 16
CMP_HIDDEN = 4 * NSA_HEAD_DIM
SEL_BLOCK = 64
N_SEL = 16
WINDOW = 512
FORCE_SCORE = 1e4
GLA_HEADS = 4
GLA_HEAD_DK = 64
GLA_HEAD_DV = 128
GLA_RANK = 16
GLA_TAU = 16.0
GLA_CHUNK = 64
GLA_SUB = 16
MEM_HEADS = 4
MEM_HEAD_DIM = 128
N_BRANCH = 3
BRANCH_WIDTH = 512
ROPE_THETA = 500000.0
ROPE_ROT = NSA_HEAD_DIM // 4
ROPE_HALF = ROPE_ROT // 2
NORM_EPS = 1e-6

LANES = 128
VMEM_LIMIT_BYTES = 48 * 1024 * 1024

F32 = jnp.float32
BF16 = jnp.bfloat16
HIGHEST = lax.Precision.HIGHEST
NEG_BIG = -1e30
MAX_CONST_SHIFT = 40.0

COL_NQ = 0
COL_GV = 512
COL_GR = 1024
COL_MQ = 1536
COL_MERGE = 2048
COL_GQ = 5120
COL_GK = 5376
COL_KV = 5632
COL_SMALL = 6400
D_IN_PAD = 6656
SMALL_GLOW_LANE = 12


def _in_proj_permutation(d_model):
    sizes = (512, 128, 128, 128, 128, 128, 128, 24, 256, 256, 512, 512, 16, 512, 3 * d_model)
    off = np.concatenate([[0], np.cumsum(sizes)])
    (o_nq, o_kc, o_vc, o_ks, o_vs, o_kw, o_vw, o_ng, o_gq, o_gk, o_gv, o_gr, o_gl, o_mq, o_mg) = off[:-1]
    d_in = int(off[-1])
    perm = np.full((D_IN_PAD,), d_in, np.int32)

    def put(new, old, n):
        perm[new:new + n] = np.arange(old, old + n)

    put(COL_NQ, o_nq, 512)
    put(COL_GV, o_gv, 512)
    put(COL_GR, o_gr, 512)
    put(COL_MQ, o_mq, 512)
    put(COL_MERGE, o_mg, 3 * d_model)
    put(COL_GQ, o_gq, 256)
    put(COL_GK, o_gk, 256)
    put(COL_KV, o_kc, 768)
    for g in range(NSA_GROUPS):
        for br in range(3):
            for r in range(NSA_REP):
                perm[COL_SMALL + g * LANES + br * NSA_REP + r] = o_ng + (g * NSA_REP + r) * 3 + br
    put(COL_SMALL + SMALL_GLOW_LANE, o_gl, GLA_RANK)
    return perm, d_in


def _relayout_in_proj(w, perm, d_in):
    pieces, start = [], 0
    for i in range(1, len(perm) + 1):
        prev = perm[i - 1]
        if i < len(perm) and (perm[i] == prev + 1 if prev != d_in else perm[i] == d_in):
            continue
        n = i - start
        if prev == d_in:
            pieces.append(jnp.zeros(w.shape[:-1] + (n,), w.dtype))
        else:
            pieces.append(w[..., perm[start]:perm[start] + n])
        start = i
    return jnp.concatenate(pieces, axis=-1)


def _cparams(*sem):
    return pltpu.CompilerParams(dimension_semantics=sem, vmem_limit_bytes=VMEM_LIMIT_BYTES)


def _rms_rows(x, g):
    ms = jnp.mean(x * x, axis=-1, keepdims=True)
    return (x * lax.rsqrt(ms + NORM_EPS) * g).astype(BF16)


def _norm_matmul_kernel(x_ref, g_ref, w_ref, o_ref):
    h = _rms_rows(x_ref[...].astype(F32), g_ref[...])
    o_ref[...] = jnp.dot(h, w_ref[...], preferred_element_type=F32).astype(o_ref.dtype)


def _norm_matmul(x, g, w, layer, *, tm, tn, out_dtype, name):
    m, k = x.shape
    n = w.shape[2]
    return pl.pallas_call(
        _norm_matmul_kernel,
        grid=(n // tn, m // tm),
        in_specs=[
            pl.BlockSpec((tm, k), lambda j, i: (i, 0)),
            pl.BlockSpec((1, k), lambda j, i: (0, 0)),
            pl.BlockSpec((None, k, tn), lambda j, i: (layer, 0, j)),
        ],
        out_specs=pl.BlockSpec((tm, tn), lambda j, i: (i, j)),
        out_shape=jax.ShapeDtypeStruct((m, n), out_dtype),
        compiler_params=_cparams("parallel", "parallel"),
        name=name,
    )(x, g.reshape(1, k), w)


def _rope_tables(pos_row, freq_col, place):
    ang = freq_col * pos_row
    tn = (((0,), (0,)), ((), ()))
    lane = lax.broadcasted_iota(jnp.int32, (1, LANES), 1)
    c = lax.dot_general(jnp.cos(ang), place, tn, preferred_element_type=F32, precision=HIGHEST)
    s = lax.dot_general(jnp.sin(ang), place, tn, preferred_element_type=F32, precision=HIGHEST)
    return c + jnp.where((lane & (NSA_HEAD_DIM - 1)) >= ROPE_ROT, 1.0, 0.0), s


def _rope_lanes(x, c, s):
    n = x.shape[-1]
    if n > LANES:
        c = jnp.concatenate([c] * (n // LANES), axis=1)
        s = jnp.concatenate([s] * (n // LANES), axis=1)
    lane = lax.broadcasted_iota(jnp.int32, (1, n), 1) & (NSA_HEAD_DIM - 1)
    up = pltpu.roll(x, n - ROPE_HALF, axis=1)
    dn = pltpu.roll(x, ROPE_HALF, axis=1)
    y = jnp.where(lane < ROPE_HALF, -up, jnp.where(lane < ROPE_ROT, dn, 0.0))
    return x * c + y * s


def _head_rms(x, bd, g):
    x2 = x * x
    hi = x2.astype(BF16)
    lo = (x2 - hi.astype(F32)).astype(BF16)
    ms = (jnp.dot(hi, bd, preferred_element_type=F32) + jnp.dot(lo, bd, preferred_element_type=F32))
    return x * lax.rsqrt(ms * (1.0 / NSA_HEAD_DIM) + NORM_EPS) * g


def _nsa_prep_kernel(q_ref, ks_ref, vs_ref, kw_ref, vw_ref, pos_ref, qg_ref, ksg_ref, kwg_ref,
                     bdq_ref, bdk_ref, f_ref, place_ref,
                     qo_ref, kso_ref, vso_ref, kwo_ref, vwo_ref):
    hd = NSA_HEAD_DIM
    tp = q_ref.shape[1]
    c, s = _rope_tables(pos_ref[0].astype(F32), f_ref[...], place_ref[...])
    q = _head_rms(q_ref[0].astype(F32), bdq_ref[...], qg_ref[...])
    q = (_rope_lanes(q, c, s) * (hd ** -0.5)).astype(qo_ref.dtype)
    for h in range(NSA_HEADS):
        qo_ref[0, h] = q[:, h * hd:(h + 1) * hd]
    ks = _rope_lanes(_head_rms(ks_ref[0].astype(F32), bdk_ref[...], ksg_ref[...]), c, s)
    kw = _rope_lanes(_head_rms(kw_ref[0].astype(F32), bdk_ref[...], kwg_ref[...]), c, s)
    vs = vs_ref[0]
    vw = vw_ref[0]
    tok = pl.program_id(1) * tp + lax.broadcasted_iota(jnp.int32, (tp, hd), 0)
    lane = lax.broadcasted_iota(jnp.int32, (tp, hd), 1)
    blk_onehot = jnp.where((tok >> 6) == lane, 1.0, 0.0).astype(kso_ref.dtype)
    ones_col = jnp.ones((tp, hd), vso_ref.dtype)
    for g in range(NSA_GROUPS):
        sl = slice(g * hd, (g + 1) * hd)
        kso_ref[0, g, :, 0:hd] = ks[:, sl].astype(kso_ref.dtype)
        kso_ref[0, g, :, hd:2 * hd] = blk_onehot
        kwo_ref[0, g] = kw[:, sl].astype(kwo_ref.dtype)
        vso_ref[0, g, :, 0:hd] = vs[:, sl].astype(vso_ref.dtype)
        vso_ref[0, g, :, hd:2 * hd] = ones_col
        vwo_ref[0, g, :, 0:hd] = vw[:, sl].astype(vwo_ref.dtype)
        vwo_ref[0, g, :, hd:2 * hd] = ones_col


def _block_diag_ones(n, width):
    i = np.arange(n)
    return jnp.asarray((i[:, None] // width == i[None, :] // width).astype(np.float32)).astype(BF16)


def _rope_freq_col():
    inv = ROPE_THETA ** (-(np.arange(ROPE_HALF, dtype=np.float64)) / ROPE_HALF)
    return jnp.asarray(inv.astype(np.float32)).reshape(ROPE_HALF, 1)


def _rope_placement():
    m = np.zeros((ROPE_HALF, LANES), np.float32)
    for f in range(ROPE_HALF):
        for base in range(0, LANES, NSA_HEAD_DIM):
            m[f, base + f] = 1.0
            m[f, base + f + ROPE_HALF] = 1.0
    return jnp.asarray(m)


def _nsa_prep(z3, pos_rows, q_norm, ks_norm, kw_norm, *, tp):
    b, s, _ = z3.shape
    hd, g = NSA_HEAD_DIM, NSA_GROUPS
    assert SEL_BLOCK == 64 and s // SEL_BLOCK <= hd
    kvb = COL_KV // LANES
    qg = jnp.tile(q_norm, NSA_HEADS).reshape(1, NSA_HEADS * hd)
    ksg = jnp.tile(ks_norm, g).reshape(1, g * hd)
    kwg = jnp.tile(kw_norm, g).reshape(1, g * hd)
    const = lambda shape: pl.BlockSpec(shape, lambda bi, i: (0,) * len(shape))
    plain_out = jax.ShapeDtypeStruct((b, g, s, hd), BF16)
    plain_spec = pl.BlockSpec((1, g, tp, hd), lambda bi, i: (bi, 0, i, 0))
    aug_out = jax.ShapeDtypeStruct((b, g, s, 2 * hd), BF16)
    aug_spec = pl.BlockSpec((1, g, tp, 2 * hd), lambda bi, i: (bi, 0, i, 0))
    return pl.pallas_call(
        _nsa_prep_kernel,
        grid=(b, s // tp),
        in_specs=[
            pl.BlockSpec((1, tp, 512), lambda bi, i: (bi, i, COL_NQ // 512)),
            pl.BlockSpec((1, tp, LANES), lambda bi, i: (bi, i, kvb + 2)),
            pl.BlockSpec((1, tp, LANES), lambda bi, i: (bi, i, kvb + 3)),
            pl.BlockSpec((1, tp, LANES), lambda bi, i: (bi, i, kvb + 4)),
            pl.BlockSpec((1, tp, LANES), lambda bi, i: (bi, i, kvb + 5)),
            pl.BlockSpec((1, 1, tp), lambda bi, i: (bi, 0, i)),
            const((1, 512)), const((1, LANES)), const((1, LANES)),
            const((512, 512)), const((LANES, LANES)),
            const((ROPE_HALF, 1)), const((ROPE_HALF, LANES)),
        ],
        out_specs=[
            pl.BlockSpec((1, NSA_HEADS, tp, hd), lambda bi, i: (bi, 0, i, 0)),
            aug_spec, aug_spec, plain_spec, aug_spec,
        ],
        out_shape=[jax.ShapeDtypeStruct((b, NSA_HEADS, s, hd), BF16), aug_out, aug_out, plain_out, aug_out],
        compiler_params=_cparams("parallel", "parallel"),
        name="nsa_prep",
    )(z3, z3, z3, z3, z3, pos_rows, qg, ksg, kwg,
      _block_diag_ones(512, hd), _block_diag_ones(LANES, hd), _rope_freq_col(), _rope_placement())


def _gelu_tanh(x):
    return 0.5 * x * (1.0 + jnp.tanh(np.sqrt(2.0 / np.pi) * (x + 0.044715 * x * x * x)))


def _nsa_cmp_kernel(x_ref, w1_ref, w2_ref, pe_ref, pos_ref, g_ref, rot_ref, f_ref, o_ref, xs_ref):
    kind = pl.program_id(0)
    hd = NSA_HEAD_DIM
    nc = x_ref.shape[1] // CMP_STRIDE
    xs_ref[...] = x_ref[0].astype(F32)
    r = jnp.dot(pe_ref[0].astype(BF16), w1_ref[0], preferred_element_type=F32)
    ab = [jnp.zeros((nc, 2 * CMP_HIDDEN), F32) for _ in range(NSA_GROUPS)]
    for p in range(CMP_STRIDE):
        xp = xs_ref[pl.ds(p, nc, stride=CMP_STRIDE), :].astype(BF16)
        w1p = w1_ref[0, p * hd:(p + 1) * hd, :]
        for g in range(NSA_GROUPS):
            ab[g] = ab[g] + jnp.dot(xp[:, g * hd:(g + 1) * hd], w1p, preferred_element_type=F32)

    for g in range(NSA_GROUPS):
        a = ab[g][:, :CMP_HIDDEN] + r[0:1, :CMP_HIDDEN]
        bm = ab[g][:, CMP_HIDDEN:] + r[1:2, CMP_HIDDEN:]
        hid = _gelu_tanh(a + pltpu.roll(bm, nc - 1, axis=0))
        comp = jnp.dot(hid.astype(BF16), w2_ref[0], preferred_element_type=F32)

        @pl.when(kind == 0)
        def _():
            ms = jnp.mean(comp * comp, axis=-1, keepdims=True)
            kn = comp * lax.rsqrt(ms + NORM_EPS) * g_ref[...]
            ang = pos_ref[0].astype(F32) * f_ref[...]
            y = jnp.dot(kn, rot_ref[...], preferred_element_type=F32, precision=HIGHEST)
            o_ref[0, 0, g, :, 0:hd] = (kn * jnp.cos(ang) + y * jnp.sin(ang)).astype(o_ref.dtype)
            o_ref[0, 0, g, :, hd:2 * hd] = jnp.zeros((nc, hd), o_ref.dtype)

        @pl.when(kind != 0)
        def _():
            o_ref[0, 0, g, :, 0:hd] = comp.astype(o_ref.dtype)
            o_ref[0, 0, g, :, hd:2 * hd] = jnp.ones((nc, hd), o_ref.dtype)


def _rope_freq_head():
    p = np.arange(NSA_HEAD_DIM)
    inv = ROPE_THETA ** (-(np.arange(ROPE_HALF, dtype=np.float64)) / ROPE_HALF)
    f = np.where(p < ROPE_ROT, inv[p % ROPE_HALF], 0.0)
    return jnp.asarray(f.astype(np.float32)).reshape(1, NSA_HEAD_DIM)


def _rope_rot_matrix(n):
    m = np.zeros((n, n), np.float32)
    for base in range(0, n, NSA_HEAD_DIM):
        for l in range(ROPE_HALF):
            m[base + l + ROPE_HALF, base + l] = -1.0
            m[base + l, base + l + ROPE_HALF] = 1.0
    return jnp.asarray(m)


def _nsa_compress(z3, w1cat, w2, pe2, pos_cmp, k_norm0):
    b, s, _ = z3.shape
    hd, g = NSA_HEAD_DIM, NSA_GROUPS
    nc = s // CMP_STRIDE
    assert g * hd == LANES
    const = lambda shape: pl.BlockSpec(shape, lambda k, bi: (0,) * len(shape))
    return pl.pallas_call(
        _nsa_cmp_kernel,
        grid=(2, b),
        in_specs=[
            pl.BlockSpec((1, s, LANES), lambda k, bi: (bi, 0, COL_KV // LANES + k)),
            pl.BlockSpec((1, CMP_STRIDE * hd, 2 * CMP_HIDDEN), lambda k, bi: (k, 0, 0)),
            pl.BlockSpec((1, CMP_HIDDEN, hd), lambda k, bi: (k, 0, 0)),
            pl.BlockSpec((1, 8, CMP_STRIDE * hd), lambda k, bi: (k, 0, 0)),
            pl.BlockSpec((1, nc, 1), lambda k, bi: (bi, 0, 0)),
            const((1, hd)), const((hd, hd)), const((1, hd)),
        ],
        out_specs=pl.BlockSpec((1, 1, g, nc, 2 * hd), lambda k, bi: (k, bi, 0, 0, 0)),
        out_shape=jax.ShapeDtypeStruct((2, b, g, nc, 2 * hd), BF16),
        scratch_shapes=[pltpu.VMEM((s, LANES), F32)],
        compiler_params=_cparams("parallel", "parallel"),
        name="nsa_compress",
    )(z3, w1cat, w2, pe2, pos_cmp, k_norm0.reshape(1, hd), _rope_rot_matrix(hd), _rope_freq_head())


def _softmax_rows(s, mask):
    s = jnp.where(mask, s, NEG_BIG)
    m = jnp.max(s, axis=-1, keepdims=True)
    p = jnp.where(mask, jnp.exp(s - m), 0.0)
    d = jnp.sum(p, axis=-1, keepdims=True)
    return p / jnp.where(d > 0, d, 1.0)


_NT = (((1,), (1,)), ((), ()))


def _select_blocks(p_sum, ov_ref, t0, tq, n_sel, n_live=None):
    n_blk = ov_ref.shape[0]
    n_live = n_blk if n_live is None else n_live
    p_hi = p_sum.astype(BF16)
    p_lo = (p_sum - p_hi.astype(F32)).astype(BF16)
    ov = ov_ref[:, 0:p_sum.shape[1]]
    imp_t = (lax.dot_general(ov, p_hi, _NT, preferred_element_type=F32)
             + lax.dot_general(ov, p_lo, _NT, preferred_element_type=F32))
    j_col = lax.broadcasted_iota(jnp.int32, (n_blk, 1), 0)
    t_lane = t0 + lax.broadcasted_iota(jnp.int32, (1, tq), 1)
    causal = j_col * SEL_BLOCK <= t_lane
    cur = t_lane >> 6
    forced = causal & ((j_col == 0) | (j_col == cur) | (j_col == cur - 1))
    score = jnp.where(forced, FORCE_SCORE, jnp.where(causal, imp_t, -FORCE_SCORE))
    ng = n_live // 8
    groups = [score[8 * v:8 * v + 8] for v in range(ng)]
    ranks = [jnp.zeros((8, tq), F32) for _ in range(ng)]
    sub = lax.broadcasted_iota(jnp.int32, (8, tq), 0)
    for jp in range(n_live):
        row = jnp.broadcast_to(score[jp:jp + 1, :], (8, tq))
        vj = jp // 8
        for v in range(ng):
            if v < vj:
                ahead = row > groups[v]
            elif v > vj:
                ahead = row >= groups[v]
            else:
                ahead = (row > groups[v]) | ((row == groups[v]) & (sub > jp % 8))
            ranks[v] = ranks[v] + jnp.where(ahead, 1.0, 0.0)
    if n_live < n_blk:
        ranks.append(jnp.full((n_blk - n_live, tq), float(n_blk), F32))
    sel_t = jnp.where(jnp.concatenate(ranks, axis=0) < n_sel, 1.0, 0.0).astype(BF16)
    eye = (lax.broadcasted_iota(jnp.int32, (tq, tq), 0) == lax.broadcasted_iota(jnp.int32, (tq, tq), 1)).astype(BF16)
    return lax.dot_general(eye, sel_t, _NT, preferred_element_type=F32)


def _sum_heads(p, tq):
    out = p[0:tq]
    for r in range(1, NSA_REP):
        out = out + p[r * tq:(r + 1) * tq]
    return out


def _store_heads(o_ref, o, tq, row0):
    hd = NSA_HEAD_DIM
    for r in range(NSA_REP):
        o_ref[0, row0:row0 + tq, r * hd:(r + 1) * hd] = o[r * tq:(r + 1) * tq].astype(o_ref.dtype)


def _store_selection(sel_ref, sel, row0):
    tq, n_blk = sel.shape
    sel = sel.astype(sel_ref.dtype)
    if n_blk < NSA_HEAD_DIM:
        sel = jnp.concatenate([sel, jnp.zeros((tq, NSA_HEAD_DIM - n_blk), sel_ref.dtype)], axis=1)
    sel_ref[0, 0, row0:row0 + tq] = sel


def _gate_maps(gl_ref, gp_ref, gm_ref):
    gates = jax.nn.sigmoid(gl_ref[0].astype(F32))
    g_hi = gates.astype(BF16)
    g_lo = (gates - g_hi.astype(F32)).astype(BF16)
    gm_ref[...] = (jnp.dot(g_hi, gp_ref[...], preferred_element_type=F32)
                   + jnp.dot(g_lo, gp_ref[...], preferred_element_type=F32))


def _to_lane_layout(lay_ref, idx, o, tq, lane0=0):
    hd = NSA_HEAD_DIM
    for r in range(NSA_REP):
        lay_ref[idx, :, r * hd:(r + 1) * hd] = o[r * tq:(r + 1) * tq, lane0:lane0 + hd]


def _gated_store(gm_ref, oc_ref, lay_ref, o_ref, with_sums):
    w = NSA_REP * NSA_HEAD_DIM
    o_sel, o_win = lay_ref[0], lay_ref[1]
    if with_sums:
        o_sel, o_win = o_sel / lay_ref[2], o_win / lay_ref[3]
    out = gm_ref[:, 0:w] * oc_ref[0].astype(F32) + gm_ref[:, w:2 * w] * o_sel + gm_ref[:, 2 * w:3 * w] * o_win
    o_ref[0] = out.astype(o_ref.dtype)


def _gate_placement():
    m = np.zeros((LANES, 3 * NSA_REP * NSA_HEAD_DIM), np.float32)
    for br in range(3):
        for r in range(NSA_REP):
            c0 = br * NSA_REP * NSA_HEAD_DIM + r * NSA_HEAD_DIM
            m[br * NSA_REP + r, c0:c0 + NSA_HEAD_DIM] = 1.0
    return jnp.asarray(m).astype(BF16)


def _sel_fast(sh_ref, q_ref, kc_ref, vc_ref, ov_ref, oc_ref, sel_ref, *, tq, n_sel):
    rep, hd = NSA_REP, NSA_HEAD_DIM
    rows = rep * tq
    tb = q_ref.shape[2]
    n_blk = ov_ref.shape[0]

    def body(n_live):
        n_all = kc_ref.shape[3]
        ncp = min(n_all, n_live * (SEL_BLOCK // CMP_STRIDE))
        kc = kc_ref[0, 0, 0][0:ncp, 0:hd]
        vc = vc_ref[0, 0, 0][0:ncp]
        n_idx = lax.broadcasted_iota(jnp.int32, (1, ncp), 1)
        for part in range(tb // tq):
            row0 = part * tq
            t0 = pl.program_id(2) * tb + row0
            q4 = q_ref[0, :, row0:row0 + tq, :].reshape(rows, hd)
            t_q = t0 + lax.broadcasted_iota(jnp.int32, (tq, 1), 0)
            valid_c = (n_idx * CMP_STRIDE + (CMP_BLOCK - 1) <= t_q) & (n_idx < n_all - 1)
            bias_c = jnp.where(valid_c, sh_ref[0], NEG_BIG)
            s_c = lax.dot_general(q4, kc, _NT, preferred_element_type=F32).reshape(rep, tq, ncp)
            p_c = jnp.exp(s_c + bias_c[None]).reshape(rows, ncp)
            ol_c = jnp.dot(p_c.astype(BF16), vc, preferred_element_type=F32)
            inv_c = 1.0 / jnp.where(ol_c[:, hd:hd + 1] > 0, ol_c[:, hd:hd + 1], 1.0)
            _store_heads(oc_ref, ol_c[:, 0:hd] * inv_c, tq, row0)
            sel = _select_blocks(_sum_heads(p_c * inv_c, tq), ov_ref, t0, tq, n_sel, n_live)
            _store_selection(sel_ref, sel, row0)

    regions = SELECT_REGIONS if (n_blk % (8 * SELECT_REGIONS) == 0
                                 and (n_blk // SELECT_REGIONS * SEL_BLOCK) % tb == 0) else 1
    steps = n_blk // regions * SEL_BLOCK // tb
    step = pl.program_id(2)
    for r in range(regions):
        pl.when((step >= r * steps) & (step < (r + 1) * steps))(functools.partial(body, (r + 1) * (n_blk // regions)))


def _sel_slow(q_ref, kc_ref, vc_ref, ov_ref, oc_ref, sel_ref, *, tq, n_sel):
    rep, hd = NSA_REP, NSA_HEAD_DIM
    rows = rep * tq
    kc = kc_ref[0, 0, 0][:, 0:hd]
    ncp = kc.shape[0]
    n_idx = lax.broadcasted_iota(jnp.int32, (1, ncp), 1)
    for part in range(q_ref.shape[2] // tq):
        row0 = part * tq
        t0 = pl.program_id(2) * q_ref.shape[2] + row0
        q4 = q_ref[0, :, row0:row0 + tq, :].reshape(rows, hd)
        t_row = t0 + (lax.broadcasted_iota(jnp.int32, (rows, 1), 0) & (tq - 1))
        s_c = lax.dot_general(q4, kc, _NT, preferred_element_type=F32)
        p_c = _softmax_rows(s_c, (n_idx * CMP_STRIDE + (CMP_BLOCK - 1) <= t_row) & (n_idx < ncp - 1))
        o_c = jnp.dot(p_c.astype(BF16), vc_ref[0, 0, 0], preferred_element_type=F32)[:, 0:hd]
        _store_heads(oc_ref, o_c, tq, row0)
        _store_selection(sel_ref, _select_blocks(_sum_heads(p_c, tq), ov_ref, t0, tq, n_sel), row0)


def _nsa_select_kernel(sh_ref, q_ref, kc_ref, vc_ref, ov_ref, oc_ref, sel_ref, *, tq, n_sel):
    data = (q_ref, kc_ref, vc_ref, ov_ref, oc_ref, sel_ref)

    @pl.when(sh_ref[3] > 0.5)
    def _():
        _sel_fast(sh_ref, *data, tq=tq, n_sel=n_sel)

    @pl.when(sh_ref[3] <= 0.5)
    def _():
        _sel_slow(*data, tq=tq, n_sel=n_sel)


def _att_fast(sh_ref, q_ref, ks_ref, vs_ref, kw_ref, vw_ref, gl_ref, gp_ref, oc_ref, sel_ref, o_ref,
              lhs_ref, acc_ref, pre_ref, gm_ref, lay_ref, *, tq, tk):
    rep, hd = NSA_REP, NSA_HEAD_DIM
    rows = rep * tq
    t0 = pl.program_id(2) * tq
    c_s, c_w = sh_ref[1], sh_ref[2]
    q4 = q_ref[0].reshape(rows, hd)
    t_q = t0 + lax.broadcasted_iota(jnp.int32, (tq, 1), 0)

    j_lane = lax.broadcasted_iota(jnp.int32, (1, hd), 1)
    shift = jnp.where((sel_ref[0, 0].astype(F32) > 0.5) & (j_lane * SEL_BLOCK < t0), c_s, NEG_BIG).astype(BF16)
    lhs_ref[:, 0:hd] = q4
    for r in range(rep):
        lhs_ref[r * tq:(r + 1) * tq, hd:2 * hd] = shift

    span = WINDOW + tq
    w0 = pl.multiple_of(jnp.maximum(t0 - WINDOW, 0), tq)
    kp = w0 + lax.broadcasted_iota(jnp.int32, (1, span), 1)
    bias_w = jnp.where((kp <= t_q) & (kp > t_q - WINDOW), c_w, NEG_BIG)
    s_w = lax.dot_general(q4, kw_ref[0, 0, pl.ds(w0, span), :], _NT, preferred_element_type=F32)
    p_w = jnp.exp(s_w.reshape(rep, tq, span) + bias_w[None]).reshape(rows, span)
    pre_ref[...] = jnp.dot(p_w.astype(BF16), vw_ref[0, 0, pl.ds(w0, span), :], preferred_element_type=F32)

    d0 = pl.multiple_of(t0, tq)
    kcol = t0 + lax.broadcasted_iota(jnp.int32, (1, tq), 1)
    bias_d = jnp.where(kcol <= t_q, c_s, NEG_BIG)
    s_d = lax.dot_general(q4, ks_ref[0, 0, pl.ds(d0, tq), :][:, 0:hd], _NT, preferred_element_type=F32)
    p_d = jnp.exp(s_d.reshape(rep, tq, tq) + bias_d[None]).reshape(rows, tq)
    acc_ref[...] = jnp.dot(p_d.astype(BF16), vs_ref[0, 0, pl.ds(d0, tq), :], preferred_element_type=F32)

    def key_tile(k0, width):
        s = lax.dot_general(lhs_ref[...], ks_ref[0, 0, pl.ds(k0, width), :], _NT, preferred_element_type=F32)
        acc_ref[...] += jnp.dot(jnp.exp(s).astype(BF16), vs_ref[0, 0, pl.ds(k0, width), :],
                                preferred_element_type=F32)

    done = 0
    for mult in KEY_TILE_MULTIPLES:
        width = mult * tk
        if width > ks_ref.shape[2]:
            continue
        left = t0 - done
        n = left // width if mult > 1 else (left + width - 1) // width

        def step(c, carry, width=width, base=done):
            key_tile(pl.multiple_of(base + c * width, tk), width)
            return carry

        lax.fori_loop(0, n, step, 0)
        done = done + n * width

    for idx, ol in ((0, acc_ref[...]), (1, pre_ref[...])):
        _to_lane_layout(lay_ref, idx, ol, tq)
        _to_lane_layout(lay_ref, 2 + idx, ol, tq, lane0=hd)
    _gate_maps(gl_ref, gp_ref, gm_ref)
    _gated_store(gm_ref, oc_ref, lay_ref, o_ref, with_sums=True)


def _att_slow(q_ref, ks_ref, vs_ref, kw_ref, vw_ref, gl_ref, gp_ref, oc_ref, sel_ref, o_ref,
              acc_ref, gm_ref, lay_ref, m_ref, l_ref, *, tq, tk):
    rep, hd = NSA_REP, NSA_HEAD_DIM
    rows = rep * tq
    t0 = pl.program_id(2) * tq
    q4 = q_ref[0].reshape(rows, hd)
    t_row = t0 + (lax.broadcasted_iota(jnp.int32, (rows, 1), 0) & (tq - 1))
    sel = sel_ref[0, 0]
    n_blk = sel.shape[1]

    m_ref[...] = jnp.full(m_ref.shape, NEG_BIG, F32)
    l_ref[...] = jnp.zeros(l_ref.shape, F32)
    acc_ref[...] = jnp.zeros(acc_ref.shape, F32)
    t_q = t0 + lax.broadcasted_iota(jnp.int32, (tq, 1), 0)

    def kv_step(c, carry):
        k0 = pl.multiple_of(c * tk, tk)
        kt = ks_ref[0, 0, pl.ds(k0, tk), :][:, 0:hd]
        s = lax.dot_general(q4, kt, _NT, preferred_element_type=F32).reshape(rep, tq, tk)
        kk = lax.broadcasted_iota(jnp.int32, (1, tk), 1)
        blk = (k0 >> 6) + (kk >> 6)
        expand = jnp.where(lax.broadcasted_iota(jnp.int32, (n_blk, 1), 0) == blk, 1.0, 0.0).astype(BF16)
        chosen = jnp.dot(sel, expand, preferred_element_type=F32)
        mask = ((chosen > 0.5) & (k0 + kk <= t_q))[None]
        s = jnp.where(mask, s, NEG_BIG)
        m_old = m_ref[...]
        m_new = jnp.maximum(m_old, jnp.max(s, axis=-1, keepdims=True))
        p = jnp.where(mask, jnp.exp(s - m_new), 0.0)
        alpha = jnp.exp(m_old - m_new)
        l_ref[...] = alpha * l_ref[...] + jnp.sum(p, axis=-1, keepdims=True)
        pv = jnp.dot(p.reshape(rows, tk).astype(BF16), vs_ref[0, 0, pl.ds(k0, tk), :], preferred_element_type=F32)
        acc_ref[...] = alpha.reshape(rows, 1) * acc_ref[...] + pv
        m_ref[...] = m_new
        return carry

    lax.fori_loop(0, (t0 + tq + tk - 1) // tk, kv_step, 0)
    o_s = acc_ref[:, 0:hd] / l_ref[...].reshape(rows, 1)

    span = WINDOW + tq
    w0 = pl.multiple_of(jnp.maximum(t0 - WINDOW, 0), tq)
    s_w = lax.dot_general(q4, kw_ref[0, 0, pl.ds(w0, span), :], _NT, preferred_element_type=F32)
    kp = w0 + lax.broadcasted_iota(jnp.int32, (1, span), 1)
    p_w = _softmax_rows(s_w, (kp <= t_row) & (kp > t_row - WINDOW))
    o_w = jnp.dot(p_w.astype(BF16), vw_ref[0, 0, pl.ds(w0, span), :], preferred_element_type=F32)[:, 0:hd]

    _to_lane_layout(lay_ref, 0, o_s, tq)
    _to_lane_layout(lay_ref, 1, o_w, tq)
    _gate_maps(gl_ref, gp_ref, gm_ref)
    _gated_store(gm_ref, oc_ref, lay_ref, o_ref, with_sums=False)


def _nsa_attend_kernel(sh_ref, q_ref, ks_ref, vs_ref, kw_ref, vw_ref, gl_ref, gp_ref, oc_ref, sel_ref, o_ref,
                       lhs_ref, acc_ref, pre_ref, gm_ref, lay_ref, m_ref, l_ref, *, tq, tk):
    data = (q_ref, ks_ref, vs_ref, kw_ref, vw_ref, gl_ref, gp_ref, oc_ref, sel_ref, o_ref)

    @pl.when(sh_ref[3] > 0.5)
    def _():
        _att_fast(sh_ref, *data, lhs_ref, acc_ref, pre_ref, gm_ref, lay_ref, tq=tq, tk=tk)

    @pl.when(sh_ref[3] <= 0.5)
    def _():
        _att_slow(*data, acc_ref, gm_ref, lay_ref, m_ref, l_ref, tq=tq, tk=tk)


def _overlap_t(n_blk, ncp):
    c_start = np.arange(ncp) * CMP_STRIDE
    b_start = np.arange(n_blk) * SEL_BLOCK
    ov = ((c_start[None, :] < b_start[:, None] + SEL_BLOCK) & (b_start[:, None] < c_start[None, :] + CMP_BLOCK))
    ov[:, ncp - 1] = False
    return jnp.asarray(ov.astype(np.float32)).astype(BF16)


def _nsa_shifts(q_norm, k_norm):
    bound = (NSA_HEAD_DIM ** 0.5) * jnp.max(jnp.abs(q_norm)) * jnp.max(jnp.abs(k_norm), axis=-1)
    bound = bound.astype(BF16).astype(F32)
    fast = jnp.all(bound <= MAX_CONST_SHIFT).astype(F32)
    return jnp.concatenate([-bound, fast[None]])


def _nsa_select(shifts, q, cmp_kv, *, tq, parts):
    b, h, s, hd = q.shape
    g, rep = NSA_GROUPS, NSA_REP
    ncp = cmp_kv.shape[3]
    n_blk = s // SEL_BLOCK
    tb = tq * parts
    assert n_blk % 8 == 0 and n_blk <= hd and s % tb == 0
    kernel = functools.partial(_nsa_select_kernel, tq=tq, n_sel=min(N_SEL, n_blk))
    return pl.pallas_call(
        kernel,
        grid=(b, g, s // tb),
        in_specs=[
            pl.BlockSpec(memory_space=pltpu.SMEM),
            pl.BlockSpec((1, rep, tb, hd), lambda bi, gi, i: (bi, gi, i, 0)),
            pl.BlockSpec((1, 1, 1, ncp, 2 * hd), lambda bi, gi, i: (0, bi, gi, 0, 0)),
            pl.BlockSpec((1, 1, 1, ncp, 2 * hd), lambda bi, gi, i: (1, bi, gi, 0, 0)),
            pl.BlockSpec((n_blk, ncp), lambda bi, gi, i: (0, 0)),
        ],
        out_specs=[pl.BlockSpec((1, tb, rep * hd), lambda bi, gi, i: (bi, i, gi)),
                   pl.BlockSpec((1, 1, tb, hd), lambda bi, gi, i: (bi, gi, i, 0))],
        out_shape=[jax.ShapeDtypeStruct((b, s, h * hd), BF16), jax.ShapeDtypeStruct((b, g, s, hd), BF16)],
        compiler_params=_cparams("parallel", "parallel", "parallel"),
        name="nsa_select",
    )(shifts, q, cmp_kv, cmp_kv, _overlap_t(n_blk, ncp))


def _nsa_attend(shifts, q, ks, vs, kw, vw, z3, o_cmp, sel, *, tq, tk):
    b, h, s, hd = q.shape
    g, rep = NSA_GROUPS, NSA_REP
    rows = rep * tq
    assert WINDOW % tq == 0
    full = lambda w: pl.BlockSpec((1, 1, s, w), lambda bi, gi, i: (bi, gi, 0, 0))
    return pl.pallas_call(
        functools.partial(_nsa_attend_kernel, tq=tq, tk=tk),
        grid=(b, g, s // tq),
        in_specs=[
            pl.BlockSpec(memory_space=pltpu.SMEM),
            pl.BlockSpec((1, rep, tq, hd), lambda bi, gi, i: (bi, gi, i, 0)),
            full(2 * hd), full(2 * hd), full(hd), full(2 * hd),
            pl.BlockSpec((1, tq, LANES), lambda bi, gi, i: (bi, i, COL_SMALL // LANES + gi)),
            pl.BlockSpec((LANES, 3 * rep * hd), lambda bi, gi, i: (0, 0)),
            pl.BlockSpec((1, tq, rep * hd), lambda bi, gi, i: (bi, i, gi)),
            pl.BlockSpec((1, 1, tq, hd), lambda bi, gi, i: (bi, gi, i, 0)),
        ],
        out_specs=pl.BlockSpec((1, tq, rep * hd), lambda bi, gi, i: (bi, i, gi)),
        out_shape=jax.ShapeDtypeStruct((b, s, h * hd), BF16),
        scratch_shapes=[pltpu.VMEM((rows, 2 * hd), BF16), pltpu.VMEM((rows, 2 * hd), F32),
                        pltpu.VMEM((rows, 2 * hd), F32), pltpu.VMEM((tq, 3 * rep * hd), F32),
                        pltpu.VMEM((4, tq, rep * hd), F32),
                        pltpu.VMEM((rep, tq, 1), F32), pltpu.VMEM((rep, tq, 1), F32)],
        compiler_params=_cparams("parallel", "parallel", "arbitrary"),
        name="nsa_attend",
    )(shifts, q, ks, vs, kw, vw, z3, _gate_placement(), o_cmp, sel)


def _head_block_diag(x, n_rows, head_of_lane, dtype):
    return jnp.concatenate([jnp.where(head_of_lane == h, x, 0.0) for h in range(GLA_HEADS)], axis=0).astype(dtype)


def _gla_kernel(q_ref, k_ref, v_ref, r_ref, sm_ref, wg_ref, bg_ref, ng_ref, tril_ref, o_ref,
                st_ref, upd_ref, oin_ref, qin_ref, *, n_chunks):
    c, sub = GLA_CHUNK, GLA_SUB
    dk, dv, nh = GLA_HEAD_DK, GLA_HEAD_DV, GLA_HEADS
    nk, nv = nh * dk, nh * dv

    @pl.when(pl.program_id(1) == 0)
    def _():
        st_ref[...] = jnp.zeros(st_ref.shape, F32)

    sm = sm_ref[0]
    x = (jnp.dot(sm, wg_ref[0], preferred_element_type=F32) + jnp.dot(sm, wg_ref[1], preferred_element_type=F32)
         + bg_ref[...])
    log_a = (jnp.minimum(x, 0.0) - jnp.log1p(jnp.exp(-jnp.abs(x)))) / GLA_TAU
    la_hi = log_a.astype(BF16)
    la_lo = (log_a - la_hi.astype(F32)).astype(BF16)
    bcum_all = (jnp.dot(tril_ref[...], la_hi, preferred_element_type=F32)
                + jnp.dot(tril_ref[...], la_lo, preferred_element_type=F32))
    head_k = lax.broadcasted_iota(jnp.int32, (1, nk), 1) >> 6
    head_v = lax.broadcasted_iota(jnp.int32, (1, nv), 1) >> 7
    state_mask = (lax.broadcasted_iota(jnp.int32, (nv, 1), 0) >> 7) == head_k
    causal = ((lax.broadcasted_iota(jnp.int32, (c, nh * c), 1) & (c - 1))
              <= lax.broadcasted_iota(jnp.int32, (c, nh * c), 0))

    decays = []
    for cc in range(n_chunks):
        rs = slice(cc * c, (cc + 1) * c)
        bcum = bcum_all[rs]
        q = q_ref[0, rs].astype(F32) * (dk ** -0.5)
        k = k_ref[0, rs].astype(F32)
        v = v_ref[0, rs].astype(F32)
        b_last = bcum[c - 1:c, :]

        score_rows = []
        for i in range(c // sub):
            lo, hi = i * sub, (i + 1) * sub
            ref = bcum[lo - 1:lo, :] if i > 0 else jnp.zeros((1, nk), F32)
            q_i = (q[lo:hi] * jnp.exp(bcum[lo:hi] - ref)).astype(BF16)
            k_i = k[0:hi] * jnp.exp(ref - bcum[0:hi])
            if hi < c:
                k_i = jnp.concatenate([k_i, jnp.zeros((c - hi, nk), F32)], axis=0)
            score_rows.append(lax.dot_general(q_i, _head_block_diag(k_i, c, head_k, BF16), _NT,
                                              preferred_element_type=F32))
        a = jnp.where(causal, jnp.concatenate(score_rows, axis=0), 0.0).astype(BF16)
        oin_ref[rs] = jnp.dot(a, _head_block_diag(v, c, head_v, BF16), preferred_element_type=F32)
        qin_ref[rs] = (q * jnp.exp(bcum)).astype(BF16)
        k_out = (k * jnp.exp(b_last - bcum)).astype(BF16)
        upd = lax.dot_general(v.astype(BF16), k_out, (((0,), (0,)), ((), ())), preferred_element_type=F32)
        upd_ref[cc] = jnp.where(state_mask, upd, 0.0)
        decays.append(jnp.exp(b_last))

    for cc in range(n_chunks):
        rs = slice(cc * c, (cc + 1) * c)
        st = st_ref[...]
        o = oin_ref[rs] + lax.dot_general(qin_ref[rs], st.astype(BF16), _NT, preferred_element_type=F32)
        st_ref[...] = st * decays[cc] + upd_ref[cc]
        r_gate = r_ref[0, rs].astype(F32)
        for h in range(nh):
            vs_ = slice(h * dv, (h + 1) * dv)
            o_h = o[:, vs_]
            ms = jnp.mean(o_h * o_h, axis=-1, keepdims=True)
            rg = r_gate[:, vs_]
            o_ref[0, rs, vs_] = (o_h * lax.rsqrt(ms + NORM_EPS) * ng_ref[...] * (rg * jax.nn.sigmoid(rg))).astype(o_ref.dtype)


def _gla(z3, w_gate, b_gate, norm_g, *, n_chunks):
    b, s, _ = z3.shape
    c = GLA_CHUNK * n_chunks
    nk = GLA_HEADS * GLA_HEAD_DK
    nv = GLA_HEADS * GLA_HEAD_DV
    wg = jnp.zeros((LANES, nk), F32).at[SMALL_GLOW_LANE:SMALL_GLOW_LANE + GLA_RANK].set(w_gate.astype(F32))
    wg_hi = wg.astype(BF16)
    wg = jnp.stack([wg_hi, (wg - wg_hi.astype(F32)).astype(BF16)])
    idx = np.arange(c)
    tril = jnp.asarray(((idx[:, None] >= idx[None, :])
                        & (idx[:, None] // GLA_CHUNK == idx[None, :] // GLA_CHUNK)).astype(np.float32)).astype(BF16)
    const = lambda shape: pl.BlockSpec(shape, lambda bi, i: (0,) * len(shape))
    return pl.pallas_call(
        functools.partial(_gla_kernel, n_chunks=n_chunks),
        grid=(b, s // c),
        in_specs=[
            pl.BlockSpec((1, c, nk), lambda bi, i: (bi, i, COL_GQ // nk)),
            pl.BlockSpec((1, c, nk), lambda bi, i: (bi, i, COL_GK // nk)),
            pl.BlockSpec((1, c, nv), lambda bi, i: (bi, i, COL_GV // nv)),
            pl.BlockSpec((1, c, nv), lambda bi, i: (bi, i, COL_GR // nv)),
            pl.BlockSpec((1, c, LANES), lambda bi, i: (bi, i, COL_SMALL // LANES)),
            const((2, LANES, nk)), const((1, nk)), const((1, GLA_HEAD_DV)), const((c, c)),
        ],
        out_specs=pl.BlockSpec((1, c, nv), lambda bi, i: (bi, i, 0)),
        out_shape=jax.ShapeDtypeStruct((b, s, nv), BF16),
        scratch_shapes=[pltpu.VMEM((nv, nk), F32), pltpu.VMEM((n_chunks, nv, nk), F32),
                        pltpu.VMEM((c, nv), F32), pltpu.VMEM((c, nk), BF16)],
        compiler_params=_cparams("parallel", "arbitrary"),
        name="gla",
    )(z3, z3, z3, z3, z3, wg, b_gate.reshape(1, nk).astype(F32), norm_g.reshape(1, GLA_HEAD_DV).astype(F32), tril)


def _mem_attn_kernel(q_ref, k_ref, v_ref, qg_ref, kg_ref, o_ref):
    dh = MEM_HEAD_DIM
    nt = (((1,), (1,)), ((), ()))
    for h in range(MEM_HEADS):
        sl = slice(h * dh, (h + 1) * dh)
        q = q_ref[0, :, sl].astype(F32)
        q = q * lax.rsqrt(jnp.mean(q * q, axis=-1, keepdims=True) + NORM_EPS) * qg_ref[...] * (dh ** -0.5)
        k = k_ref[0, :, sl].astype(F32)
        k = k * lax.rsqrt(jnp.mean(k * k, axis=-1, keepdims=True) + NORM_EPS) * kg_ref[...]
        s = lax.dot_general(q.astype(BF16), k.astype(BF16), nt, preferred_element_type=F32)
        m = jnp.max(s, axis=-1, keepdims=True)
        p = jnp.exp(s - m)
        p = p / jnp.sum(p, axis=-1, keepdims=True)
        o = jnp.dot(p.astype(BF16), v_ref[0, :, sl].astype(BF16), preferred_element_type=F32)
        o_ref[0, :, sl] = o.astype(o_ref.dtype)


def _mem_attention(z3, kv, q_norm, k_norm, *, tq):
    b, s, _ = z3.shape
    m = kv.shape[1]
    w = MEM_HEADS * MEM_HEAD_DIM
    const = lambda shape: pl.BlockSpec(shape, lambda bi, i: (0,) * len(shape))
    return pl.pallas_call(
        _mem_attn_kernel,
        grid=(b, s // tq),
        in_specs=[
            pl.BlockSpec((1, tq, w), lambda bi, i: (bi, i, COL_MQ // w)),
            pl.BlockSpec((1, m, w), lambda bi, i: (bi, 0, 0)),
            pl.BlockSpec((1, m, w), lambda bi, i: (bi, 0, 1)),
            const((1, MEM_HEAD_DIM)), const((1, MEM_HEAD_DIM)),
        ],
        out_specs=pl.BlockSpec((1, tq, w), lambda bi, i: (bi, i, 0)),
        out_shape=jax.ShapeDtypeStruct((b, s, w), BF16),
        compiler_params=_cparams("parallel", "parallel"),
        name="mem_attention",
    )(z3, kv, kv, q_norm.reshape(1, MEM_HEAD_DIM).astype(F32), k_norm.reshape(1, MEM_HEAD_DIM).astype(F32))


def _merge_kernel(x_ref, on_ref, og_ref, om_ref, m0_ref, m1_ref, m2_ref, bm_ref, wb_ref, wo_ref, o_ref):
    merged = None
    for br, (ref, mg_ref) in enumerate(((on_ref, m0_ref), (og_ref, m1_ref), (om_ref, m2_ref))):
        y = jnp.dot(ref[...], wb_ref[br], preferred_element_type=F32)
        gate = jax.nn.sigmoid(mg_ref[...].astype(F32) + bm_ref[br:br + 1, :])
        merged = gate * y if merged is None else merged + gate * y
    o_ref[...] = x_ref[...] + jnp.dot(merged.astype(BF16), wo_ref[...], preferred_element_type=F32)


def _merge_out(x2, o_nsa, o_gla, o_mem, z2, b_merge, w_branch, w_out, layer, *, tm):
    t, d = x2.shape
    bw = BRANCH_WIDTH
    row = lambda w: pl.BlockSpec((tm, w), lambda i: (i, 0))
    gate_cols = lambda br: pl.BlockSpec((tm, d), lambda i: (i, COL_MERGE // d + br))
    return pl.pallas_call(
        _merge_kernel,
        grid=(t // tm,),
        in_specs=[
            row(d), row(bw), row(bw), row(bw),
            gate_cols(0), gate_cols(1), gate_cols(2),
            pl.BlockSpec((N_BRANCH, d), lambda i: (0, 0)),
            pl.BlockSpec((None, N_BRANCH, bw, d), lambda i: (layer, 0, 0, 0)),
            pl.BlockSpec((None, d, d), lambda i: (layer, 0, 0)),
        ],
        out_specs=row(d),
        out_shape=jax.ShapeDtypeStruct((t, d), F32),
        compiler_params=_cparams("parallel"),
        name="merge_out",
    )(x2, o_nsa, o_gla, o_mem, z2, z2, z2, b_merge.astype(F32), w_branch, w_out)


def _mlp_kernel(x_ref, g_ref, wu_ref, wd_ref, o_ref, h_ref):
    @pl.when(pl.program_id(1) == 0)
    def _():
        h_ref[...] = _rms_rows(x_ref[...], g_ref[...])
        o_ref[...] = x_ref[...]

    u = jnp.dot(h_ref[...], wu_ref[...], preferred_element_type=F32)
    u = jnp.square(jnp.maximum(u, 0.0)).astype(BF16)
    o_ref[...] += jnp.dot(u, wd_ref[...], preferred_element_type=F32)


def _mlp(x2, g, w_up, w_down, layer, *, tm, th):
    t, d = x2.shape
    hid = w_up.shape[2]
    return pl.pallas_call(
        _mlp_kernel,
        grid=(t // tm, hid // th),
        in_specs=[
            pl.BlockSpec((tm, d), lambda i, j: (i, 0)),
            pl.BlockSpec((1, d), lambda i, j: (0, 0)),
            pl.BlockSpec((None, d, th), lambda i, j: (layer, 0, j)),
            pl.BlockSpec((None, th, d), lambda i, j: (layer, j, 0)),
        ],
        out_specs=pl.BlockSpec((tm, d), lambda i, j: (i, 0)),
        out_shape=jax.ShapeDtypeStruct((t, d), F32),
        scratch_shapes=[pltpu.VMEM((tm, d), BF16)],
        compiler_params=_cparams("parallel", "arbitrary"),
        name="mlp",
    )(x2, g.reshape(1, d).astype(F32), w_up, w_down)


IN_PROJ_ROWS, IN_PROJ_COL_TILES = 1024, 2
NSA_PREP_ROWS = 512
NSA_QUERY_TILE, NSA_SELECT_PARTS, NSA_KEY_TILE = 256, 2, 512
SELECT_REGIONS = 4
KEY_TILE_MULTIPLES = (4, 2, 1)
GLA_CHUNKS_PER_STEP = 8
MEM_KV_ROWS, MEM_KV_COLS, MEM_QUERY_TILE = 512, 512, 1024
MERGE_ROWS = 512
MLP_ROWS, MLP_HIDDEN_TILE = 1024, 1024


def _tile(n, pref):
    t = min(n, pref)
    assert n % t == 0, (n, pref)
    return t


def _layer(x, mem2, pos_rows, pos_cmp, p, big, layer):
    b, s, d = x.shape
    t = b * s
    hd = NSA_HEAD_DIM
    x2 = x.reshape(t, d)

    z2 = _norm_matmul(x2, p["ln_mix"].astype(F32), big["w_in"], layer, tm=_tile(t, IN_PROJ_ROWS),
                      tn=D_IN_PAD // IN_PROJ_COL_TILES, out_dtype=BF16, name="in_proj")
    z3 = z2.reshape(b, s, D_IN_PAD)

    q, ks, vs, kw, vw = _nsa_prep(z3, pos_rows, p["nsa_q_norm"].astype(F32), p["nsa_k_norm"][1].astype(F32),
                                  p["nsa_k_norm"][2].astype(F32), tp=_tile(s, NSA_PREP_ROWS))
    pe2 = jnp.pad(p["cmp_pe"].reshape(2, 2, CMP_STRIDE * hd), ((0, 0), (0, 6), (0, 0))).astype(F32)
    cmp_kv = _nsa_compress(z3, p["cmp_w1"], p["cmp_w2"], pe2, pos_cmp, p["nsa_k_norm"][0].astype(F32))
    shifts = _nsa_shifts(p["nsa_q_norm"].astype(F32), p["nsa_k_norm"].astype(F32))
    tq = _tile(s, NSA_QUERY_TILE)
    o_cmp, sel = _nsa_select(shifts, q, cmp_kv, tq=tq, parts=NSA_SELECT_PARTS)
    o_nsa = _nsa_attend(shifts, q, ks, vs, kw, vw, z3, o_cmp, sel, tq=tq, tk=_tile(s, NSA_KEY_TILE))

    o_gla = _gla(z3, p["gla_w_gate"], p["gla_b_gate"], p["gla_norm"], n_chunks=GLA_CHUNKS_PER_STEP)

    kv = _norm_matmul(mem2, p["mem_norm"].astype(F32), big["mem_w_kv"], layer,
                      tm=_tile(mem2.shape[0], MEM_KV_ROWS), tn=MEM_KV_COLS, out_dtype=BF16, name="mem_kv")
    kv = kv.reshape(b, mem2.shape[0] // b, 2 * MEM_HEADS * MEM_HEAD_DIM)
    o_mem = _mem_attention(z3, kv, p["mem_q_norm"], p["mem_k_norm"], tq=_tile(s, MEM_QUERY_TILE))

    x2 = _merge_out(x2, o_nsa.reshape(t, -1), o_gla.reshape(t, -1), o_mem.reshape(t, -1), z2,
                    p["b_merge"], big["w_branch"], big["w_out"], layer, tm=_tile(t, MERGE_ROWS))
    x2 = _mlp(x2, p["ln_mlp"], big["w_up"], big["w_down"], layer, tm=_tile(t, MLP_ROWS), th=MLP_HIDDEN_TILE)
    return x2.reshape(b, s, d)


def kernel(x, mem, positions, ln_mix, w_in, b_merge, nsa_q_norm, nsa_k_norm, cmp_pe, cmp_w1, cmp_w2,
           gla_w_gate, gla_b_gate, gla_norm, mem_norm, mem_w_kv, mem_q_norm, mem_k_norm, w_branch, w_out,
           ln_mlp, w_up, w_down):
    b, s, d = x.shape
    assert d == 1024 and s % WINDOW == 0 and s >= 2 * WINDOW
    depth = w_in.shape[0]
    perm_np, d_in = _in_proj_permutation(d)
    assert w_in.shape[2] == d_in
    w_in = _relayout_in_proj(w_in.astype(BF16), [int(c) for c in perm_np], d_in)
    half = CMP_STRIDE * NSA_HEAD_DIM
    cmp_w1 = jnp.concatenate([cmp_w1[:, :, :half], cmp_w1[:, :, half:]], axis=-1).astype(BF16)
    cmp_w2, mem_w_kv, w_branch, w_out, w_up, w_down = (
        a.astype(BF16) for a in (cmp_w2, mem_w_kv, w_branch, w_out, w_up, w_down))
    pos3 = positions.astype(jnp.int32).reshape(b, s, 1)
    nc = s // CMP_STRIDE
    cmp_end = np.minimum(np.arange(nc) * CMP_STRIDE + CMP_BLOCK - 1, s - 1)
    pos_cmp = pos3[:, cmp_end, :]
    pos_rows = positions.astype(jnp.int32).reshape(b, 1, s)
    mem2 = mem.reshape(b * mem.shape[1], d)
    names = ("ln_mix", "b_merge", "nsa_q_norm", "nsa_k_norm", "cmp_pe", "cmp_w1", "cmp_w2",
             "gla_w_gate", "gla_b_gate", "gla_norm", "mem_norm", "mem_q_norm", "mem_k_norm", "ln_mlp")
    stacked = (ln_mix, b_merge, nsa_q_norm, nsa_k_norm, cmp_pe, cmp_w1, cmp_w2, gla_w_gate, gla_b_gate,
               gla_norm, mem_norm, mem_q_norm, mem_k_norm, ln_mlp)
    big = dict(w_in=w_in, mem_w_kv=mem_w_kv, w_branch=w_branch, w_out=w_out, w_up=w_up, w_down=w_down)
    for l in range(depth):
        x = _layer(x, mem2, pos_rows, pos_cmp, {n: a[l] for n, a in zip(names, stacked)}, big, l)
    return x
```

```python
import functools

import numpy as np
import jax
import jax.numpy as jnp
from jax import lax
from jax.experimental import pallas as pl
from jax.experimental.pallas import tpu as pltpu

NSA_HEADS = 8
NSA_GROUPS = 2
NSA_REP = NSA_HEADS // NSA_GROUPS
NSA_HEAD_DIM = 64
CMP_BLOCK = 32
CMP_STRIDE = 16
CMP_HIDDEN = 4 * NSA_HEAD_DIM
SEL_BLOCK = 64
N_SEL = 16
WINDOW = 512
FORCE_SCORE = 1e4
GLA_HEADS = 4
GLA_HEAD_DK = 64
GLA_HEAD_DV = 128
GLA_RANK = 16
GLA_TAU = 16.0
GLA_CHUNK = 64
GLA_SUB = 16
MEM_HEADS = 4
MEM_HEAD_DIM = 128
N_BRANCH = 3
BRANCH_WIDTH = 512
ROPE_THETA = 500000.0
ROPE_ROT = NSA_HEAD_DIM // 4
ROPE_HALF = ROPE_ROT // 2
NORM_EPS = 1e-6

LANES = 128
VMEM_LIMIT_BYTES = 48 * 1024 * 1024

F32 = jnp.float32
BF16 = jnp.bfloat16
HIGHEST = lax.Precision.HIGHEST
NEG_BIG = -1e30
MAX_CONST_SHIFT = 40.0

COL_NQ = 0
COL_GV = 512
COL_GR = 1024
COL_MQ = 1536
COL_MERGE = 2048
COL_GQ = 5120
COL_GK = 5376
COL_KV = 5632
COL_SMALL = 6400
D_IN_PAD = 6656
SMALL_GLOW_LANE = 12


def _in_proj_permutation(d_model):
    sizes = (512, 128, 128, 128, 128, 128, 128, 24, 256, 256, 512, 512, 16, 512, 3 * d_model)
    off = np.concatenate([[0], np.cumsum(sizes)])
    (o_nq, o_kc, o_vc, o_ks, o_vs, o_kw, o_vw, o_ng, o_gq, o_gk, o_gv, o_gr, o_gl, o_mq, o_mg) = off[:-1]
    d_in = int(off[-1])
    perm = np.full((D_IN_PAD,), d_in, np.int32)

    def put(new, old, n):
        perm[new:new + n] = np.arange(old, old + n)

    put(COL_NQ, o_nq, 512)
    put(COL_GV, o_gv, 512)
    put(COL_GR, o_gr, 512)
    put(COL_MQ, o_mq, 512)
    put(COL_MERGE, o_mg, 3 * d_model)
    put(COL_GQ, o_gq, 256)
    put(COL_GK, o_gk, 256)
    put(COL_KV, o_kc, 768)
    for g in range(NSA_GROUPS):
        for br in range(3):
            for r in range(NSA_REP):
                perm[COL_SMALL + g * LANES + br * NSA_REP + r] = o_ng + (g * NSA_REP + r) * 3 + br
    put(COL_SMALL + SMALL_GLOW_LANE, o_gl, GLA_RANK)
    return perm, d_in


def _relayout_in_proj(w, perm, d_in):
    pieces, start = [], 0
    for i in range(1, len(perm) + 1):
        prev = perm[i - 1]
        if i < len(perm) and (perm[i] == prev + 1 if prev != d_in else perm[i] == d_in):
            continue
        n = i - start
        if prev == d_in:
            pieces.append(jnp.zeros(w.shape[:-1] + (n,), w.dtype))
        else:
            pieces.append(w[..., perm[start]:perm[start] + n])
        start = i
    return jnp.concatenate(pieces, axis=-1)


def _cparams(*sem):
    return pltpu.CompilerParams(dimension_semantics=sem, vmem_limit_bytes=VMEM_LIMIT_BYTES)


def _rms_rows(x, g):
    ms = jnp.mean(x * x, axis=-1, keepdims=True)
    return (x * lax.rsqrt(ms + NORM_EPS) * g).astype(BF16)


def _norm_matmul_kernel(x_ref, g_ref, w_ref, o_ref):
    h = _rms_rows(x_ref[...].astype(F32), g_ref[...])
    o_ref[...] = jnp.dot(h, w_ref[...], preferred_element_type=F32).astype(o_ref.dtype)


def _norm_matmul(x, g, w, layer, *, tm, tn, out_dtype, name):
    m, k = x.shape
    n = w.shape[2]
    return pl.pallas_call(
        _norm_matmul_kernel,
        grid=(n // tn, m // tm),
        in_specs=[
            pl.BlockSpec((tm, k), lambda j, i: (i, 0)),
            pl.BlockSpec((1, k), lambda j, i: (0, 0)),
            pl.BlockSpec((None, k, tn), lambda j, i: (layer, 0, j)),
        ],
        out_specs=pl.BlockSpec((tm, tn), lambda j, i: (i, j)),
        out_shape=jax.ShapeDtypeStruct((m, n), out_dtype),
        compiler_params=_cparams("parallel", "parallel"),
        name=name,
    )(x, g.reshape(1, k), w)


def _rope_tables(pos_row, freq_col, place):
    ang = freq_col * pos_row
    tn = (((0,), (0,)), ((), ()))
    lane = lax.broadcasted_iota(jnp.int32, (1, LANES), 1)
    c = lax.dot_general(jnp.cos(ang), place, tn, preferred_element_type=F32, precision=HIGHEST)
    s = lax.dot_general(jnp.sin(ang), place, tn, preferred_element_type=F32, precision=HIGHEST)
    return c + jnp.where((lane & (NSA_HEAD_DIM - 1)) >= ROPE_ROT, 1.0, 0.0), s


def _rope_lanes(x, c, s):
    n = x.shape[-1]
    if n > LANES:
        c = jnp.concatenate([c] * (n // LANES), axis=1)
        s = jnp.concatenate([s] * (n // LANES), axis=1)
    lane = lax.broadcasted_iota(jnp.int32, (1, n), 1) & (NSA_HEAD_DIM - 1)
    up = pltpu.roll(x, n - ROPE_HALF, axis=1)
    dn = pltpu.roll(x, ROPE_HALF, axis=1)
    y = jnp.where(lane < ROPE_HALF, -up, jnp.where(lane < ROPE_ROT, dn, 0.0))
    return x * c + y * s


def _head_rms(x, bd, g):
    x2 = x * x
    hi = x2.astype(BF16)
    lo = (x2 - hi.astype(F32)).astype(BF16)
    ms = (jnp.dot(hi, bd, preferred_element_type=F32) + jnp.dot(lo, bd, preferred_element_type=F32))
    return x * lax.rsqrt(ms * (1.0 / NSA_HEAD_DIM) + NORM_EPS) * g


def _nsa_prep_kernel(q_ref, ks_ref, vs_ref, kw_ref, vw_ref, pos_ref, qg_ref, ksg_ref, kwg_ref,
                     bdq_ref, bdk_ref, f_ref, place_ref,
                     qo_ref, kso_ref, vso_ref, kwo_ref, vwo_ref):
    hd = NSA_HEAD_DIM
    tp = q_ref.shape[1]
    c, s = _rope_tables(pos_ref[0].astype(F32), f_ref[...], place_ref[...])
    q = _head_rms(q_ref[0].astype(F32), bdq_ref[...], qg_ref[...])
    q = (_rope_lanes(q, c, s) * (hd ** -0.5)).astype(qo_ref.dtype)
    for h in range(NSA_HEADS):
        qo_ref[0, h] = q[:, h * hd:(h + 1) * hd]
    ks = _rope_lanes(_head_rms(ks_ref[0].astype(F32), bdk_ref[...], ksg_ref[...]), c, s)
    kw = _rope_lanes(_head_rms(kw_ref[0].astype(F32), bdk_ref[...], kwg_ref[...]), c, s)
    vs = vs_ref[0]
    vw = vw_ref[0]
    tok = pl.program_id(1) * tp + lax.broadcasted_iota(jnp.int32, (tp, hd), 0)
    lane = lax.broadcasted_iota(jnp.int32, (tp, hd), 1)
    blk_onehot = jnp.where((tok >> 6) == lane, 1.0, 0.0).astype(kso_ref.dtype)
    ones_col = jnp.ones((tp, hd), vso_ref.dtype)
    for g in range(NSA_GROUPS):
        sl = slice(g * hd, (g + 1) * hd)
        kso_ref[0, g, :, 0:hd] = ks[:, sl].astype(kso_ref.dtype)
        kso_ref[0, g, :, hd:2 * hd] = blk_onehot
        kwo_ref[0, g] = kw[:, sl].astype(kwo_ref.dtype)
        vso_ref[0, g, :, 0:hd] = vs[:, sl].astype(vso_ref.dtype)
        vso_ref[0, g, :, hd:2 * hd] = ones_col
        vwo_ref[0, g, :, 0:hd] = vw[:, sl].astype(vwo_ref.dtype)
        vwo_ref[0, g, :, hd:2 * hd] = ones_col


def _block_diag_ones(n, width):
    i = np.arange(n)
    return jnp.asarray((i[:, None] // width == i[None, :] // width).astype(np.float32)).astype(BF16)


def _rope_inv_freq():
    return jnp.power(ROPE_THETA, -jnp.arange(ROPE_HALF, dtype=F32) / ROPE_HALF)


def _rope_freq_col():
    return _rope_inv_freq().reshape(ROPE_HALF, 1)


def _rope_placement():
    m = np.zeros((ROPE_HALF, LANES), np.float32)
    for f in range(ROPE_HALF):
        for base in range(0, LANES, NSA_HEAD_DIM):
            m[f, base + f] = 1.0
            m[f, base + f + ROPE_HALF] = 1.0
    return jnp.asarray(m)


def _nsa_prep(z3, pos_rows, q_norm, ks_norm, kw_norm, *, tp):
    b, s, _ = z3.shape
    hd, g = NSA_HEAD_DIM, NSA_GROUPS
    assert SEL_BLOCK == 64 and s // SEL_BLOCK <= hd
    kvb = COL_KV // LANES
    qg = jnp.tile(q_norm, NSA_HEADS).reshape(1, NSA_HEADS * hd)
    ksg = jnp.tile(ks_norm, g).reshape(1, g * hd)
    kwg = jnp.tile(kw_norm, g).reshape(1, g * hd)
    const = lambda shape: pl.BlockSpec(shape, lambda bi, i: (0,) * len(shape))
    plain_out = jax.ShapeDtypeStruct((b, g, s, hd), BF16)
    plain_spec = pl.BlockSpec((1, g, tp, hd), lambda bi, i: (bi, 0, i, 0))
    aug_out = jax.ShapeDtypeStruct((b, g, s, 2 * hd), BF16)
    aug_spec = pl.BlockSpec((1, g, tp, 2 * hd), lambda bi, i: (bi, 0, i, 0))
    return pl.pallas_call(
        _nsa_prep_kernel,
        grid=(b, s // tp),
        in_specs=[
            pl.BlockSpec((1, tp, 512), lambda bi, i: (bi, i, COL_NQ // 512)),
            pl.BlockSpec((1, tp, LANES), lambda bi, i: (bi, i, kvb + 2)),
            pl.BlockSpec((1, tp, LANES), lambda bi, i: (bi, i, kvb + 3)),
            pl.BlockSpec((1, tp, LANES), lambda bi, i: (bi, i, kvb + 4)),
            pl.BlockSpec((1, tp, LANES), lambda bi, i: (bi, i, kvb + 5)),
            pl.BlockSpec((1, 1, tp), lambda bi, i: (bi, 0, i)),
            const((1, 512)), const((1, LANES)), const((1, LANES)),
            const((512, 512)), const((LANES, LANES)),
            const((ROPE_HALF, 1)), const((ROPE_HALF, LANES)),
        ],
        out_specs=[
            pl.BlockSpec((1, NSA_HEADS, tp, hd), lambda bi, i: (bi, 0, i, 0)),
            aug_spec, aug_spec, plain_spec, aug_spec,
        ],
        out_shape=[jax.ShapeDtypeStruct((b, NSA_HEADS, s, hd), BF16), aug_out, aug_out, plain_out, aug_out],
        compiler_params=_cparams("parallel", "parallel"),
        name="nsa_prep",
    )(z3, z3, z3, z3, z3, pos_rows, qg, ksg, kwg,
      _block_diag_ones(512, hd), _block_diag_ones(LANES, hd), _rope_freq_col(), _rope_placement())


def _gelu_tanh(x):
    return 0.5 * x * (1.0 + jnp.tanh(np.sqrt(2.0 / np.pi) * (x + 0.044715 * x * x * x)))


def _nsa_cmp_kernel(x_ref, w1_ref, w2_ref, pe_ref, pos_ref, g_ref, rot_ref, f_ref, o_ref, xs_ref):
    kind = pl.program_id(0)
    hd = NSA_HEAD_DIM
    nc = x_ref.shape[1] // CMP_STRIDE
    xs_ref[...] = x_ref[0].astype(F32)
    r = jnp.dot(pe_ref[0].astype(BF16), w1_ref[0], preferred_element_type=F32)
    ab = [jnp.zeros((nc, 2 * CMP_HIDDEN), F32) for _ in range(NSA_GROUPS)]
    for p in range(CMP_STRIDE):
        xp = xs_ref[pl.ds(p, nc, stride=CMP_STRIDE), :].astype(BF16)
        w1p = w1_ref[0, p * hd:(p + 1) * hd, :]
        for g in range(NSA_GROUPS):
            ab[g] = ab[g] + jnp.dot(xp[:, g * hd:(g + 1) * hd], w1p, preferred_element_type=F32)

    for g in range(NSA_GROUPS):
        a = ab[g][:, :CMP_HIDDEN] + r[0:1, :CMP_HIDDEN]
        bm = ab[g][:, CMP_HIDDEN:] + r[1:2, CMP_HIDDEN:]
        hid = _gelu_tanh(a + pltpu.roll(bm, nc - 1, axis=0))
        comp = jnp.dot(hid.astype(BF16), w2_ref[0], preferred_element_type=F32)

        @pl.when(kind == 0)
        def _():
            ms = jnp.mean(comp * comp, axis=-1, keepdims=True)
            kn = comp * lax.rsqrt(ms + NORM_EPS) * g_ref[...]
            ang = pos_ref[0].astype(F32) * f_ref[...]
            y = jnp.dot(kn, rot_ref[...], preferred_element_type=F32, precision=HIGHEST)
            o_ref[0, 0, g, :, 0:hd] = (kn * jnp.cos(ang) + y * jnp.sin(ang)).astype(o_ref.dtype)
            o_ref[0, 0, g, :, hd:2 * hd] = jnp.zeros((nc, hd), o_ref.dtype)

        @pl.when(kind != 0)
        def _():
            o_ref[0, 0, g, :, 0:hd] = comp.astype(o_ref.dtype)
            o_ref[0, 0, g, :, hd:2 * hd] = jnp.ones((nc, hd), o_ref.dtype)


def _rope_freq_head():
    inv = _rope_inv_freq()
    return jnp.concatenate([inv, inv, jnp.zeros((NSA_HEAD_DIM - ROPE_ROT,), F32)]).reshape(1, NSA_HEAD_DIM)


def _rope_rot_matrix(n):
    m = np.zeros((n, n), np.float32)
    for base in range(0, n, NSA_HEAD_DIM):
        for l in range(ROPE_HALF):
            m[base + l + ROPE_HALF, base + l] = -1.0
            m[base + l, base + l + ROPE_HALF] = 1.0
    return jnp.asarray(m)


def _nsa_compress(z3, w1cat, w2, pe2, pos_cmp, k_norm0):
    b, s, _ = z3.shape
    hd, g = NSA_HEAD_DIM, NSA_GROUPS
    nc = s // CMP_STRIDE
    assert g * hd == LANES
    const = lambda shape: pl.BlockSpec(shape, lambda k, bi: (0,) * len(shape))
    return pl.pallas_call(
        _nsa_cmp_kernel,
        grid=(2, b),
        in_specs=[
            pl.BlockSpec((1, s, LANES), lambda k, bi: (bi, 0, COL_KV // LANES + k)),
            pl.BlockSpec((1, CMP_STRIDE * hd, 2 * CMP_HIDDEN), lambda k, bi: (k, 0, 0)),
            pl.BlockSpec((1, CMP_HIDDEN, hd), lambda k, bi: (k, 0, 0)),
            pl.BlockSpec((1, 8, CMP_STRIDE * hd), lambda k, bi: (k, 0, 0)),
            pl.BlockSpec((1, nc, 1), lambda k, bi: (bi, 0, 0)),
            const((1, hd)), const((hd, hd)), const((1, hd)),
        ],
        out_specs=pl.BlockSpec((1, 1, g, nc, 2 * hd), lambda k, bi: (k, bi, 0, 0, 0)),
        out_shape=jax.ShapeDtypeStruct((2, b, g, nc, 2 * hd), BF16),
        scratch_shapes=[pltpu.VMEM((s, LANES), F32)],
        compiler_params=_cparams("parallel", "parallel"),
        name="nsa_compress",
    )(z3, w1cat, w2, pe2, pos_cmp, k_norm0.reshape(1, hd), _rope_rot_matrix(hd), _rope_freq_head())


def _softmax_rows(s, mask):
    s = jnp.where(mask, s, NEG_BIG)
    m = jnp.max(s, axis=-1, keepdims=True)
    p = jnp.where(mask, jnp.exp(s - m), 0.0)
    d = jnp.sum(p, axis=-1, keepdims=True)
    return p / jnp.where(d > 0, d, 1.0)


_NT = (((1,), (1,)), ((), ()))


def _select_blocks(p_sum, ov_ref, t0, tq, n_sel, n_live=None):
    n_blk = ov_ref.shape[0]
    n_live = n_blk if n_live is None else n_live
    p_hi = p_sum.astype(BF16)
    p_lo = (p_sum - p_hi.astype(F32)).astype(BF16)
    ov = ov_ref[:, 0:p_sum.shape[1]]
    imp_t = (lax.dot_general(ov, p_hi, _NT, preferred_element_type=F32)
             + lax.dot_general(ov, p_lo, _NT, preferred_element_type=F32))
    j_col = lax.broadcasted_iota(jnp.int32, (n_blk, 1), 0)
    t_lane = t0 + lax.broadcasted_iota(jnp.int32, (1, tq), 1)
    causal = j_col * SEL_BLOCK <= t_lane
    cur = t_lane >> 6
    forced = causal & ((j_col == 0) | (j_col == cur) | (j_col == cur - 1))
    score = jnp.where(forced, FORCE_SCORE, jnp.where(causal, imp_t, -FORCE_SCORE))
    ng = n_live // 8
    groups = [score[8 * v:8 * v + 8] for v in range(ng)]
    ranks = [jnp.zeros((8, tq), F32) for _ in range(ng)]
    sub = lax.broadcasted_iota(jnp.int32, (8, tq), 0)
    for jp in range(n_live):
        row = jnp.broadcast_to(score[jp:jp + 1, :], (8, tq))
        vj = jp // 8
        for v in range(ng):
            if v < vj:
                ahead = row > groups[v]
            elif v > vj:
                ahead = row >= groups[v]
            else:
                ahead = (row > groups[v]) | ((row == groups[v]) & (sub > jp % 8))
            ranks[v] = ranks[v] + jnp.where(ahead, 1.0, 0.0)
    if n_live < n_blk:
        ranks.append(jnp.full((n_blk - n_live, tq), float(n_blk), F32))
    sel_t = jnp.where(jnp.concatenate(ranks, axis=0) < n_sel, 1.0, 0.0).astype(BF16)
    eye = (lax.broadcasted_iota(jnp.int32, (tq, tq), 0) == lax.broadcasted_iota(jnp.int32, (tq, tq), 1)).astype(BF16)
    return lax.dot_general(eye, sel_t, _NT, preferred_element_type=F32)


def _sum_heads(p, tq):
    out = p[0:tq]
    for r in range(1, NSA_REP):
        out = out + p[r * tq:(r + 1) * tq]
    return out


def _store_heads(o_ref, o, tq, row0):
    hd = NSA_HEAD_DIM
    for r in range(NSA_REP):
        o_ref[0, row0:row0 + tq, r * hd:(r + 1) * hd] = o[r * tq:(r + 1) * tq].astype(o_ref.dtype)


def _store_selection(sel_ref, sel, row0):
    tq, n_blk = sel.shape
    sel = sel.astype(sel_ref.dtype)
    if n_blk < NSA_HEAD_DIM:
        sel = jnp.concatenate([sel, jnp.zeros((tq, NSA_HEAD_DIM - n_blk), sel_ref.dtype)], axis=1)
    sel_ref[0, 0, row0:row0 + tq] = sel


def _gate_maps(gl_ref, gp_ref, gm_ref):
    gates = jax.nn.sigmoid(gl_ref[0].astype(F32))
    g_hi = gates.astype(BF16)
    g_lo = (gates - g_hi.astype(F32)).astype(BF16)
    gm_ref[...] = (jnp.dot(g_hi, gp_ref[...], preferred_element_type=F32)
                   + jnp.dot(g_lo, gp_ref[...], preferred_element_type=F32))


def _to_lane_layout(lay_ref, idx, o, tq, lane0=0):
    hd = NSA_HEAD_DIM
    for r in range(NSA_REP):
        lay_ref[idx, :, r * hd:(r + 1) * hd] = o[r * tq:(r + 1) * tq, lane0:lane0 + hd]


def _gated_store(gm_ref, oc_ref, lay_ref, o_ref, with_sums):
    w = NSA_REP * NSA_HEAD_DIM
    o_sel, o_win = lay_ref[0], lay_ref[1]
    if with_sums:
        o_sel, o_win = o_sel / lay_ref[2], o_win / lay_ref[3]
    out = gm_ref[:, 0:w] * oc_ref[0].astype(F32) + gm_ref[:, w:2 * w] * o_sel + gm_ref[:, 2 * w:3 * w] * o_win
    o_ref[0] = out.astype(o_ref.dtype)


def _gate_placement():
    m = np.zeros((LANES, 3 * NSA_REP * NSA_HEAD_DIM), np.float32)
    for br in range(3):
        for r in range(NSA_REP):
            c0 = br * NSA_REP * NSA_HEAD_DIM + r * NSA_HEAD_DIM
            m[br * NSA_REP + r, c0:c0 + NSA_HEAD_DIM] = 1.0
    return jnp.asarray(m).astype(BF16)


def _sel_fast(sh_ref, q_ref, kc_ref, vc_ref, ov_ref, oc_ref, sel_ref, *, tq, n_sel):
    rep, hd = NSA_REP, NSA_HEAD_DIM
    rows = rep * tq
    tb = q_ref.shape[2]
    n_blk = ov_ref.shape[0]

    def body(n_live):
        n_all = kc_ref.shape[3]
        ncp = min(n_all, n_live * (SEL_BLOCK // CMP_STRIDE))
        kc = kc_ref[0, 0, 0][0:ncp, 0:hd]
        vc = vc_ref[0, 0, 0][0:ncp]
        n_idx = lax.broadcasted_iota(jnp.int32, (1, ncp), 1)
        for part in range(tb // tq):
            row0 = part * tq
            t0 = pl.program_id(2) * tb + row0
            q4 = q_ref[0, :, row0:row0 + tq, :].reshape(rows, hd)
            t_q = t0 + lax.broadcasted_iota(jnp.int32, (tq, 1), 0)
            valid_c = (n_idx * CMP_STRIDE + (CMP_BLOCK - 1) <= t_q) & (n_idx < n_all - 1)
            bias_c = jnp.where(valid_c, sh_ref[0], NEG_BIG)
            s_c = lax.dot_general(q4, kc, _NT, preferred_element_type=F32).reshape(rep, tq, ncp)
            p_c = jnp.exp(s_c + bias_c[None]).reshape(rows, ncp)
            ol_c = jnp.dot(p_c.astype(BF16), vc, preferred_element_type=F32)
            inv_c = 1.0 / jnp.where(ol_c[:, hd:hd + 1] > 0, ol_c[:, hd:hd + 1], 1.0)
            _store_heads(oc_ref, ol_c[:, 0:hd] * inv_c, tq, row0)
            sel = _select_blocks(_sum_heads(p_c * inv_c, tq), ov_ref, t0, tq, n_sel, n_live)
            _store_selection(sel_ref, sel, row0)

    regions = SELECT_REGIONS if (n_blk % (8 * SELECT_REGIONS) == 0
                                 and (n_blk // SELECT_REGIONS * SEL_BLOCK) % tb == 0) else 1
    steps = n_blk // regions * SEL_BLOCK // tb
    step = pl.program_id(2)
    for r in range(regions):
        pl.when((step >= r * steps) & (step < (r + 1) * steps))(functools.partial(body, (r + 1) * (n_blk // regions)))


def _sel_slow(q_ref, kc_ref, vc_ref, ov_ref, oc_ref, sel_ref, *, tq, n_sel):
    rep, hd = NSA_REP, NSA_HEAD_DIM
    rows = rep * tq
    kc = kc_ref[0, 0, 0][:, 0:hd]
    ncp = kc.shape[0]
    n_idx = lax.broadcasted_iota(jnp.int32, (1, ncp), 1)
    for part in range(q_ref.shape[2] // tq):
        row0 = part * tq
        t0 = pl.program_id(2) * q_ref.shape[2] + row0
        q4 = q_ref[0, :, row0:row0 + tq, :].reshape(rows, hd)
        t_row = t0 + (lax.broadcasted_iota(jnp.int32, (rows, 1), 0) & (tq - 1))
        s_c = lax.dot_general(q4, kc, _NT, preferred_element_type=F32)
        p_c = _softmax_rows(s_c, (n_idx * CMP_STRIDE + (CMP_BLOCK - 1) <= t_row) & (n_idx < ncp - 1))
        o_c = jnp.dot(p_c.astype(BF16), vc_ref[0, 0, 0], preferred_element_type=F32)[:, 0:hd]
        _store_heads(oc_ref, o_c, tq, row0)
        _store_selection(sel_ref, _select_blocks(_sum_heads(p_c, tq), ov_ref, t0, tq, n_sel), row0)


def _nsa_select_kernel(sh_ref, q_ref, kc_ref, vc_ref, ov_ref, oc_ref, sel_ref, *, tq, n_sel):
    data = (q_ref, kc_ref, vc_ref, ov_ref, oc_ref, sel_ref)

    @pl.when(sh_ref[3] > 0.5)
    def _():
        _sel_fast(sh_ref, *data, tq=tq, n_sel=n_sel)

    @pl.when(sh_ref[3] <= 0.5)
    def _():
        _sel_slow(*data, tq=tq, n_sel=n_sel)


def _att_fast(sh_ref, q_ref, ks_ref, vs_ref, kw_ref, vw_ref, gl_ref, gp_ref, oc_ref, sel_ref, o_ref,
              lhs_ref, acc_ref, pre_ref, gm_ref, lay_ref, *, tq, tk):
    rep, hd = NSA_REP, NSA_HEAD_DIM
    rows = rep * tq
    t0 = pl.program_id(2) * tq
    c_s, c_w = sh_ref[1], sh_ref[2]
    q4 = q_ref[0].reshape(rows, hd)
    t_q = t0 + lax.broadcasted_iota(jnp.int32, (tq, 1), 0)

    j_lane = lax.broadcasted_iota(jnp.int32, (1, hd), 1)
    shift = jnp.where((sel_ref[0, 0].astype(F32) > 0.5) & (j_lane * SEL_BLOCK < t0), c_s, NEG_BIG).astype(BF16)
    lhs_ref[:, 0:hd] = q4
    for r in range(rep):
        lhs_ref[r * tq:(r + 1) * tq, hd:2 * hd] = shift

    span = WINDOW + tq
    w0 = pl.multiple_of(jnp.maximum(t0 - WINDOW, 0), tq)
    kp = w0 + lax.broadcasted_iota(jnp.int32, (1, span), 1)
    bias_w = jnp.where((kp <= t_q) & (kp > t_q - WINDOW), c_w, NEG_BIG)
    s_w = lax.dot_general(q4, kw_ref[0, 0, pl.ds(w0, span), :], _NT, preferred_element_type=F32)
    p_w = jnp.exp(s_w.reshape(rep, tq, span) + bias_w[None]).reshape(rows, span)
    pre_ref[...] = jnp.dot(p_w.astype(BF16), vw_ref[0, 0, pl.ds(w0, span), :], preferred_element_type=F32)

    d0 = pl.multiple_of(t0, tq)
    kcol = t0 + lax.broadcasted_iota(jnp.int32, (1, tq), 1)
    blk_of_key = jnp.where(lax.broadcasted_iota(jnp.int32, (hd, 1), 0) == (kcol >> 6), 1.0, 0.0).astype(BF16)
    chosen_d = jnp.dot(sel_ref[0, 0], blk_of_key, preferred_element_type=F32)
    bias_d = jnp.where((kcol <= t_q) & (chosen_d > 0.5), c_s, NEG_BIG)
    s_d = lax.dot_general(q4, ks_ref[0, 0, pl.ds(d0, tq), :][:, 0:hd], _NT, preferred_element_type=F32)
    p_d = jnp.exp(s_d.reshape(rep, tq, tq) + bias_d[None]).reshape(rows, tq)
    acc_ref[...] = jnp.dot(p_d.astype(BF16), vs_ref[0, 0, pl.ds(d0, tq), :], preferred_element_type=F32)

    def key_tile(k0, width):
        s = lax.dot_general(lhs_ref[...], ks_ref[0, 0, pl.ds(k0, width), :], _NT, preferred_element_type=F32)
        acc_ref[...] += jnp.dot(jnp.exp(s).astype(BF16), vs_ref[0, 0, pl.ds(k0, width), :],
                                preferred_element_type=F32)

    done = 0
    for mult in KEY_TILE_MULTIPLES:
        width = mult * tk
        if width > ks_ref.shape[2]:
            continue
        left = t0 - done
        n = left // width if mult > 1 else (left + width - 1) // width

        def step(c, carry, width=width, base=done):
            key_tile(pl.multiple_of(base + c * width, tk), width)
            return carry

        lax.fori_loop(0, n, step, 0)
        done = done + n * width

    for idx, ol in ((0, acc_ref[...]), (1, pre_ref[...])):
        _to_lane_layout(lay_ref, idx, ol, tq)
        _to_lane_layout(lay_ref, 2 + idx, ol, tq, lane0=hd)
    _gate_maps(gl_ref, gp_ref, gm_ref)
    _gated_store(gm_ref, oc_ref, lay_ref, o_ref, with_sums=True)


def _att_slow(q_ref, ks_ref, vs_ref, kw_ref, vw_ref, gl_ref, gp_ref, oc_ref, sel_ref, o_ref,
              acc_ref, gm_ref, lay_ref, m_ref, l_ref, *, tq, tk):
    rep, hd = NSA_REP, NSA_HEAD_DIM
    rows = rep * tq
    t0 = pl.program_id(2) * tq
    q4 = q_ref[0].reshape(rows, hd)
    t_row = t0 + (lax.broadcasted_iota(jnp.int32, (rows, 1), 0) & (tq - 1))
    sel = sel_ref[0, 0]
    n_blk = sel.shape[1]

    m_ref[...] = jnp.full(m_ref.shape, NEG_BIG, F32)
    l_ref[...] = jnp.zeros(l_ref.shape, F32)
    acc_ref[...] = jnp.zeros(acc_ref.shape, F32)
    t_q = t0 + lax.broadcasted_iota(jnp.int32, (tq, 1), 0)

    def kv_step(c, carry):
        k0 = pl.multiple_of(c * tk, tk)
        kt = ks_ref[0, 0, pl.ds(k0, tk), :][:, 0:hd]
        s = lax.dot_general(q4, kt, _NT, preferred_element_type=F32).reshape(rep, tq, tk)
        kk = lax.broadcasted_iota(jnp.int32, (1, tk), 1)
        blk = (k0 >> 6) + (kk >> 6)
        expand = jnp.where(lax.broadcasted_iota(jnp.int32, (n_blk, 1), 0) == blk, 1.0, 0.0).astype(BF16)
        chosen = jnp.dot(sel, expand, preferred_element_type=F32)
        mask = ((chosen > 0.5) & (k0 + kk <= t_q))[None]
        s = jnp.where(mask, s, NEG_BIG)
        m_old = m_ref[...]
        m_new = jnp.maximum(m_old, jnp.max(s, axis=-1, keepdims=True))
        p = jnp.where(mask, jnp.exp(s - m_new), 0.0)
        alpha = jnp.exp(m_old - m_new)
        l_ref[...] = alpha * l_ref[...] + jnp.sum(p, axis=-1, keepdims=True)
        pv = jnp.dot(p.reshape(rows, tk).astype(BF16), vs_ref[0, 0, pl.ds(k0, tk), :], preferred_element_type=F32)
        acc_ref[...] = alpha.reshape(rows, 1) * acc_ref[...] + pv
        m_ref[...] = m_new
        return carry

    lax.fori_loop(0, (t0 + tq + tk - 1) // tk, kv_step, 0)
    o_s = acc_ref[:, 0:hd] / l_ref[...].reshape(rows, 1)

    span = WINDOW + tq
    w0 = pl.multiple_of(jnp.maximum(t0 - WINDOW, 0), tq)
    s_w = lax.dot_general(q4, kw_ref[0, 0, pl.ds(w0, span), :], _NT, preferred_element_type=F32)
    kp = w0 + lax.broadcasted_iota(jnp.int32, (1, span), 1)
    p_w = _softmax_rows(s_w, (kp <= t_row) & (kp > t_row - WINDOW))
    o_w = jnp.dot(p_w.astype(BF16), vw_ref[0, 0, pl.ds(w0, span), :], preferred_element_type=F32)[:, 0:hd]

    _to_lane_layout(lay_ref, 0, o_s, tq)
    _to_lane_layout(lay_ref, 1, o_w, tq)
    _gate_maps(gl_ref, gp_ref, gm_ref)
    _gated_store(gm_ref, oc_ref, lay_ref, o_ref, with_sums=False)


def _nsa_attend_kernel(sh_ref, q_ref, ks_ref, vs_ref, kw_ref, vw_ref, gl_ref, gp_ref, oc_ref, sel_ref, o_ref,
                       lhs_ref, acc_ref, pre_ref, gm_ref, lay_ref, m_ref, l_ref, *, tq, tk):
    data = (q_ref, ks_ref, vs_ref, kw_ref, vw_ref, gl_ref, gp_ref, oc_ref, sel_ref, o_ref)

    @pl.when(sh_ref[3] > 0.5)
    def _():
        _att_fast(sh_ref, *data, lhs_ref, acc_ref, pre_ref, gm_ref, lay_ref, tq=tq, tk=tk)

    @pl.when(sh_ref[3] <= 0.5)
    def _():
        _att_slow(*data, acc_ref, gm_ref, lay_ref, m_ref, l_ref, tq=tq, tk=tk)


def _overlap_t(n_blk, ncp):
    c_start = np.arange(ncp) * CMP_STRIDE
    b_start = np.arange(n_blk) * SEL_BLOCK
    ov = ((c_start[None, :] < b_start[:, None] + SEL_BLOCK) & (b_start[:, None] < c_start[None, :] + CMP_BLOCK))
    ov[:, ncp - 1] = False
    return jnp.asarray(ov.astype(np.float32)).astype(BF16)


def _nsa_shifts(q_norm, k_norm):
    bound = (NSA_HEAD_DIM ** 0.5) * jnp.max(jnp.abs(q_norm)) * jnp.max(jnp.abs(k_norm), axis=-1)
    bound = bound.astype(BF16).astype(F32)
    fast = jnp.all(bound <= MAX_CONST_SHIFT).astype(F32)
    return jnp.concatenate([-bound, fast[None]])


def _nsa_select(shifts, q, cmp_kv, *, tq, parts):
    b, h, s, hd = q.shape
    g, rep = NSA_GROUPS, NSA_REP
    ncp = cmp_kv.shape[3]
    n_blk = s // SEL_BLOCK
    tb = tq * parts
    assert n_blk % 8 == 0 and n_blk <= hd and s % tb == 0
    kernel = functools.partial(_nsa_select_kernel, tq=tq, n_sel=min(N_SEL, n_blk))
    return pl.pallas_call(
        kernel,
        grid=(b, g, s // tb),
        in_specs=[
            pl.BlockSpec(memory_space=pltpu.SMEM),
            pl.BlockSpec((1, rep, tb, hd), lambda bi, gi, i: (bi, gi, i, 0)),
            pl.BlockSpec((1, 1, 1, ncp, 2 * hd), lambda bi, gi, i: (0, bi, gi, 0, 0)),
            pl.BlockSpec((1, 1, 1, ncp, 2 * hd), lambda bi, gi, i: (1, bi, gi, 0, 0)),
            pl.BlockSpec((n_blk, ncp), lambda bi, gi, i: (0, 0)),
        ],
        out_specs=[pl.BlockSpec((1, tb, rep * hd), lambda bi, gi, i: (bi, i, gi)),
                   pl.BlockSpec((1, 1, tb, hd), lambda bi, gi, i: (bi, gi, i, 0))],
        out_shape=[jax.ShapeDtypeStruct((b, s, h * hd), BF16), jax.ShapeDtypeStruct((b, g, s, hd), BF16)],
        compiler_params=_cparams("parallel", "parallel", "parallel"),
        name="nsa_select",
    )(shifts, q, cmp_kv, cmp_kv, _overlap_t(n_blk, ncp))


def _nsa_attend(shifts, q, ks, vs, kw, vw, z3, o_cmp, sel, *, tq, tk):
    b, h, s, hd = q.shape
    g, rep = NSA_GROUPS, NSA_REP
    rows = rep * tq
    assert WINDOW % tq == 0
    full = lambda w: pl.BlockSpec((1, 1, s, w), lambda bi, gi, i: (bi, gi, 0, 0))
    return pl.pallas_call(
        functools.partial(_nsa_attend_kernel, tq=tq, tk=tk),
        grid=(b, g, s // tq),
        in_specs=[
            pl.BlockSpec(memory_space=pltpu.SMEM),
            pl.BlockSpec((1, rep, tq, hd), lambda bi, gi, i: (bi, gi, i, 0)),
            full(2 * hd), full(2 * hd), full(hd), full(2 * hd),
            pl.BlockSpec((1, tq, LANES), lambda bi, gi, i: (bi, i, COL_SMALL // LANES + gi)),
            pl.BlockSpec((LANES, 3 * rep * hd), lambda bi, gi, i: (0, 0)),
            pl.BlockSpec((1, tq, rep * hd), lambda bi, gi, i: (bi, i, gi)),
            pl.BlockSpec((1, 1, tq, hd), lambda bi, gi, i: (bi, gi, i, 0)),
        ],
        out_specs=pl.BlockSpec((1, tq, rep * hd), lambda bi, gi, i: (bi, i, gi)),
        out_shape=jax.ShapeDtypeStruct((b, s, h * hd), BF16),
        scratch_shapes=[pltpu.VMEM((rows, 2 * hd), BF16), pltpu.VMEM((rows, 2 * hd), F32),
                        pltpu.VMEM((rows, 2 * hd), F32), pltpu.VMEM((tq, 3 * rep * hd), F32),
                        pltpu.VMEM((4, tq, rep * hd), F32),
                        pltpu.VMEM((rep, tq, 1), F32), pltpu.VMEM((rep, tq, 1), F32)],
        compiler_params=_cparams("parallel", "parallel", "arbitrary"),
        name="nsa_attend",
    )(shifts, q, ks, vs, kw, vw, z3, _gate_placement(), o_cmp, sel)


def _head_block_diag(x, n_rows, head_of_lane, dtype):
    return jnp.concatenate([jnp.where(head_of_lane == h, x, 0.0) for h in range(GLA_HEADS)], axis=0).astype(dtype)


def _gla_kernel(q_ref, k_ref, v_ref, r_ref, sm_ref, wg_ref, bg_ref, ng_ref, tril_ref, o_ref,
                st_ref, upd_ref, oin_ref, qin_ref, *, n_chunks):
    c, sub = GLA_CHUNK, GLA_SUB
    dk, dv, nh = GLA_HEAD_DK, GLA_HEAD_DV, GLA_HEADS
    nk, nv = nh * dk, nh * dv

    @pl.when(pl.program_id(1) == 0)
    def _():
        st_ref[...] = jnp.zeros(st_ref.shape, F32)

    sm = sm_ref[0]
    x = (jnp.dot(sm, wg_ref[0], preferred_element_type=F32) + jnp.dot(sm, wg_ref[1], preferred_element_type=F32)
         + bg_ref[...])
    log_a = (jnp.minimum(x, 0.0) - jnp.log1p(jnp.exp(-jnp.abs(x)))) / GLA_TAU
    la_hi = log_a.astype(BF16)
    la_lo = (log_a - la_hi.astype(F32)).astype(BF16)
    bcum_all = (jnp.dot(tril_ref[...], la_hi, preferred_element_type=F32)
                + jnp.dot(tril_ref[...], la_lo, preferred_element_type=F32))
    head_k = lax.broadcasted_iota(jnp.int32, (1, nk), 1) >> 6
    head_v = lax.broadcasted_iota(jnp.int32, (1, nv), 1) >> 7
    state_mask = (lax.broadcasted_iota(jnp.int32, (nv, 1), 0) >> 7) == head_k
    causal = ((lax.broadcasted_iota(jnp.int32, (c, nh * c), 1) & (c - 1))
              <= lax.broadcasted_iota(jnp.int32, (c, nh * c), 0))

    decays = []
    for cc in range(n_chunks):
        rs = slice(cc * c, (cc + 1) * c)
        bcum = bcum_all[rs]
        q = q_ref[0, rs].astype(F32) * (dk ** -0.5)
        k = k_ref[0, rs].astype(F32)
        v = v_ref[0, rs].astype(F32)
        b_last = bcum[c - 1:c, :]

        score_rows = []
        for i in range(c // sub):
            lo, hi = i * sub, (i + 1) * sub
            ref = bcum[lo - 1:lo, :] if i > 0 else jnp.zeros((1, nk), F32)
            q_i = (q[lo:hi] * jnp.exp(bcum[lo:hi] - ref)).astype(BF16)
            k_i = k[0:hi] * jnp.exp(ref - bcum[0:hi])
            if hi < c:
                k_i = jnp.concatenate([k_i, jnp.zeros((c - hi, nk), F32)], axis=0)
            score_rows.append(lax.dot_general(q_i, _head_block_diag(k_i, c, head_k, BF16), _NT,
                                              preferred_element_type=F32))
        a = jnp.where(causal, jnp.concatenate(score_rows, axis=0), 0.0).astype(BF16)
        oin_ref[rs] = jnp.dot(a, _head_block_diag(v, c, head_v, BF16), preferred_element_type=F32)
        qin_ref[rs] = (q * jnp.exp(bcum)).astype(BF16)
        k_out = (k * jnp.exp(b_last - bcum)).astype(BF16)
        upd = lax.dot_general(v.astype(BF16), k_out, (((0,), (0,)), ((), ())), preferred_element_type=F32)
        upd_ref[cc] = jnp.where(state_mask, upd, 0.0)
        decays.append(jnp.exp(b_last))

    for cc in range(n_chunks):
        rs = slice(cc * c, (cc + 1) * c)
        st = st_ref[...]
        o = oin_ref[rs] + lax.dot_general(qin_ref[rs], st.astype(BF16), _NT, preferred_element_type=F32)
        st_ref[...] = st * decays[cc] + upd_ref[cc]
        r_gate = r_ref[0, rs].astype(F32)
        for h in range(nh):
            vs_ = slice(h * dv, (h + 1) * dv)
            o_h = o[:, vs_]
            ms = jnp.mean(o_h * o_h, axis=-1, keepdims=True)
            rg = r_gate[:, vs_]
            o_ref[0, rs, vs_] = (o_h * lax.rsqrt(ms + NORM_EPS) * ng_ref[...] * (rg * jax.nn.sigmoid(rg))).astype(o_ref.dtype)


def _gla(z3, w_gate, b_gate, norm_g, *, n_chunks):
    b, s, _ = z3.shape
    c = GLA_CHUNK * n_chunks
    nk = GLA_HEADS * GLA_HEAD_DK
    nv = GLA_HEADS * GLA_HEAD_DV
    wg = jnp.zeros((LANES, nk), F32).at[SMALL_GLOW_LANE:SMALL_GLOW_LANE + GLA_RANK].set(w_gate.astype(F32))
    wg_hi = wg.astype(BF16)
    wg = jnp.stack([wg_hi, (wg - wg_hi.astype(F32)).astype(BF16)])
    idx = np.arange(c)
    tril = jnp.asarray(((idx[:, None] >= idx[None, :])
                        & (idx[:, None] // GLA_CHUNK == idx[None, :] // GLA_CHUNK)).astype(np.float32)).astype(BF16)
    const = lambda shape: pl.BlockSpec(shape, lambda bi, i: (0,) * len(shape))
    return pl.pallas_call(
        functools.partial(_gla_kernel, n_chunks=n_chunks),
        grid=(b, s // c),
        in_specs=[
            pl.BlockSpec((1, c, nk), lambda bi, i: (bi, i, COL_GQ // nk)),
            pl.BlockSpec((1, c, nk), lambda bi, i: (bi, i, COL_GK // nk)),
            pl.BlockSpec((1, c, nv), lambda bi, i: (bi, i, COL_GV // nv)),
            pl.BlockSpec((1, c, nv), lambda bi, i: (bi, i, COL_GR // nv)),
            pl.BlockSpec((1, c, LANES), lambda bi, i: (bi, i, COL_SMALL // LANES)),
            const((2, LANES, nk)), const((1, nk)), const((1, GLA_HEAD_DV)), const((c, c)),
        ],
        out_specs=pl.BlockSpec((1, c, nv), lambda bi, i: (bi, i, 0)),
        out_shape=jax.ShapeDtypeStruct((b, s, nv), BF16),
        scratch_shapes=[pltpu.VMEM((nv, nk), F32), pltpu.VMEM((n_chunks, nv, nk), F32),
                        pltpu.VMEM((c, nv), F32), pltpu.VMEM((c, nk), BF16)],
        compiler_params=_cparams("parallel", "arbitrary"),
        name="gla",
    )(z3, z3, z3, z3, z3, wg, b_gate.reshape(1, nk).astype(F32), norm_g.reshape(1, GLA_HEAD_DV).astype(F32), tril)


def _mem_attn_kernel(q_ref, k_ref, v_ref, qg_ref, kg_ref, o_ref):
    dh = MEM_HEAD_DIM
    nt = (((1,), (1,)), ((), ()))
    for h in range(MEM_HEADS):
        sl = slice(h * dh, (h + 1) * dh)
        q = q_ref[0, :, sl].astype(F32)
        q = q * lax.rsqrt(jnp.mean(q * q, axis=-1, keepdims=True) + NORM_EPS) * qg_ref[...] * (dh ** -0.5)
        k = k_ref[0, :, sl].astype(F32)
        k = k * lax.rsqrt(jnp.mean(k * k, axis=-1, keepdims=True) + NORM_EPS) * kg_ref[...]
        s = lax.dot_general(q.astype(BF16), k.astype(BF16), nt, preferred_element_type=F32)
        m = jnp.max(s, axis=-1, keepdims=True)
        p = jnp.exp(s - m)
        p = p / jnp.sum(p, axis=-1, keepdims=True)
        o = jnp.dot(p.astype(BF16), v_ref[0, :, sl].astype(BF16), preferred_element_type=F32)
        o_ref[0, :, sl] = o.astype(o_ref.dtype)


def _mem_attention(z3, kv, q_norm, k_norm, *, tq):
    b, s, _ = z3.shape
    m = kv.shape[1]
    w = MEM_HEADS * MEM_HEAD_DIM
    const = lambda shape: pl.BlockSpec(shape, lambda bi, i: (0,) * len(shape))
    return pl.pallas_call(
        _mem_attn_kernel,
        grid=(b, s // tq),
        in_specs=[
            pl.BlockSpec((1, tq, w), lambda bi, i: (bi, i, COL_MQ // w)),
            pl.BlockSpec((1, m, w), lambda bi, i: (bi, 0, 0)),
            pl.BlockSpec((1, m, w), lambda bi, i: (bi, 0, 1)),
            const((1, MEM_HEAD_DIM)), const((1, MEM_HEAD_DIM)),
        ],
        out_specs=pl.BlockSpec((1, tq, w), lambda bi, i: (bi, i, 0)),
        out_shape=jax.ShapeDtypeStruct((b, s, w), BF16),
        compiler_params=_cparams("parallel", "parallel"),
        name="mem_attention",
    )(z3, kv, kv, q_norm.reshape(1, MEM_HEAD_DIM).astype(F32), k_norm.reshape(1, MEM_HEAD_DIM).astype(F32))


def _merge_kernel(x_ref, on_ref, og_ref, om_ref, m0_ref, m1_ref, m2_ref, bm_ref, wb_ref, wo_ref, o_ref):
    merged = None
    for br, (ref, mg_ref) in enumerate(((on_ref, m0_ref), (og_ref, m1_ref), (om_ref, m2_ref))):
        y = jnp.dot(ref[...], wb_ref[br], preferred_element_type=F32)
        gate = jax.nn.sigmoid(mg_ref[...].astype(F32) + bm_ref[br:br + 1, :])
        merged = gate * y if merged is None else merged + gate * y
    o_ref[...] = x_ref[...] + jnp.dot(merged.astype(BF16), wo_ref[...], preferred_element_type=F32)


def _merge_out(x2, o_nsa, o_gla, o_mem, z2, b_merge, w_branch, w_out, layer, *, tm):
    t, d = x2.shape
    bw = BRANCH_WIDTH
    row = lambda w: pl.BlockSpec((tm, w), lambda i: (i, 0))
    gate_cols = lambda br: pl.BlockSpec((tm, d), lambda i: (i, COL_MERGE // d + br))
    return pl.pallas_call(
        _merge_kernel,
        grid=(t // tm,),
        in_specs=[
            row(d), row(bw), row(bw), row(bw),
            gate_cols(0), gate_cols(1), gate_cols(2),
            pl.BlockSpec((N_BRANCH, d), lambda i: (0, 0)),
            pl.BlockSpec((None, N_BRANCH, bw, d), lambda i: (layer, 0, 0, 0)),
            pl.BlockSpec((None, d, d), lambda i: (layer, 0, 0)),
        ],
        out_specs=row(d),
        out_shape=jax.ShapeDtypeStruct((t, d), F32),
        compiler_params=_cparams("parallel"),
        name="merge_out",
    )(x2, o_nsa, o_gla, o_mem, z2, z2, z2, b_merge.astype(F32), w_branch, w_out)


def _mlp_kernel(x_ref, g_ref, wu_ref, wd_ref, o_ref, h_ref):
    @pl.when(pl.program_id(1) == 0)
    def _():
        h_ref[...] = _rms_rows(x_ref[...], g_ref[...])
        o_ref[...] = x_ref[...]

    u = jnp.dot(h_ref[...], wu_ref[...], preferred_element_type=F32)
    u = jnp.square(jnp.maximum(u, 0.0)).astype(BF16)
    o_ref[...] += jnp.dot(u, wd_ref[...], preferred_element_type=F32)


def _mlp(x2, g, w_up, w_down, layer, *, tm, th):
    t, d = x2.shape
    hid = w_up.shape[2]
    return pl.pallas_call(
        _mlp_kernel,
        grid=(t // tm, hid // th),
        in_specs=[
            pl.BlockSpec((tm, d), lambda i, j: (i, 0)),
            pl.BlockSpec((1, d), lambda i, j: (0, 0)),
            pl.BlockSpec((None, d, th), lambda i, j: (layer, 0, j)),
            pl.BlockSpec((None, th, d), lambda i, j: (layer, j, 0)),
        ],
        out_specs=pl.BlockSpec((tm, d), lambda i, j: (i, 0)),
        out_shape=jax.ShapeDtypeStruct((t, d), F32),
        scratch_shapes=[pltpu.VMEM((tm, d), BF16)],
        compiler_params=_cparams("parallel", "arbitrary"),
        name="mlp",
    )(x2, g.reshape(1, d).astype(F32), w_up, w_down)


IN_PROJ_ROWS, IN_PROJ_COL_TILES = 1024, 2
NSA_PREP_ROWS = 512
NSA_QUERY_TILE, NSA_SELECT_PARTS, NSA_KEY_TILE = 256, 2, 512
SELECT_REGIONS = 4
KEY_TILE_MULTIPLES = (4, 2, 1)
GLA_CHUNKS_PER_STEP = 8
MEM_KV_ROWS, MEM_KV_COLS, MEM_QUERY_TILE = 512, 512, 1024
MERGE_ROWS = 512
MLP_ROWS, MLP_HIDDEN_TILE = 1024, 1024


def _tile(n, pref):
    t = min(n, pref)
    assert n % t == 0, (n, pref)
    return t


def _layer(x, mem2, pos_rows, pos_cmp, p, big, layer):
    b, s, d = x.shape
    t = b * s
    hd = NSA_HEAD_DIM
    x2 = x.reshape(t, d)

    z2 = _norm_matmul(x2, p["ln_mix"].astype(F32), big["w_in"], layer, tm=_tile(t, IN_PROJ_ROWS),
                      tn=D_IN_PAD // IN_PROJ_COL_TILES, out_dtype=BF16, name="in_proj")
    z3 = z2.reshape(b, s, D_IN_PAD)

    q, ks, vs, kw, vw = _nsa_prep(z3, pos_rows, p["nsa_q_norm"].astype(F32), p["nsa_k_norm"][1].astype(F32),
                                  p["nsa_k_norm"][2].astype(F32), tp=_tile(s, NSA_PREP_ROWS))
    pe2 = jnp.pad(p["cmp_pe"].reshape(2, 2, CMP_STRIDE * hd), ((0, 0), (0, 6), (0, 0))).astype(F32)
    cmp_kv = _nsa_compress(z3, p["cmp_w1"], p["cmp_w2"], pe2, pos_cmp, p["nsa_k_norm"][0].astype(F32))
    shifts = _nsa_shifts(p["nsa_q_norm"].astype(F32), p["nsa_k_norm"].astype(F32))
    tq = _tile(s, NSA_QUERY_TILE)
    o_cmp, sel = _nsa_select(shifts, q, cmp_kv, tq=tq, parts=NSA_SELECT_PARTS)
    o_nsa = _nsa_attend(shifts, q, ks, vs, kw, vw, z3, o_cmp, sel, tq=tq, tk=_tile(s, NSA_KEY_TILE))

    o_gla = _gla(z3, p["gla_w_gate"], p["gla_b_gate"], p["gla_norm"], n_chunks=GLA_CHUNKS_PER_STEP)

    kv = _norm_matmul(mem2, p["mem_norm"].astype(F32), big["mem_w_kv"], layer,
                      tm=_tile(mem2.shape[0], MEM_KV_ROWS), tn=MEM_KV_COLS, out_dtype=BF16, name="mem_kv")
    kv = kv.reshape(b, mem2.shape[0] // b, 2 * MEM_HEADS * MEM_HEAD_DIM)
    o_mem = _mem_attention(z3, kv, p["mem_q_norm"], p["mem_k_norm"], tq=_tile(s, MEM_QUERY_TILE))

    x2 = _merge_out(x2, o_nsa.reshape(t, -1), o_gla.reshape(t, -1), o_mem.reshape(t, -1), z2,
                    p["b_merge"], big["w_branch"], big["w_out"], layer, tm=_tile(t, MERGE_ROWS))
    x2 = _mlp(x2, p["ln_mlp"], big["w_up"], big["w_down"], layer, tm=_tile(t, MLP_ROWS), th=MLP_HIDDEN_TILE)
    return x2.reshape(b, s, d)


def kernel(x, mem, positions, ln_mix, w_in, b_merge, nsa_q_norm, nsa_k_norm, cmp_pe, cmp_w1, cmp_w2,
           gla_w_gate, gla_b_gate, gla_norm, mem_norm, mem_w_kv, mem_q_norm, mem_k_norm, w_branch, w_out,
           ln_mlp, w_up, w_down):
    b, s, d = x.shape
    assert d == 1024 and s % WINDOW == 0 and s >= 2 * WINDOW
    depth = w_in.shape[0]
    perm_np, d_in = _in_proj_permutation(d)
    assert w_in.shape[2] == d_in
    w_in = _relayout_in_proj(w_in.astype(BF16), [int(c) for c in perm_np], d_in)
    half = CMP_STRIDE * NSA_HEAD_DIM
    cmp_w1 = jnp.concatenate([cmp_w1[:, :, :half], cmp_w1[:, :, half:]], axis=-1).astype(BF16)
    cmp_w2, mem_w_kv, w_branch, w_out, w_up, w_down = (
        a.astype(BF16) for a in (cmp_w2, mem_w_kv, w_branch, w_out, w_up, w_down))
    pos3 = positions.astype(jnp.int32).reshape(b, s, 1)
    nc = s // CMP_STRIDE
    cmp_end = np.minimum(np.arange(nc) * CMP_STRIDE + CMP_BLOCK - 1, s - 1)
    pos_cmp = pos3[:, cmp_end, :]
    pos_rows = positions.astype(jnp.int32).reshape(b, 1, s)
    mem2 = mem.reshape(b * mem.shape[1], d)
    names = ("ln_mix", "b_merge", "nsa_q_norm", "nsa_k_norm", "cmp_pe", "cmp_w1", "cmp_w2",
             "gla_w_gate", "gla_b_gate", "gla_norm", "mem_norm", "mem_q_norm", "mem_k_norm", "ln_mlp")
    stacked = (ln_mix, b_merge, nsa_q_norm, nsa_k_norm, cmp_pe, cmp_w1, cmp_w2, gla_w_gate, gla_b_gate,
               gla_norm, mem_norm, mem_q_norm, mem_k_norm, ln_mlp)
    big = dict(w_in=w_in, mem_w_kv=mem_w_kv, w_branch=w_branch, w_out=w_out, w_up=w_up, w_down=w_down)
    for l in range(depth):
        x = _layer(x, mem2, pos_rows, pos_cmp, {n: a[l] for n, a in zip(names, stacked)}, big, l)
    return x
```
